```python
import jax, jax.numpy as jnp
from jax import lax
import numpy as np

D_MODEL = 1024
BATCH = 8
SEQ = 2048
DEPTH = 2
DEC_BATCH = 128
DEC_SEQ = 4
PAST_LEN = 16384
PAGE_SIZE = 128

N_META = 16
N_A_LAYERS = (DEPTH + 1) // 2
N_C_LAYERS = DEPTH // 2
EPS = 1e-6

M_HEADS = 4
M_DK = 256
M_DV = 256
M_WIDTH = M_HEADS * M_DV
M_CHUNK = 64
LRU_WIDTH = 1024
LRU_BLOCKS = 8
LRU_BS = LRU_WIDTH // LRU_BLOCKS
CONV_W = 4
LRU_C = 8.0
G_HEADS = 4
G_DK = 256
G_DV = 512
G_KW = G_HEADS * G_DK
G_VW = G_HEADS * G_DV
G_RANK = 16
G_TAU = 16.0
G_CHUNK = 64

EVEN_SPLITS = (M_HEADS * M_DK, M_HEADS * M_DK, M_WIDTH, M_WIDTH, M_HEADS, M_HEADS, M_WIDTH, LRU_WIDTH, LRU_WIDTH)
EVEN_IN = sum(EVEN_SPLITS)
EVEN_OUT = M_WIDTH + LRU_WIDTH
ODD_SPLITS = (G_KW, G_KW, G_VW, G_VW, G_RANK)
ODD_IN = sum(ODD_SPLITS)

kernel_name = 'hybrid_mlstm_rglru_gla_step'

F32 = jnp.float32


def _split(a, sizes):
    idx = [int(i) for i in np.cumsum(sizes)[:-1]]
    return jnp.split(a, idx, axis=-1)


def _rmsnorm(x, g):
    xf = x.astype(F32)
    y = xf * lax.rsqrt(jnp.mean(xf * xf, axis=-1, keepdims=True) + EPS)
    return y * g.astype(F32)


def _head_norm(h, g, nh):
    B, L, W = h.shape
    hh = h.reshape(B, L, nh, W // nh)
    hh = hh * lax.rsqrt(jnp.mean(hh * hh, axis=-1, keepdims=True) + EPS)
    return hh.reshape(B, L, W) * g.astype(F32)


def _heads(a, nh):
    B, L, W = a.shape
    return a.reshape(B, L, nh, W // nh).transpose(0, 2, 1, 3)


def _chunk_len(L, c):
    return c if L % c == 0 else L


def _to_chunks(a, T):
    B, H, L = a.shape[:3]
    a = a.reshape((B, H, L // T, T) + a.shape[3:])
    return jnp.moveaxis(a, 2, 0)


def _from_chunks(a):
    NC, B, H, T = a.shape[:4]
    return jnp.moveaxis(a, 0, 2).reshape((B, H, NC * T) + a.shape[4:])


def _mlstm_segment(q, k, v, li, lf, C, n, m):
    L = q.shape[2]
    T = _chunk_len(L, M_CHUNK)
    causal = jnp.tril(jnp.ones((T, T), dtype=bool))

    def step(carry, xs):
        C, n, m = carry
        qc, kc, vc, ic, fc = xs
        b = jnp.cumsum(fc, axis=-1)
        inter = b + m[..., None]
        D = b[..., :, None] - b[..., None, :] + ic[..., None, :]
        D = jnp.where(causal, D, -jnp.inf)
        mt = jnp.maximum(inter, jnp.max(D, axis=-1))
        w_inter = jnp.exp(inter - mt)
        s = jnp.einsum('bhtd,bhjd->bhtj', qc, kc) * jnp.exp(D - mt[..., None])
        num = w_inter[..., None] * jnp.einsum('bhvd,bhtd->bhtv', C, qc) + jnp.einsum('bhtj,bhjv->bhtv', s, vc)
        den = w_inter * jnp.einsum('bhd,bhtd->bht', n, qc) + jnp.sum(s, axis=-1)
        h = num / jnp.maximum(jnp.abs(den), jnp.exp(-mt))[..., None]
        m_new = mt[..., -1]
        wC = jnp.exp(inter[..., -1] - m_new)
        wk = jnp.exp(b[..., -1:] - b + ic - m_new[..., None])
        C_new = wC[..., None, None] * C + jnp.einsum('bhjv,bhjd->bhvd', vc * wk[..., None], kc)
        n_new = wC[..., None] * n + jnp.einsum('bhj,bhjd->bhd', wk, kc)
        return (C_new, n_new, m_new), h

    xs = (_to_chunks(q, T), _to_chunks(k, T), _to_chunks(v, T), _to_chunks(li, T), _to_chunks(lf, T))
    (C, n, m), h = lax.scan(step, (C, n, m), xs)
    return _from_chunks(h), (C, n, m)


def _gla_segment(q, k, v, lg, S):
    L = q.shape[2]
    T = _chunk_len(L, G_CHUNK)
    causal = jnp.tril(jnp.ones((T, T), dtype=bool))

    def step(S, xs):
        qc, kc, vc, gc = xs
        b = jnp.cumsum(gc, axis=2)
        bL = b[:, :, -1:]
        qd = qc * jnp.exp(b)
        A = jnp.einsum('bhtd,bhjd->bhtj', qd, kc * jnp.exp(-b))
        A = jnp.where(causal, A, 0.0)
        o = jnp.einsum('bhtd,bhdv->bhtv', qd, S) + jnp.einsum('bhtj,bhjv->bhtv', A, vc)
        S_new = jnp.exp(bL)[:, :, 0, :, None] * S + jnp.einsum('bhjd,bhjv->bhdv', kc * jnp.exp(bL - b), vc)
        return S_new, o

    xs = (_to_chunks(q, T), _to_chunks(k, T), _to_chunks(v, T), _to_chunks(lg, T))
    S, o = lax.scan(step, S, xs)
    return _from_chunks(o), S


def _causal_conv(x, buf, w, b):
    L = x.shape[1]
    xp = jnp.concatenate([buf.astype(F32), x], axis=1)
    wf = w.astype(F32)
    y = b.astype(F32) + wf[0] * xp[:, 0:L]
    for j in range(1, CONV_W):
        y = y + wf[j] * xp[:, j:j + L]
    return y, xp[:, -(CONV_W - 1):]


def _rglru(xc, h0, wa, ba, wx, bx, lam):
    B, L, W = xc.shape
    xb = xc.reshape(B, L, LRU_BLOCKS, LRU_BS)
    r = jax.nn.sigmoid(jnp.einsum('blni,nij->blnj', xb, wa.astype(F32)).reshape(B, L, W) + ba.astype(F32))
    i = jax.nn.sigmoid(jnp.einsum('blni,nij->blnj', xb, wx.astype(F32)).reshape(B, L, W) + bx.astype(F32))
    log_a = -LRU_C * r * jax.nn.softplus(-lam.astype(F32))
    a = jnp.exp(log_a)
    u = jnp.sqrt(-jnp.expm1(2.0 * log_a)) * (i * xc)
    u = u.at[:, 0].add(a[:, 0] * h0.astype(F32))

    def comb(lhs, rhs):
        al, bl = lhs
        ar, br = rhs
        return al * ar, ar * bl + br

    _, h = lax.associative_scan(comb, (a, u), axis=1)
    return h, h[:, -1]


def _even_layer(x, segs, C0, n0, m0, h0, conv0, pre_g, post_g, w_in, w_out, b_i, b_f, m_norm_g,
                conv_w, conv_b, wa, ba, wx, bx, lam):
    B, L, _ = x.shape
    hn = _rmsnorm(x, pre_g).astype(x.dtype)
    proj = jnp.einsum('bld,de->ble', hn, w_in).astype(F32)
    q, k, v, og, ig, fg, z, xr, gr = _split(proj, EVEN_SPLITS)
    qh = _heads(q, M_HEADS)
    kh = _heads(k, M_HEADS) * (M_DK ** -0.5)
    vh = _heads(v, M_HEADS)
    li = (ig + b_i.astype(F32)).transpose(0, 2, 1)
    lf = jax.nn.log_sigmoid(fg + b_f.astype(F32)).transpose(0, 2, 1)
    mstate = (C0.astype(F32), n0.astype(F32), m0.astype(F32))
    hs = []
    off = 0
    for sl in segs:
        idx = slice(off, off + sl)
        hseg, mstate = _mlstm_segment(qh[:, :, idx], kh[:, :, idx], vh[:, :, idx], li[:, :, idx], lf[:, :, idx], *mstate)
        hs.append(hseg)
        off += sl
    hm = jnp.concatenate(hs, axis=2).transpose(0, 2, 1, 3).reshape(B, L, M_WIDTH)
    hm = jax.nn.sigmoid(og) * hm
    hm = _head_norm(hm, m_norm_g, M_HEADS) * jax.nn.silu(z)
    xc, conv_new = _causal_conv(xr, conv0, conv_w, conv_b)
    hl, h_new = _rglru(xc, h0, wa, ba, wx, bx, lam)
    hl = hl * jax.nn.silu(gr)
    mix = jnp.concatenate([hm, hl], axis=-1).astype(x.dtype)
    out = jnp.einsum('ble,ed->bld', mix, w_out)
    y = x + _rmsnorm(out, post_g).astype(x.dtype)
    return y, (mstate[0], mstate[1], mstate[2], h_new, conv_new)


def _odd_layer(x, segs, S0, pre_g, post_g, w_in, w_out, a_up, a_b, g_norm_g):
    B, L, _ = x.shape
    hn = _rmsnorm(x, pre_g).astype(x.dtype)
    proj = jnp.einsum('bld,de->ble', hn, w_in).astype(F32)
    q, k, v, r, alow = _split(proj, ODD_SPLITS)
    qh = _heads(q, G_HEADS) * (G_DK ** -0.5)
    kh = _heads(k, G_HEADS)
    vh = _heads(v, G_HEADS)
    lg = jax.nn.log_sigmoid(jnp.einsum('blr,rk->blk', alow, a_up.astype(F32)) + a_b.astype(F32)) / G_TAU
    gh = _heads(lg, G_HEADS)
    S = S0.astype(F32)
    os_ = []
    off = 0
    for sl in segs:
        idx = slice(off, off + sl)
        oseg, S = _gla_segment(qh[:, :, idx], kh[:, :, idx], vh[:, :, idx], gh[:, :, idx], S)
        os_.append(oseg)
        off += sl
    o = jnp.concatenate(os_, axis=2).transpose(0, 2, 1, 3).reshape(B, L, G_VW)
    o = (_head_norm(o, g_norm_g, G_HEADS) * jax.nn.silu(r)).astype(x.dtype)
    out = jnp.einsum('ble,ed->bld', o, w_out)
    y = x + _rmsnorm(out, post_g).astype(x.dtype)
    return y, S


def setup_inputs(seed: int = 0) -> dict:
    key = jax.random.key(seed)
    ks = jax.random.split(key, 32)
    NA, NC = N_A_LAYERS, N_C_LAYERS

    def nrm(i, shape, scale=1.0):
        return scale * jax.random.normal(ks[i], shape, F32)

    u = jax.random.uniform(ks[20], (NA, LRU_WIDTH), F32, 0.9, 0.999)
    s = u ** (1.0 / LRU_C)
    lam = jnp.log(s) - jnp.log1p(-s)
    return {
        'x_prompt': nrm(0, (BATCH, SEQ, D_MODEL)),
        'x_sample': nrm(1, (DEC_BATCH, DEC_SEQ, D_MODEL)),
        'state_mlstm_C': nrm(2, (NA, DEC_BATCH, M_HEADS, M_DV, M_DK), 0.05),
        'state_mlstm_n': nrm(3, (NA, DEC_BATCH, M_HEADS, M_DK), 0.05),
        'state_mlstm_m': jax.random.uniform(ks[4], (NA, DEC_BATCH, M_HEADS), F32, 0.0, 3.0),
        'state_rglru_h': nrm(5, (NA, DEC_BATCH, LRU_WIDTH), 0.5),
        'state_rglru_conv': nrm(6, (NA, DEC_BATCH, CONV_W - 1, LRU_WIDTH)),
        'state_gla_S': nrm(7, (NC, DEC_BATCH, G_HEADS, G_DK, G_DV), 0.05),
        'meta_tokens': nrm(8, (N_META, D_MODEL)),
        'pre_norm_a': 1.0 + nrm(9, (NA, D_MODEL), 0.05),
        'post_norm_a': 1.0 + nrm(10, (NA, D_MODEL), 0.05),
        'w_in_a': nrm(11, (NA, D_MODEL, EVEN_IN), D_MODEL ** -0.5),
        'w_out_a': nrm(12, (NA, EVEN_OUT, D_MODEL), EVEN_OUT ** -0.5),
        'mlstm_b_i': nrm(13, (NA, M_HEADS), 0.1),
        'mlstm_b_f': jnp.linspace(3.0, 6.0, M_HEADS, dtype=F32)[None] + nrm(14, (NA, M_HEADS), 0.1),
        'mlstm_norm': 1.0 + nrm(15, (NA, M_WIDTH), 0.05),
        'conv_w': nrm(16, (NA, CONV_W, LRU_WIDTH), CONV_W ** -0.5),
        'conv_b': nrm(17, (NA, LRU_WIDTH), 0.02),
        'lru_w_a': nrm(18, (NA, LRU_BLOCKS, LRU_BS, LRU_BS), LRU_BS ** -0.5),
        'lru_b_a': nrm(19, (NA, LRU_WIDTH), 0.02),
        'lru_w_x': nrm(21, (NA, LRU_BLOCKS, LRU_BS, LRU_BS), LRU_BS ** -0.5),
        'lru_b_x': nrm(22, (NA, LRU_WIDTH), 0.02),
        'lru_lambda': lam,
        'pre_norm_c': 1.0 + nrm(23, (NC, D_MODEL), 0.05),
        'post_norm_c': 1.0 + nrm(24, (NC, D_MODEL), 0.05),
        'w_in_c': nrm(25, (NC, D_MODEL, ODD_IN), D_MODEL ** -0.5),
        'w_out_c': nrm(26, (NC, G_VW, D_MODEL), G_VW ** -0.5),
        'gla_alpha_up': nrm(27, (NC, G_RANK, G_KW), G_RANK ** -0.5),
        'gla_alpha_b': 2.0 + nrm(28, (NC, G_KW), 0.1),
        'gla_norm': 1.0 + nrm(29, (NC, G_VW), 0.05),
    }


def reference(x_prompt, x_sample, state_mlstm_C, state_mlstm_n, state_mlstm_m, state_rglru_h, state_rglru_conv,
              state_gla_S, meta_tokens, pre_norm_a, post_norm_a, w_in_a, w_out_a, mlstm_b_i, mlstm_b_f, mlstm_norm,
              conv_w, conv_b, lru_w_a, lru_b_a, lru_w_x, lru_b_x, lru_lambda, pre_norm_c, post_norm_c, w_in_c,
              w_out_c, gla_alpha_up, gla_alpha_b, gla_norm):
    B = x_prompt.shape[0]
    meta = jnp.broadcast_to(meta_tokens[None].astype(x_prompt.dtype), (B, N_META, D_MODEL))
    xp = jnp.concatenate([meta, x_prompt], axis=1)
    segs_p = (N_META, x_prompt.shape[1])
    xs = x_sample
    segs_s = (x_sample.shape[1],)

    pC, pn, pm, ph, pconv, pS = [], [], [], [], [], []
    sC, sn, sm, sh, sconv, sS = [], [], [], [], [], []
    for layer in range(DEPTH):
        j = layer // 2
        if layer % 2 == 0:
            wts = (pre_norm_a[j], post_norm_a[j], w_in_a[j], w_out_a[j], mlstm_b_i[j], mlstm_b_f[j], mlstm_norm[j],
                   conv_w[j], conv_b[j], lru_w_a[j], lru_b_a[j], lru_w_x[j], lru_b_x[j], lru_lambda[j])
            zC = jnp.zeros((B, M_HEADS, M_DV, M_DK), F32)
            zn = jnp.zeros((B, M_HEADS, M_DK), F32)
            zm = jnp.zeros((B, M_HEADS), F32)
            zh = jnp.zeros((B, LRU_WIDTH), F32)
            zconv = jnp.zeros((B, CONV_W - 1, LRU_WIDTH), F32)
            xp, st = _even_layer(xp, segs_p, zC, zn, zm, zh, zconv, *wts)
            pC.append(st[0]); pn.append(st[1]); pm.append(st[2]); ph.append(st[3]); pconv.append(st[4])
            xs, st = _even_layer(xs, segs_s, state_mlstm_C[j], state_mlstm_n[j], state_mlstm_m[j],
                                 state_rglru_h[j], state_rglru_conv[j], *wts)
            sC.append(st[0]); sn.append(st[1]); sm.append(st[2]); sh.append(st[3]); sconv.append(st[4])
        else:
            wts = (pre_norm_c[j], post_norm_c[j], w_in_c[j], w_out_c[j], gla_alpha_up[j], gla_alpha_b[j], gla_norm[j])
            zS = jnp.zeros((B, G_HEADS, G_DK, G_DV), F32)
            xp, S = _odd_layer(xp, segs_p, zS, *wts)
            pS.append(S)
            xs, S = _odd_layer(xs, segs_s, state_gla_S[j], *wts)
            sS.append(S)

    y_prompt = xp[:, N_META:]
    return (y_prompt, xs,
            jnp.stack(pC), jnp.stack(pn), jnp.stack(pm), jnp.stack(ph), jnp.stack(pconv), jnp.stack(pS),
            jnp.stack(sC), jnp.stack(sn), jnp.stack(sm), jnp.stack(sh), jnp.stack(sconv), jnp.stack(sS))
```

```python
import functools

import jax
import jax.numpy as jnp
from jax import lax
from jax.experimental import pallas as pl
from jax.experimental.pallas import tpu as pltpu

F32 = jnp.float32
BF16 = jnp.bfloat16

D_MODEL = 1024
N_META = 16
EPS = 1e-6
N_HEADS = 4
M_DK = 256
M_DV = 256
M_WIDTH = N_HEADS * M_DV
LRU_WIDTH = 1024
LRU_BLOCKS = 8
LRU_BS = LRU_WIDTH // LRU_BLOCKS
CONV_W = 4
LRU_C = 8.0
G_DK = 256
G_DV = 512
G_KW = N_HEADS * G_DK
G_VW = N_HEADS * G_DV
G_RANK = 16
G_TAU = 16.0

LANES = 128
SUBLANES = 8
NEG = -1e30
SAMPLE_PAD = SUBLANES
VMEM_LIMIT = 56 * 1024 * 1024


def _cparams(sem):
    return pltpu.CompilerParams(dimension_semantics=sem, vmem_limit_bytes=VMEM_LIMIT)


def _sigmoid(x):
    return 1.0 / (1.0 + jnp.exp(-x))


def _silu(x):
    return x * _sigmoid(x)


def _log_sigmoid(x):
    return jnp.minimum(x, 0.0) - jnp.log1p(jnp.exp(-jnp.abs(x)))


def _softplus(x):
    return jnp.maximum(x, 0.0) + jnp.log1p(jnp.exp(-jnp.abs(x)))


def _dot(a, b):
    return jnp.dot(a.astype(BF16), b.astype(BF16), preferred_element_type=F32)


def _dot_nt(a, b):
    return lax.dot_general(a.astype(BF16), b.astype(BF16), (((1,), (1,)), ((), ())),
                           preferred_element_type=F32)


def _dot_tn(a, b):
    return lax.dot_general(a.astype(BF16), b.astype(BF16), (((0,), (0,)), ((), ())),
                           preferred_element_type=F32)


def _transpose_rows(x):
    t, n = x.shape
    if t < LANES:
        x = jnp.concatenate([x, jnp.zeros((LANES - t, n), x.dtype)], axis=0)
    return jnp.transpose(x)


def _row_to_col(r):
    return jnp.transpose(jnp.broadcast_to(r, (LANES, r.shape[1])))[:, 0:1]


def _cumsum_rows(x):
    t = x.shape[0]
    if t <= 2 * SUBLANES:
        rid = lax.broadcasted_iota(jnp.int32, (t, 1), 0)
        s = 1
        while s < t:
            x = x + jnp.where(rid >= s, pltpu.roll(x, s, axis=0), 0.0)
            s *= 2
        return x
    row = lax.broadcasted_iota(jnp.int32, (t, t), 0)
    col = lax.broadcasted_iota(jnp.int32, (t, t), 1)
    tri = jnp.where(row >= col, 1.0, 0.0).astype(BF16)
    hi = x.astype(BF16)
    r1 = x - hi.astype(F32)
    mid = r1.astype(BF16)
    lo = (r1 - mid.astype(F32)).astype(BF16)
    acc = jnp.dot(tri, lo, preferred_element_type=F32)
    acc = acc + jnp.dot(tri, mid, preferred_element_type=F32)
    return acc + jnp.dot(tri, hi, preferred_element_type=F32)


def _norm_proj_kernel(x_ref, g_ref, wm_ref, ws_ref, om_ref, os_ref, hn_ref):
    @pl.when(pl.program_id(1) == 0)
    def _():
        x = x_ref[...]
        y = x * lax.rsqrt(jnp.mean(x * x, axis=-1, keepdims=True) + EPS) * g_ref[...]
        hn = y.astype(BF16)
        hn_ref[...] = hn
        os_ref[...] = jnp.dot(hn, ws_ref[...], preferred_element_type=F32)

    om_ref[...] = jnp.dot(hn_ref[...], wm_ref[...], preferred_element_type=F32)


def _norm_proj(x2d, g, w_main, w_small, tm, tn):
    n, d = x2d.shape
    e = w_main.shape[1]
    es = w_small.shape[1]
    return pl.pallas_call(
        _norm_proj_kernel,
        grid=(n // tm, e // tn),
        in_specs=[
            pl.BlockSpec((tm, d), lambda i, j: (i, 0)),
            pl.BlockSpec((1, d), lambda i, j: (0, 0)),
            pl.BlockSpec((d, tn), lambda i, j: (0, j)),
            pl.BlockSpec((d, es), lambda i, j: (0, 0)),
        ],
        out_specs=[
            pl.BlockSpec((tm, tn), lambda i, j: (i, j)),
            pl.BlockSpec((tm, es), lambda i, j: (i, 0)),
        ],
        out_shape=[jax.ShapeDtypeStruct((n, e), F32), jax.ShapeDtypeStruct((n, es), F32)],
        scratch_shapes=[pltpu.VMEM((tm, d), BF16)],
        compiler_params=_cparams(("parallel", "arbitrary")),
        name="norm_proj",
    )(x2d, g.reshape(1, d), w_main, w_small)


def _out_proj_kernel(n_in, *refs):
    x_ref, g_ref = refs[0], refs[1]
    a_refs = refs[2:2 + n_in]
    w_refs = refs[2 + n_in:2 + 2 * n_in]
    y_ref = refs[2 + 2 * n_in]
    out = jnp.dot(a_refs[0][...], w_refs[0][...], preferred_element_type=F32)
    for a_ref, w_ref in zip(a_refs[1:], w_refs[1:]):
        out = out + jnp.dot(a_ref[...], w_ref[...], preferred_element_type=F32)
    nrm = out * lax.rsqrt(jnp.mean(out * out, axis=-1, keepdims=True) + EPS) * g_ref[...]
    y_ref[...] = x_ref[...] + nrm


def _out_proj(x2d, g, acts, weights, tm):
    n, d = x2d.shape
    n_in = len(acts)
    in_specs = [pl.BlockSpec((tm, d), lambda i: (i, 0)), pl.BlockSpec((1, d), lambda i: (0, 0))]
    in_specs += [pl.BlockSpec((tm, a.shape[1]), lambda i: (i, 0)) for a in acts]
    in_specs += [pl.BlockSpec(w.shape, lambda i: (0, 0)) for w in weights]
    return pl.pallas_call(
        functools.partial(_out_proj_kernel, n_in),
        grid=(n // tm,),
        in_specs=in_specs,
        out_specs=pl.BlockSpec((tm, d), lambda i: (i, 0)),
        out_shape=jax.ShapeDtypeStruct((n, d), F32),
        compiler_params=_cparams(("parallel",)),
        name="out_proj",
    )(x2d, g.reshape(1, d), *acts, *weights)


def _mlstm_kernel(q_ref, k_ref, v_ref, og_ref, z_ref, gate_ref, gbias_ref, ng_ref, c0_ref, n0_ref, m0_ref,
                  hm_ref, c_out, n_out, m_out, c_s, n_s, m_s, *, t, l_valid):
    c = pl.program_id(1)

    @pl.when(c == 0)
    def _():
        c_s[...] = c0_ref[0]
        n_s[...] = n0_ref[0]
        m_s[...] = m0_ref[0]

    g = gate_ref[0] + gbias_ref[...]
    pos = c * t + lax.broadcasted_iota(jnp.int32, (t, 1), 0)
    valid = pos < l_valid
    li = jnp.where(valid, g[:, :LANES], NEG)
    lf = jnp.where(valid, _log_sigmoid(g[:, LANES:]), 0.0)
    b = _cumsum_rows(lf)
    r_t = _transpose_rows(li - b)[:, :t]
    row = lax.broadcasted_iota(jnp.int32, (t, t), 0)
    col = lax.broadcasted_iota(jnp.int32, (t, t), 1)
    causal = row >= col
    lane = lax.broadcasted_iota(jnp.int32, (1, LANES), 1)
    m_vec = m_s[...]
    m_next = m_vec
    ng = ng_ref[...]

    for h in range(N_HEADS):
        sl = slice(h * M_DK, (h + 1) * M_DK)
        q = q_ref[0, :, sl]
        k = k_ref[0, :, sl] * (M_DK ** -0.5)
        v = v_ref[0, :, sl]
        b_col = b[:, h:h + 1]
        li_col = li[:, h:h + 1]
        m_h = m_vec[:, h:h + 1]
        c_h = c_s[h]
        n_h = n_s[h]
        inter = b_col + m_h
        dmat = jnp.where(causal, b_col + r_t[h:h + 1, :], NEG)
        mt = jnp.maximum(inter, jnp.max(dmat, axis=-1, keepdims=True))
        w_inter = jnp.exp(inter - mt)
        s = _dot_nt(q, k) * jnp.exp(dmat - mt)
        num = w_inter * _dot_nt(q, c_h) + _dot(s, v)
        den = w_inter * jnp.sum(q * n_h, axis=-1, keepdims=True) + jnp.sum(s, axis=-1, keepdims=True)
        hh = num / jnp.maximum(jnp.abs(den), jnp.exp(-mt))
        m_new = mt[t - 1:t, :]
        w_c = jnp.exp(inter[t - 1:t, :] - m_new)
        w_k = jnp.exp(b_col[t - 1:t, :] - b_col + li_col - m_new)
        c_s[h] = w_c * c_h + _dot_tn(v * w_k, k)
        n_s[h] = w_c * n_h + jnp.sum(k * w_k, axis=0, keepdims=True)
        m_next = jnp.where(lane == h, m_new, m_next)
        hh = _sigmoid(og_ref[0, :, sl]) * hh
        hh = hh * lax.rsqrt(jnp.mean(hh * hh, axis=-1, keepdims=True) + EPS)
        hm_ref[0, :, sl] = (hh * ng[:, sl] * _silu(z_ref[0, :, sl])).astype(hm_ref.dtype)

    m_s[...] = m_next

    @pl.when(c == pl.num_programs(1) - 1)
    def _():
        c_out[0] = c_s[...]
        n_out[0] = n_s[...]
        m_out[0] = m_s[...]


def _mlstm(proj3, gates3, row0, nb, t, nc, l_valid, gbias, ng, c0, n0, m0):
    wd = M_WIDTH
    seq = lambda col: pl.BlockSpec((1, t, wd), lambda b, c: (row0 + b, c, col))
    full2 = lambda a: pl.BlockSpec(a.shape, lambda b, c: (0, 0))
    kern = functools.partial(_mlstm_kernel, t=t, l_valid=l_valid)
    return pl.pallas_call(
        kern,
        grid=(nb, nc),
        in_specs=[
            seq(0), seq(1), seq(2), seq(3), seq(4),
            pl.BlockSpec((1, t, 2 * LANES), lambda b, c: (row0 + b, c, 0)),
            full2(gbias), full2(ng),
            pl.BlockSpec((1, N_HEADS, M_DV, M_DK), lambda b, c: (b, 0, 0, 0)),
            pl.BlockSpec((1, N_HEADS, 1, M_DK), lambda b, c: (b, 0, 0, 0)),
            pl.BlockSpec((1, 1, LANES), lambda b, c: (b, 0, 0)),
        ],
        out_specs=[
            pl.BlockSpec((1, t, wd), lambda b, c: (b, c, 0)),
            pl.BlockSpec((1, N_HEADS, M_DV, M_DK), lambda b, c: (b, 0, 0, 0)),
            pl.BlockSpec((1, N_HEADS, 1, M_DK), lambda b, c: (b, 0, 0, 0)),
            pl.BlockSpec((1, 1, LANES), lambda b, c: (b, 0, 0)),
        ],
        out_shape=[
            jax.ShapeDtypeStruct((nb, t * nc, wd), BF16),
            jax.ShapeDtypeStruct((nb, N_HEADS, M_DV, M_DK), F32),
            jax.ShapeDtypeStruct((nb, N_HEADS, 1, M_DK), F32),
            jax.ShapeDtypeStruct((nb, 1, LANES), F32),
        ],
        scratch_shapes=[
            pltpu.VMEM((N_HEADS, M_DV, M_DK), F32),
            pltpu.VMEM((N_HEADS, 1, M_DK), F32),
            pltpu.VMEM((1, LANES), F32),
        ],
        compiler_params=_cparams(("parallel", "arbitrary")),
        name="mlstm",
    )(proj3, proj3, proj3, proj3, proj3, gates3, gbias, ng, c0, n0, m0)


def _shift_rows(x, prev8, s):
    rolled = pltpu.roll(x, s, axis=0)
    rid = lax.broadcasted_iota(jnp.int32, (SUBLANES, 1), 0)
    head = jnp.where(rid < s, pltpu.roll(prev8, s, axis=0), rolled[:SUBLANES])
    if x.shape[0] == SUBLANES:
        return head
    return jnp.concatenate([head, rolled[SUBLANES:]], axis=0)


def _rglru_kernel(x_ref, gr_ref, cw_ref, cb_ref, wa_ref, ba_ref, wx_ref, bx_ref, lam_ref, h0_ref, conv0_ref,
                  hl_ref, hlast_ref, h_s, prev_s, *, t, l_valid):
    c = pl.program_id(1)

    @pl.when(c == 0)
    def _():
        h_s[...] = h0_ref[0]
        prev_s[...] = conv0_ref[0]

    x = x_ref[0]
    prev8 = prev_s[...]
    cw = cw_ref[...]
    xc = cb_ref[...] + cw[CONV_W - 1:CONV_W, :] * x
    for s in range(1, CONV_W):
        xc = xc + cw[CONV_W - 1 - s:CONV_W - s, :] * _shift_rows(x, prev8, s)
    prev_s[...] = x[t - SUBLANES:, :]

    ra, ri = [], []
    for n in range(LRU_BLOCKS):
        xb = xc[:, n * LRU_BS:(n + 1) * LRU_BS].astype(BF16)
        ra.append(jnp.dot(xb, wa_ref[n], preferred_element_type=F32))
        ri.append(jnp.dot(xb, wx_ref[n], preferred_element_type=F32))
    r = _sigmoid(jnp.concatenate(ra, axis=-1) + ba_ref[...])
    i = _sigmoid(jnp.concatenate(ri, axis=-1) + bx_ref[...])
    log_a = -LRU_C * r * _softplus(-lam_ref[...])
    a = jnp.exp(log_a)
    u = jnp.sqrt(-jnp.tanh(log_a) * (1.0 + a * a)) * (i * xc)

    rid = lax.broadcasted_iota(jnp.int32, (t, 1), 0)
    s = 1
    while s < t:
        a_sh = jnp.where(rid >= s, pltpu.roll(a, s, axis=0), 1.0)
        u_sh = jnp.where(rid >= s, pltpu.roll(u, s, axis=0), 0.0)
        u = a * u_sh + u
        a = a * a_sh
        s *= 2
    h = a * h_s[...] + u
    last = l_valid - 1 - (l_valid - 1) // t * t
    h_s[...] = h[t - 1:t, :]
    hl_ref[0] = (h * _silu(gr_ref[0])).astype(hl_ref.dtype)

    @pl.when(c == (l_valid - 1) // t)
    def _():
        hlast_ref[0] = h[last:last + 1, :]


def _rglru(proj3, row0, nb, t, nc, l_valid, cw, cb, wa, ba, wx, bx, lam, h0, conv0):
    w = LRU_WIDTH
    full2 = lambda a: pl.BlockSpec(a.shape, lambda b, c: (0, 0))
    full3 = lambda a: pl.BlockSpec(a.shape, lambda b, c: (0, 0, 0))
    kern = functools.partial(_rglru_kernel, t=t, l_valid=l_valid)
    return pl.pallas_call(
        kern,
        grid=(nb, nc),
        in_specs=[
            pl.BlockSpec((1, t, w), lambda b, c: (row0 + b, c, 5)),
            pl.BlockSpec((1, t, w), lambda b, c: (row0 + b, c, 6)),
            full2(cw), full2(cb), full3(wa), full2(ba), full3(wx), full2(bx), full2(lam),
            pl.BlockSpec((1, 1, w), lambda b, c: (b, 0, 0)),
            pl.BlockSpec((1, SUBLANES, w), lambda b, c: (b, 0, 0)),
        ],
        out_specs=[
            pl.BlockSpec((1, t, w), lambda b, c: (b, c, 0)),
            pl.BlockSpec((1, 1, w), lambda b, c: (b, 0, 0)),
        ],
        out_shape=[
            jax.ShapeDtypeStruct((nb, t * nc, w), BF16),
            jax.ShapeDtypeStruct((nb, 1, w), F32),
        ],
        scratch_shapes=[pltpu.VMEM((1, w), F32), pltpu.VMEM((SUBLANES, w), F32)],
        compiler_params=_cparams(("parallel", "arbitrary")),
        name="rglru",
    )(proj3, proj3, cw, cb, wa, ba, wx, bx, lam, h0, conv0)


def _gla_kernel(q_ref, k_ref, v_ref, r_ref, al_ref, aup_ref, ab_ref, ng_ref, s0_ref,
                o_ref, s_out, s_s, *, t, l_valid):
    c = pl.program_id(1)

    @pl.when(c == 0)
    def _():
        s_s[...] = s0_ref[0]

    pos = c * t + lax.broadcasted_iota(jnp.int32, (t, 1), 0)
    valid = pos < l_valid
    pre = _dot(al_ref[0], aup_ref[...]) + ab_ref[...]
    lg = jnp.where(valid, _log_sigmoid(pre) / G_TAU, 0.0)
    b = _cumsum_rows(lg)
    b_last = b[t - 1:t, :]
    eb = jnp.exp(b)
    enb = jnp.exp(-b)
    erb = jnp.exp(b_last - b)
    ebl_col = _row_to_col(jnp.exp(b_last))
    row = lax.broadcasted_iota(jnp.int32, (t, t), 0)
    col = lax.broadcasted_iota(jnp.int32, (t, t), 1)
    causal = row >= col
    ng = ng_ref[...]

    for h in range(N_HEADS):
        ks = slice(h * G_DK, (h + 1) * G_DK)
        vs = slice(h * G_DV, (h + 1) * G_DV)
        k = jnp.where(valid, k_ref[0, :, ks], 0.0)
        v = v_ref[0, :, vs]
        qd = (q_ref[0, :, ks] * (G_DK ** -0.5)) * eb[:, ks]
        a = jnp.where(causal, _dot_nt(qd, k * enb[:, ks]), 0.0)
        s_h = s_s[h]
        o = _dot(qd, s_h) + _dot(a, v)
        s_s[h] = ebl_col[ks, :] * s_h + _dot_tn(k * erb[:, ks], v)
        o = o * lax.rsqrt(jnp.mean(o * o, axis=-1, keepdims=True) + EPS)
        o_ref[0, :, vs] = (o * ng[:, vs] * _silu(r_ref[0, :, vs])).astype(o_ref.dtype)

    @pl.when(c == pl.num_programs(1) - 1)
    def _():
        s_out[0] = s_s[...]


def _gla(proj3, al3, row0, nb, t, nc, l_valid, aup, ab, ng, s0):
    full2 = lambda a: pl.BlockSpec(a.shape, lambda b, c: (0, 0))
    kern = functools.partial(_gla_kernel, t=t, l_valid=l_valid)
    return pl.pallas_call(
        kern,
        grid=(nb, nc),
        in_specs=[
            pl.BlockSpec((1, t, G_KW), lambda b, c: (row0 + b, c, 0)),
            pl.BlockSpec((1, t, G_KW), lambda b, c: (row0 + b, c, 1)),
            pl.BlockSpec((1, t, G_VW), lambda b, c: (row0 + b, c, 1)),
            pl.BlockSpec((1, t, G_VW), lambda b, c: (row0 + b, c, 2)),
            pl.BlockSpec((1, t, LANES), lambda b, c: (row0 + b, c, 0)),
            full2(aup), full2(ab), full2(ng),
            pl.BlockSpec((1, N_HEADS, G_DK, G_DV), lambda b, c: (b, 0, 0, 0)),
        ],
        out_specs=[
            pl.BlockSpec((1, t, G_VW), lambda b, c: (b, c, 0)),
            pl.BlockSpec((1, N_HEADS, G_DK, G_DV), lambda b, c: (b, 0, 0, 0)),
        ],
        out_shape=[
            jax.ShapeDtypeStruct((nb, t * nc, G_VW), BF16),
            jax.ShapeDtypeStruct((nb, N_HEADS, G_DK, G_DV), F32),
        ],
        scratch_shapes=[pltpu.VMEM((N_HEADS, G_DK, G_DV), F32)],
        compiler_params=_cparams(("parallel", "arbitrary")),
        name="gla",
    )(proj3, proj3, proj3, proj3, al3, aup, ab, ng, s0)


def _pad_lanes(a, width):
    return jnp.pad(a, [(0, 0)] * (a.ndim - 1) + [(0, width - a.shape[-1])])


def _even_layer(xm, xs, nbp, seq, nbs, ls, st, pre_g, post_g, w_in, w_out, b_i, b_f, m_norm_g,
                conv_w, conv_b, wa, ba, wx, bx, lam):
    c0s, n0s, m0s, h0s, conv0s = st
    wq, wk, wv, wog, wig, wfg, wz, wxr, wgr = jnp.split(
        w_in, [1024, 2048, 3072, 4096, 4100, 4104, 5128, 6152], axis=1)
    w_main = jnp.concatenate([wq, wk, wv, wog, wz, wxr, wgr], axis=1).astype(BF16)
    w_gate = jnp.concatenate([_pad_lanes(wig, LANES), _pad_lanes(wfg, LANES)], axis=1).astype(BF16)
    gbias = jnp.concatenate([_pad_lanes(b_i[None], LANES), _pad_lanes(b_f[None], LANES)], axis=1)
    ng = m_norm_g[None]
    cw, cb = conv_w, conv_b[None]
    wab, wxb = wa.astype(BF16), wx.astype(BF16)
    bav, bxv, lamv = ba[None], bx[None], lam[None]
    w_o1, w_o2 = w_out[:M_WIDTH].astype(BF16), w_out[M_WIDTH:].astype(BF16)
    e = w_main.shape[1]

    pm, gm = _norm_proj(xm, pre_g, w_main, w_gate, 512, 1024)
    ps, gs = _norm_proj(xs, pre_g, w_main, w_gate, 128, 1024)
    n_meta_rows = nbp * N_META

    zc = jnp.zeros((nbp, N_HEADS, M_DV, M_DK), F32)
    zn = jnp.zeros((nbp, N_HEADS, 1, M_DK), F32)
    zm = jnp.zeros((nbp, 1, LANES), F32)
    zh = jnp.zeros((nbp, 1, LRU_WIDTH), F32)
    zconv = jnp.zeros((nbp, SUBLANES, LRU_WIDTH), F32)
    ps_meta = ps.reshape(-1, N_META, e)
    gs_meta = gs.reshape(-1, N_META, 2 * LANES)
    hm_meta, c1, n1, m1 = _mlstm(ps_meta, gs_meta, 0, nbp, N_META, 1, N_META, gbias, ng, zc, zn, zm)
    hl_meta, h1 = _rglru(ps_meta, 0, nbp, N_META, 1, N_META, cw, cb, wab, bav, wxb, bxv, lamv, zh, zconv)
    conv1 = ps_meta[:nbp, N_META - SUBLANES:, 5 * 1024:6 * 1024]

    pm3 = pm.reshape(nbp, seq, e)
    gm3 = gm.reshape(nbp, seq, 2 * LANES)
    tm_ = 64
    hm_main, pc, pn, pmm = _mlstm(pm3, gm3, 0, nbp, tm_, seq // tm_, seq, gbias, ng, c1, n1, m1)
    tl = 256
    hl_main, ph = _rglru(pm3, 0, nbp, tl, seq // tl, seq, cw, cb, wab, bav, wxb, bxv, lamv, h1, conv1)
    pconv = pm3[:, seq - (CONV_W - 1):, 5 * 1024:6 * 1024]

    ps_s = ps.reshape(-1, SAMPLE_PAD, e)
    gs_s = gs.reshape(-1, SAMPLE_PAD, 2 * LANES)
    r0 = n_meta_rows // SAMPLE_PAD
    hm_s, sc, sn, sm = _mlstm(ps_s, gs_s, r0, nbs, SAMPLE_PAD, 1, ls, gbias, ng,
                              c0s, n0s[:, :, None, :], _pad_lanes(m0s, LANES)[:, None, :])
    conv0p = jnp.pad(conv0s, ((0, 0), (SUBLANES - (CONV_W - 1), 0), (0, 0)))
    hl_s, sh = _rglru(ps_s, r0, nbs, SAMPLE_PAD, 1, ls, cw, cb, wab, bav, wxb, bxv, lamv,
                      h0s[:, None, :], conv0p)
    xr_s = ps_s[r0:, :ls, 5 * 1024:6 * 1024]
    sconv = jnp.concatenate([conv0s, xr_s], axis=1)[:, -(CONV_W - 1):]

    ym = _out_proj(xm, post_g, [hm_main.reshape(-1, M_WIDTH), hl_main.reshape(-1, LRU_WIDTH)], [w_o1, w_o2], 512)
    hm_small = jnp.concatenate([hm_meta.reshape(-1, M_WIDTH), hm_s.reshape(-1, M_WIDTH)], axis=0)
    hl_small = jnp.concatenate([hl_meta.reshape(-1, LRU_WIDTH), hl_s.reshape(-1, LRU_WIDTH)], axis=0)
    ys = _out_proj(xs, post_g, [hm_small, hl_small], [w_o1, w_o2], 128)

    p_state = (pc, pn[:, :, 0, :], pmm[:, 0, :N_HEADS], ph[:, 0, :], pconv)
    s_state = (sc, sn[:, :, 0, :], sm[:, 0, :N_HEADS], sh[:, 0, :], sconv)
    return ym, ys, p_state, s_state


def _odd_layer(xm, xs, nbp, seq, nbs, ls, s0s, pre_g, post_g, w_in, w_out, a_up, a_b, g_norm_g):
    w_main = w_in[:, :2 * G_KW + 2 * G_VW].astype(BF16)
    w_low = _pad_lanes(w_in[:, 2 * G_KW + 2 * G_VW:], LANES).astype(BF16)
    aup = jnp.pad(a_up, ((0, LANES - G_RANK), (0, 0))).astype(BF16)
    ab = a_b[None]
    ng = g_norm_g[None]
    w_o = w_out.astype(BF16)
    e = w_main.shape[1]

    pm, am = _norm_proj(xm, pre_g, w_main, w_low, 512, 1024)
    ps, as_ = _norm_proj(xs, pre_g, w_main, w_low, 128, 1024)
    n_meta_rows = nbp * N_META

    zs = jnp.zeros((nbp, N_HEADS, G_DK, G_DV), F32)
    o_meta, s1 = _gla(ps.reshape(-1, N_META, e), as_.reshape(-1, N_META, LANES), 0, nbp, N_META, 1, N_META,
                      aup, ab, ng, zs)
    tg = 64
    o_main, p_s = _gla(pm.reshape(nbp, seq, e), am.reshape(nbp, seq, LANES), 0, nbp, tg, seq // tg, seq,
                       aup, ab, ng, s1)
    r0 = n_meta_rows // SAMPLE_PAD
    o_s, s_s = _gla(ps.reshape(-1, SAMPLE_PAD, e), as_.reshape(-1, SAMPLE_PAD, LANES), r0, nbs, SAMPLE_PAD, 1, ls,
                    aup, ab, ng, s0s)

    ym = _out_proj(xm, post_g, [o_main.reshape(-1, G_VW)], [w_o], 512)
    o_small = jnp.concatenate([o_meta.reshape(-1, G_VW), o_s.reshape(-1, G_VW)], axis=0)
    ys = _out_proj(xs, post_g, [o_small], [w_o], 128)
    return ym, ys, p_s, s_s


def kernel(x_prompt, x_sample, state_mlstm_C, state_mlstm_n, state_mlstm_m, state_rglru_h, state_rglru_conv,
           state_gla_S, meta_tokens, pre_norm_a, post_norm_a, w_in_a, w_out_a, mlstm_b_i, mlstm_b_f, mlstm_norm,
           conv_w, conv_b, lru_w_a, lru_b_a, lru_w_x, lru_b_x, lru_lambda, pre_norm_c, post_norm_c, w_in_c,
           w_out_c, gla_alpha_up, gla_alpha_b, gla_norm):
    nbp, seq, d = x_prompt.shape
    nbs, ls, _ = x_sample.shape
    depth = pre_norm_a.shape[0] + pre_norm_c.shape[0]
    assert ls >= CONV_W - 1 and ls <= SAMPLE_PAD and N_META % SUBLANES == 0

    xm = x_prompt.reshape(nbp * seq, d)
    meta = jnp.broadcast_to(meta_tokens[None].astype(x_prompt.dtype), (nbp, N_META, d)).reshape(nbp * N_META, d)
    xs_pad = jnp.pad(x_sample, ((0, 0), (0, SAMPLE_PAD - ls), (0, 0))).reshape(nbs * SAMPLE_PAD, d)
    xs = jnp.concatenate([meta, xs_pad], axis=0)

    p_lists = [[] for _ in range(6)]
    s_lists = [[] for _ in range(6)]
    for layer in range(depth):
        j = layer // 2
        if layer % 2 == 0:
            st = (state_mlstm_C[j], state_mlstm_n[j], state_mlstm_m[j], state_rglru_h[j], state_rglru_conv[j])
            xm, xs, pst, sst = _even_layer(
                xm, xs, nbp, seq, nbs, ls, st, pre_norm_a[j], post_norm_a[j], w_in_a[j], w_out_a[j],
                mlstm_b_i[j], mlstm_b_f[j], mlstm_norm[j], conv_w[j], conv_b[j], lru_w_a[j], lru_b_a[j],
                lru_w_x[j], lru_b_x[j], lru_lambda[j])
            for i in range(5):
                p_lists[i].append(pst[i])
                s_lists[i].append(sst[i])
        else:
            xm, xs, p_s, s_s = _odd_layer(
                xm, xs, nbp, seq, nbs, ls, state_gla_S[j], pre_norm_c[j], post_norm_c[j], w_in_c[j], w_out_c[j],
                gla_alpha_up[j], gla_alpha_b[j], gla_norm[j])
            p_lists[5].append(p_s)
            s_lists[5].append(s_s)

    y_prompt = xm.reshape(nbp, seq, d)
    y_sample = xs[nbp * N_META:].reshape(nbs, SAMPLE_PAD, d)[:, :ls]
    return (y_prompt, y_sample) + tuple(jnp.stack(l) for l in p_lists) + tuple(jnp.stack(l) for l in s_lists)
```

```python
import functools

import jax
import jax.numpy as jnp
from jax import lax
from jax.experimental import pallas as pl
from jax.experimental.pallas import tpu as pltpu

F32 = jnp.float32
BF16 = jnp.bfloat16

D_MODEL = 1024
N_META = 16
EPS = 1e-6
N_HEADS = 4
M_DK = 256
M_DV = 256
M_WIDTH = N_HEADS * M_DV
LRU_WIDTH = 1024
LRU_BLOCKS = 8
LRU_BS = LRU_WIDTH // LRU_BLOCKS
CONV_W = 4
LRU_C = 8.0
G_DK = 256
G_DV = 512
G_KW = N_HEADS * G_DK
G_VW = N_HEADS * G_DV
G_RANK = 16
G_TAU = 16.0

LANES = 128
SUBLANES = 8
NEG = -1e30
SAMPLE_PAD = SUBLANES
VMEM_LIMIT = 56 * 1024 * 1024
COL_TILE = 1024

MLSTM_TILES = {"meta": (8, N_META), "main": (4, 128), "sample": (4, SAMPLE_PAD)}
GLA_TILES = {"meta": (8, N_META), "main": (2, 64), "sample": (4, SAMPLE_PAD)}
LRU_TILES = {"meta": (8, N_META), "main": (1, 256), "sample": (16, SAMPLE_PAD)}
ROW_TILE_MAIN = 512


def _cparams(sem):
    return pltpu.CompilerParams(dimension_semantics=sem, vmem_limit_bytes=VMEM_LIMIT)


def _sigmoid(x):
    return 1.0 / (1.0 + jnp.exp(-x))


def _silu(x):
    return x * _sigmoid(x)


def _log_sigmoid(x):
    return jnp.minimum(x, 0.0) - jnp.log1p(jnp.exp(-jnp.abs(x)))


def _softplus(x):
    return jnp.maximum(x, 0.0) + jnp.log1p(jnp.exp(-jnp.abs(x)))


def _dot(a, b):
    return jnp.dot(a.astype(BF16), b.astype(BF16), preferred_element_type=F32)


def _dot_nt(a, b):
    return lax.dot_general(a.astype(BF16), b.astype(BF16), (((1,), (1,)), ((), ())),
                           preferred_element_type=F32)


def _dot_tn(a, b):
    return lax.dot_general(a.astype(BF16), b.astype(BF16), (((0,), (0,)), ((), ())),
                           preferred_element_type=F32)


def _transpose_rows(x):
    t, n = x.shape
    if t < LANES:
        x = jnp.concatenate([x, jnp.zeros((LANES - t, n), x.dtype)], axis=0)
    return jnp.transpose(x)


def _row_to_col(r):
    return jnp.transpose(jnp.broadcast_to(r, (LANES, r.shape[1])))[:, 0:1]


def _cumsum_rows(x):
    t = x.shape[0]
    if t <= 2 * SUBLANES:
        rid = lax.broadcasted_iota(jnp.int32, (t, 1), 0)
        s = 1
        while s < t:
            x = x + jnp.where(rid >= s, pltpu.roll(x, s, axis=0), 0.0)
            s *= 2
        return x
    row = lax.broadcasted_iota(jnp.int32, (t, t), 0)
    col = lax.broadcasted_iota(jnp.int32, (t, t), 1)
    tri = jnp.where(row >= col, 1.0, 0.0).astype(BF16)
    hi = x.astype(BF16)
    r1 = x - hi.astype(F32)
    mid = r1.astype(BF16)
    lo = (r1 - mid.astype(F32)).astype(BF16)
    acc = jnp.dot(tri, lo, preferred_element_type=F32)
    acc = acc + jnp.dot(tri, mid, preferred_element_type=F32)
    return acc + jnp.dot(tri, hi, preferred_element_type=F32)


def _pair_loop(nb, row_fn):
    if nb == 1:
        row_fn(0)
        return

    def body(i, carry):
        row_fn(2 * i)
        row_fn(2 * i + 1)
        return carry

    lax.fori_loop(0, nb // 2, body, 0)


def _norm_proj_kernel(n_lo, x_ref, g_ref, wm_ref, ws_ref, olo_ref, ohi_ref, os_ref, hn_ref):
    j = pl.program_id(1)

    @pl.when(j == 0)
    def _():
        x = x_ref[...]
        y = x * lax.rsqrt(jnp.mean(x * x, axis=-1, keepdims=True) + EPS) * g_ref[...]
        hn = y.astype(BF16)
        hn_ref[...] = hn
        os_ref[...] = jnp.dot(hn, ws_ref[...], preferred_element_type=F32)

    acc = jnp.dot(hn_ref[...], wm_ref[...], preferred_element_type=F32)

    @pl.when(j < n_lo)
    def _():
        olo_ref[...] = acc.astype(olo_ref.dtype)

    @pl.when(j >= n_lo)
    def _():
        ohi_ref[...] = acc


def _norm_proj(x2d, g, w_main, w_small, e_lo, tm):
    n, d = x2d.shape
    e = w_main.shape[1]
    es = w_small.shape[1]
    tn = COL_TILE
    n_lo = e_lo // tn
    return pl.pallas_call(
        functools.partial(_norm_proj_kernel, n_lo),
        grid=(n // tm, e // tn),
        in_specs=[
            pl.BlockSpec((tm, d), lambda i, j: (i, 0)),
            pl.BlockSpec((1, d), lambda i, j: (0, 0)),
            pl.BlockSpec((d, tn), lambda i, j: (0, j)),
            pl.BlockSpec((d, es), lambda i, j: (0, 0)),
        ],
        out_specs=[
            pl.BlockSpec((tm, tn), lambda i, j: (i, jnp.minimum(j, n_lo - 1))),
            pl.BlockSpec((tm, tn), lambda i, j: (i, jnp.maximum(j - n_lo, 0))),
            pl.BlockSpec((tm, es), lambda i, j: (i, 0)),
        ],
        out_shape=[
            jax.ShapeDtypeStruct((n, e_lo), BF16),
            jax.ShapeDtypeStruct((n, e - e_lo), F32),
            jax.ShapeDtypeStruct((n, es), F32),
        ],
        scratch_shapes=[pltpu.VMEM((tm, d), BF16)],
        compiler_params=_cparams(("parallel", "arbitrary")),
        name="norm_proj",
    )(x2d, g.reshape(1, d), w_main, w_small)


def _out_proj_kernel(n_in, *refs):
    x_ref, g_ref = refs[0], refs[1]
    a_refs = refs[2:2 + n_in]
    w_refs = refs[2 + n_in:2 + 2 * n_in]
    y_ref = refs[2 + 2 * n_in]
    out = jnp.dot(a_refs[0][...], w_refs[0][...], preferred_element_type=F32)
    for a_ref, w_ref in zip(a_refs[1:], w_refs[1:]):
        out = out + jnp.dot(a_ref[...], w_ref[...], preferred_element_type=F32)
    nrm = out * lax.rsqrt(jnp.mean(out * out, axis=-1, keepdims=True) + EPS) * g_ref[...]
    y_ref[...] = x_ref[...] + nrm


def _out_proj(x2d, g, acts, weights, tm):
    n, d = x2d.shape
    n_in = len(acts)
    in_specs = [pl.BlockSpec((tm, d), lambda i: (i, 0)), pl.BlockSpec((1, d), lambda i: (0, 0))]
    in_specs += [pl.BlockSpec((tm, a.shape[1]), lambda i: (i, 0)) for a in acts]
    in_specs += [pl.BlockSpec(w.shape, lambda i: (0, 0)) for w in weights]
    return pl.pallas_call(
        functools.partial(_out_proj_kernel, n_in),
        grid=(n // tm,),
        in_specs=in_specs,
        out_specs=pl.BlockSpec((tm, d), lambda i: (i, 0)),
        out_shape=jax.ShapeDtypeStruct((n, d), F32),
        compiler_params=_cparams(("parallel",)),
        name="out_proj",
    )(x2d, g.reshape(1, d), *acts, *weights)


def _mlstm_kernel(q_ref, k_ref, v_ref, og_ref, z_ref, gate_ref, gbias_ref, ng_ref, c0_ref, n0_ref, m0_ref,
                  hm_ref, c_ref, n_ref, m_ref, *, nb, t, l_valid):
    c = pl.program_id(1)

    @pl.when(c == 0)
    def _():
        c_ref[...] = c0_ref[...]
        n_ref[...] = n0_ref[...]
        m_ref[...] = m0_ref[...]

    pos = c * t + lax.broadcasted_iota(jnp.int32, (t, 1), 0)
    valid = pos < l_valid
    row = lax.broadcasted_iota(jnp.int32, (t, t), 0)
    col = lax.broadcasted_iota(jnp.int32, (t, t), 1)
    causal = row >= col
    lane = lax.broadcasted_iota(jnp.int32, (1, LANES), 1)
    gbias = gbias_ref[...]
    ng = ng_ref[...]

    def one_row(bi):
        g = gate_ref[bi] + gbias
        li = jnp.where(valid, g[:, :LANES], NEG)
        lf = jnp.where(valid, _log_sigmoid(g[:, LANES:]), 0.0)
        b = _cumsum_rows(lf)
        r_t = _transpose_rows(li - b)[:, :t]
        m_vec = m_ref[bi]
        m_next = m_vec
        for h in range(N_HEADS):
            sl = slice(h * M_DK, (h + 1) * M_DK)
            q = q_ref[bi, :, sl]
            k = k_ref[bi, :, sl] * (M_DK ** -0.5)
            v = v_ref[bi, :, sl]
            b_col = b[:, h:h + 1]
            li_col = li[:, h:h + 1]
            m_h = m_vec[:, h:h + 1]
            c_h = c_ref[bi, h]
            n_h = n_ref[bi, h]
            inter = b_col + m_h
            dmat = jnp.where(causal, b_col + r_t[h:h + 1, :], NEG)
            mt = jnp.maximum(inter, jnp.max(dmat, axis=-1, keepdims=True))
            w_inter = jnp.exp(inter - mt)
            s = _dot_nt(q, k) * jnp.exp(dmat - mt)
            num = w_inter * _dot_nt(q, c_h) + _dot(s, v)
            den = (w_inter * jnp.sum(q.astype(F32) * n_h, axis=-1, keepdims=True)
                   + jnp.sum(s, axis=-1, keepdims=True))
            hh = num / jnp.maximum(jnp.abs(den), jnp.exp(-mt))
            m_new = mt[t - 1:t, :]
            w_c = jnp.exp(inter[t - 1:t, :] - m_new)
            w_k = jnp.exp(b_col[t - 1:t, :] - b_col + li_col - m_new)
            kf = k.astype(F32)
            c_ref[bi, h] = w_c * c_h + _dot_tn(v.astype(F32) * w_k, k)
            n_ref[bi, h] = w_c * n_h + jnp.sum(kf * w_k, axis=0, keepdims=True)
            m_next = jnp.where(lane == h, m_new, m_next)
            hh = _sigmoid(og_ref[bi, :, sl]) * hh
            hh = hh * lax.rsqrt(jnp.mean(hh * hh, axis=-1, keepdims=True) + EPS)
            hm_ref[bi, :, sl] = (hh * ng[:, sl] * _silu(z_ref[bi, :, sl])).astype(hm_ref.dtype)
        m_ref[bi] = m_next

    _pair_loop(nb, one_row)


def _mlstm(qkv3, rest3, gates3, row0, nseq, tiles, nc, l_valid, gbias, ng, c0, n0, m0):
    nb, t = tiles
    assert nseq % nb == 0 and row0 % nb == 0
    wd = M_WIDTH
    r0 = row0 // nb
    seq = lambda col: pl.BlockSpec((nb, t, wd), lambda b, c: (r0 + b, c, col))
    full2 = lambda a: pl.BlockSpec(a.shape, lambda b, c: (0, 0))
    st_c = pl.BlockSpec((nb, N_HEADS, M_DV, M_DK), lambda b, c: (b, 0, 0, 0))
    st_n = pl.BlockSpec((nb, N_HEADS, 1, M_DK), lambda b, c: (b, 0, 0, 0))
    st_m = pl.BlockSpec((nb, 1, LANES), lambda b, c: (b, 0, 0))
    kern = functools.partial(_mlstm_kernel, nb=nb, t=t, l_valid=l_valid)
    return pl.pallas_call(
        kern,
        grid=(nseq // nb, nc),
        in_specs=[
            seq(0), seq(1), seq(2), seq(0), seq(1),
            pl.BlockSpec((nb, t, 2 * LANES), lambda b, c: (r0 + b, c, 0)),
            full2(gbias), full2(ng), st_c, st_n, st_m,
        ],
        out_specs=[pl.BlockSpec((nb, t, wd), lambda b, c: (b, c, 0)), st_c, st_n, st_m],
        out_shape=[
            jax.ShapeDtypeStruct((nseq, t * nc, wd), BF16),
            jax.ShapeDtypeStruct((nseq, N_HEADS, M_DV, M_DK), F32),
            jax.ShapeDtypeStruct((nseq, N_HEADS, 1, M_DK), F32),
            jax.ShapeDtypeStruct((nseq, 1, LANES), F32),
        ],
        compiler_params=_cparams(("parallel", "arbitrary")),
        name="mlstm",
    )(qkv3, qkv3, qkv3, rest3, rest3, gates3, gbias, ng, c0, n0, m0)


def _rglru_kernel(x_ref, gr_ref, cw_ref, cb_ref, wa_ref, ba_ref, wx_ref, bx_ref, lam_ref, h0_ref, conv0_ref,
                  hl_ref, hlast_ref, h_s, prev_s, *, nb, t, l_valid):
    c = pl.program_id(1)
    w = LRU_WIDTH
    ng = t // SUBLANES

    @pl.when(c == 0)
    def _():
        h_s[...] = h0_ref[...]
        prev_s[...] = conv0_ref[...]

    x4 = x_ref[...].reshape(nb, ng, SUBLANES, w)
    prev8 = prev_s[...]
    if ng > 1:
        prev4 = jnp.concatenate([prev8[:, None], x4[:, :ng - 1]], axis=1)
    else:
        prev4 = prev8[:, None]
    prev_s[...] = x4[:, ng - 1]
    x3 = x4.reshape(nb * ng, SUBLANES, w)
    prev3 = prev4.reshape(nb * ng, SUBLANES, w)
    rid = lax.broadcasted_iota(jnp.int32, (1, SUBLANES, 1), 1)

    cw = cw_ref[...]
    xc = cb_ref[...] + cw[CONV_W - 1:CONV_W, :] * x3
    for s in range(1, CONV_W):
        delayed = jnp.where(rid >= s, pltpu.roll(x3, s, axis=1), pltpu.roll(prev3, s, axis=1))
        xc = xc + cw[CONV_W - 1 - s:CONV_W - s, :] * delayed

    xf = xc.reshape(nb * t, w)
    ra, ri = [], []
    for n in range(LRU_BLOCKS):
        xb = xf[:, n * LRU_BS:(n + 1) * LRU_BS].astype(BF16)
        ra.append(jnp.dot(xb, wa_ref[n], preferred_element_type=F32))
        ri.append(jnp.dot(xb, wx_ref[n], preferred_element_type=F32))
    r = _sigmoid(jnp.concatenate(ra, axis=-1) + ba_ref[...])
    i = _sigmoid(jnp.concatenate(ri, axis=-1) + bx_ref[...])
    log_a = -LRU_C * r * _softplus(-lam_ref[...])
    a = jnp.exp(log_a)
    u = jnp.sqrt(-jnp.tanh(log_a) * (1.0 + a * a)) * (i * xf)

    a = a.reshape(nb * ng, SUBLANES, w)
    u = u.reshape(nb * ng, SUBLANES, w)
    for s in (1, 2, 4):
        a_sh = jnp.where(rid >= s, pltpu.roll(a, s, axis=1), 1.0)
        u_sh = jnp.where(rid >= s, pltpu.roll(u, s, axis=1), 0.0)
        u = a * u_sh + u
        a = a * a_sh
    a = a.reshape(nb, ng, SUBLANES, w)
    u = u.reshape(nb, ng, SUBLANES, w)
    h = h_s[...]
    g_last, r_last = divmod((l_valid - 1) % t, SUBLANES)
    gr4 = gr_ref[...].reshape(nb, ng, SUBLANES, w)
    for g in range(ng):
        hg = a[:, g] * h + u[:, g]
        h = hg[:, SUBLANES - 1:SUBLANES]
        hl_ref[:, g * SUBLANES:(g + 1) * SUBLANES, :] = (hg * _silu(gr4[:, g])).astype(hl_ref.dtype)
        if g == g_last:
            @pl.when(c == (l_valid - 1) // t)
            def _():
                hlast_ref[...] = hg[:, r_last:r_last + 1]
    h_s[...] = h


def _rglru(rest3, row0, nseq, tiles, nc, l_valid, cw, cb, wa, ba, wx, bx, lam, h0, conv0):
    nb, t = tiles
    assert nseq % nb == 0 and row0 % nb == 0
    w = LRU_WIDTH
    r0 = row0 // nb
    full2 = lambda a: pl.BlockSpec(a.shape, lambda b, c: (0, 0))
    full3 = lambda a: pl.BlockSpec(a.shape, lambda b, c: (0, 0, 0))
    kern = functools.partial(_rglru_kernel, nb=nb, t=t, l_valid=l_valid)
    return pl.pallas_call(
        kern,
        grid=(nseq // nb, nc),
        in_specs=[
            pl.BlockSpec((nb, t, w), lambda b, c: (r0 + b, c, 2)),
            pl.BlockSpec((nb, t, w), lambda b, c: (r0 + b, c, 3)),
            full2(cw), full2(cb), full3(wa), full2(ba), full3(wx), full2(bx), full2(lam),
            pl.BlockSpec((nb, 1, w), lambda b, c: (b, 0, 0)),
            pl.BlockSpec((nb, SUBLANES, w), lambda b, c: (b, 0, 0)),
        ],
        out_specs=[
            pl.BlockSpec((nb, t, w), lambda b, c: (b, c, 0)),
            pl.BlockSpec((nb, 1, w), lambda b, c: (b, 0, 0)),
        ],
        out_shape=[
            jax.ShapeDtypeStruct((nseq, t * nc, w), BF16),
            jax.ShapeDtypeStruct((nseq, 1, w), F32),
        ],
        scratch_shapes=[pltpu.VMEM((nb, 1, w), F32), pltpu.VMEM((nb, SUBLANES, w), F32)],
        compiler_params=_cparams(("parallel", "arbitrary")),
        name="rglru",
    )(rest3, rest3, cw, cb, wa, ba, wx, bx, lam, h0, conv0)


def _gla_kernel(q_ref, k_ref, v_ref, r_ref, al_ref, aup_ref, ab_ref, ng_ref, s0_ref,
                o_ref, s_ref, *, nb, t, l_valid):
    c = pl.program_id(1)

    @pl.when(c == 0)
    def _():
        s_ref[...] = s0_ref[...]

    pos = c * t + lax.broadcasted_iota(jnp.int32, (t, 1), 0)
    valid = pos < l_valid
    row = lax.broadcasted_iota(jnp.int32, (t, t), 0)
    col = lax.broadcasted_iota(jnp.int32, (t, t), 1)
    causal = row >= col
    aup = aup_ref[...]
    ab = ab_ref[...]
    ng = ng_ref[...]

    def one_row(bi):
        pre = _dot(al_ref[bi], aup) + ab
        lg = jnp.where(valid, _log_sigmoid(pre) / G_TAU, 0.0)
        b = _cumsum_rows(lg)
        b_last = b[t - 1:t, :]
        eb = jnp.exp(b)
        enb = jnp.exp(-b)
        erb = jnp.exp(b_last - b)
        ebl_col = _row_to_col(jnp.exp(b_last))
        for h in range(N_HEADS):
            ks = slice(h * G_DK, (h + 1) * G_DK)
            vs = slice(h * G_DV, (h + 1) * G_DV)
            k = jnp.where(valid, k_ref[bi, :, ks].astype(F32), 0.0)
            v = v_ref[bi, :, vs]
            qd = (q_ref[bi, :, ks].astype(F32) * (G_DK ** -0.5)) * eb[:, ks]
            a = jnp.where(causal, _dot_nt(qd, k * enb[:, ks]), 0.0)
            s_h = s_ref[bi, h]
            o = _dot(qd, s_h) + _dot(a, v)
            s_ref[bi, h] = ebl_col[ks, :] * s_h + _dot_tn(k * erb[:, ks], v)
            o = o * lax.rsqrt(jnp.mean(o * o, axis=-1, keepdims=True) + EPS)
            o_ref[bi, :, vs] = (o * ng[:, vs] * _silu(r_ref[bi, :, vs])).astype(o_ref.dtype)

    _pair_loop(nb, one_row)


def _gla(qkv3, r3, al3, row0, nseq, tiles, nc, l_valid, aup, ab, ng, s0):
    nb, t = tiles
    assert nseq % nb == 0 and row0 % nb == 0
    r0 = row0 // nb
    full2 = lambda a: pl.BlockSpec(a.shape, lambda b, c: (0, 0))
    st = pl.BlockSpec((nb, N_HEADS, G_DK, G_DV), lambda b, c: (b, 0, 0, 0))
    kern = functools.partial(_gla_kernel, nb=nb, t=t, l_valid=l_valid)
    return pl.pallas_call(
        kern,
        grid=(nseq // nb, nc),
        in_specs=[
            pl.BlockSpec((nb, t, G_KW), lambda b, c: (r0 + b, c, 0)),
            pl.BlockSpec((nb, t, G_KW), lambda b, c: (r0 + b, c, 1)),
            pl.BlockSpec((nb, t, G_VW), lambda b, c: (r0 + b, c, 1)),
            pl.BlockSpec((nb, t, G_VW), lambda b, c: (r0 + b, c, 0)),
            pl.BlockSpec((nb, t, LANES), lambda b, c: (r0 + b, c, 0)),
            full2(aup), full2(ab), full2(ng), st,
        ],
        out_specs=[pl.BlockSpec((nb, t, G_VW), lambda b, c: (b, c, 0)), st],
        out_shape=[
            jax.ShapeDtypeStruct((nseq, t * nc, G_VW), BF16),
            jax.ShapeDtypeStruct((nseq, N_HEADS, G_DK, G_DV), F32),
        ],
        compiler_params=_cparams(("parallel", "arbitrary")),
        name="gla",
    )(qkv3, qkv3, qkv3, r3, al3, aup, ab, ng, s0)


def _pad_lanes(a, width):
    return jnp.pad(a, [(0, 0)] * (a.ndim - 1) + [(0, width - a.shape[-1])])


def _even_layer(xm, xs, nbp, seq, nbs, ls, st, pre_g, post_g, w_in, w_out, b_i, b_f, m_norm_g,
                conv_w, conv_b, wa, ba, wx, bx, lam):
    c0s, n0s, m0s, h0s, conv0s = st
    wq, wk, wv, wog, wig, wfg, wz, wxr, wgr = jnp.split(
        w_in, [1024, 2048, 3072, 4096, 4100, 4104, 5128, 6152], axis=1)
    w_main = jnp.concatenate([wq, wk, wv, wog, wz, wxr, wgr], axis=1).astype(BF16)
    w_gate = jnp.concatenate([_pad_lanes(wig, LANES), _pad_lanes(wfg, LANES)], axis=1).astype(BF16)
    gbias = jnp.concatenate([_pad_lanes(b_i[None], LANES), _pad_lanes(b_f[None], LANES)], axis=1)
    ng = m_norm_g[None]
    cw, cb = conv_w, conv_b[None]
    wab, wxb = wa.astype(BF16), wx.astype(BF16)
    bav, bxv, lamv = ba[None], bx[None], lam[None]
    w_o1, w_o2 = w_out[:M_WIDTH].astype(BF16), w_out[M_WIDTH:].astype(BF16)
    e_lo = 3 * M_WIDTH
    e_hi = w_main.shape[1] - e_lo
    lru = (cw, cb, wab, bav, wxb, bxv, lamv)

    qm, pm, gm = _norm_proj(xm, pre_g, w_main, w_gate, e_lo, ROW_TILE_MAIN)
    qs, ps, gs = _norm_proj(xs, pre_g, w_main, w_gate, e_lo, xs.shape[0])
    n_meta_rows = nbp * N_META
    xr_cols = slice(2 * LRU_WIDTH, 3 * LRU_WIDTH)

    zc = jnp.zeros((nbp, N_HEADS, M_DV, M_DK), F32)
    zn = jnp.zeros((nbp, N_HEADS, 1, M_DK), F32)
    zm = jnp.zeros((nbp, 1, LANES), F32)
    zh = jnp.zeros((nbp, 1, LRU_WIDTH), F32)
    zconv = jnp.zeros((nbp, SUBLANES, LRU_WIDTH), F32)
    qs_meta = qs.reshape(-1, N_META, e_lo)
    ps_meta = ps.reshape(-1, N_META, e_hi)
    gs_meta = gs.reshape(-1, N_META, 2 * LANES)
    hm_meta, c1, n1, m1 = _mlstm(qs_meta, ps_meta, gs_meta, 0, nbp, MLSTM_TILES["meta"], 1, N_META,
                                 gbias, ng, zc, zn, zm)
    hl_meta, h1 = _rglru(ps_meta, 0, nbp, LRU_TILES["meta"], 1, N_META, *lru, zh, zconv)
    conv1 = ps_meta[:nbp, N_META - SUBLANES:, xr_cols]

    qm3 = qm.reshape(nbp, seq, e_lo)
    pm3 = pm.reshape(nbp, seq, e_hi)
    gm3 = gm.reshape(nbp, seq, 2 * LANES)
    tm_ = MLSTM_TILES["main"][1]
    hm_main, pc, pn, pmm = _mlstm(qm3, pm3, gm3, 0, nbp, MLSTM_TILES["main"], seq // tm_, seq,
                                  gbias, ng, c1, n1, m1)
    tl = LRU_TILES["main"][1]
    hl_main, ph = _rglru(pm3, 0, nbp, LRU_TILES["main"], seq // tl, seq, *lru, h1, conv1)
    pconv = pm3[:, seq - (CONV_W - 1):, xr_cols]

    qs_s = qs.reshape(-1, SAMPLE_PAD, e_lo)
    ps_s = ps.reshape(-1, SAMPLE_PAD, e_hi)
    gs_s = gs.reshape(-1, SAMPLE_PAD, 2 * LANES)
    r0 = n_meta_rows // SAMPLE_PAD
    hm_s, sc, sn, sm = _mlstm(qs_s, ps_s, gs_s, r0, nbs, MLSTM_TILES["sample"], 1, ls, gbias, ng,
                              c0s, n0s[:, :, None, :], _pad_lanes(m0s, LANES)[:, None, :])
    conv0p = jnp.pad(conv0s, ((0, 0), (SUBLANES - (CONV_W - 1), 0), (0, 0)))
    hl_s, sh = _rglru(ps_s, r0, nbs, LRU_TILES["sample"], 1, ls, *lru, h0s[:, None, :], conv0p)
    xr_s = ps_s[r0:, :ls, xr_cols]
    sconv = jnp.concatenate([conv0s, xr_s], axis=1)[:, -(CONV_W - 1):]

    ym = _out_proj(xm, post_g, [hm_main.reshape(-1, M_WIDTH), hl_main.reshape(-1, LRU_WIDTH)], [w_o1, w_o2],
                   ROW_TILE_MAIN)
    hm_small = jnp.concatenate([hm_meta.reshape(-1, M_WIDTH), hm_s.reshape(-1, M_WIDTH)], axis=0)
    hl_small = jnp.concatenate([hl_meta.reshape(-1, LRU_WIDTH), hl_s.reshape(-1, LRU_WIDTH)], axis=0)
    ys = _out_proj(xs, post_g, [hm_small, hl_small], [w_o1, w_o2], xs.shape[0])

    p_state = (pc, pn[:, :, 0, :], pmm[:, 0, :N_HEADS], ph[:, 0, :], pconv)
    s_state = (sc, sn[:, :, 0, :], sm[:, 0, :N_HEADS], sh[:, 0, :], sconv)
    return ym, ys, p_state, s_state


def _odd_layer(xm, xs, nbp, seq, nbs, ls, s0s, pre_g, post_g, w_in, w_out, a_up, a_b, g_norm_g):
    e_lo = 2 * G_KW + G_VW
    w_main = w_in[:, :2 * G_KW + 2 * G_VW].astype(BF16)
    w_low = _pad_lanes(w_in[:, 2 * G_KW + 2 * G_VW:], LANES).astype(BF16)
    aup = jnp.pad(a_up, ((0, LANES - G_RANK), (0, 0))).astype(BF16)
    ab = a_b[None]
    ng = g_norm_g[None]
    w_o = w_out.astype(BF16)

    qm, rm, am = _norm_proj(xm, pre_g, w_main, w_low, e_lo, ROW_TILE_MAIN)
    qs, rs, as_ = _norm_proj(xs, pre_g, w_main, w_low, e_lo, xs.shape[0])
    n_meta_rows = nbp * N_META

    zs = jnp.zeros((nbp, N_HEADS, G_DK, G_DV), F32)
    o_meta, s1 = _gla(qs.reshape(-1, N_META, e_lo), rs.reshape(-1, N_META, G_VW), as_.reshape(-1, N_META, LANES),
                      0, nbp, GLA_TILES["meta"], 1, N_META, aup, ab, ng, zs)
    tg = GLA_TILES["main"][1]
    o_main, p_s = _gla(qm.reshape(nbp, seq, e_lo), rm.reshape(nbp, seq, G_VW), am.reshape(nbp, seq, LANES),
                       0, nbp, GLA_TILES["main"], seq // tg, seq, aup, ab, ng, s1)
    r0 = n_meta_rows // SAMPLE_PAD
    o_s, s_s = _gla(qs.reshape(-1, SAMPLE_PAD, e_lo), rs.reshape(-1, SAMPLE_PAD, G_VW),
                    as_.reshape(-1, SAMPLE_PAD, LANES), r0, nbs, GLA_TILES["sample"], 1, ls, aup, ab, ng, s0s)

    ym = _out_proj(xm, post_g, [o_main.reshape(-1, G_VW)], [w_o], ROW_TILE_MAIN)
    o_small = jnp.concatenate([o_meta.reshape(-1, G_VW), o_s.reshape(-1, G_VW)], axis=0)
    ys = _out_proj(xs, post_g, [o_small], [w_o], xs.shape[0])
    return ym, ys, p_s, s_s


def kernel(x_prompt, x_sample, state_mlstm_C, state_mlstm_n, state_mlstm_m, state_rglru_h, state_rglru_conv,
           state_gla_S, meta_tokens, pre_norm_a, post_norm_a, w_in_a, w_out_a, mlstm_b_i, mlstm_b_f, mlstm_norm,
           conv_w, conv_b, lru_w_a, lru_b_a, lru_w_x, lru_b_x, lru_lambda, pre_norm_c, post_norm_c, w_in_c,
           w_out_c, gla_alpha_up, gla_alpha_b, gla_norm):
    nbp, seq, d = x_prompt.shape
    nbs, ls, _ = x_sample.shape
    depth = pre_norm_a.shape[0] + pre_norm_c.shape[0]
    assert ls >= CONV_W - 1 and ls <= SAMPLE_PAD and N_META % SUBLANES == 0

    xm = x_prompt.reshape(nbp * seq, d)
    meta = jnp.broadcast_to(meta_tokens[None].astype(x_prompt.dtype), (nbp, N_META, d)).reshape(nbp * N_META, d)
    xs_pad = jnp.pad(x_sample, ((0, 0), (0, SAMPLE_PAD - ls), (0, 0))).reshape(nbs * SAMPLE_PAD, d)
    xs = jnp.concatenate([meta, xs_pad], axis=0)

    p_lists = [[] for _ in range(6)]
    s_lists = [[] for _ in range(6)]
    for layer in range(depth):
        j = layer // 2
        if layer % 2 == 0:
            st = (state_mlstm_C[j], state_mlstm_n[j], state_mlstm_m[j], state_rglru_h[j], state_rglru_conv[j])
            xm, xs, pst, sst = _even_layer(
                xm, xs, nbp, seq, nbs, ls, st, pre_norm_a[j], post_norm_a[j], w_in_a[j], w_out_a[j],
                mlstm_b_i[j], mlstm_b_f[j], mlstm_norm[j], conv_w[j], conv_b[j], lru_w_a[j], lru_b_a[j],
                lru_w_x[j], lru_b_x[j], lru_lambda[j])
            for i in range(5):
                p_lists[i].append(pst[i])
                s_lists[i].append(sst[i])
        else:
            xm, xs, p_s, s_s = _odd_layer(
                xm, xs, nbp, seq, nbs, ls, state_gla_S[j], pre_norm_c[j], post_norm_c[j], w_in_c[j], w_out_c[j],
                gla_alpha_up[j], gla_alpha_b[j], gla_norm[j])
            p_lists[5].append(p_s)
            s_lists[5].append(s_s)

    y_prompt = xm.reshape(nbp, seq, d)
    y_sample = xs[nbp * N_META:].reshape(nbs, SAMPLE_PAD, d)[:, :ls]
    return (y_prompt, y_sample) + tuple(jnp.stack(l) for l in p_lists) + tuple(jnp.stack(l) for l in s_lists)
```

```python
import functools

import jax
import jax.numpy as jnp
from jax import lax
from jax.experimental import pallas as pl
from jax.experimental.pallas import tpu as pltpu

F32 = jnp.float32
BF16 = jnp.bfloat16

D_MODEL = 1024
N_META = 16
EPS = 1e-6
N_HEADS = 4
M_DK = 256
M_DV = 256
M_WIDTH = N_HEADS * M_DV
LRU_WIDTH = 1024
LRU_BLOCKS = 8
LRU_BS = LRU_WIDTH // LRU_BLOCKS
CONV_W = 4
LRU_C = 8.0
G_DK = 256
G_DV = 512
G_KW = N_HEADS * G_DK
G_VW = N_HEADS * G_DV
G_RANK = 16
G_TAU = 16.0

LANES = 128
SUBLANES = 8
NEG = -1e30
SAMPLE_PAD = SUBLANES
VMEM_LIMIT = 56 * 1024 * 1024
COL_TILE = 1024

MLSTM_TILES = {"meta": (8, N_META), "main": (4, 128), "sample": (4, SAMPLE_PAD)}
GLA_TILES = {"meta": (8, N_META), "main": (2, 64), "sample": (4, SAMPLE_PAD)}
LRU_TILES = {"meta": (8, N_META), "main": (1, 256), "sample": (16, SAMPLE_PAD)}
PROJ_ROW_TILE = 1024
OUT_ROW_TILE = 512


def _cparams(sem):
    return pltpu.CompilerParams(dimension_semantics=sem, vmem_limit_bytes=VMEM_LIMIT)


def _sigmoid(x):
    return 1.0 / (1.0 + jnp.exp(-x))


def _silu(x):
    return x * _sigmoid(x)


def _log_sigmoid(x):
    return jnp.minimum(x, 0.0) - jnp.log1p(jnp.exp(-jnp.abs(x)))


def _softplus(x):
    return jnp.maximum(x, 0.0) + jnp.log1p(jnp.exp(-jnp.abs(x)))


def _dot(a, b):
    return jnp.dot(a.astype(BF16), b.astype(BF16), preferred_element_type=F32)


def _dot_nt(a, b):
    return lax.dot_general(a.astype(BF16), b.astype(BF16), (((1,), (1,)), ((), ())),
                           preferred_element_type=F32)


def _dot_tn(a, b):
    return lax.dot_general(a.astype(BF16), b.astype(BF16), (((0,), (0,)), ((), ())),
                           preferred_element_type=F32)


def _transpose_rows(x):
    t, n = x.shape
    if t < LANES:
        x = jnp.concatenate([x, jnp.zeros((LANES - t, n), x.dtype)], axis=0)
    return jnp.transpose(x)


def _row_to_col(r):
    return jnp.transpose(jnp.broadcast_to(r, (LANES, r.shape[1])))[:, 0:1]


def _cumsum_rows(x):
    t = x.shape[0]
    if t <= 2 * SUBLANES:
        rid = lax.broadcasted_iota(jnp.int32, (t, 1), 0)
        s = 1
        while s < t:
            x = x + jnp.where(rid >= s, pltpu.roll(x, s, axis=0), 0.0)
            s *= 2
        return x
    row = lax.broadcasted_iota(jnp.int32, (t, t), 0)
    col = lax.broadcasted_iota(jnp.int32, (t, t), 1)
    tri = jnp.where(row >= col, 1.0, 0.0).astype(BF16)
    hi = x.astype(BF16)
    r1 = x - hi.astype(F32)
    mid = r1.astype(BF16)
    lo = (r1 - mid.astype(F32)).astype(BF16)
    acc = jnp.dot(tri, lo, preferred_element_type=F32)
    acc = acc + jnp.dot(tri, mid, preferred_element_type=F32)
    return acc + jnp.dot(tri, hi, preferred_element_type=F32)


def _pair_loop(nb, row_fn):
    if nb == 1:
        row_fn(0)
        return

    def body(i, carry):
        row_fn(2 * i)
        row_fn(2 * i + 1)
        return carry

    lax.fori_loop(0, nb // 2, body, 0)


def _norm_proj_kernel(n_lo, x_ref, g_ref, wm_ref, ws_ref, olo_ref, ohi_ref, os_ref, hn_ref):
    j = pl.program_id(1)

    @pl.when(j == 0)
    def _():
        x = x_ref[...]
        y = x * lax.rsqrt(jnp.mean(x * x, axis=-1, keepdims=True) + EPS) * g_ref[...]
        hn = y.astype(BF16)
        hn_ref[...] = hn
        os_ref[...] = jnp.dot(hn, ws_ref[...], preferred_element_type=F32)

    acc = jnp.dot(hn_ref[...], wm_ref[...], preferred_element_type=F32)

    @pl.when(j < n_lo)
    def _():
        olo_ref[...] = acc.astype(olo_ref.dtype)

    @pl.when(j >= n_lo)
    def _():
        ohi_ref[...] = acc


def _norm_proj(x2d, g, w_main, w_small, e_lo, tm):
    n, d = x2d.shape
    e = w_main.shape[1]
    es = w_small.shape[1]
    tn = COL_TILE
    n_lo = e_lo // tn
    return pl.pallas_call(
        functools.partial(_norm_proj_kernel, n_lo),
        grid=(n // tm, e // tn),
        in_specs=[
            pl.BlockSpec((tm, d), lambda i, j: (i, 0)),
            pl.BlockSpec((1, d), lambda i, j: (0, 0)),
            pl.BlockSpec((d, tn), lambda i, j: (0, j)),
            pl.BlockSpec((d, es), lambda i, j: (0, 0)),
        ],
        out_specs=[
            pl.BlockSpec((tm, tn), lambda i, j: (i, jnp.minimum(j, n_lo - 1))),
            pl.BlockSpec((tm, tn), lambda i, j: (i, jnp.maximum(j - n_lo, 0))),
            pl.BlockSpec((tm, es), lambda i, j: (i, 0)),
        ],
        out_shape=[
            jax.ShapeDtypeStruct((n, e_lo), BF16),
            jax.ShapeDtypeStruct((n, e - e_lo), F32),
            jax.ShapeDtypeStruct((n, es), F32),
        ],
        scratch_shapes=[pltpu.VMEM((tm, d), BF16)],
        compiler_params=_cparams(("parallel", "arbitrary")),
        name="norm_proj",
    )(x2d, g.reshape(1, d), w_main, w_small)


def _out_proj_kernel(n_in, *refs):
    x_ref, g_ref = refs[0], refs[1]
    a_refs = refs[2:2 + n_in]
    w_refs = refs[2 + n_in:2 + 2 * n_in]
    y_ref = refs[2 + 2 * n_in]
    out = jnp.dot(a_refs[0][...], w_refs[0][...], preferred_element_type=F32)
    for a_ref, w_ref in zip(a_refs[1:], w_refs[1:]):
        out = out + jnp.dot(a_ref[...], w_ref[...], preferred_element_type=F32)
    nrm = out * lax.rsqrt(jnp.mean(out * out, axis=-1, keepdims=True) + EPS) * g_ref[...]
    y_ref[...] = x_ref[...] + nrm


def _out_proj(x2d, g, acts, weights, tm):
    n, d = x2d.shape
    n_in = len(acts)
    in_specs = [pl.BlockSpec((tm, d), lambda i: (i, 0)), pl.BlockSpec((1, d), lambda i: (0, 0))]
    in_specs += [pl.BlockSpec((tm, a.shape[1]), lambda i: (i, 0)) for a in acts]
    in_specs += [pl.BlockSpec(w.shape, lambda i: (0, 0)) for w in weights]
    return pl.pallas_call(
        functools.partial(_out_proj_kernel, n_in),
        grid=(n // tm,),
        in_specs=in_specs,
        out_specs=pl.BlockSpec((tm, d), lambda i: (i, 0)),
        out_shape=jax.ShapeDtypeStruct((n, d), F32),
        compiler_params=_cparams(("parallel",)),
        name="out_proj",
    )(x2d, g.reshape(1, d), *acts, *weights)


def _mlstm_kernel(q_ref, k_ref, v_ref, og_ref, z_ref, gate_ref, gbias_ref, ng_ref, c0_ref, n0_ref, m0_ref,
                  hm_ref, c_ref, n_ref, m_ref, *, nb, t, nc, l_valid):
    c = pl.program_id(1)
    if nc == 1:
        c_in, n_in, m_in = c0_ref, n0_ref, m0_ref
    else:
        c_in, n_in, m_in = c_ref, n_ref, m_ref

        @pl.when(c == 0)
        def _():
            c_ref[...] = c0_ref[...]
            n_ref[...] = n0_ref[...]
            m_ref[...] = m0_ref[...]

    pos = c * t + lax.broadcasted_iota(jnp.int32, (t, 1), 0)
    valid = pos < l_valid
    row = lax.broadcasted_iota(jnp.int32, (t, t), 0)
    col = lax.broadcasted_iota(jnp.int32, (t, t), 1)
    causal = row >= col
    lane = lax.broadcasted_iota(jnp.int32, (1, LANES), 1)
    gbias = gbias_ref[...]
    ng = ng_ref[...]

    def one_row(bi):
        g = gate_ref[bi] + gbias
        li = jnp.where(valid, g[:, :LANES], NEG)
        lf = jnp.where(valid, _log_sigmoid(g[:, LANES:]), 0.0)
        b = _cumsum_rows(lf)
        r_t = _transpose_rows(li - b)[:, :t]
        m_vec = m_in[bi]
        m_next = m_vec
        for h in range(N_HEADS):
            sl = slice(h * M_DK, (h + 1) * M_DK)
            q = q_ref[bi, :, sl]
            k = k_ref[bi, :, sl] * (M_DK ** -0.5)
            v = v_ref[bi, :, sl]
            b_col = b[:, h:h + 1]
            li_col = li[:, h:h + 1]
            m_h = m_vec[:, h:h + 1]
            c_h = c_in[bi, h]
            n_h = n_in[bi, h]
            inter = b_col + m_h
            dmat = jnp.where(causal, b_col + r_t[h:h + 1, :], NEG)
            mt = jnp.maximum(inter, jnp.max(dmat, axis=-1, keepdims=True))
            w_inter = jnp.exp(inter - mt)
            s = _dot_nt(q, k) * jnp.exp(dmat - mt)
            num = w_inter * _dot_nt(q, c_h) + _dot(s, v)
            den = (w_inter * jnp.sum(q.astype(F32) * n_h, axis=-1, keepdims=True)
                   + jnp.sum(s, axis=-1, keepdims=True))
            hh = num / jnp.maximum(jnp.abs(den), jnp.exp(-mt))
            m_new = mt[t - 1:t, :]
            w_c = jnp.exp(inter[t - 1:t, :] - m_new)
            w_k = jnp.exp(b_col[t - 1:t, :] - b_col + li_col - m_new)
            kf = k.astype(F32)
            c_ref[bi, h] = w_c * c_h + _dot_tn(v.astype(F32) * w_k, k)
            n_ref[bi, h] = w_c * n_h + jnp.sum(kf * w_k, axis=0, keepdims=True)
            m_next = jnp.where(lane == h, m_new, m_next)
            hh = _sigmoid(og_ref[bi, :, sl]) * hh
            hh = hh * lax.rsqrt(jnp.mean(hh * hh, axis=-1, keepdims=True) + EPS)
            hm_ref[bi, :, sl] = (hh * ng[:, sl] * _silu(z_ref[bi, :, sl])).astype(hm_ref.dtype)
        m_ref[bi] = m_next

    _pair_loop(nb, one_row)


def _mlstm(qkv3, rest3, gates3, row0, nseq, tiles, nc, l_valid, gbias, ng, c0, n0, m0):
    nb, t = tiles
    assert nseq % nb == 0 and row0 % nb == 0
    wd = M_WIDTH
    r0 = row0 // nb
    seq = lambda col: pl.BlockSpec((nb, t, wd), lambda b, c: (r0 + b, c, col))
    full2 = lambda a: pl.BlockSpec(a.shape, lambda b, c: (0, 0))
    st_c = pl.BlockSpec((nb, N_HEADS, M_DV, M_DK), lambda b, c: (b, 0, 0, 0))
    st_n = pl.BlockSpec((nb, N_HEADS, 1, M_DK), lambda b, c: (b, 0, 0, 0))
    st_m = pl.BlockSpec((nb, 1, LANES), lambda b, c: (b, 0, 0))
    kern = functools.partial(_mlstm_kernel, nb=nb, t=t, nc=nc, l_valid=l_valid)
    return pl.pallas_call(
        kern,
        grid=(nseq // nb, nc),
        in_specs=[
            seq(0), seq(1), seq(2), seq(0), seq(1),
            pl.BlockSpec((nb, t, 2 * LANES), lambda b, c: (r0 + b, c, 0)),
            full2(gbias), full2(ng), st_c, st_n, st_m,
        ],
        out_specs=[pl.BlockSpec((nb, t, wd), lambda b, c: (b, c, 0)), st_c, st_n, st_m],
        out_shape=[
            jax.ShapeDtypeStruct((nseq, t * nc, wd), BF16),
            jax.ShapeDtypeStruct((nseq, N_HEADS, M_DV, M_DK), F32),
            jax.ShapeDtypeStruct((nseq, N_HEADS, 1, M_DK), F32),
            jax.ShapeDtypeStruct((nseq, 1, LANES), F32),
        ],
        compiler_params=_cparams(("parallel", "arbitrary")),
        name="mlstm",
    )(qkv3, qkv3, qkv3, rest3, rest3, gates3, gbias, ng, c0, n0, m0)


def _rglru_kernel(x_ref, gr_ref, cw_ref, cb_ref, wa_ref, ba_ref, wx_ref, bx_ref, lam_ref, h0_ref, conv0_ref,
                  hl_ref, hlast_ref, h_s, prev_s, *, nb, t, l_valid):
    c = pl.program_id(1)
    w = LRU_WIDTH
    ng = t // SUBLANES

    @pl.when(c == 0)
    def _():
        h_s[...] = h0_ref[...]
        prev_s[...] = conv0_ref[...]

    x4 = x_ref[...].reshape(nb, ng, SUBLANES, w)
    prev8 = prev_s[...]
    if ng > 1:
        prev4 = jnp.concatenate([prev8[:, None], x4[:, :ng - 1]], axis=1)
    else:
        prev4 = prev8[:, None]
    prev_s[...] = x4[:, ng - 1]
    x3 = x4.reshape(nb * ng, SUBLANES, w)
    prev3 = prev4.reshape(nb * ng, SUBLANES, w)
    rid = lax.broadcasted_iota(jnp.int32, (1, SUBLANES, 1), 1)

    cw = cw_ref[...]
    xc = cb_ref[...] + cw[CONV_W - 1:CONV_W, :] * x3
    for s in range(1, CONV_W):
        delayed = jnp.where(rid >= s, pltpu.roll(x3, s, axis=1), pltpu.roll(prev3, s, axis=1))
        xc = xc + cw[CONV_W - 1 - s:CONV_W - s, :] * delayed

    xf = xc.reshape(nb * t, w)
    ra, ri = [], []
    for n in range(LRU_BLOCKS):
        xb = xf[:, n * LRU_BS:(n + 1) * LRU_BS].astype(BF16)
        ra.append(jnp.dot(xb, wa_ref[n], preferred_element_type=F32))
        ri.append(jnp.dot(xb, wx_ref[n], preferred_element_type=F32))
    r = _sigmoid(jnp.concatenate(ra, axis=-1) + ba_ref[...])
    i = _sigmoid(jnp.concatenate(ri, axis=-1) + bx_ref[...])
    log_a = -LRU_C * r * _softplus(-lam_ref[...])
    a = jnp.exp(log_a)
    u = jnp.sqrt(-jnp.tanh(log_a) * (1.0 + a * a)) * (i * xf)

    a = a.reshape(nb * ng, SUBLANES, w)
    u = u.reshape(nb * ng, SUBLANES, w)
    for s in (1, 2, 4):
        a_sh = jnp.where(rid >= s, pltpu.roll(a, s, axis=1), 1.0)
        u_sh = jnp.where(rid >= s, pltpu.roll(u, s, axis=1), 0.0)
        u = a * u_sh + u
        a = a * a_sh
    a = a.reshape(nb, ng, SUBLANES, w)
    u = u.reshape(nb, ng, SUBLANES, w)
    h = h_s[...]
    g_last, r_last = divmod((l_valid - 1) % t, SUBLANES)
    gr4 = gr_ref[...].reshape(nb, ng, SUBLANES, w)
    for g in range(ng):
        hg = a[:, g] * h + u[:, g]
        h = hg[:, SUBLANES - 1:SUBLANES]
        hl_ref[:, g * SUBLANES:(g + 1) * SUBLANES, :] = (hg * _silu(gr4[:, g])).astype(hl_ref.dtype)
        if g == g_last:
            @pl.when(c == (l_valid - 1) // t)
            def _():
                hlast_ref[...] = hg[:, r_last:r_last + 1]
    h_s[...] = h


def _rglru(rest3, row0, nseq, tiles, nc, l_valid, cw, cb, wa, ba, wx, bx, lam, h0, conv0):
    nb, t = tiles
    assert nseq % nb == 0 and row0 % nb == 0
    w = LRU_WIDTH
    r0 = row0 // nb
    full2 = lambda a: pl.BlockSpec(a.shape, lambda b, c: (0, 0))
    full3 = lambda a: pl.BlockSpec(a.shape, lambda b, c: (0, 0, 0))
    kern = functools.partial(_rglru_kernel, nb=nb, t=t, l_valid=l_valid)
    return pl.pallas_call(
        kern,
        grid=(nseq // nb, nc),
        in_specs=[
            pl.BlockSpec((nb, t, w), lambda b, c: (r0 + b, c, 2)),
            pl.BlockSpec((nb, t, w), lambda b, c: (r0 + b, c, 3)),
            full2(cw), full2(cb), full3(wa), full2(ba), full3(wx), full2(bx), full2(lam),
            pl.BlockSpec((nb, 1, w), lambda b, c: (b, 0, 0)),
            pl.BlockSpec((nb, SUBLANES, w), lambda b, c: (b, 0, 0)),
        ],
        out_specs=[
            pl.BlockSpec((nb, t, w), lambda b, c: (b, c, 0)),
            pl.BlockSpec((nb, 1, w), lambda b, c: (b, 0, 0)),
        ],
        out_shape=[
            jax.ShapeDtypeStruct((nseq, t * nc, w), BF16),
            jax.ShapeDtypeStruct((nseq, 1, w), F32),
        ],
        scratch_shapes=[pltpu.VMEM((nb, 1, w), F32), pltpu.VMEM((nb, SUBLANES, w), F32)],
        compiler_params=_cparams(("parallel", "arbitrary")),
        name="rglru",
    )(rest3, rest3, cw, cb, wa, ba, wx, bx, lam, h0, conv0)


def _gla_kernel(q_ref, k_ref, v_ref, r_ref, al_ref, aup_ref, ab_ref, ng_ref, s0_ref,
                o_ref, s_ref, *, nb, t, nc, l_valid):
    c = pl.program_id(1)
    if nc == 1:
        s_in = s0_ref
    else:
        s_in = s_ref

        @pl.when(c == 0)
        def _():
            s_ref[...] = s0_ref[...]

    pos = c * t + lax.broadcasted_iota(jnp.int32, (t, 1), 0)
    valid = pos < l_valid
    row = lax.broadcasted_iota(jnp.int32, (t, t), 0)
    col = lax.broadcasted_iota(jnp.int32, (t, t), 1)
    causal = row >= col
    aup = aup_ref[...]
    ab = ab_ref[...]
    ng = ng_ref[...]

    def one_row(bi):
        pre = _dot(al_ref[bi], aup) + ab
        lg = jnp.where(valid, _log_sigmoid(pre) / G_TAU, 0.0)
        b = _cumsum_rows(lg)
        b_last = b[t - 1:t, :]
        eb = jnp.exp(b)
        enb = jnp.exp(-b)
        erb = jnp.exp(b_last - b)
        ebl_col = _row_to_col(jnp.exp(b_last))
        for h in range(N_HEADS):
            ks = slice(h * G_DK, (h + 1) * G_DK)
            vs = slice(h * G_DV, (h + 1) * G_DV)
            k = jnp.where(valid, k_ref[bi, :, ks].astype(F32), 0.0)
            v = v_ref[bi, :, vs]
            qd = (q_ref[bi, :, ks].astype(F32) * (G_DK ** -0.5)) * eb[:, ks]
            a = jnp.where(causal, _dot_nt(qd, k * enb[:, ks]), 0.0)
            s_h = s_in[bi, h]
            o = _dot(qd, s_h) + _dot(a, v)
            s_ref[bi, h] = ebl_col[ks, :] * s_h + _dot_tn(k * erb[:, ks], v)
            o = o * lax.rsqrt(jnp.mean(o * o, axis=-1, keepdims=True) + EPS)
            o_ref[bi, :, vs] = (o * ng[:, vs] * _silu(r_ref[bi, :, vs])).astype(o_ref.dtype)

    _pair_loop(nb, one_row)


def _gla(qkv3, r3, al3, row0, nseq, tiles, nc, l_valid, aup, ab, ng, s0):
    nb, t = tiles
    assert nseq % nb == 0 and row0 % nb == 0
    r0 = row0 // nb
    full2 = lambda a: pl.BlockSpec(a.shape, lambda b, c: (0, 0))
    st = pl.BlockSpec((nb, N_HEADS, G_DK, G_DV), lambda b, c: (b, 0, 0, 0))
    kern = functools.partial(_gla_kernel, nb=nb, t=t, nc=nc, l_valid=l_valid)
    return pl.pallas_call(
        kern,
        grid=(nseq // nb, nc),
        in_specs=[
            pl.BlockSpec((nb, t, G_KW), lambda b, c: (r0 + b, c, 0)),
            pl.BlockSpec((nb, t, G_KW), lambda b, c: (r0 + b, c, 1)),
            pl.BlockSpec((nb, t, G_VW), lambda b, c: (r0 + b, c, 1)),
            pl.BlockSpec((nb, t, G_VW), lambda b, c: (r0 + b, c, 0)),
            pl.BlockSpec((nb, t, LANES), lambda b, c: (r0 + b, c, 0)),
            full2(aup), full2(ab), full2(ng), st,
        ],
        out_specs=[pl.BlockSpec((nb, t, G_VW), lambda b, c: (b, c, 0)), st],
        out_shape=[
            jax.ShapeDtypeStruct((nseq, t * nc, G_VW), BF16),
            jax.ShapeDtypeStruct((nseq, N_HEADS, G_DK, G_DV), F32),
        ],
        compiler_params=_cparams(("parallel", "arbitrary")),
        name="gla",
    )(qkv3, qkv3, qkv3, r3, al3, aup, ab, ng, s0)


def _pad_lanes(a, width):
    return jnp.pad(a, [(0, 0)] * (a.ndim - 1) + [(0, width - a.shape[-1])])


def _even_layer(xm, xs, nbp, seq, nbs, ls, st, pre_g, post_g, w_in, w_out, b_i, b_f, m_norm_g,
                conv_w, conv_b, wa, ba, wx, bx, lam):
    c0s, n0s, m0s, h0s, conv0s = st
    wq, wk, wv, wog, wig, wfg, wz, wxr, wgr = jnp.split(
        w_in, [1024, 2048, 3072, 4096, 4100, 4104, 5128, 6152], axis=1)
    w_main = jnp.concatenate([wq, wk, wv, wog, wz, wxr, wgr], axis=1).astype(BF16)
    w_gate = jnp.concatenate([_pad_lanes(wig, LANES), _pad_lanes(wfg, LANES)], axis=1).astype(BF16)
    gbias = jnp.concatenate([_pad_lanes(b_i[None], LANES), _pad_lanes(b_f[None], LANES)], axis=1)
    ng = m_norm_g[None]
    cw, cb = conv_w, conv_b[None]
    wab, wxb = wa.astype(BF16), wx.astype(BF16)
    bav, bxv, lamv = ba[None], bx[None], lam[None]
    w_o1, w_o2 = w_out[:M_WIDTH].astype(BF16), w_out[M_WIDTH:].astype(BF16)
    e_lo = 3 * M_WIDTH
    e_hi = w_main.shape[1] - e_lo
    lru = (cw, cb, wab, bav, wxb, bxv, lamv)

    qm, pm, gm = _norm_proj(xm, pre_g, w_main, w_gate, e_lo, PROJ_ROW_TILE)
    qs, ps, gs = _norm_proj(xs, pre_g, w_main, w_gate, e_lo, xs.shape[0])
    n_meta_rows = nbp * N_META
    xr_cols = slice(2 * LRU_WIDTH, 3 * LRU_WIDTH)

    zc = jnp.zeros((nbp, N_HEADS, M_DV, M_DK), F32)
    zn = jnp.zeros((nbp, N_HEADS, 1, M_DK), F32)
    zm = jnp.zeros((nbp, 1, LANES), F32)
    zh = jnp.zeros((nbp, 1, LRU_WIDTH), F32)
    zconv = jnp.zeros((nbp, SUBLANES, LRU_WIDTH), F32)
    qs_meta = qs.reshape(-1, N_META, e_lo)
    ps_meta = ps.reshape(-1, N_META, e_hi)
    gs_meta = gs.reshape(-1, N_META, 2 * LANES)
    hm_meta, c1, n1, m1 = _mlstm(qs_meta, ps_meta, gs_meta, 0, nbp, MLSTM_TILES["meta"], 1, N_META,
                                 gbias, ng, zc, zn, zm)
    hl_meta, h1 = _rglru(ps_meta, 0, nbp, LRU_TILES["meta"], 1, N_META, *lru, zh, zconv)
    conv1 = ps_meta[:nbp, N_META - SUBLANES:, xr_cols]

    qm3 = qm.reshape(nbp, seq, e_lo)
    pm3 = pm.reshape(nbp, seq, e_hi)
    gm3 = gm.reshape(nbp, seq, 2 * LANES)
    tm_ = MLSTM_TILES["main"][1]
    hm_main, pc, pn, pmm = _mlstm(qm3, pm3, gm3, 0, nbp, MLSTM_TILES["main"], seq // tm_, seq,
                                  gbias, ng, c1, n1, m1)
    tl = LRU_TILES["main"][1]
    hl_main, ph = _rglru(pm3, 0, nbp, LRU_TILES["main"], seq // tl, seq, *lru, h1, conv1)
    pconv = pm3[:, seq - (CONV_W - 1):, xr_cols]

    qs_s = qs.reshape(-1, SAMPLE_PAD, e_lo)
    ps_s = ps.reshape(-1, SAMPLE_PAD, e_hi)
    gs_s = gs.reshape(-1, SAMPLE_PAD, 2 * LANES)
    r0 = n_meta_rows // SAMPLE_PAD
    hm_s, sc, sn, sm = _mlstm(qs_s, ps_s, gs_s, r0, nbs, MLSTM_TILES["sample"], 1, ls, gbias, ng,
                              c0s, n0s[:, :, None, :], _pad_lanes(m0s, LANES)[:, None, :])
    conv0p = jnp.pad(conv0s, ((0, 0), (SUBLANES - (CONV_W - 1), 0), (0, 0)))
    hl_s, sh = _rglru(ps_s, r0, nbs, LRU_TILES["sample"], 1, ls, *lru, h0s[:, None, :], conv0p)
    xr_s = ps_s[r0:, :ls, xr_cols]
    sconv = jnp.concatenate([conv0s, xr_s], axis=1)[:, -(CONV_W - 1):]

    ym = _out_proj(xm, post_g, [hm_main.reshape(-1, M_WIDTH), hl_main.reshape(-1, LRU_WIDTH)], [w_o1, w_o2],
                   OUT_ROW_TILE)
    hm_small = jnp.concatenate([hm_meta.reshape(-1, M_WIDTH), hm_s.reshape(-1, M_WIDTH)], axis=0)
    hl_small = jnp.concatenate([hl_meta.reshape(-1, LRU_WIDTH), hl_s.reshape(-1, LRU_WIDTH)], axis=0)
    ys = _out_proj(xs, post_g, [hm_small, hl_small], [w_o1, w_o2], xs.shape[0])

    p_state = (pc, pn[:, :, 0, :], pmm[:, 0, :N_HEADS], ph[:, 0, :], pconv)
    s_state = (sc, sn[:, :, 0, :], sm[:, 0, :N_HEADS], sh[:, 0, :], sconv)
    return ym, ys, p_state, s_state


def _odd_layer(xm, xs, nbp, seq, nbs, ls, s0s, pre_g, post_g, w_in, w_out, a_up, a_b, g_norm_g):
    e_lo = 2 * G_KW + G_VW
    w_main = w_in[:, :2 * G_KW + 2 * G_VW].astype(BF16)
    w_low = _pad_lanes(w_in[:, 2 * G_KW + 2 * G_VW:], LANES).astype(BF16)
    aup = jnp.pad(a_up, ((0, LANES - G_RANK), (0, 0))).astype(BF16)
    ab = a_b[None]
    ng = g_norm_g[None]
    w_o = w_out.astype(BF16)

    qm, rm, am = _norm_proj(xm, pre_g, w_main, w_low, e_lo, PROJ_ROW_TILE)
    qs, rs, as_ = _norm_proj(xs, pre_g, w_main, w_low, e_lo, xs.shape[0])
    n_meta_rows = nbp * N_META

    zs = jnp.zeros((nbp, N_HEADS, G_DK, G_DV), F32)
    o_meta, s1 = _gla(qs.reshape(-1, N_META, e_lo), rs.reshape(-1, N_META, G_VW), as_.reshape(-1, N_META, LANES),
                      0, nbp, GLA_TILES["meta"], 1, N_META, aup, ab, ng, zs)
    tg = GLA_TILES["main"][1]
    o_main, p_s = _gla(qm.reshape(nbp, seq, e_lo), rm.reshape(nbp, seq, G_VW), am.reshape(nbp, seq, LANES),
                       0, nbp, GLA_TILES["main"], seq // tg, seq, aup, ab, ng, s1)
    r0 = n_meta_rows // SAMPLE_PAD
    o_s, s_s = _gla(qs.reshape(-1, SAMPLE_PAD, e_lo), rs.reshape(-1, SAMPLE_PAD, G_VW),
                    as_.reshape(-1, SAMPLE_PAD, LANES), r0, nbs, GLA_TILES["sample"], 1, ls, aup, ab, ng, s0s)

    ym = _out_proj(xm, post_g, [o_main.reshape(-1, G_VW)], [w_o], OUT_ROW_TILE)
    o_small = jnp.concatenate([o_meta.reshape(-1, G_VW), o_s.reshape(-1, G_VW)], axis=0)
    ys = _out_proj(xs, post_g, [o_small], [w_o], xs.shape[0])
    return ym, ys, p_s, s_s


def kernel(x_prompt, x_sample, state_mlstm_C, state_mlstm_n, state_mlstm_m, state_rglru_h, state_rglru_conv,
           state_gla_S, meta_tokens, pre_norm_a, post_norm_a, w_in_a, w_out_a, mlstm_b_i, mlstm_b_f, mlstm_norm,
           conv_w, conv_b, lru_w_a, lru_b_a, lru_w_x, lru_b_x, lru_lambda, pre_norm_c, post_norm_c, w_in_c,
           w_out_c, gla_alpha_up, gla_alpha_b, gla_norm):
    nbp, seq, d = x_prompt.shape
    nbs, ls, _ = x_sample.shape
    depth = pre_norm_a.shape[0] + pre_norm_c.shape[0]
    assert ls >= CONV_W - 1 and ls <= SAMPLE_PAD and N_META % SUBLANES == 0

    xm = x_prompt.reshape(nbp * seq, d)
    meta = jnp.broadcast_to(meta_tokens[None].astype(x_prompt.dtype), (nbp, N_META, d)).reshape(nbp * N_META, d)
    xs_pad = jnp.pad(x_sample, ((0, 0), (0, SAMPLE_PAD - ls), (0, 0))).reshape(nbs * SAMPLE_PAD, d)
    xs = jnp.concatenate([meta, xs_pad], axis=0)

    p_lists = [[] for _ in range(6)]
    s_lists = [[] for _ in range(6)]
    for layer in range(depth):
        j = layer // 2
        if layer % 2 == 0:
            st = (state_mlstm_C[j], state_mlstm_n[j], state_mlstm_m[j], state_rglru_h[j], state_rglru_conv[j])
            xm, xs, pst, sst = _even_layer(
                xm, xs, nbp, seq, nbs, ls, st, pre_norm_a[j], post_norm_a[j], w_in_a[j], w_out_a[j],
                mlstm_b_i[j], mlstm_b_f[j], mlstm_norm[j], conv_w[j], conv_b[j], lru_w_a[j], lru_b_a[j],
                lru_w_x[j], lru_b_x[j], lru_lambda[j])
            for i in range(5):
                p_lists[i].append(pst[i])
                s_lists[i].append(sst[i])
        else:
            xm, xs, p_s, s_s = _odd_layer(
                xm, xs, nbp, seq, nbs, ls, state_gla_S[j], pre_norm_c[j], post_norm_c[j], w_in_c[j], w_out_c[j],
                gla_alpha_up[j], gla_alpha_b[j], gla_norm[j])
            p_lists[5].append(p_s)
            s_lists[5].append(s_s)

    y_prompt = xm.reshape(nbp, seq, d)
    y_sample = xs[nbp * N_META:].reshape(nbs, SAMPLE_PAD, d)[:, :ls]
    return (y_prompt, y_sample) + tuple(jnp.stack(l) for l in p_lists) + tuple(jnp.stack(l) for l in s_lists)
```

```python
import functools

import jax
import jax.numpy as jnp
from jax import lax
from jax.experimental import pallas as pl
from jax.experimental.pallas import tpu as pltpu

F32 = jnp.float32
BF16 = jnp.bfloat16

D_MODEL = 1024
N_META = 16
EPS = 1e-6
N_HEADS = 4
M_DK = 256
M_DV = 256
M_WIDTH = N_HEADS * M_DV
LRU_WIDTH = 1024
LRU_BLOCKS = 8
LRU_BS = LRU_WIDTH // LRU_BLOCKS
CONV_W = 4
LRU_C = 8.0
G_DK = 256
G_DV = 512
G_KW = N_HEADS * G_DK
G_VW = N_HEADS * G_DV
G_RANK = 16
G_TAU = 16.0

LANES = 128
SUBLANES = 8
NEG = -1e30
SAMPLE_PAD = SUBLANES
VMEM_LIMIT = 56 * 1024 * 1024
COL_TILE = 1024

MLSTM_TILES = {"meta": (8, N_META), "main": (4, 128), "sample": (4, SAMPLE_PAD)}
GLA_TILES = {"meta": (8, N_META), "main": (2, 64), "sample": (4, SAMPLE_PAD)}
LRU_TILES = {"meta": (8, N_META), "main": (1, 256), "sample": (16, SAMPLE_PAD)}
PROJ_ROW_TILE = 1024
OUT_ROW_TILE = 512


def _row_tile(n, cap):
    return next(tm for tm in range(min(n, cap) // SUBLANES * SUBLANES, 0, -SUBLANES) if n % tm == 0)


def _cparams(sem):
    return pltpu.CompilerParams(dimension_semantics=sem, vmem_limit_bytes=VMEM_LIMIT)


def _sigmoid(x):
    return 0.5 * jnp.tanh(0.5 * x) + 0.5


def _silu(x):
    return x * _sigmoid(x)


def _log_sigmoid(x):
    return jnp.minimum(x, 0.0) - jnp.log1p(jnp.exp(-jnp.abs(x)))


def _softplus(x):
    return jnp.maximum(x, 0.0) + jnp.log1p(jnp.exp(-jnp.abs(x)))


def _dot(a, b):
    return jnp.dot(a.astype(BF16), b.astype(BF16), preferred_element_type=F32)


def _dot_nt(a, b):
    return lax.dot_general(a.astype(BF16), b.astype(BF16), (((1,), (1,)), ((), ())),
                           preferred_element_type=F32)


def _dot_tn(a, b):
    return lax.dot_general(a.astype(BF16), b.astype(BF16), (((0,), (0,)), ((), ())),
                           preferred_element_type=F32)


def _transpose_rows(x):
    t, n = x.shape
    if t < LANES:
        x = jnp.concatenate([x, jnp.zeros((LANES - t, n), x.dtype)], axis=0)
    return jnp.transpose(x)


def _row_to_col(r):
    return jnp.transpose(jnp.broadcast_to(r, (LANES, r.shape[1])))[:, 0:1]


def _cumsum_rows(x):
    t = x.shape[0]
    if t <= 2 * SUBLANES:
        rid = lax.broadcasted_iota(jnp.int32, (t, 1), 0)
        s = 1
        while s < t:
            x = x + jnp.where(rid >= s, pltpu.roll(x, s, axis=0), 0.0)
            s *= 2
        return x
    row = lax.broadcasted_iota(jnp.int32, (t, t), 0)
    col = lax.broadcasted_iota(jnp.int32, (t, t), 1)
    tri = jnp.where(row >= col, 1.0, 0.0).astype(BF16)
    hi = x.astype(BF16)
    r1 = x - hi.astype(F32)
    mid = r1.astype(BF16)
    lo = (r1 - mid.astype(F32)).astype(BF16)
    acc = jnp.dot(tri, lo, preferred_element_type=F32)
    acc = acc + jnp.dot(tri, mid, preferred_element_type=F32)
    return acc + jnp.dot(tri, hi, preferred_element_type=F32)


def _head_rms(x, nh):
    hd = x.shape[1] // nh
    parts = []
    for h in range(nh):
        xh = x[:, h * hd:(h + 1) * hd]
        parts.append(xh * lax.rsqrt(jnp.mean(xh * xh, axis=-1, keepdims=True) + EPS))
    return jnp.concatenate(parts, axis=-1)


def _group_loop(nb, group_fn):
    if nb <= 2:
        group_fn(list(range(nb)))
        return

    def body(i, carry):
        group_fn([2 * i, 2 * i + 1])
        return carry

    lax.fori_loop(0, nb // 2, body, 0)


def _norm_proj_kernel(n_lo, x_ref, g_ref, wm_ref, ws_ref, olo_ref, ohi_ref, os_ref, hn_ref):
    j = pl.program_id(1)

    @pl.when(j == 0)
    def _():
        x = x_ref[...]
        y = x * lax.rsqrt(jnp.mean(x * x, axis=-1, keepdims=True) + EPS) * g_ref[...]
        hn = y.astype(BF16)
        hn_ref[...] = hn
        os_ref[...] = jnp.dot(hn, ws_ref[...], preferred_element_type=F32)

    acc = jnp.dot(hn_ref[...], wm_ref[...], preferred_element_type=F32)

    @pl.when(j < n_lo)
    def _():
        olo_ref[...] = acc.astype(olo_ref.dtype)

    @pl.when(j >= n_lo)
    def _():
        ohi_ref[...] = acc


def _norm_proj(x2d, g, w_main, w_small, e_lo, tm):
    n, d = x2d.shape
    e = w_main.shape[1]
    es = w_small.shape[1]
    tn = COL_TILE
    n_lo = e_lo // tn
    return pl.pallas_call(
        functools.partial(_norm_proj_kernel, n_lo),
        grid=(n // tm, e // tn),
        in_specs=[
            pl.BlockSpec((tm, d), lambda i, j: (i, 0)),
            pl.BlockSpec((1, d), lambda i, j: (0, 0)),
            pl.BlockSpec((d, tn), lambda i, j: (0, j)),
            pl.BlockSpec((d, es), lambda i, j: (0, 0)),
        ],
        out_specs=[
            pl.BlockSpec((tm, tn), lambda i, j: (i, jnp.minimum(j, n_lo - 1))),
            pl.BlockSpec((tm, tn), lambda i, j: (i, jnp.maximum(j - n_lo, 0))),
            pl.BlockSpec((tm, es), lambda i, j: (i, 0)),
        ],
        out_shape=[
            jax.ShapeDtypeStruct((n, e_lo), BF16),
            jax.ShapeDtypeStruct((n, e - e_lo), F32),
            jax.ShapeDtypeStruct((n, es), F32),
        ],
        scratch_shapes=[pltpu.VMEM((tm, d), BF16)],
        compiler_params=_cparams(("parallel", "arbitrary")),
        name="norm_proj",
    )(x2d, g.reshape(1, d), w_main, w_small)


def _residual_norm(x_ref, g_ref, y_ref, out):
    nrm = out * lax.rsqrt(jnp.mean(out * out, axis=-1, keepdims=True) + EPS) * g_ref[...]
    y_ref[...] = x_ref[...] + nrm


def _out_even_kernel(x_ref, g_ref, h_ref, og_ref, z_ref, ng_ref, hl_ref, w1_ref, w2_ref, y_ref):
    hm = _sigmoid(og_ref[...]) * h_ref[...].astype(F32)
    hm = _head_rms(hm, N_HEADS) * ng_ref[...] * _silu(z_ref[...])
    out = jnp.dot(hm.astype(BF16), w1_ref[...], preferred_element_type=F32)
    out = out + jnp.dot(hl_ref[...], w2_ref[...], preferred_element_type=F32)
    _residual_norm(x_ref, g_ref, y_ref, out)


def _out_even(x2d, g, h_raw, rest, ng, hl, w1, w2, tm):
    n, d = x2d.shape
    row = lambda width, col: pl.BlockSpec((tm, width), lambda i: (i, col))
    const = lambda a: pl.BlockSpec(a.shape, lambda i: (0, 0))
    return pl.pallas_call(
        _out_even_kernel,
        grid=(n // tm,),
        in_specs=[row(d, 0), const(g), row(M_WIDTH, 0), row(M_WIDTH, 0), row(M_WIDTH, 1), const(ng),
                  row(LRU_WIDTH, 0), const(w1), const(w2)],
        out_specs=row(d, 0),
        out_shape=jax.ShapeDtypeStruct((n, d), F32),
        compiler_params=_cparams(("parallel",)),
        name="out_even",
    )(x2d, g, h_raw, rest, rest, ng, hl, w1, w2)


def _out_odd_kernel(x_ref, g_ref, o_ref, r_ref, ng_ref, w_ref, y_ref):
    o = _head_rms(o_ref[...].astype(F32), N_HEADS) * ng_ref[...] * _silu(r_ref[...])
    out = jnp.dot(o.astype(BF16), w_ref[...], preferred_element_type=F32)
    _residual_norm(x_ref, g_ref, y_ref, out)


def _out_odd(x2d, g, o_raw, r, ng, w, tm):
    n, d = x2d.shape
    row = lambda width: pl.BlockSpec((tm, width), lambda i: (i, 0))
    const = lambda a: pl.BlockSpec(a.shape, lambda i: (0, 0))
    return pl.pallas_call(
        _out_odd_kernel,
        grid=(n // tm,),
        in_specs=[row(d), const(g), row(G_VW), row(G_VW), const(ng), const(w)],
        out_specs=row(d),
        out_shape=jax.ShapeDtypeStruct((n, d), F32),
        compiler_params=_cparams(("parallel",)),
        name="out_odd",
    )(x2d, g, o_raw, r, ng, w)


def _mlstm_kernel(q_ref, k_ref, v_ref, gate_ref, gbias_ref, c0_ref, n0_ref, m0_ref,
                  h_ref, c_ref, n_ref, m_ref, *, nb, t, nc, l_valid):
    c = pl.program_id(1)
    if nc == 1:
        c_in, n_in, m_in = c0_ref, n0_ref, m0_ref
    else:
        c_in, n_in, m_in = c_ref, n_ref, m_ref

        @pl.when(c == 0)
        def _():
            c_ref[...] = c0_ref[...]
            n_ref[...] = n0_ref[...]
            m_ref[...] = m0_ref[...]

    pos = c * t + lax.broadcasted_iota(jnp.int32, (t, 1), 0)
    valid = pos < l_valid
    row = lax.broadcasted_iota(jnp.int32, (t, t), 0)
    col = lax.broadcasted_iota(jnp.int32, (t, t), 1)
    causal = row >= col
    lane = lax.broadcasted_iota(jnp.int32, (1, LANES), 1)
    gbias = gbias_ref[...]
    heads = [(h, slice(h * M_DK, (h + 1) * M_DK)) for h in range(N_HEADS)]

    def group(rows):
        pairs = [(i, r, h, sl) for i, r in enumerate(rows) for h, sl in heads]
        gates = []
        for r in rows:
            g = gate_ref[r] + gbias
            li = jnp.where(valid, g[:, :LANES], NEG)
            lf = jnp.where(valid, _log_sigmoid(g[:, LANES:]), 0.0)
            b = _cumsum_rows(lf)
            gates.append((li, b, _transpose_rows(li - b)[:, :t], m_in[r]))
        st = {}
        for i, r, h, sl in pairs:
            li, b, r_t, m_vec = gates[i]
            b_col = b[:, h:h + 1]
            inter = b_col + m_vec[:, h:h + 1]
            dmat = jnp.where(causal, b_col + r_t[h:h + 1, :], NEG)
            mt = jnp.maximum(inter, jnp.max(dmat, axis=-1, keepdims=True))
            k = k_ref[r, :, sl] * (M_DK ** -0.5)
            s = _dot_nt(q_ref[r, :, sl], k) * jnp.exp(dmat - mt)
            st[i, h] = (inter, mt, k, s)
        for i, r, h, sl in pairs:
            inter, mt, k, s = st[i, h]
            q = q_ref[r, :, sl]
            w_inter = jnp.exp(inter - mt)
            num = w_inter * _dot_nt(q, c_in[r, h]) + _dot(s, v_ref[r, :, sl])
            den = (w_inter * jnp.sum(q.astype(F32) * n_in[r, h], axis=-1, keepdims=True)
                   + jnp.sum(s, axis=-1, keepdims=True))
            h_ref[r, :, sl] = (num / jnp.maximum(jnp.abs(den), jnp.exp(-mt))).astype(h_ref.dtype)
        m_next = [gt[3] for gt in gates]
        for i, r, h, sl in pairs:
            li, b, _, _ = gates[i]
            inter, mt, k, _ = st[i, h]
            b_col = b[:, h:h + 1]
            m_new = mt[t - 1:t, :]
            w_c = jnp.exp(inter[t - 1:t, :] - m_new)
            w_k = jnp.exp(b_col[t - 1:t, :] - b_col + li[:, h:h + 1] - m_new)
            c_ref[r, h] = w_c * c_in[r, h] + _dot_tn(v_ref[r, :, sl].astype(F32) * w_k, k)
            n_ref[r, h] = w_c * n_in[r, h] + jnp.sum(k.astype(F32) * w_k, axis=0, keepdims=True)
            m_next[i] = jnp.where(lane == h, m_new, m_next[i])
        for i, r in enumerate(rows):
            m_ref[r] = m_next[i]

    _group_loop(nb, group)


def _mlstm(qkv3, gates3, row0, nseq, tiles, nc, l_valid, gbias, c0, n0, m0):
    nb, t = tiles
    assert nseq % nb == 0 and row0 % nb == 0
    wd = M_WIDTH
    r0 = row0 // nb
    seq = lambda col: pl.BlockSpec((nb, t, wd), lambda b, c: (r0 + b, c, col))
    st_c = pl.BlockSpec((nb, N_HEADS, M_DV, M_DK), lambda b, c: (b, 0, 0, 0))
    st_n = pl.BlockSpec((nb, N_HEADS, 1, M_DK), lambda b, c: (b, 0, 0, 0))
    st_m = pl.BlockSpec((nb, 1, LANES), lambda b, c: (b, 0, 0))
    kern = functools.partial(_mlstm_kernel, nb=nb, t=t, nc=nc, l_valid=l_valid)
    return pl.pallas_call(
        kern,
        grid=(nseq // nb, nc),
        in_specs=[
            seq(0), seq(1), seq(2),
            pl.BlockSpec((nb, t, 2 * LANES), lambda b, c: (r0 + b, c, 0)),
            pl.BlockSpec(gbias.shape, lambda b, c: (0, 0)), st_c, st_n, st_m,
        ],
        out_specs=[pl.BlockSpec((nb, t, wd), lambda b, c: (b, c, 0)), st_c, st_n, st_m],
        out_shape=[
            jax.ShapeDtypeStruct((nseq, t * nc, wd), BF16),
            jax.ShapeDtypeStruct((nseq, N_HEADS, M_DV, M_DK), F32),
            jax.ShapeDtypeStruct((nseq, N_HEADS, 1, M_DK), F32),
            jax.ShapeDtypeStruct((nseq, 1, LANES), F32),
        ],
        compiler_params=_cparams(("parallel", "arbitrary")),
        name="mlstm",
    )(qkv3, qkv3, qkv3, gates3, gbias, c0, n0, m0)


def _rglru_kernel(x_ref, gr_ref, cw_ref, cb_ref, wa_ref, ba_ref, wx_ref, bx_ref, lam_ref, h0_ref, conv0_ref,
                  hl_ref, hlast_ref, h_s, xs_s, *, nb, t, l_valid):
    c = pl.program_id(1)
    w = LRU_WIDTH
    ng = t // SUBLANES

    @pl.when(c == 0)
    def _():
        h_s[...] = h0_ref[...]
        xs_s[:, 0:SUBLANES, :] = conv0_ref[...]

    x = x_ref[...]
    xs_s[:, SUBLANES:SUBLANES + t, :] = x
    cw = cw_ref[...]
    xc = cb_ref[...] + cw[CONV_W - 1:CONV_W, :] * x
    for s in range(1, CONV_W):
        xc = xc + cw[CONV_W - 1 - s:CONV_W - s, :] * xs_s[:, SUBLANES - s:SUBLANES - s + t, :]
    xs_s[:, 0:SUBLANES, :] = x[:, t - SUBLANES:, :]

    xf = xc.reshape(nb * t, w)
    ra, ri = [], []
    for n in range(LRU_BLOCKS):
        xb = xf[:, n * LRU_BS:(n + 1) * LRU_BS].astype(BF16)
        ra.append(jnp.dot(xb, wa_ref[n], preferred_element_type=F32))
        ri.append(jnp.dot(xb, wx_ref[n], preferred_element_type=F32))
    r = _sigmoid(jnp.concatenate(ra, axis=-1) + ba_ref[...])
    i = _sigmoid(jnp.concatenate(ri, axis=-1) + bx_ref[...])
    log_a = -LRU_C * r * _softplus(-lam_ref[...])
    a = jnp.exp(log_a)
    u = jnp.sqrt(-jnp.tanh(log_a) * (1.0 + a * a)) * (i * xf)

    rid = lax.broadcasted_iota(jnp.int32, (1, SUBLANES, 1), 1)
    a = a.reshape(nb * ng, SUBLANES, w)
    u = u.reshape(nb * ng, SUBLANES, w)
    for s in (1, 2, 4):
        a_sh = jnp.where(rid >= s, pltpu.roll(a, s, axis=1), 1.0)
        u_sh = jnp.where(rid >= s, pltpu.roll(u, s, axis=1), 0.0)
        u = a * u_sh + u
        a = a * a_sh
    a = a.reshape(nb, ng, SUBLANES, w)
    u = u.reshape(nb, ng, SUBLANES, w)
    h = h_s[...]
    g_last, r_last = divmod((l_valid - 1) % t, SUBLANES)
    gr4 = gr_ref[...].reshape(nb, ng, SUBLANES, w)
    for g in range(ng):
        hg = a[:, g] * h + u[:, g]
        h = hg[:, SUBLANES - 1:SUBLANES]
        hl_ref[:, g * SUBLANES:(g + 1) * SUBLANES, :] = (hg * _silu(gr4[:, g])).astype(hl_ref.dtype)
        if g == g_last:
            @pl.when(c == (l_valid - 1) // t)
            def _():
                hlast_ref[...] = hg[:, r_last:r_last + 1]
    h_s[...] = h


def _rglru(rest3, row0, nseq, tiles, nc, l_valid, cw, cb, wa, ba, wx, bx, lam, h0, conv0):
    nb, t = tiles
    assert nseq % nb == 0 and row0 % nb == 0
    w = LRU_WIDTH
    r0 = row0 // nb
    full2 = lambda a: pl.BlockSpec(a.shape, lambda b, c: (0, 0))
    full3 = lambda a: pl.BlockSpec(a.shape, lambda b, c: (0, 0, 0))
    kern = functools.partial(_rglru_kernel, nb=nb, t=t, l_valid=l_valid)
    return pl.pallas_call(
        kern,
        grid=(nseq // nb, nc),
        in_specs=[
            pl.BlockSpec((nb, t, w), lambda b, c: (r0 + b, c, 2)),
            pl.BlockSpec((nb, t, w), lambda b, c: (r0 + b, c, 3)),
            full2(cw), full2(cb), full3(wa), full2(ba), full3(wx), full2(bx), full2(lam),
            pl.BlockSpec((nb, 1, w), lambda b, c: (b, 0, 0)),
            pl.BlockSpec((nb, SUBLANES, w), lambda b, c: (b, 0, 0)),
        ],
        out_specs=[
            pl.BlockSpec((nb, t, w), lambda b, c: (b, c, 0)),
            pl.BlockSpec((nb, 1, w), lambda b, c: (b, 0, 0)),
        ],
        out_shape=[
            jax.ShapeDtypeStruct((nseq, t * nc, w), BF16),
            jax.ShapeDtypeStruct((nseq, 1, w), F32),
        ],
        scratch_shapes=[pltpu.VMEM((nb, 1, w), F32), pltpu.VMEM((nb, SUBLANES + t, w), F32)],
        compiler_params=_cparams(("parallel", "arbitrary")),
        name="rglru",
    )(rest3, rest3, cw, cb, wa, ba, wx, bx, lam, h0, conv0)


def _gla_kernel(q_ref, k_ref, v_ref, al_ref, aup_ref, ab_ref, s0_ref, o_ref, s_ref, *, nb, t, nc, l_valid):
    c = pl.program_id(1)
    if nc == 1:
        s_in = s0_ref
    else:
        s_in = s_ref

        @pl.when(c == 0)
        def _():
            s_ref[...] = s0_ref[...]

    pos = c * t + lax.broadcasted_iota(jnp.int32, (t, 1), 0)
    valid = pos < l_valid
    row = lax.broadcasted_iota(jnp.int32, (t, t), 0)
    col = lax.broadcasted_iota(jnp.int32, (t, t), 1)
    causal = row >= col
    aup = aup_ref[...]
    ab = ab_ref[...]
    heads = [(h, slice(h * G_DK, (h + 1) * G_DK), slice(h * G_DV, (h + 1) * G_DV)) for h in range(N_HEADS)]

    def group(rows):
        pairs = [(i, r, h, ks, vs) for i, r in enumerate(rows) for h, ks, vs in heads]
        dec = []
        for r in rows:
            pre = _dot(al_ref[r], aup) + ab
            lg = jnp.where(valid, _log_sigmoid(pre) / G_TAU, 0.0)
            b = _cumsum_rows(lg)
            b_last = b[t - 1:t, :]
            k = jnp.where(valid, k_ref[r].astype(F32), 0.0)
            qd = ((q_ref[r].astype(F32) * (G_DK ** -0.5)) * jnp.exp(b)).astype(BF16)
            kn = (k * jnp.exp(-b)).astype(BF16)
            kr = (k * jnp.exp(b_last - b)).astype(BF16)
            dec.append((qd, kn, kr, _row_to_col(jnp.exp(b_last))))
        att = {}
        for i, r, h, ks, vs in pairs:
            qd, kn, _, _ = dec[i]
            att[i, h] = jnp.where(causal, _dot_nt(qd[:, ks], kn[:, ks]), 0.0).astype(BF16)
        for i, r, h, ks, vs in pairs:
            qd, _, kr, ebl_col = dec[i]
            s_h = s_in[r, h]
            v = v_ref[r, :, vs]
            o_ref[r, :, vs] = (_dot(qd[:, ks], s_h) + _dot(att[i, h], v)).astype(o_ref.dtype)
            s_ref[r, h] = ebl_col[ks, :] * s_h + _dot_tn(kr[:, ks], v)

    _group_loop(nb, group)


def _gla(qkv3, al3, row0, nseq, tiles, nc, l_valid, aup, ab, s0):
    nb, t = tiles
    assert nseq % nb == 0 and row0 % nb == 0
    r0 = row0 // nb
    full2 = lambda a: pl.BlockSpec(a.shape, lambda b, c: (0, 0))
    st = pl.BlockSpec((nb, N_HEADS, G_DK, G_DV), lambda b, c: (b, 0, 0, 0))
    kern = functools.partial(_gla_kernel, nb=nb, t=t, nc=nc, l_valid=l_valid)
    return pl.pallas_call(
        kern,
        grid=(nseq // nb, nc),
        in_specs=[
            pl.BlockSpec((nb, t, G_KW), lambda b, c: (r0 + b, c, 0)),
            pl.BlockSpec((nb, t, G_KW), lambda b, c: (r0 + b, c, 1)),
            pl.BlockSpec((nb, t, G_VW), lambda b, c: (r0 + b, c, 1)),
            pl.BlockSpec((nb, t, LANES), lambda b, c: (r0 + b, c, 0)),
            full2(aup), full2(ab), st,
        ],
        out_specs=[pl.BlockSpec((nb, t, G_VW), lambda b, c: (b, c, 0)), st],
        out_shape=[
            jax.ShapeDtypeStruct((nseq, t * nc, G_VW), BF16),
            jax.ShapeDtypeStruct((nseq, N_HEADS, G_DK, G_DV), F32),
        ],
        compiler_params=_cparams(("parallel", "arbitrary")),
        name="gla",
    )(qkv3, qkv3, qkv3, al3, aup, ab, s0)


def _pad_lanes(a, width):
    return jnp.pad(a, [(0, 0)] * (a.ndim - 1) + [(0, width - a.shape[-1])])


def _even_layer(xm, xs, nbp, seq, nbs, ls, st, pre_g, post_g, w_in, w_out, b_i, b_f, m_norm_g,
                conv_w, conv_b, wa, ba, wx, bx, lam):
    c0s, n0s, m0s, h0s, conv0s = st
    wq, wk, wv, wog, wig, wfg, wz, wxr, wgr = jnp.split(
        w_in, [1024, 2048, 3072, 4096, 4100, 4104, 5128, 6152], axis=1)
    w_main = jnp.concatenate([wq, wk, wv, wog, wz, wxr, wgr], axis=1).astype(BF16)
    w_gate = jnp.concatenate([_pad_lanes(wig, LANES), _pad_lanes(wfg, LANES)], axis=1).astype(BF16)
    gbias = jnp.concatenate([_pad_lanes(b_i[None], LANES), _pad_lanes(b_f[None], LANES)], axis=1)
    ng = m_norm_g[None]
    post = post_g[None]
    cw, cb = conv_w, conv_b[None]
    wab, wxb = wa.astype(BF16), wx.astype(BF16)
    bav, bxv, lamv = ba[None], bx[None], lam[None]
    w_o1, w_o2 = w_out[:M_WIDTH].astype(BF16), w_out[M_WIDTH:].astype(BF16)
    e_lo = 3 * M_WIDTH
    e_hi = w_main.shape[1] - e_lo
    lru = (cw, cb, wab, bav, wxb, bxv, lamv)

    qm, pm, gm = _norm_proj(xm, pre_g, w_main, w_gate, e_lo, PROJ_ROW_TILE)
    qs, ps, gs = _norm_proj(xs, pre_g, w_main, w_gate, e_lo, xs.shape[0])
    n_meta_rows = nbp * N_META
    xr_cols = slice(2 * LRU_WIDTH, 3 * LRU_WIDTH)

    zc = jnp.zeros((nbp, N_HEADS, M_DV, M_DK), F32)
    zn = jnp.zeros((nbp, N_HEADS, 1, M_DK), F32)
    zm = jnp.zeros((nbp, 1, LANES), F32)
    zh = jnp.zeros((nbp, 1, LRU_WIDTH), F32)
    zconv = jnp.zeros((nbp, SUBLANES, LRU_WIDTH), F32)
    qs_meta = qs.reshape(-1, N_META, e_lo)
    ps_meta = ps.reshape(-1, N_META, e_hi)
    gs_meta = gs.reshape(-1, N_META, 2 * LANES)
    h_meta, c1, n1, m1 = _mlstm(qs_meta, gs_meta, 0, nbp, MLSTM_TILES["meta"], 1, N_META, gbias, zc, zn, zm)
    hl_meta, h1 = _rglru(ps_meta, 0, nbp, LRU_TILES["meta"], 1, N_META, *lru, zh, zconv)
    conv1 = ps_meta[:nbp, N_META - SUBLANES:, xr_cols]

    qm3 = qm.reshape(nbp, seq, e_lo)
    pm3 = pm.reshape(nbp, seq, e_hi)
    gm3 = gm.reshape(nbp, seq, 2 * LANES)
    tm_ = MLSTM_TILES["main"][1]
    h_main, pc, pn, pmm = _mlstm(qm3, gm3, 0, nbp, MLSTM_TILES["main"], seq // tm_, seq, gbias, c1, n1, m1)
    tl = LRU_TILES["main"][1]
    hl_main, ph = _rglru(pm3, 0, nbp, LRU_TILES["main"], seq // tl, seq, *lru, h1, conv1)
    pconv = pm3[:, seq - (CONV_W - 1):, xr_cols]

    qs_s = qs.reshape(-1, SAMPLE_PAD, e_lo)
    ps_s = ps.reshape(-1, SAMPLE_PAD, e_hi)
    gs_s = gs.reshape(-1, SAMPLE_PAD, 2 * LANES)
    r0 = n_meta_rows // SAMPLE_PAD
    h_s, sc, sn, sm = _mlstm(qs_s, gs_s, r0, nbs, MLSTM_TILES["sample"], 1, ls, gbias,
                             c0s, n0s[:, :, None, :], _pad_lanes(m0s, LANES)[:, None, :])
    conv0p = jnp.pad(conv0s, ((0, 0), (SUBLANES - (CONV_W - 1), 0), (0, 0)))
    hl_s, sh = _rglru(ps_s, r0, nbs, LRU_TILES["sample"], 1, ls, *lru, h0s[:, None, :], conv0p)
    xr_s = ps_s[r0:, :ls, xr_cols]
    sconv = jnp.concatenate([conv0s, xr_s], axis=1)[:, -(CONV_W - 1):]

    ym = _out_even(xm, post, h_main.reshape(-1, M_WIDTH), pm, ng, hl_main.reshape(-1, LRU_WIDTH), w_o1, w_o2,
                   OUT_ROW_TILE)
    h_small = jnp.concatenate([h_meta.reshape(-1, M_WIDTH), h_s.reshape(-1, M_WIDTH)], axis=0)
    hl_small = jnp.concatenate([hl_meta.reshape(-1, LRU_WIDTH), hl_s.reshape(-1, LRU_WIDTH)], axis=0)
    ys = _out_even(xs, post, h_small, ps, ng, hl_small, w_o1, w_o2, _row_tile(xs.shape[0], OUT_ROW_TILE))

    p_state = (pc, pn[:, :, 0, :], pmm[:, 0, :N_HEADS], ph[:, 0, :], pconv)
    s_state = (sc, sn[:, :, 0, :], sm[:, 0, :N_HEADS], sh[:, 0, :], sconv)
    return ym, ys, p_state, s_state


def _odd_layer(xm, xs, nbp, seq, nbs, ls, s0s, pre_g, post_g, w_in, w_out, a_up, a_b, g_norm_g):
    e_lo = 2 * G_KW + G_VW
    w_main = w_in[:, :2 * G_KW + 2 * G_VW].astype(BF16)
    w_low = _pad_lanes(w_in[:, 2 * G_KW + 2 * G_VW:], LANES).astype(BF16)
    aup = jnp.pad(a_up, ((0, LANES - G_RANK), (0, 0))).astype(BF16)
    ab = a_b[None]
    ng = g_norm_g[None]
    post = post_g[None]
    w_o = w_out.astype(BF16)

    qm, rm, am = _norm_proj(xm, pre_g, w_main, w_low, e_lo, PROJ_ROW_TILE)
    qs, rs, as_ = _norm_proj(xs, pre_g, w_main, w_low, e_lo, xs.shape[0])
    n_meta_rows = nbp * N_META

    zs = jnp.zeros((nbp, N_HEADS, G_DK, G_DV), F32)
    o_meta, s1 = _gla(qs.reshape(-1, N_META, e_lo), as_.reshape(-1, N_META, LANES),
                      0, nbp, GLA_TILES["meta"], 1, N_META, aup, ab, zs)
    tg = GLA_TILES["main"][1]
    o_main, p_s = _gla(qm.reshape(nbp, seq, e_lo), am.reshape(nbp, seq, LANES),
                       0, nbp, GLA_TILES["main"], seq // tg, seq, aup, ab, s1)
    r0 = n_meta_rows // SAMPLE_PAD
    o_s, s_s = _gla(qs.reshape(-1, SAMPLE_PAD, e_lo), as_.reshape(-1, SAMPLE_PAD, LANES),
                    r0, nbs, GLA_TILES["sample"], 1, ls, aup, ab, s0s)

    ym = _out_odd(xm, post, o_main.reshape(-1, G_VW), rm, ng, w_o, OUT_ROW_TILE)
    o_small = jnp.concatenate([o_meta.reshape(-1, G_VW), o_s.reshape(-1, G_VW)], axis=0)
    ys = _out_odd(xs, post, o_small, rs, ng, w_o, _row_tile(xs.shape[0], OUT_ROW_TILE))
    return ym, ys, p_s, s_s


def kernel(x_prompt, x_sample, state_mlstm_C, state_mlstm_n, state_mlstm_m, state_rglru_h, state_rglru_conv,
           state_gla_S, meta_tokens, pre_norm_a, post_norm_a, w_in_a, w_out_a, mlstm_b_i, mlstm_b_f, mlstm_norm,
           conv_w, conv_b, lru_w_a, lru_b_a, lru_w_x, lru_b_x, lru_lambda, pre_norm_c, post_norm_c, w_in_c,
           w_out_c, gla_alpha_up, gla_alpha_b, gla_norm):
    nbp, seq, d = x_prompt.shape
    nbs, ls, _ = x_sample.shape
    depth = pre_norm_a.shape[0] + pre_norm_c.shape[0]
    assert ls >= CONV_W - 1 and ls <= SAMPLE_PAD and N_META % SUBLANES == 0

    xm = x_prompt.reshape(nbp * seq, d)
    meta = jnp.broadcast_to(meta_tokens[None].astype(x_prompt.dtype), (nbp, N_META, d)).reshape(nbp * N_META, d)
    xs_pad = jnp.pad(x_sample, ((0, 0), (0, SAMPLE_PAD - ls), (0, 0))).reshape(nbs * SAMPLE_PAD, d)
    xs = jnp.concatenate([meta, xs_pad], axis=0)

    p_lists = [[] for _ in range(6)]
    s_lists = [[] for _ in range(6)]
    for layer in range(depth):
        j = layer // 2
        if layer % 2 == 0:
            st = (state_mlstm_C[j], state_mlstm_n[j], state_mlstm_m[j], state_rglru_h[j], state_rglru_conv[j])
            xm, xs, pst, sst = _even_layer(
                xm, xs, nbp, seq, nbs, ls, st, pre_norm_a[j], post_norm_a[j], w_in_a[j], w_out_a[j],
                mlstm_b_i[j], mlstm_b_f[j], mlstm_norm[j], conv_w[j], conv_b[j], lru_w_a[j], lru_b_a[j],
                lru_w_x[j], lru_b_x[j], lru_lambda[j])
            for i in range(5):
                p_lists[i].append(pst[i])
                s_lists[i].append(sst[i])
        else:
            xm, xs, p_s, s_s = _odd_layer(
                xm, xs, nbp, seq, nbs, ls, state_gla_S[j], pre_norm_c[j], post_norm_c[j], w_in_c[j], w_out_c[j],
                gla_alpha_up[j], gla_alpha_b[j], gla_norm[j])
            p_lists[5].append(p_s)
            s_lists[5].append(s_s)

    y_prompt = xm.reshape(nbp, seq, d)
    y_sample = xs[nbp * N_META:].reshape(nbs, SAMPLE_PAD, d)[:, :ls]
    return (y_prompt, y_sample) + tuple(jnp.stack(l) for l in p_lists) + tuple(jnp.stack(l) for l in s_lists)
```

```python
import functools

import jax
import jax.numpy as jnp
from jax import lax
from jax.experimental import pallas as pl
from jax.experimental.pallas import tpu as pltpu

F32 = jnp.float32
BF16 = jnp.bfloat16

D_MODEL = 1024
N_META = 16
EPS = 1e-6
N_HEADS = 4
M_DK = 256
M_DV = 256
M_WIDTH = N_HEADS * M_DV
LRU_WIDTH = 1024
LRU_BLOCKS = 8
LRU_BS = LRU_WIDTH // LRU_BLOCKS
CONV_W = 4
LRU_C = 8.0
G_DK = 256
G_DV = 512
G_KW = N_HEADS * G_DK
G_VW = N_HEADS * G_DV
G_RANK = 16
G_TAU = 16.0

LANES = 128
SUBLANES = 8
NEG = -1e30
SAMPLE_PAD = SUBLANES
VMEM_LIMIT = 56 * 1024 * 1024
COL_TILE = 1024

MLSTM_TILES = {"meta": (8, N_META), "main": (4, 128), "sample": (4, SAMPLE_PAD)}
GLA_TILES = {"meta": (8, N_META), "main": (2, 64), "sample": (4, SAMPLE_PAD)}
LRU_TILES = {"meta": (8, N_META), "main": (1, 256), "sample": (16, SAMPLE_PAD)}
PROJ_ROW_TILE = 1024
OUT_ROW_TILE = 512
FUSED_ROW_TILE = 256


def _row_tile(n, cap):
    return next(tm for tm in range(min(n, cap) // SUBLANES * SUBLANES, 0, -SUBLANES) if n % tm == 0)


def _cparams(sem):
    return pltpu.CompilerParams(dimension_semantics=sem, vmem_limit_bytes=VMEM_LIMIT)


def _sigmoid(x):
    return 0.5 * jnp.tanh(0.5 * x) + 0.5


def _silu(x):
    return x * _sigmoid(x)


def _log_sigmoid(x):
    return jnp.minimum(x, 0.0) - jnp.log1p(jnp.exp(-jnp.abs(x)))


def _softplus(x):
    return jnp.maximum(x, 0.0) + jnp.log1p(jnp.exp(-jnp.abs(x)))


def _dot(a, b):
    return jnp.dot(a.astype(BF16), b.astype(BF16), preferred_element_type=F32)


def _dot_nt(a, b):
    return lax.dot_general(a.astype(BF16), b.astype(BF16), (((1,), (1,)), ((), ())),
                           preferred_element_type=F32)


def _dot_tn(a, b):
    return lax.dot_general(a.astype(BF16), b.astype(BF16), (((0,), (0,)), ((), ())),
                           preferred_element_type=F32)


def _transpose_rows(x):
    t, n = x.shape
    if t < LANES:
        x = jnp.concatenate([x, jnp.zeros((LANES - t, n), x.dtype)], axis=0)
    return jnp.transpose(x)


def _row_to_col(r):
    return jnp.transpose(jnp.broadcast_to(r, (LANES, r.shape[1])))[:, 0:1]


def _cumsum_rows(x):
    t = x.shape[0]
    if t <= 2 * SUBLANES:
        rid = lax.broadcasted_iota(jnp.int32, (t, 1), 0)
        s = 1
        while s < t:
            x = x + jnp.where(rid >= s, pltpu.roll(x, s, axis=0), 0.0)
            s *= 2
        return x
    row = lax.broadcasted_iota(jnp.int32, (t, t), 0)
    col = lax.broadcasted_iota(jnp.int32, (t, t), 1)
    tri = jnp.where(row >= col, 1.0, 0.0).astype(BF16)
    hi = x.astype(BF16)
    r1 = x - hi.astype(F32)
    mid = r1.astype(BF16)
    lo = (r1 - mid.astype(F32)).astype(BF16)
    acc = jnp.dot(tri, lo, preferred_element_type=F32)
    acc = acc + jnp.dot(tri, mid, preferred_element_type=F32)
    return acc + jnp.dot(tri, hi, preferred_element_type=F32)


def _head_rms(x, nh):
    hd = x.shape[1] // nh
    parts = []
    for h in range(nh):
        xh = x[:, h * hd:(h + 1) * hd]
        parts.append(xh * lax.rsqrt(jnp.mean(xh * xh, axis=-1, keepdims=True) + EPS))
    return jnp.concatenate(parts, axis=-1)


def _group_loop(nb, group_fn):
    if nb <= 2:
        group_fn(list(range(nb)))
        return

    def body(i, carry):
        group_fn([2 * i, 2 * i + 1])
        return carry

    lax.fori_loop(0, nb // 2, body, 0)


def _norm_proj_kernel(n_lo, x_ref, g_ref, wm_ref, ws_ref, olo_ref, ohi_ref, os_ref, hn_ref):
    j = pl.program_id(1)

    @pl.when(j == 0)
    def _():
        x = x_ref[...]
        y = x * lax.rsqrt(jnp.mean(x * x, axis=-1, keepdims=True) + EPS) * g_ref[...]
        hn = y.astype(BF16)
        hn_ref[...] = hn
        os_ref[...] = jnp.dot(hn, ws_ref[...], preferred_element_type=F32)

    acc = jnp.dot(hn_ref[...], wm_ref[...], preferred_element_type=F32)

    @pl.when(j < n_lo)
    def _():
        olo_ref[...] = acc.astype(olo_ref.dtype)

    @pl.when(j >= n_lo)
    def _():
        ohi_ref[...] = acc


def _norm_proj(x2d, g, w_main, w_small, e_lo, tm):
    n, d = x2d.shape
    e = w_main.shape[1]
    es = w_small.shape[1]
    tn = COL_TILE
    n_lo = e_lo // tn
    return pl.pallas_call(
        functools.partial(_norm_proj_kernel, n_lo),
        grid=(n // tm, e // tn),
        in_specs=[
            pl.BlockSpec((tm, d), lambda i, j: (i, 0)),
            pl.BlockSpec((1, d), lambda i, j: (0, 0)),
            pl.BlockSpec((d, tn), lambda i, j: (0, j)),
            pl.BlockSpec((d, es), lambda i, j: (0, 0)),
        ],
        out_specs=[
            pl.BlockSpec((tm, tn), lambda i, j: (i, jnp.minimum(j, n_lo - 1))),
            pl.BlockSpec((tm, tn), lambda i, j: (i, jnp.maximum(j - n_lo, 0))),
            pl.BlockSpec((tm, es), lambda i, j: (i, 0)),
        ],
        out_shape=[
            jax.ShapeDtypeStruct((n, e_lo), BF16),
            jax.ShapeDtypeStruct((n, e - e_lo), F32),
            jax.ShapeDtypeStruct((n, es), F32),
        ],
        scratch_shapes=[pltpu.VMEM((tm, d), BF16)],
        compiler_params=_cparams(("parallel", "arbitrary")),
        name="norm_proj",
    )(x2d, g.reshape(1, d), w_main, w_small)


def _residual_norm(x_ref, g_ref, y_ref, out):
    nrm = out * lax.rsqrt(jnp.mean(out * out, axis=-1, keepdims=True) + EPS) * g_ref[...]
    y_ref[...] = x_ref[...] + nrm


def _out_even_kernel(x_ref, g_ref, h_ref, og_ref, z_ref, ng_ref, hl_ref, w1_ref, w2_ref, y_ref):
    hm = _sigmoid(og_ref[...]) * h_ref[...].astype(F32)
    hm = _head_rms(hm, N_HEADS) * ng_ref[...] * _silu(z_ref[...])
    out = jnp.dot(hm.astype(BF16), w1_ref[...], preferred_element_type=F32)
    out = out + jnp.dot(hl_ref[...], w2_ref[...], preferred_element_type=F32)
    _residual_norm(x_ref, g_ref, y_ref, out)


def _out_even(x2d, g, h_raw, rest, ng, hl, w1, w2, tm):
    n, d = x2d.shape
    row = lambda width, col: pl.BlockSpec((tm, width), lambda i: (i, col))
    const = lambda a: pl.BlockSpec(a.shape, lambda i: (0, 0))
    return pl.pallas_call(
        _out_even_kernel,
        grid=(n // tm,),
        in_specs=[row(d, 0), const(g), row(M_WIDTH, 0), row(M_WIDTH, 0), row(M_WIDTH, 1), const(ng),
                  row(LRU_WIDTH, 0), const(w1), const(w2)],
        out_specs=row(d, 0),
        out_shape=jax.ShapeDtypeStruct((n, d), F32),
        compiler_params=_cparams(("parallel",)),
        name="out_even",
    )(x2d, g, h_raw, rest, rest, ng, hl, w1, w2)


def _out_odd_kernel(x_ref, g_ref, o_ref, r_ref, ng_ref, w_ref, y_ref):
    o = _head_rms(o_ref[...].astype(F32), N_HEADS) * ng_ref[...] * _silu(r_ref[...])
    out = jnp.dot(o.astype(BF16), w_ref[...], preferred_element_type=F32)
    _residual_norm(x_ref, g_ref, y_ref, out)


def _out_odd(x2d, g, o_raw, r, ng, w, tm):
    n, d = x2d.shape
    row = lambda width: pl.BlockSpec((tm, width), lambda i: (i, 0))
    const = lambda a: pl.BlockSpec(a.shape, lambda i: (0, 0))
    return pl.pallas_call(
        _out_odd_kernel,
        grid=(n // tm,),
        in_specs=[row(d), const(g), row(G_VW), row(G_VW), const(ng), const(w)],
        out_specs=row(d),
        out_shape=jax.ShapeDtypeStruct((n, d), F32),
        compiler_params=_cparams(("parallel",)),
        name="out_odd",
    )(x2d, g, o_raw, r, ng, w)


def _mlstm_kernel(q_ref, k_ref, v_ref, gate_ref, gbias_ref, c0_ref, n0_ref, m0_ref,
                  h_ref, c_ref, n_ref, m_ref, *, nb, t, nc, l_valid):
    c = pl.program_id(1)
    if nc == 1:
        c_in, n_in, m_in = c0_ref, n0_ref, m0_ref
    else:
        c_in, n_in, m_in = c_ref, n_ref, m_ref

        @pl.when(c == 0)
        def _():
            c_ref[...] = c0_ref[...]
            n_ref[...] = n0_ref[...]
            m_ref[...] = m0_ref[...]

    pos = c * t + lax.broadcasted_iota(jnp.int32, (t, 1), 0)
    valid = pos < l_valid
    row = lax.broadcasted_iota(jnp.int32, (t, t), 0)
    col = lax.broadcasted_iota(jnp.int32, (t, t), 1)
    causal = row >= col
    lane = lax.broadcasted_iota(jnp.int32, (1, LANES), 1)
    gbias = gbias_ref[...]
    heads = [(h, slice(h * M_DK, (h + 1) * M_DK)) for h in range(N_HEADS)]

    def group(rows):
        pairs = [(i, r, h, sl) for i, r in enumerate(rows) for h, sl in heads]
        gates = []
        for r in rows:
            g = gate_ref[r] + gbias
            li = jnp.where(valid, g[:, :LANES], NEG)
            lf = jnp.where(valid, _log_sigmoid(g[:, LANES:]), 0.0)
            b = _cumsum_rows(lf)
            gates.append((li, b, _transpose_rows(li - b)[:, :t], m_in[r]))
        st = {}
        for i, r, h, sl in pairs:
            li, b, r_t, m_vec = gates[i]
            b_col = b[:, h:h + 1]
            inter = b_col + m_vec[:, h:h + 1]
            dmat = jnp.where(causal, b_col + r_t[h:h + 1, :], NEG)
            mt = jnp.maximum(inter, jnp.max(dmat, axis=-1, keepdims=True))
            k = k_ref[r, :, sl] * (M_DK ** -0.5)
            s = _dot_nt(q_ref[r, :, sl], k) * jnp.exp(dmat - mt)
            st[i, h] = (inter, mt, k, s)
        for i, r, h, sl in pairs:
            inter, mt, k, s = st[i, h]
            q = q_ref[r, :, sl]
            w_inter = jnp.exp(inter - mt)
            num = w_inter * _dot_nt(q, c_in[r, h]) + _dot(s, v_ref[r, :, sl])
            den = (w_inter * jnp.sum(q.astype(F32) * n_in[r, h], axis=-1, keepdims=True)
                   + jnp.sum(s, axis=-1, keepdims=True))
            h_ref[r, :, sl] = (num / jnp.maximum(jnp.abs(den), jnp.exp(-mt))).astype(h_ref.dtype)
        m_next = [gt[3] for gt in gates]
        for i, r, h, sl in pairs:
            li, b, _, _ = gates[i]
            inter, mt, k, _ = st[i, h]
            b_col = b[:, h:h + 1]
            m_new = mt[t - 1:t, :]
            w_c = jnp.exp(inter[t - 1:t, :] - m_new)
            w_k = jnp.exp(b_col[t - 1:t, :] - b_col + li[:, h:h + 1] - m_new)
            c_ref[r, h] = w_c * c_in[r, h] + _dot_tn(v_ref[r, :, sl].astype(F32) * w_k, k)
            n_ref[r, h] = w_c * n_in[r, h] + jnp.sum(k.astype(F32) * w_k, axis=0, keepdims=True)
            m_next[i] = jnp.where(lane == h, m_new, m_next[i])
        for i, r in enumerate(rows):
            m_ref[r] = m_next[i]

    _group_loop(nb, group)


def _mlstm(qkv3, gates3, row0, nseq, tiles, nc, l_valid, gbias, c0, n0, m0):
    nb, t = tiles
    assert nseq % nb == 0 and row0 % nb == 0
    wd = M_WIDTH
    r0 = row0 // nb
    seq = lambda col: pl.BlockSpec((nb, t, wd), lambda b, c: (r0 + b, c, col))
    st_c = pl.BlockSpec((nb, N_HEADS, M_DV, M_DK), lambda b, c: (b, 0, 0, 0))
    st_n = pl.BlockSpec((nb, N_HEADS, 1, M_DK), lambda b, c: (b, 0, 0, 0))
    st_m = pl.BlockSpec((nb, 1, LANES), lambda b, c: (b, 0, 0))
    kern = functools.partial(_mlstm_kernel, nb=nb, t=t, nc=nc, l_valid=l_valid)
    return pl.pallas_call(
        kern,
        grid=(nseq // nb, nc),
        in_specs=[
            seq(0), seq(1), seq(2),
            pl.BlockSpec((nb, t, 2 * LANES), lambda b, c: (r0 + b, c, 0)),
            pl.BlockSpec(gbias.shape, lambda b, c: (0, 0)), st_c, st_n, st_m,
        ],
        out_specs=[pl.BlockSpec((nb, t, wd), lambda b, c: (b, c, 0)), st_c, st_n, st_m],
        out_shape=[
            jax.ShapeDtypeStruct((nseq, t * nc, wd), BF16),
            jax.ShapeDtypeStruct((nseq, N_HEADS, M_DV, M_DK), F32),
            jax.ShapeDtypeStruct((nseq, N_HEADS, 1, M_DK), F32),
            jax.ShapeDtypeStruct((nseq, 1, LANES), F32),
        ],
        compiler_params=_cparams(("parallel", "arbitrary")),
        name="mlstm",
    )(qkv3, qkv3, qkv3, gates3, gbias, c0, n0, m0)


def _rglru_kernel(x_ref, gr_ref, cw_ref, cb_ref, wa_ref, ba_ref, wx_ref, bx_ref, lam_ref, h0_ref, conv0_ref,
                  hl_ref, hlast_ref, h_s, xs_s, *, nb, t, l_valid):
    c = pl.program_id(1)
    w = LRU_WIDTH
    ng = t // SUBLANES

    @pl.when(c == 0)
    def _():
        h_s[...] = h0_ref[...]
        xs_s[:, 0:SUBLANES, :] = conv0_ref[...]

    x = x_ref[...]
    xs_s[:, SUBLANES:SUBLANES + t, :] = x
    cw = cw_ref[...]
    xc = cb_ref[...] + cw[CONV_W - 1:CONV_W, :] * x
    for s in range(1, CONV_W):
        xc = xc + cw[CONV_W - 1 - s:CONV_W - s, :] * xs_s[:, SUBLANES - s:SUBLANES - s + t, :]
    xs_s[:, 0:SUBLANES, :] = x[:, t - SUBLANES:, :]

    xf = xc.reshape(nb * t, w)
    ra, ri = [], []
    for n in range(LRU_BLOCKS):
        xb = xf[:, n * LRU_BS:(n + 1) * LRU_BS].astype(BF16)
        ra.append(jnp.dot(xb, wa_ref[n], preferred_element_type=F32))
        ri.append(jnp.dot(xb, wx_ref[n], preferred_element_type=F32))
    r = _sigmoid(jnp.concatenate(ra, axis=-1) + ba_ref[...])
    i = _sigmoid(jnp.concatenate(ri, axis=-1) + bx_ref[...])
    log_a = -LRU_C * r * _softplus(-lam_ref[...])
    a = jnp.exp(log_a)
    u = jnp.sqrt(-jnp.tanh(log_a) * (1.0 + a * a)) * (i * xf)

    rid = lax.broadcasted_iota(jnp.int32, (1, SUBLANES, 1), 1)
    a = a.reshape(nb * ng, SUBLANES, w)
    u = u.reshape(nb * ng, SUBLANES, w)
    for s in (1, 2, 4):
        a_sh = jnp.where(rid >= s, pltpu.roll(a, s, axis=1), 1.0)
        u_sh = jnp.where(rid >= s, pltpu.roll(u, s, axis=1), 0.0)
        u = a * u_sh + u
        a = a * a_sh
    a = a.reshape(nb, ng, SUBLANES, w)
    u = u.reshape(nb, ng, SUBLANES, w)
    h = h_s[...]
    g_last, r_last = divmod((l_valid - 1) % t, SUBLANES)
    gr4 = gr_ref[...].reshape(nb, ng, SUBLANES, w)
    for g in range(ng):
        hg = a[:, g] * h + u[:, g]
        h = hg[:, SUBLANES - 1:SUBLANES]
        hl_ref[:, g * SUBLANES:(g + 1) * SUBLANES, :] = (hg * _silu(gr4[:, g])).astype(hl_ref.dtype)
        if g == g_last:
            @pl.when(c == (l_valid - 1) // t)
            def _():
                hlast_ref[...] = hg[:, r_last:r_last + 1]
    h_s[...] = h


def _rglru(rest3, row0, nseq, tiles, nc, l_valid, cw, cb, wa, ba, wx, bx, lam, h0, conv0):
    nb, t = tiles
    assert nseq % nb == 0 and row0 % nb == 0
    w = LRU_WIDTH
    r0 = row0 // nb
    full2 = lambda a: pl.BlockSpec(a.shape, lambda b, c: (0, 0))
    full3 = lambda a: pl.BlockSpec(a.shape, lambda b, c: (0, 0, 0))
    kern = functools.partial(_rglru_kernel, nb=nb, t=t, l_valid=l_valid)
    return pl.pallas_call(
        kern,
        grid=(nseq // nb, nc),
        in_specs=[
            pl.BlockSpec((nb, t, w), lambda b, c: (r0 + b, c, 2)),
            pl.BlockSpec((nb, t, w), lambda b, c: (r0 + b, c, 3)),
            full2(cw), full2(cb), full3(wa), full2(ba), full3(wx), full2(bx), full2(lam),
            pl.BlockSpec((nb, 1, w), lambda b, c: (b, 0, 0)),
            pl.BlockSpec((nb, SUBLANES, w), lambda b, c: (b, 0, 0)),
        ],
        out_specs=[
            pl.BlockSpec((nb, t, w), lambda b, c: (b, c, 0)),
            pl.BlockSpec((nb, 1, w), lambda b, c: (b, 0, 0)),
        ],
        out_shape=[
            jax.ShapeDtypeStruct((nseq, t * nc, w), BF16),
            jax.ShapeDtypeStruct((nseq, 1, w), F32),
        ],
        scratch_shapes=[pltpu.VMEM((nb, 1, w), F32), pltpu.VMEM((nb, SUBLANES + t, w), F32)],
        compiler_params=_cparams(("parallel", "arbitrary")),
        name="rglru",
    )(rest3, rest3, cw, cb, wa, ba, wx, bx, lam, h0, conv0)


def _gla_kernel(q_ref, k_ref, v_ref, al_ref, aup_ref, ab_ref, s0_ref, o_ref, s_ref, *, nb, t, nc, l_valid):
    c = pl.program_id(1)
    if nc == 1:
        s_in = s0_ref
    else:
        s_in = s_ref

        @pl.when(c == 0)
        def _():
            s_ref[...] = s0_ref[...]

    pos = c * t + lax.broadcasted_iota(jnp.int32, (t, 1), 0)
    valid = pos < l_valid
    row = lax.broadcasted_iota(jnp.int32, (t, t), 0)
    col = lax.broadcasted_iota(jnp.int32, (t, t), 1)
    causal = row >= col
    aup = aup_ref[...]
    ab = ab_ref[...]
    heads = [(h, slice(h * G_DK, (h + 1) * G_DK), slice(h * G_DV, (h + 1) * G_DV)) for h in range(N_HEADS)]

    def group(rows):
        pairs = [(i, r, h, ks, vs) for i, r in enumerate(rows) for h, ks, vs in heads]
        dec = []
        for r in rows:
            pre = _dot(al_ref[r], aup) + ab
            lg = jnp.where(valid, _log_sigmoid(pre) / G_TAU, 0.0)
            b = _cumsum_rows(lg)
            b_last = b[t - 1:t, :]
            k = jnp.where(valid, k_ref[r].astype(F32), 0.0)
            qd = ((q_ref[r].astype(F32) * (G_DK ** -0.5)) * jnp.exp(b)).astype(BF16)
            kn = (k * jnp.exp(-b)).astype(BF16)
            kr = (k * jnp.exp(b_last - b)).astype(BF16)
            dec.append((qd, kn, kr, _row_to_col(jnp.exp(b_last))))
        att = {}
        for i, r, h, ks, vs in pairs:
            qd, kn, _, _ = dec[i]
            att[i, h] = jnp.where(causal, _dot_nt(qd[:, ks], kn[:, ks]), 0.0).astype(BF16)
        for i, r, h, ks, vs in pairs:
            qd, _, kr, ebl_col = dec[i]
            s_h = s_in[r, h]
            v = v_ref[r, :, vs]
            o_ref[r, :, vs] = (_dot(qd[:, ks], s_h) + _dot(att[i, h], v)).astype(o_ref.dtype)
            s_ref[r, h] = ebl_col[ks, :] * s_h + _dot_tn(kr[:, ks], v)

    _group_loop(nb, group)


def _gla(qkv3, al3, row0, nseq, tiles, nc, l_valid, aup, ab, s0):
    nb, t = tiles
    assert nseq % nb == 0 and row0 % nb == 0
    r0 = row0 // nb
    full2 = lambda a: pl.BlockSpec(a.shape, lambda b, c: (0, 0))
    st = pl.BlockSpec((nb, N_HEADS, G_DK, G_DV), lambda b, c: (b, 0, 0, 0))
    kern = functools.partial(_gla_kernel, nb=nb, t=t, nc=nc, l_valid=l_valid)
    return pl.pallas_call(
        kern,
        grid=(nseq // nb, nc),
        in_specs=[
            pl.BlockSpec((nb, t, G_KW), lambda b, c: (r0 + b, c, 0)),
            pl.BlockSpec((nb, t, G_KW), lambda b, c: (r0 + b, c, 1)),
            pl.BlockSpec((nb, t, G_VW), lambda b, c: (r0 + b, c, 1)),
            pl.BlockSpec((nb, t, LANES), lambda b, c: (r0 + b, c, 0)),
            full2(aup), full2(ab), st,
        ],
        out_specs=[pl.BlockSpec((nb, t, G_VW), lambda b, c: (b, c, 0)), st],
        out_shape=[
            jax.ShapeDtypeStruct((nseq, t * nc, G_VW), BF16),
            jax.ShapeDtypeStruct((nseq, N_HEADS, G_DK, G_DV), F32),
        ],
        compiler_params=_cparams(("parallel", "arbitrary")),
        name="gla",
    )(qkv3, qkv3, qkv3, al3, aup, ab, s0)


def _gla_layer_kernel(x_ref, pre_ref, post_ref, win_ref, wlow_ref, aup_ref, ab_ref, ng_ref, wout_ref, s0_ref,
                      y_ref, s_ref, qkv_s, r_s, al_s, hn_s, o_s, xkeep_s, *, rows, tiles_per_seq):
    g = pl.program_id(0)
    slot = g % 2
    prev = 1 - slot
    t = GLA_TILES["main"][1]
    e_lo = 2 * G_KW + G_VW
    e = e_lo + G_VW
    n_chunks = rows // t
    cols_per_chunk = e // n_chunks

    @pl.when(g == 0)
    def _():
        qkv_s[1] = jnp.zeros(qkv_s.shape[1:], qkv_s.dtype)
        r_s[1] = jnp.zeros(r_s.shape[1:], r_s.dtype)
        al_s[1] = jnp.zeros(al_s.shape[1:], al_s.dtype)
        xkeep_s[...] = jnp.zeros(xkeep_s.shape, xkeep_s.dtype)

    @pl.when(jnp.maximum(g - 1, 0) % tiles_per_seq == 0)
    def _():
        s_ref[...] = s0_ref[...]

    x = x_ref[0]
    hn_s[...] = (x * lax.rsqrt(jnp.mean(x * x, axis=-1, keepdims=True) + EPS) * pre_ref[...]).astype(BF16)
    al_s[slot] = jnp.dot(hn_s[...], wlow_ref[...], preferred_element_type=F32)

    row = lax.broadcasted_iota(jnp.int32, (t, t), 0)
    col = lax.broadcasted_iota(jnp.int32, (t, t), 1)
    causal = row >= col
    aup = aup_ref[...]
    ab = ab_ref[...]
    heads = [(h, slice(h * G_DK, (h + 1) * G_DK), slice(h * G_DV, (h + 1) * G_DV)) for h in range(N_HEADS)]

    def project(c0, c1):
        acc = jnp.dot(hn_s[...], win_ref[:, c0:c1], preferred_element_type=F32)
        if c0 < e_lo:
            hi = min(c1, e_lo)
            qkv_s[slot, :, c0:hi] = acc[:, :hi - c0].astype(BF16)
        if c1 > e_lo:
            lo = max(c0, e_lo)
            r_s[slot, :, lo - e_lo:c1 - e_lo] = acc[:, lo - c0:]

    n_sub = 3
    sub = cols_per_chunk // n_sub
    for j in range(n_chunks):
        c0 = j * cols_per_chunk
        rs = slice(j * t, (j + 1) * t)
        pre = _dot(al_s[prev, rs, :], aup) + ab
        project(c0, c0 + sub)
        b = _cumsum_rows(_log_sigmoid(pre) / G_TAU)
        b_last = b[t - 1:t, :]
        project(c0 + sub, c0 + 2 * sub)
        k = qkv_s[prev, rs, G_KW:2 * G_KW].astype(F32)
        qd = ((qkv_s[prev, rs, 0:G_KW].astype(F32) * (G_DK ** -0.5)) * jnp.exp(b)).astype(BF16)
        kn = (k * jnp.exp(-b)).astype(BF16)
        kr = (k * jnp.exp(b_last - b)).astype(BF16)
        ebl_col = _row_to_col(jnp.exp(b_last))
        att = [jnp.where(causal, _dot_nt(qd[:, ks], kn[:, ks]), 0.0).astype(BF16) for _, ks, _ in heads]
        project(c0 + 2 * sub, c0 + cols_per_chunk)
        for h, ks, vs in heads:
            s_h = s_ref[0, h]
            v = qkv_s[prev, rs, 2 * G_KW + h * G_DV:2 * G_KW + (h + 1) * G_DV]
            o_s[rs, vs] = (_dot(qd[:, ks], s_h) + _dot(att[h], v)).astype(o_s.dtype)
            s_ref[0, h] = ebl_col[ks, :] * s_h + _dot_tn(kr[:, ks], v)

    o = _head_rms(o_s[...].astype(F32), N_HEADS) * ng_ref[...] * _silu(r_s[prev])
    out = jnp.dot(o.astype(BF16), wout_ref[...], preferred_element_type=F32)
    nrm = out * lax.rsqrt(jnp.mean(out * out, axis=-1, keepdims=True) + EPS) * post_ref[...]
    y_ref[0] = xkeep_s[...] + nrm
    xkeep_s[...] = x


def _gla_layer(x3, pre, post, w_in, w_low, aup, ab, ng, w_out, s0, rows):
    nseq, seq, d = x3.shape
    assert seq % rows == 0 and rows % GLA_TILES["main"][1] == 0
    tps = seq // rows
    n_tiles = nseq * tps
    e = w_in.shape[1]
    e_lo = 2 * G_KW + G_VW
    const = lambda a: pl.BlockSpec(a.shape, lambda g: (0,) * a.ndim, pipeline_mode=pl.Buffered(1))
    cur = lambda g: jnp.minimum(g, n_tiles - 1)
    old = lambda g: jnp.maximum(g - 1, 0)
    st = pl.BlockSpec((1, N_HEADS, G_DK, G_DV), lambda g: (old(g) // tps, 0, 0, 0))
    kern = functools.partial(_gla_layer_kernel, rows=rows, tiles_per_seq=tps)
    return pl.pallas_call(
        kern,
        grid=(n_tiles + 1,),
        in_specs=[
            pl.BlockSpec((1, rows, d), lambda g: (cur(g) // tps, cur(g) % tps, 0)),
            const(pre), const(post), const(w_in), const(w_low), const(aup), const(ab), const(ng), const(w_out), st,
        ],
        out_specs=[pl.BlockSpec((1, rows, d), lambda g: (old(g) // tps, old(g) % tps, 0)), st],
        out_shape=[
            jax.ShapeDtypeStruct((nseq, seq, d), F32),
            jax.ShapeDtypeStruct((nseq, N_HEADS, G_DK, G_DV), F32),
        ],
        scratch_shapes=[
            pltpu.VMEM((2, rows, e_lo), BF16),
            pltpu.VMEM((2, rows, e - e_lo), F32),
            pltpu.VMEM((2, rows, LANES), F32),
            pltpu.VMEM((rows, d), BF16),
            pltpu.VMEM((rows, G_VW), BF16),
            pltpu.VMEM((rows, d), F32),
        ],
        compiler_params=_cparams(("arbitrary",)),
        name="gla_layer",
    )(x3, pre, post, w_in, w_low, aup, ab, ng, w_out, s0)


def _pad_lanes(a, width):
    return jnp.pad(a, [(0, 0)] * (a.ndim - 1) + [(0, width - a.shape[-1])])


def _even_layer(xm, xs, nbp, seq, nbs, ls, st, pre_g, post_g, w_in, w_out, b_i, b_f, m_norm_g,
                conv_w, conv_b, wa, ba, wx, bx, lam):
    c0s, n0s, m0s, h0s, conv0s = st
    wq, wk, wv, wog, wig, wfg, wz, wxr, wgr = jnp.split(
        w_in, [1024, 2048, 3072, 4096, 4100, 4104, 5128, 6152], axis=1)
    w_main = jnp.concatenate([wq, wk, wv, wog, wz, wxr, wgr], axis=1).astype(BF16)
    w_gate = jnp.concatenate([_pad_lanes(wig, LANES), _pad_lanes(wfg, LANES)], axis=1).astype(BF16)
    gbias = jnp.concatenate([_pad_lanes(b_i[None], LANES), _pad_lanes(b_f[None], LANES)], axis=1)
    ng = m_norm_g[None]
    post = post_g[None]
    cw, cb = conv_w, conv_b[None]
    wab, wxb = wa.astype(BF16), wx.astype(BF16)
    bav, bxv, lamv = ba[None], bx[None], lam[None]
    w_o1, w_o2 = w_out[:M_WIDTH].astype(BF16), w_out[M_WIDTH:].astype(BF16)
    e_lo = 3 * M_WIDTH
    e_hi = w_main.shape[1] - e_lo
    lru = (cw, cb, wab, bav, wxb, bxv, lamv)

    qm, pm, gm = _norm_proj(xm, pre_g, w_main, w_gate, e_lo, PROJ_ROW_TILE)
    qs, ps, gs = _norm_proj(xs, pre_g, w_main, w_gate, e_lo, xs.shape[0])
    n_meta_rows = nbp * N_META
    xr_cols = slice(2 * LRU_WIDTH, 3 * LRU_WIDTH)

    zc = jnp.zeros((nbp, N_HEADS, M_DV, M_DK), F32)
    zn = jnp.zeros((nbp, N_HEADS, 1, M_DK), F32)
    zm = jnp.zeros((nbp, 1, LANES), F32)
    zh = jnp.zeros((nbp, 1, LRU_WIDTH), F32)
    zconv = jnp.zeros((nbp, SUBLANES, LRU_WIDTH), F32)
    qs_meta = qs.reshape(-1, N_META, e_lo)
    ps_meta = ps.reshape(-1, N_META, e_hi)
    gs_meta = gs.reshape(-1, N_META, 2 * LANES)
    h_meta, c1, n1, m1 = _mlstm(qs_meta, gs_meta, 0, nbp, MLSTM_TILES["meta"], 1, N_META, gbias, zc, zn, zm)
    hl_meta, h1 = _rglru(ps_meta, 0, nbp, LRU_TILES["meta"], 1, N_META, *lru, zh, zconv)
    conv1 = ps_meta[:nbp, N_META - SUBLANES:, xr_cols]

    qm3 = qm.reshape(nbp, seq, e_lo)
    pm3 = pm.reshape(nbp, seq, e_hi)
    gm3 = gm.reshape(nbp, seq, 2 * LANES)
    tm_ = MLSTM_TILES["main"][1]
    h_main, pc, pn, pmm = _mlstm(qm3, gm3, 0, nbp, MLSTM_TILES["main"], seq // tm_, seq, gbias, c1, n1, m1)
    tl = LRU_TILES["main"][1]
    hl_main, ph = _rglru(pm3, 0, nbp, LRU_TILES["main"], seq // tl, seq, *lru, h1, conv1)
    pconv = pm3[:, seq - (CONV_W - 1):, xr_cols]

    qs_s = qs.reshape(-1, SAMPLE_PAD, e_lo)
    ps_s = ps.reshape(-1, SAMPLE_PAD, e_hi)
    gs_s = gs.reshape(-1, SAMPLE_PAD, 2 * LANES)
    r0 = n_meta_rows // SAMPLE_PAD
    h_s, sc, sn, sm = _mlstm(qs_s, gs_s, r0, nbs, MLSTM_TILES["sample"], 1, ls, gbias,
                             c0s, n0s[:, :, None, :], _pad_lanes(m0s, LANES)[:, None, :])
    conv0p = jnp.pad(conv0s, ((0, 0), (SUBLANES - (CONV_W - 1), 0), (0, 0)))
    hl_s, sh = _rglru(ps_s, r0, nbs, LRU_TILES["sample"], 1, ls, *lru, h0s[:, None, :], conv0p)
    xr_s = ps_s[r0:, :ls, xr_cols]
    sconv = jnp.concatenate([conv0s, xr_s], axis=1)[:, -(CONV_W - 1):]

    ym = _out_even(xm, post, h_main.reshape(-1, M_WIDTH), pm, ng, hl_main.reshape(-1, LRU_WIDTH), w_o1, w_o2,
                   OUT_ROW_TILE)
    h_small = jnp.concatenate([h_meta.reshape(-1, M_WIDTH), h_s.reshape(-1, M_WIDTH)], axis=0)
    hl_small = jnp.concatenate([hl_meta.reshape(-1, LRU_WIDTH), hl_s.reshape(-1, LRU_WIDTH)], axis=0)
    ys = _out_even(xs, post, h_small, ps, ng, hl_small, w_o1, w_o2, _row_tile(xs.shape[0], OUT_ROW_TILE))

    p_state = (pc, pn[:, :, 0, :], pmm[:, 0, :N_HEADS], ph[:, 0, :], pconv)
    s_state = (sc, sn[:, :, 0, :], sm[:, 0, :N_HEADS], sh[:, 0, :], sconv)
    return ym, ys, p_state, s_state


def _odd_layer(xm, xs, nbp, seq, nbs, ls, s0s, pre_g, post_g, w_in, w_out, a_up, a_b, g_norm_g):
    e_lo = 2 * G_KW + G_VW
    w_main = w_in[:, :2 * G_KW + 2 * G_VW].astype(BF16)
    w_low = _pad_lanes(w_in[:, 2 * G_KW + 2 * G_VW:], LANES).astype(BF16)
    aup = jnp.pad(a_up, ((0, LANES - G_RANK), (0, 0))).astype(BF16)
    ab = a_b[None]
    ng = g_norm_g[None]
    post = post_g[None]
    w_o = w_out.astype(BF16)

    qs, rs, as_ = _norm_proj(xs, pre_g, w_main, w_low, e_lo, xs.shape[0])
    n_meta_rows = nbp * N_META

    zs = jnp.zeros((nbp, N_HEADS, G_DK, G_DV), F32)
    o_meta, s1 = _gla(qs.reshape(-1, N_META, e_lo), as_.reshape(-1, N_META, LANES),
                      0, nbp, GLA_TILES["meta"], 1, N_META, aup, ab, zs)
    ym3, p_s = _gla_layer(xm.reshape(nbp, seq, -1), pre_g[None], post, w_main, w_low, aup, ab, ng, w_o, s1,
                          FUSED_ROW_TILE)
    ym = ym3.reshape(xm.shape)
    r0 = n_meta_rows // SAMPLE_PAD
    o_s, s_s = _gla(qs.reshape(-1, SAMPLE_PAD, e_lo), as_.reshape(-1, SAMPLE_PAD, LANES),
                    r0, nbs, GLA_TILES["sample"], 1, ls, aup, ab, s0s)

    o_small = jnp.concatenate([o_meta.reshape(-1, G_VW), o_s.reshape(-1, G_VW)], axis=0)
    ys = _out_odd(xs, post, o_small, rs, ng, w_o, _row_tile(xs.shape[0], OUT_ROW_TILE))
    return ym, ys, p_s, s_s


def kernel(x_prompt, x_sample, state_mlstm_C, state_mlstm_n, state_mlstm_m, state_rglru_h, state_rglru_conv,
           state_gla_S, meta_tokens, pre_norm_a, post_norm_a, w_in_a, w_out_a, mlstm_b_i, mlstm_b_f, mlstm_norm,
           conv_w, conv_b, lru_w_a, lru_b_a, lru_w_x, lru_b_x, lru_lambda, pre_norm_c, post_norm_c, w_in_c,
           w_out_c, gla_alpha_up, gla_alpha_b, gla_norm):
    nbp, seq, d = x_prompt.shape
    nbs, ls, _ = x_sample.shape
    depth = pre_norm_a.shape[0] + pre_norm_c.shape[0]
    assert ls >= CONV_W - 1 and ls <= SAMPLE_PAD and N_META % SUBLANES == 0

    xm = x_prompt.reshape(nbp * seq, d)
    meta = jnp.broadcast_to(meta_tokens[None].astype(x_prompt.dtype), (nbp, N_META, d)).reshape(nbp * N_META, d)
    xs_pad = jnp.pad(x_sample, ((0, 0), (0, SAMPLE_PAD - ls), (0, 0))).reshape(nbs * SAMPLE_PAD, d)
    xs = jnp.concatenate([meta, xs_pad], axis=0)

    p_lists = [[] for _ in range(6)]
    s_lists = [[] for _ in range(6)]
    for layer in range(depth):
        j = layer // 2
        if layer % 2 == 0:
            st = (state_mlstm_C[j], state_mlstm_n[j], state_mlstm_m[j], state_rglru_h[j], state_rglru_conv[j])
            xm, xs, pst, sst = _even_layer(
                xm, xs, nbp, seq, nbs, ls, st, pre_norm_a[j], post_norm_a[j], w_in_a[j], w_out_a[j],
                mlstm_b_i[j], mlstm_b_f[j], mlstm_norm[j], conv_w[j], conv_b[j], lru_w_a[j], lru_b_a[j],
                lru_w_x[j], lru_b_x[j], lru_lambda[j])
            for i in range(5):
                p_lists[i].append(pst[i])
                s_lists[i].append(sst[i])
        else:
            xm, xs, p_s, s_s = _odd_layer(
                xm, xs, nbp, seq, nbs, ls, state_gla_S[j], pre_norm_c[j], post_norm_c[j], w_in_c[j], w_out_c[j],
                gla_alpha_up[j], gla_alpha_b[j], gla_norm[j])
            p_lists[5].append(p_s)
            s_lists[5].append(s_s)

    y_prompt = xm.reshape(nbp, seq, d)
    y_sample = xs[nbp * N_META:].reshape(nbs, SAMPLE_PAD, d)[:, :ls]
    return (y_prompt, y_sample) + tuple(jnp.stack(l) for l in p_lists) + tuple(jnp.stack(l) for l in s_lists)
```

```python
import functools

import jax
import jax.numpy as jnp
from jax import lax
from jax.experimental import pallas as pl
from jax.experimental.pallas import tpu as pltpu

F32 = jnp.float32
BF16 = jnp.bfloat16

D_MODEL = 1024
N_META = 16
EPS = 1e-6
N_HEADS = 4
M_DK = 256
M_DV = 256
M_WIDTH = N_HEADS * M_DV
LRU_WIDTH = 1024
LRU_BLOCKS = 8
LRU_BS = LRU_WIDTH // LRU_BLOCKS
CONV_W = 4
LRU_C = 8.0
G_DK = 256
G_DV = 512
G_KW = N_HEADS * G_DK
G_VW = N_HEADS * G_DV
G_RANK = 16
G_TAU = 16.0

LANES = 128
SUBLANES = 8
NEG = -1e30
SAMPLE_PAD = SUBLANES
VMEM_LIMIT = 56 * 1024 * 1024
COL_TILE = 1024

MLSTM_TILES = {"meta": (8, N_META), "main": (4, 128), "sample": (4, SAMPLE_PAD)}
GLA_TILES = {"meta": (8, N_META), "main": (2, 64), "sample": (4, SAMPLE_PAD)}
LRU_TILES = {"meta": (8, N_META), "main": (1, 256), "sample": (16, SAMPLE_PAD)}
PROJ_ROW_TILE = 1024
OUT_ROW_TILE = 512
FUSED_ROW_TILE = 256


def _row_tile(n, cap):
    return next(tm for tm in range(min(n, cap) // SUBLANES * SUBLANES, 0, -SUBLANES) if n % tm == 0)


def _cparams(sem):
    return pltpu.CompilerParams(dimension_semantics=sem, vmem_limit_bytes=VMEM_LIMIT)


def _sigmoid(x):
    return 0.5 * jnp.tanh(0.5 * x) + 0.5


def _silu(x):
    return x * _sigmoid(x)


def _log_sigmoid(x):
    return jnp.minimum(x, 0.0) - jnp.log1p(jnp.exp(-jnp.abs(x)))


def _softplus(x):
    return jnp.maximum(x, 0.0) + jnp.log1p(jnp.exp(-jnp.abs(x)))


def _dot(a, b):
    return jnp.dot(a.astype(BF16), b.astype(BF16), preferred_element_type=F32)


def _dot_nt(a, b):
    return lax.dot_general(a.astype(BF16), b.astype(BF16), (((1,), (1,)), ((), ())),
                           preferred_element_type=F32)


def _dot_tn(a, b):
    return lax.dot_general(a.astype(BF16), b.astype(BF16), (((0,), (0,)), ((), ())),
                           preferred_element_type=F32)


def _transpose_rows(x):
    t, n = x.shape
    if t < LANES:
        x = jnp.concatenate([x, jnp.zeros((LANES - t, n), x.dtype)], axis=0)
    return jnp.transpose(x)


def _row_to_col(r):
    return jnp.transpose(jnp.broadcast_to(r, (LANES, r.shape[1])))[:, 0:1]


def _cumsum_rows(x):
    t = x.shape[0]
    if t <= 2 * SUBLANES:
        rid = lax.broadcasted_iota(jnp.int32, (t, 1), 0)
        s = 1
        while s < t:
            x = x + jnp.where(rid >= s, pltpu.roll(x, s, axis=0), 0.0)
            s *= 2
        return x
    row = lax.broadcasted_iota(jnp.int32, (t, t), 0)
    col = lax.broadcasted_iota(jnp.int32, (t, t), 1)
    tri = jnp.where(row >= col, 1.0, 0.0).astype(BF16)
    hi = x.astype(BF16)
    r1 = x - hi.astype(F32)
    mid = r1.astype(BF16)
    lo = (r1 - mid.astype(F32)).astype(BF16)
    acc = jnp.dot(tri, lo, preferred_element_type=F32)
    acc = acc + jnp.dot(tri, mid, preferred_element_type=F32)
    return acc + jnp.dot(tri, hi, preferred_element_type=F32)


def _head_rms(x, nh):
    hd = x.shape[1] // nh
    parts = []
    for h in range(nh):
        xh = x[:, h * hd:(h + 1) * hd]
        parts.append(xh * lax.rsqrt(jnp.mean(xh * xh, axis=-1, keepdims=True) + EPS))
    return jnp.concatenate(parts, axis=-1)


def _group_loop(nb, group_fn):
    if nb <= 2:
        group_fn(list(range(nb)))
        return

    def body(i, carry):
        group_fn([2 * i, 2 * i + 1])
        return carry

    lax.fori_loop(0, nb // 2, body, 0)


def _norm_proj_kernel(n_lo, x_ref, g_ref, wm_ref, ws_ref, olo_ref, ohi_ref, os_ref, hn_ref):
    j = pl.program_id(1)

    @pl.when(j == 0)
    def _():
        x = x_ref[...]
        y = x * lax.rsqrt(jnp.mean(x * x, axis=-1, keepdims=True) + EPS) * g_ref[...]
        hn = y.astype(BF16)
        hn_ref[...] = hn
        os_ref[...] = jnp.dot(hn, ws_ref[...], preferred_element_type=F32)

    acc = jnp.dot(hn_ref[...], wm_ref[...], preferred_element_type=F32)

    @pl.when(j < n_lo)
    def _():
        olo_ref[...] = acc.astype(olo_ref.dtype)

    @pl.when(j >= n_lo)
    def _():
        ohi_ref[...] = acc


def _norm_proj(x2d, g, w_main, w_small, e_lo, tm):
    n, d = x2d.shape
    e = w_main.shape[1]
    es = w_small.shape[1]
    tn = COL_TILE
    n_lo = e_lo // tn
    return pl.pallas_call(
        functools.partial(_norm_proj_kernel, n_lo),
        grid=(n // tm, e // tn),
        in_specs=[
            pl.BlockSpec((tm, d), lambda i, j: (i, 0)),
            pl.BlockSpec((1, d), lambda i, j: (0, 0)),
            pl.BlockSpec((d, tn), lambda i, j: (0, j)),
            pl.BlockSpec((d, es), lambda i, j: (0, 0)),
        ],
        out_specs=[
            pl.BlockSpec((tm, tn), lambda i, j: (i, jnp.minimum(j, n_lo - 1))),
            pl.BlockSpec((tm, tn), lambda i, j: (i, jnp.maximum(j - n_lo, 0))),
            pl.BlockSpec((tm, es), lambda i, j: (i, 0)),
        ],
        out_shape=[
            jax.ShapeDtypeStruct((n, e_lo), BF16),
            jax.ShapeDtypeStruct((n, e - e_lo), F32),
            jax.ShapeDtypeStruct((n, es), F32),
        ],
        scratch_shapes=[pltpu.VMEM((tm, d), BF16)],
        compiler_params=_cparams(("parallel", "arbitrary")),
        name="norm_proj",
    )(x2d, g.reshape(1, d), w_main, w_small)


def _residual_norm(x_ref, g_ref, y_ref, out):
    nrm = out * lax.rsqrt(jnp.mean(out * out, axis=-1, keepdims=True) + EPS) * g_ref[...]
    y_ref[...] = x_ref[...] + nrm


def _out_even_kernel(x_ref, g_ref, h_ref, og_ref, z_ref, ng_ref, hl_ref, w1_ref, w2_ref, y_ref):
    hm = _sigmoid(og_ref[...]) * h_ref[...].astype(F32)
    hm = _head_rms(hm, N_HEADS) * ng_ref[...] * _silu(z_ref[...])
    out = jnp.dot(hm.astype(BF16), w1_ref[...], preferred_element_type=F32)
    out = out + jnp.dot(hl_ref[...], w2_ref[...], preferred_element_type=F32)
    _residual_norm(x_ref, g_ref, y_ref, out)


def _out_even(x2d, g, h_raw, rest, ng, hl, w1, w2, tm):
    n, d = x2d.shape
    row = lambda width, col: pl.BlockSpec((tm, width), lambda i: (i, col))
    const = lambda a: pl.BlockSpec(a.shape, lambda i: (0, 0))
    return pl.pallas_call(
        _out_even_kernel,
        grid=(n // tm,),
        in_specs=[row(d, 0), const(g), row(M_WIDTH, 0), row(M_WIDTH, 0), row(M_WIDTH, 1), const(ng),
                  row(LRU_WIDTH, 0), const(w1), const(w2)],
        out_specs=row(d, 0),
        out_shape=jax.ShapeDtypeStruct((n, d), F32),
        compiler_params=_cparams(("parallel",)),
        name="out_even",
    )(x2d, g, h_raw, rest, rest, ng, hl, w1, w2)


def _out_odd_kernel(x_ref, g_ref, o_ref, r_ref, ng_ref, w_ref, y_ref):
    o = _head_rms(o_ref[...].astype(F32), N_HEADS) * ng_ref[...] * _silu(r_ref[...])
    out = jnp.dot(o.astype(BF16), w_ref[...], preferred_element_type=F32)
    _residual_norm(x_ref, g_ref, y_ref, out)


def _out_odd(x2d, g, o_raw, r, ng, w, tm):
    n, d = x2d.shape
    row = lambda width: pl.BlockSpec((tm, width), lambda i: (i, 0))
    const = lambda a: pl.BlockSpec(a.shape, lambda i: (0, 0))
    return pl.pallas_call(
        _out_odd_kernel,
        grid=(n // tm,),
        in_specs=[row(d), const(g), row(G_VW), row(G_VW), const(ng), const(w)],
        out_specs=row(d),
        out_shape=jax.ShapeDtypeStruct((n, d), F32),
        compiler_params=_cparams(("parallel",)),
        name="out_odd",
    )(x2d, g, o_raw, r, ng, w)


def _mlstm_kernel(q_ref, k_ref, v_ref, gate_ref, gbias_ref, c0_ref, n0_ref, m0_ref,
                  h_ref, c_ref, n_ref, m_ref, *, nb, t, nc, l_valid):
    c = pl.program_id(1)
    if nc == 1:
        c_in, n_in, m_in = c0_ref, n0_ref, m0_ref
    else:
        c_in, n_in, m_in = c_ref, n_ref, m_ref

        @pl.when(c == 0)
        def _():
            c_ref[...] = c0_ref[...]
            n_ref[...] = n0_ref[...]
            m_ref[...] = m0_ref[...]

    pos = c * t + lax.broadcasted_iota(jnp.int32, (t, 1), 0)
    valid = pos < l_valid
    row = lax.broadcasted_iota(jnp.int32, (t, t), 0)
    col = lax.broadcasted_iota(jnp.int32, (t, t), 1)
    causal = row >= col
    lane = lax.broadcasted_iota(jnp.int32, (1, LANES), 1)
    gbias = gbias_ref[...]
    heads = [(h, slice(h * M_DK, (h + 1) * M_DK)) for h in range(N_HEADS)]

    def group(rows):
        pairs = [(i, r, h, sl) for i, r in enumerate(rows) for h, sl in heads]
        gates = []
        for r in rows:
            g = gate_ref[r] + gbias
            li = jnp.where(valid, g[:, :LANES], NEG)
            lf = jnp.where(valid, _log_sigmoid(g[:, LANES:]), 0.0)
            b = _cumsum_rows(lf)
            gates.append((li, b, _transpose_rows(li - b)[:, :t], m_in[r]))
        st = {}
        for i, r, h, sl in pairs:
            li, b, r_t, m_vec = gates[i]
            b_col = b[:, h:h + 1]
            inter = b_col + m_vec[:, h:h + 1]
            dmat = jnp.where(causal, b_col + r_t[h:h + 1, :], NEG)
            mt = jnp.maximum(inter, jnp.max(dmat, axis=-1, keepdims=True))
            k = k_ref[r, :, sl] * (M_DK ** -0.5)
            s = _dot_nt(q_ref[r, :, sl], k) * jnp.exp(dmat - mt)
            st[i, h] = (inter, mt, k, s)
        for i, r, h, sl in pairs:
            inter, mt, k, s = st[i, h]
            q = q_ref[r, :, sl]
            w_inter = jnp.exp(inter - mt)
            num = w_inter * _dot_nt(q, c_in[r, h]) + _dot(s, v_ref[r, :, sl])
            den = (w_inter * jnp.sum(q.astype(F32) * n_in[r, h], axis=-1, keepdims=True)
                   + jnp.sum(s, axis=-1, keepdims=True))
            h_ref[r, :, sl] = (num / jnp.maximum(jnp.abs(den), jnp.exp(-mt))).astype(h_ref.dtype)
        m_next = [gt[3] for gt in gates]
        for i, r, h, sl in pairs:
            li, b, _, _ = gates[i]
            inter, mt, k, _ = st[i, h]
            b_col = b[:, h:h + 1]
            m_new = mt[t - 1:t, :]
            w_c = jnp.exp(inter[t - 1:t, :] - m_new)
            w_k = jnp.exp(b_col[t - 1:t, :] - b_col + li[:, h:h + 1] - m_new)
            c_ref[r, h] = w_c * c_in[r, h] + _dot_tn(v_ref[r, :, sl].astype(F32) * w_k, k)
            n_ref[r, h] = w_c * n_in[r, h] + jnp.sum(k.astype(F32) * w_k, axis=0, keepdims=True)
            m_next[i] = jnp.where(lane == h, m_new, m_next[i])
        for i, r in enumerate(rows):
            m_ref[r] = m_next[i]

    _group_loop(nb, group)


def _mlstm(qkv3, gates3, row0, nseq, tiles, nc, l_valid, gbias, c0, n0, m0):
    nb, t = tiles
    assert nseq % nb == 0 and row0 % nb == 0
    wd = M_WIDTH
    r0 = row0 // nb
    seq = lambda col: pl.BlockSpec((nb, t, wd), lambda b, c: (r0 + b, c, col))
    st_c = pl.BlockSpec((nb, N_HEADS, M_DV, M_DK), lambda b, c: (b, 0, 0, 0))
    st_n = pl.BlockSpec((nb, N_HEADS, 1, M_DK), lambda b, c: (b, 0, 0, 0))
    st_m = pl.BlockSpec((nb, 1, LANES), lambda b, c: (b, 0, 0))
    kern = functools.partial(_mlstm_kernel, nb=nb, t=t, nc=nc, l_valid=l_valid)
    return pl.pallas_call(
        kern,
        grid=(nseq // nb, nc),
        in_specs=[
            seq(0), seq(1), seq(2),
            pl.BlockSpec((nb, t, 2 * LANES), lambda b, c: (r0 + b, c, 0)),
            pl.BlockSpec(gbias.shape, lambda b, c: (0, 0)), st_c, st_n, st_m,
        ],
        out_specs=[pl.BlockSpec((nb, t, wd), lambda b, c: (b, c, 0)), st_c, st_n, st_m],
        out_shape=[
            jax.ShapeDtypeStruct((nseq, t * nc, wd), BF16),
            jax.ShapeDtypeStruct((nseq, N_HEADS, M_DV, M_DK), F32),
            jax.ShapeDtypeStruct((nseq, N_HEADS, 1, M_DK), F32),
            jax.ShapeDtypeStruct((nseq, 1, LANES), F32),
        ],
        compiler_params=_cparams(("parallel", "arbitrary")),
        name="mlstm",
    )(qkv3, qkv3, qkv3, gates3, gbias, c0, n0, m0)


def _rglru_kernel(x_ref, gr_ref, cw_ref, cb_ref, wa_ref, ba_ref, wx_ref, bx_ref, lam_ref, h0_ref, conv0_ref,
                  hl_ref, hlast_ref, h_s, xs_s, *, nb, t, l_valid):
    c = pl.program_id(1)
    w = LRU_WIDTH
    ng = t // SUBLANES

    @pl.when(c == 0)
    def _():
        h_s[...] = h0_ref[...]
        xs_s[:, 0:SUBLANES, :] = conv0_ref[...]

    x = x_ref[...]
    xs_s[:, SUBLANES:SUBLANES + t, :] = x
    cw = cw_ref[...]
    xc = cb_ref[...] + cw[CONV_W - 1:CONV_W, :] * x
    for s in range(1, CONV_W):
        xc = xc + cw[CONV_W - 1 - s:CONV_W - s, :] * xs_s[:, SUBLANES - s:SUBLANES - s + t, :]
    xs_s[:, 0:SUBLANES, :] = x[:, t - SUBLANES:, :]

    xf = xc.reshape(nb * t, w)
    ra, ri = [], []
    for n in range(LRU_BLOCKS):
        xb = xf[:, n * LRU_BS:(n + 1) * LRU_BS].astype(BF16)
        ra.append(jnp.dot(xb, wa_ref[n], preferred_element_type=F32))
        ri.append(jnp.dot(xb, wx_ref[n], preferred_element_type=F32))
    r = _sigmoid(jnp.concatenate(ra, axis=-1) + ba_ref[...])
    i = _sigmoid(jnp.concatenate(ri, axis=-1) + bx_ref[...])
    log_a = -LRU_C * r * _softplus(-lam_ref[...])
    a = jnp.exp(log_a)
    u = jnp.sqrt(-jnp.tanh(log_a) * (1.0 + a * a)) * (i * xf)

    rid = lax.broadcasted_iota(jnp.int32, (1, SUBLANES, 1), 1)
    a = a.reshape(nb * ng, SUBLANES, w)
    u = u.reshape(nb * ng, SUBLANES, w)
    for s in (1, 2, 4):
        a_sh = jnp.where(rid >= s, pltpu.roll(a, s, axis=1), 1.0)
        u_sh = jnp.where(rid >= s, pltpu.roll(u, s, axis=1), 0.0)
        u = a * u_sh + u
        a = a * a_sh
    a = a.reshape(nb, ng, SUBLANES, w)
    u = u.reshape(nb, ng, SUBLANES, w)
    h = h_s[...]
    g_last, r_last = divmod((l_valid - 1) % t, SUBLANES)
    gr4 = gr_ref[...].reshape(nb, ng, SUBLANES, w)
    for g in range(ng):
        hg = a[:, g] * h + u[:, g]
        h = hg[:, SUBLANES - 1:SUBLANES]
        hl_ref[:, g * SUBLANES:(g + 1) * SUBLANES, :] = (hg * _silu(gr4[:, g])).astype(hl_ref.dtype)
        if g == g_last:
            @pl.when(c == (l_valid - 1) // t)
            def _():
                hlast_ref[...] = hg[:, r_last:r_last + 1]
    h_s[...] = h


def _rglru(rest3, row0, nseq, tiles, nc, l_valid, cw, cb, wa, ba, wx, bx, lam, h0, conv0):
    nb, t = tiles
    assert nseq % nb == 0 and row0 % nb == 0
    w = LRU_WIDTH
    r0 = row0 // nb
    full2 = lambda a: pl.BlockSpec(a.shape, lambda b, c: (0, 0))
    full3 = lambda a: pl.BlockSpec(a.shape, lambda b, c: (0, 0, 0))
    kern = functools.partial(_rglru_kernel, nb=nb, t=t, l_valid=l_valid)
    return pl.pallas_call(
        kern,
        grid=(nseq // nb, nc),
        in_specs=[
            pl.BlockSpec((nb, t, w), lambda b, c: (r0 + b, c, 2)),
            pl.BlockSpec((nb, t, w), lambda b, c: (r0 + b, c, 3)),
            full2(cw), full2(cb), full3(wa), full2(ba), full3(wx), full2(bx), full2(lam),
            pl.BlockSpec((nb, 1, w), lambda b, c: (b, 0, 0)),
            pl.BlockSpec((nb, SUBLANES, w), lambda b, c: (b, 0, 0)),
        ],
        out_specs=[
            pl.BlockSpec((nb, t, w), lambda b, c: (b, c, 0)),
            pl.BlockSpec((nb, 1, w), lambda b, c: (b, 0, 0)),
        ],
        out_shape=[
            jax.ShapeDtypeStruct((nseq, t * nc, w), BF16),
            jax.ShapeDtypeStruct((nseq, 1, w), F32),
        ],
        scratch_shapes=[pltpu.VMEM((nb, 1, w), F32), pltpu.VMEM((nb, SUBLANES + t, w), F32)],
        compiler_params=_cparams(("parallel", "arbitrary")),
        name="rglru",
    )(rest3, rest3, cw, cb, wa, ba, wx, bx, lam, h0, conv0)


def _gla_kernel(q_ref, k_ref, v_ref, al_ref, aup_ref, ab_ref, s0_ref, o_ref, s_ref, *, nb, t, nc, l_valid):
    c = pl.program_id(1)
    if nc == 1:
        s_in = s0_ref
    else:
        s_in = s_ref

        @pl.when(c == 0)
        def _():
            s_ref[...] = s0_ref[...]

    pos = c * t + lax.broadcasted_iota(jnp.int32, (t, 1), 0)
    valid = pos < l_valid
    row = lax.broadcasted_iota(jnp.int32, (t, t), 0)
    col = lax.broadcasted_iota(jnp.int32, (t, t), 1)
    causal = row >= col
    aup = aup_ref[...]
    ab = ab_ref[...]
    heads = [(h, slice(h * G_DK, (h + 1) * G_DK), slice(h * G_DV, (h + 1) * G_DV)) for h in range(N_HEADS)]

    def group(rows):
        pairs = [(i, r, h, ks, vs) for i, r in enumerate(rows) for h, ks, vs in heads]
        dec = []
        for r in rows:
            pre = _dot(al_ref[r], aup) + ab
            lg = jnp.where(valid, _log_sigmoid(pre) / G_TAU, 0.0)
            b = _cumsum_rows(lg)
            b_last = b[t - 1:t, :]
            k = jnp.where(valid, k_ref[r].astype(F32), 0.0)
            qd = ((q_ref[r].astype(F32) * (G_DK ** -0.5)) * jnp.exp(b)).astype(BF16)
            kn = (k * jnp.exp(-b)).astype(BF16)
            kr = (k * jnp.exp(b_last - b)).astype(BF16)
            dec.append((qd, kn, kr, _row_to_col(jnp.exp(b_last))))
        att = {}
        for i, r, h, ks, vs in pairs:
            qd, kn, _, _ = dec[i]
            att[i, h] = jnp.where(causal, _dot_nt(qd[:, ks], kn[:, ks]), 0.0).astype(BF16)
        for i, r, h, ks, vs in pairs:
            qd, _, kr, ebl_col = dec[i]
            s_h = s_in[r, h]
            v = v_ref[r, :, vs]
            o_ref[r, :, vs] = (_dot(qd[:, ks], s_h) + _dot(att[i, h], v)).astype(o_ref.dtype)
            s_ref[r, h] = ebl_col[ks, :] * s_h + _dot_tn(kr[:, ks], v)

    _group_loop(nb, group)


def _gla(qkv3, al3, row0, nseq, tiles, nc, l_valid, aup, ab, s0):
    nb, t = tiles
    assert nseq % nb == 0 and row0 % nb == 0
    r0 = row0 // nb
    full2 = lambda a: pl.BlockSpec(a.shape, lambda b, c: (0, 0))
    st = pl.BlockSpec((nb, N_HEADS, G_DK, G_DV), lambda b, c: (b, 0, 0, 0))
    kern = functools.partial(_gla_kernel, nb=nb, t=t, nc=nc, l_valid=l_valid)
    return pl.pallas_call(
        kern,
        grid=(nseq // nb, nc),
        in_specs=[
            pl.BlockSpec((nb, t, G_KW), lambda b, c: (r0 + b, c, 0)),
            pl.BlockSpec((nb, t, G_KW), lambda b, c: (r0 + b, c, 1)),
            pl.BlockSpec((nb, t, G_VW), lambda b, c: (r0 + b, c, 1)),
            pl.BlockSpec((nb, t, LANES), lambda b, c: (r0 + b, c, 0)),
            full2(aup), full2(ab), st,
        ],
        out_specs=[pl.BlockSpec((nb, t, G_VW), lambda b, c: (b, c, 0)), st],
        out_shape=[
            jax.ShapeDtypeStruct((nseq, t * nc, G_VW), BF16),
            jax.ShapeDtypeStruct((nseq, N_HEADS, G_DK, G_DV), F32),
        ],
        compiler_params=_cparams(("parallel", "arbitrary")),
        name="gla",
    )(qkv3, qkv3, qkv3, al3, aup, ab, s0)


def _mlstm_proj_kernel(x_ref, pre_ref, win_ref, wgate_ref, gbias_ref, c0_ref, n0_ref, m0_ref,
                       rest_ref, h_ref, c_ref, n_ref, m_ref, qkv_s, gate_s, hn_s, *, rows, tiles_per_seq):
    g = pl.program_id(0)
    slot = g % 2
    prev = 1 - slot
    t = MLSTM_TILES["main"][1]
    e_lo = 3 * M_WIDTH
    e = win_ref.shape[1]
    n_chunks = rows // t
    cols_per_chunk = e // n_chunks

    @pl.when(g == 0)
    def _():
        qkv_s[1] = jnp.zeros(qkv_s.shape[1:], qkv_s.dtype)
        gate_s[1] = jnp.zeros(gate_s.shape[1:], gate_s.dtype)

    @pl.when(jnp.maximum(g - 1, 0) % tiles_per_seq == 0)
    def _():
        c_ref[...] = c0_ref[...]
        n_ref[...] = n0_ref[...]
        m_ref[...] = m0_ref[...]

    x = x_ref[0]
    hn_s[...] = (x * lax.rsqrt(jnp.mean(x * x, axis=-1, keepdims=True) + EPS) * pre_ref[...]).astype(BF16)
    gate_s[slot] = jnp.dot(hn_s[...], wgate_ref[...], preferred_element_type=F32)

    def project(c0, c1):
        acc = jnp.dot(hn_s[...], win_ref[:, c0:c1], preferred_element_type=F32)
        if c0 < e_lo:
            hi = min(c1, e_lo)
            qkv_s[slot, :, c0:hi] = acc[:, :hi - c0].astype(BF16)
        if c1 > e_lo:
            lo = max(c0, e_lo)
            rest_ref[0, :, lo - e_lo:c1 - e_lo] = acc[:, lo - c0:]

    row = lax.broadcasted_iota(jnp.int32, (t, t), 0)
    col = lax.broadcasted_iota(jnp.int32, (t, t), 1)
    causal = row >= col
    lane = lax.broadcasted_iota(jnp.int32, (1, LANES), 1)
    gbias = gbias_ref[...]
    heads = [(h, slice(h * M_DK, (h + 1) * M_DK)) for h in range(N_HEADS)]
    n_sub = 7
    sub = cols_per_chunk // n_sub
    pieces = [(j * cols_per_chunk + i * sub, j * cols_per_chunk + (i + 1) * sub)
              for j in range(n_chunks) for i in range(n_sub)]

    for j in range(n_chunks):
        rs = slice(j * t, (j + 1) * t)
        todo = pieces[j * n_sub:(j + 1) * n_sub]
        gt = gate_s[prev, rs, :] + gbias
        li = gt[:, :LANES]
        b = _cumsum_rows(_log_sigmoid(gt[:, LANES:]))
        project(*todo[0])
        project(*todo[1])
        r_t = _transpose_rows(li - b)[:, :t]
        m_vec = m_ref[0]
        st = []
        for h, sl in heads:
            b_col = b[:, h:h + 1]
            inter = b_col + m_vec[:, h:h + 1]
            dmat = jnp.where(causal, b_col + r_t[h:h + 1, :], NEG)
            mt = jnp.maximum(inter, jnp.max(dmat, axis=-1, keepdims=True))
            k = qkv_s[prev, rs, M_WIDTH + h * M_DK:M_WIDTH + (h + 1) * M_DK] * (M_DK ** -0.5)
            s = _dot_nt(qkv_s[prev, rs, sl], k) * jnp.exp(dmat - mt)
            st.append((inter, mt, k, s))
        project(*todo[2])
        project(*todo[3])
        for h, sl in heads:
            inter, mt, k, s = st[h]
            q = qkv_s[prev, rs, sl]
            v = qkv_s[prev, rs, 2 * M_WIDTH + h * M_DV:2 * M_WIDTH + (h + 1) * M_DV]
            w_inter = jnp.exp(inter - mt)
            num = w_inter * _dot_nt(q, c_ref[0, h]) + _dot(s, v)
            den = (w_inter * jnp.sum(q.astype(F32) * n_ref[0, h], axis=-1, keepdims=True)
                   + jnp.sum(s, axis=-1, keepdims=True))
            h_ref[0, rs, sl] = (num / jnp.maximum(jnp.abs(den), jnp.exp(-mt))).astype(h_ref.dtype)
        project(*todo[4])
        project(*todo[5])
        m_next = m_vec
        for h, sl in heads:
            inter, mt, k, _ = st[h]
            v = qkv_s[prev, rs, 2 * M_WIDTH + h * M_DV:2 * M_WIDTH + (h + 1) * M_DV]
            b_col = b[:, h:h + 1]
            m_new = mt[t - 1:t, :]
            w_c = jnp.exp(inter[t - 1:t, :] - m_new)
            w_k = jnp.exp(b_col[t - 1:t, :] - b_col + li[:, h:h + 1] - m_new)
            c_ref[0, h] = w_c * c_ref[0, h] + _dot_tn(v.astype(F32) * w_k, k)
            n_ref[0, h] = w_c * n_ref[0, h] + jnp.sum(k.astype(F32) * w_k, axis=0, keepdims=True)
            m_next = jnp.where(lane == h, m_new, m_next)
        m_ref[0] = m_next
        project(*todo[6])


def _mlstm_proj(x3, pre, w_main, w_gate, gbias, c0, n0, m0, rows):
    nseq, seq, d = x3.shape
    assert seq % rows == 0 and rows % MLSTM_TILES["main"][1] == 0
    tps = seq // rows
    n_tiles = nseq * tps
    e = w_main.shape[1]
    e_lo = 3 * M_WIDTH
    const = lambda a: pl.BlockSpec(a.shape, lambda g: (0,) * a.ndim, pipeline_mode=pl.Buffered(1))
    cur = lambda g: jnp.minimum(g, n_tiles - 1)
    old = lambda g: jnp.maximum(g - 1, 0)
    st_c = pl.BlockSpec((1, N_HEADS, M_DV, M_DK), lambda g: (old(g) // tps, 0, 0, 0))
    st_n = pl.BlockSpec((1, N_HEADS, 1, M_DK), lambda g: (old(g) // tps, 0, 0, 0))
    st_m = pl.BlockSpec((1, 1, LANES), lambda g: (old(g) // tps, 0, 0))
    kern = functools.partial(_mlstm_proj_kernel, rows=rows, tiles_per_seq=tps)
    return pl.pallas_call(
        kern,
        grid=(n_tiles + 1,),
        in_specs=[
            pl.BlockSpec((1, rows, d), lambda g: (cur(g) // tps, cur(g) % tps, 0)),
            const(pre), const(w_main), const(w_gate), const(gbias), st_c, st_n, st_m,
        ],
        out_specs=[
            pl.BlockSpec((1, rows, e - e_lo), lambda g: (cur(g) // tps, cur(g) % tps, 0)),
            pl.BlockSpec((1, rows, M_WIDTH), lambda g: (old(g) // tps, old(g) % tps, 0)),
            st_c, st_n, st_m,
        ],
        out_shape=[
            jax.ShapeDtypeStruct((nseq, seq, e - e_lo), F32),
            jax.ShapeDtypeStruct((nseq, seq, M_WIDTH), BF16),
            jax.ShapeDtypeStruct((nseq, N_HEADS, M_DV, M_DK), F32),
            jax.ShapeDtypeStruct((nseq, N_HEADS, 1, M_DK), F32),
            jax.ShapeDtypeStruct((nseq, 1, LANES), F32),
        ],
        scratch_shapes=[
            pltpu.VMEM((2, rows, e_lo), BF16),
            pltpu.VMEM((2, rows, 2 * LANES), F32),
            pltpu.VMEM((rows, d), BF16),
        ],
        compiler_params=_cparams(("arbitrary",)),
        name="mlstm_proj",
    )(x3, pre, w_main, w_gate, gbias, c0, n0, m0)


def _gla_layer_kernel(x_ref, pre_ref, post_ref, win_ref, wlow_ref, aup_ref, ab_ref, ng_ref, wout_ref, s0_ref,
                      y_ref, s_ref, qkv_s, r_s, al_s, hn_s, o_s, xkeep_s, *, rows, tiles_per_seq):
    g = pl.program_id(0)
    slot = g % 2
    prev = 1 - slot
    t = GLA_TILES["main"][1]
    e_lo = 2 * G_KW + G_VW
    e = e_lo + G_VW
    n_chunks = rows // t
    cols_per_chunk = e // n_chunks

    @pl.when(g == 0)
    def _():
        qkv_s[1] = jnp.zeros(qkv_s.shape[1:], qkv_s.dtype)
        r_s[1] = jnp.zeros(r_s.shape[1:], r_s.dtype)
        al_s[1] = jnp.zeros(al_s.shape[1:], al_s.dtype)
        xkeep_s[...] = jnp.zeros(xkeep_s.shape, xkeep_s.dtype)

    @pl.when(jnp.maximum(g - 1, 0) % tiles_per_seq == 0)
    def _():
        s_ref[...] = s0_ref[...]

    x = x_ref[0]
    hn_s[...] = (x * lax.rsqrt(jnp.mean(x * x, axis=-1, keepdims=True) + EPS) * pre_ref[...]).astype(BF16)
    al_s[slot] = jnp.dot(hn_s[...], wlow_ref[...], preferred_element_type=F32)

    row = lax.broadcasted_iota(jnp.int32, (t, t), 0)
    col = lax.broadcasted_iota(jnp.int32, (t, t), 1)
    causal = row >= col
    aup = aup_ref[...]
    ab = ab_ref[...]
    heads = [(h, slice(h * G_DK, (h + 1) * G_DK), slice(h * G_DV, (h + 1) * G_DV)) for h in range(N_HEADS)]

    def project(c0, c1):
        acc = jnp.dot(hn_s[...], win_ref[:, c0:c1], preferred_element_type=F32)
        if c0 < e_lo:
            hi = min(c1, e_lo)
            qkv_s[slot, :, c0:hi] = acc[:, :hi - c0].astype(BF16)
        if c1 > e_lo:
            lo = max(c0, e_lo)
            r_s[slot, :, lo - e_lo:c1 - e_lo] = acc[:, lo - c0:]

    n_sub = 3
    sub = cols_per_chunk // n_sub
    for j in range(n_chunks):
        c0 = j * cols_per_chunk
        rs = slice(j * t, (j + 1) * t)
        pre = _dot(al_s[prev, rs, :], aup) + ab
        project(c0, c0 + sub)
        b = _cumsum_rows(_log_sigmoid(pre) / G_TAU)
        b_last = b[t - 1:t, :]
        project(c0 + sub, c0 + 2 * sub)
        k = qkv_s[prev, rs, G_KW:2 * G_KW].astype(F32)
        qd = ((qkv_s[prev, rs, 0:G_KW].astype(F32) * (G_DK ** -0.5)) * jnp.exp(b)).astype(BF16)
        kn = (k * jnp.exp(-b)).astype(BF16)
        kr = (k * jnp.exp(b_last - b)).astype(BF16)
        ebl_col = _row_to_col(jnp.exp(b_last))
        att = [jnp.where(causal, _dot_nt(qd[:, ks], kn[:, ks]), 0.0).astype(BF16) for _, ks, _ in heads]
        project(c0 + 2 * sub, c0 + cols_per_chunk)
        for h, ks, vs in heads:
            s_h = s_ref[0, h]
            v = qkv_s[prev, rs, 2 * G_KW + h * G_DV:2 * G_KW + (h + 1) * G_DV]
            o_s[rs, vs] = (_dot(qd[:, ks], s_h) + _dot(att[h], v)).astype(o_s.dtype)
            s_ref[0, h] = ebl_col[ks, :] * s_h + _dot_tn(kr[:, ks], v)

    o = _head_rms(o_s[...].astype(F32), N_HEADS) * ng_ref[...] * _silu(r_s[prev])
    out = jnp.dot(o.astype(BF16), wout_ref[...], preferred_element_type=F32)
    nrm = out * lax.rsqrt(jnp.mean(out * out, axis=-1, keepdims=True) + EPS) * post_ref[...]
    y_ref[0] = xkeep_s[...] + nrm
    xkeep_s[...] = x


def _gla_layer(x3, pre, post, w_in, w_low, aup, ab, ng, w_out, s0, rows):
    nseq, seq, d = x3.shape
    assert seq % rows == 0 and rows % GLA_TILES["main"][1] == 0
    tps = seq // rows
    n_tiles = nseq * tps
    e = w_in.shape[1]
    e_lo = 2 * G_KW + G_VW
    const = lambda a: pl.BlockSpec(a.shape, lambda g: (0,) * a.ndim, pipeline_mode=pl.Buffered(1))
    cur = lambda g: jnp.minimum(g, n_tiles - 1)
    old = lambda g: jnp.maximum(g - 1, 0)
    st = pl.BlockSpec((1, N_HEADS, G_DK, G_DV), lambda g: (old(g) // tps, 0, 0, 0))
    kern = functools.partial(_gla_layer_kernel, rows=rows, tiles_per_seq=tps)
    return pl.pallas_call(
        kern,
        grid=(n_tiles + 1,),
        in_specs=[
            pl.BlockSpec((1, rows, d), lambda g: (cur(g) // tps, cur(g) % tps, 0)),
            const(pre), const(post), const(w_in), const(w_low), const(aup), const(ab), const(ng), const(w_out), st,
        ],
        out_specs=[pl.BlockSpec((1, rows, d), lambda g: (old(g) // tps, old(g) % tps, 0)), st],
        out_shape=[
            jax.ShapeDtypeStruct((nseq, seq, d), F32),
            jax.ShapeDtypeStruct((nseq, N_HEADS, G_DK, G_DV), F32),
        ],
        scratch_shapes=[
            pltpu.VMEM((2, rows, e_lo), BF16),
            pltpu.VMEM((2, rows, e - e_lo), F32),
            pltpu.VMEM((2, rows, LANES), F32),
            pltpu.VMEM((rows, d), BF16),
            pltpu.VMEM((rows, G_VW), BF16),
            pltpu.VMEM((rows, d), F32),
        ],
        compiler_params=_cparams(("arbitrary",)),
        name="gla_layer",
    )(x3, pre, post, w_in, w_low, aup, ab, ng, w_out, s0)


def _pad_lanes(a, width):
    return jnp.pad(a, [(0, 0)] * (a.ndim - 1) + [(0, width - a.shape[-1])])


def _even_layer(xm, xs, nbp, seq, nbs, ls, st, pre_g, post_g, w_in, w_out, b_i, b_f, m_norm_g,
                conv_w, conv_b, wa, ba, wx, bx, lam):
    c0s, n0s, m0s, h0s, conv0s = st
    wq, wk, wv, wog, wig, wfg, wz, wxr, wgr = jnp.split(
        w_in, [1024, 2048, 3072, 4096, 4100, 4104, 5128, 6152], axis=1)
    w_main = jnp.concatenate([wq, wk, wv, wog, wz, wxr, wgr], axis=1).astype(BF16)
    w_gate = jnp.concatenate([_pad_lanes(wig, LANES), _pad_lanes(wfg, LANES)], axis=1).astype(BF16)
    gbias = jnp.concatenate([_pad_lanes(b_i[None], LANES), _pad_lanes(b_f[None], LANES)], axis=1)
    ng = m_norm_g[None]
    post = post_g[None]
    cw, cb = conv_w, conv_b[None]
    wab, wxb = wa.astype(BF16), wx.astype(BF16)
    bav, bxv, lamv = ba[None], bx[None], lam[None]
    w_o1, w_o2 = w_out[:M_WIDTH].astype(BF16), w_out[M_WIDTH:].astype(BF16)
    e_lo = 3 * M_WIDTH
    e_hi = w_main.shape[1] - e_lo
    lru = (cw, cb, wab, bav, wxb, bxv, lamv)

    qs, ps, gs = _norm_proj(xs, pre_g, w_main, w_gate, e_lo, xs.shape[0])
    n_meta_rows = nbp * N_META
    xr_cols = slice(2 * LRU_WIDTH, 3 * LRU_WIDTH)

    zc = jnp.zeros((nbp, N_HEADS, M_DV, M_DK), F32)
    zn = jnp.zeros((nbp, N_HEADS, 1, M_DK), F32)
    zm = jnp.zeros((nbp, 1, LANES), F32)
    zh = jnp.zeros((nbp, 1, LRU_WIDTH), F32)
    zconv = jnp.zeros((nbp, SUBLANES, LRU_WIDTH), F32)
    qs_meta = qs.reshape(-1, N_META, e_lo)
    ps_meta = ps.reshape(-1, N_META, e_hi)
    gs_meta = gs.reshape(-1, N_META, 2 * LANES)
    h_meta, c1, n1, m1 = _mlstm(qs_meta, gs_meta, 0, nbp, MLSTM_TILES["meta"], 1, N_META, gbias, zc, zn, zm)
    hl_meta, h1 = _rglru(ps_meta, 0, nbp, LRU_TILES["meta"], 1, N_META, *lru, zh, zconv)
    conv1 = ps_meta[:nbp, N_META - SUBLANES:, xr_cols]

    pm3, h_main, pc, pn, pmm = _mlstm_proj(xm.reshape(nbp, seq, -1), pre_g[None], w_main, w_gate, gbias,
                                           c1, n1, m1, FUSED_ROW_TILE)
    pm = pm3.reshape(nbp * seq, e_hi)
    tl = LRU_TILES["main"][1]
    hl_main, ph = _rglru(pm3, 0, nbp, LRU_TILES["main"], seq // tl, seq, *lru, h1, conv1)
    pconv = pm3[:, seq - (CONV_W - 1):, xr_cols]

    qs_s = qs.reshape(-1, SAMPLE_PAD, e_lo)
    ps_s = ps.reshape(-1, SAMPLE_PAD, e_hi)
    gs_s = gs.reshape(-1, SAMPLE_PAD, 2 * LANES)
    r0 = n_meta_rows // SAMPLE_PAD
    h_s, sc, sn, sm = _mlstm(qs_s, gs_s, r0, nbs, MLSTM_TILES["sample"], 1, ls, gbias,
                             c0s, n0s[:, :, None, :], _pad_lanes(m0s, LANES)[:, None, :])
    conv0p = jnp.pad(conv0s, ((0, 0), (SUBLANES - (CONV_W - 1), 0), (0, 0)))
    hl_s, sh = _rglru(ps_s, r0, nbs, LRU_TILES["sample"], 1, ls, *lru, h0s[:, None, :], conv0p)
    xr_s = ps_s[r0:, :ls, xr_cols]
    sconv = jnp.concatenate([conv0s, xr_s], axis=1)[:, -(CONV_W - 1):]

    ym = _out_even(xm, post, h_main.reshape(-1, M_WIDTH), pm, ng, hl_main.reshape(-1, LRU_WIDTH), w_o1, w_o2,
                   OUT_ROW_TILE)
    h_small = jnp.concatenate([h_meta.reshape(-1, M_WIDTH), h_s.reshape(-1, M_WIDTH)], axis=0)
    hl_small = jnp.concatenate([hl_meta.reshape(-1, LRU_WIDTH), hl_s.reshape(-1, LRU_WIDTH)], axis=0)
    ys = _out_even(xs, post, h_small, ps, ng, hl_small, w_o1, w_o2, _row_tile(xs.shape[0], OUT_ROW_TILE))

    p_state = (pc, pn[:, :, 0, :], pmm[:, 0, :N_HEADS], ph[:, 0, :], pconv)
    s_state = (sc, sn[:, :, 0, :], sm[:, 0, :N_HEADS], sh[:, 0, :], sconv)
    return ym, ys, p_state, s_state


def _odd_layer(xm, xs, nbp, seq, nbs, ls, s0s, pre_g, post_g, w_in, w_out, a_up, a_b, g_norm_g):
    e_lo = 2 * G_KW + G_VW
    w_main = w_in[:, :2 * G_KW + 2 * G_VW].astype(BF16)
    w_low = _pad_lanes(w_in[:, 2 * G_KW + 2 * G_VW:], LANES).astype(BF16)
    aup = jnp.pad(a_up, ((0, LANES - G_RANK), (0, 0))).astype(BF16)
    ab = a_b[None]
    ng = g_norm_g[None]
    post = post_g[None]
    w_o = w_out.astype(BF16)

    qs, rs, as_ = _norm_proj(xs, pre_g, w_main, w_low, e_lo, xs.shape[0])
    n_meta_rows = nbp * N_META

    zs = jnp.zeros((nbp, N_HEADS, G_DK, G_DV), F32)
    o_meta, s1 = _gla(qs.reshape(-1, N_META, e_lo), as_.reshape(-1, N_META, LANES),
                      0, nbp, GLA_TILES["meta"], 1, N_META, aup, ab, zs)
    ym3, p_s = _gla_layer(xm.reshape(nbp, seq, -1), pre_g[None], post, w_main, w_low, aup, ab, ng, w_o, s1,
                          FUSED_ROW_TILE)
    ym = ym3.reshape(xm.shape)
    r0 = n_meta_rows // SAMPLE_PAD
    o_s, s_s = _gla(qs.reshape(-1, SAMPLE_PAD, e_lo), as_.reshape(-1, SAMPLE_PAD, LANES),
                    r0, nbs, GLA_TILES["sample"], 1, ls, aup, ab, s0s)

    o_small = jnp.concatenate([o_meta.reshape(-1, G_VW), o_s.reshape(-1, G_VW)], axis=0)
    ys = _out_odd(xs, post, o_small, rs, ng, w_o, _row_tile(xs.shape[0], OUT_ROW_TILE))
    return ym, ys, p_s, s_s


def kernel(x_prompt, x_sample, state_mlstm_C, state_mlstm_n, state_mlstm_m, state_rglru_h, state_rglru_conv,
           state_gla_S, meta_tokens, pre_norm_a, post_norm_a, w_in_a, w_out_a, mlstm_b_i, mlstm_b_f, mlstm_norm,
           conv_w, conv_b, lru_w_a, lru_b_a, lru_w_x, lru_b_x, lru_lambda, pre_norm_c, post_norm_c, w_in_c,
           w_out_c, gla_alpha_up, gla_alpha_b, gla_norm):
    nbp, seq, d = x_prompt.shape
    nbs, ls, _ = x_sample.shape
    depth = pre_norm_a.shape[0] + pre_norm_c.shape[0]
    assert ls >= CONV_W - 1 and ls <= SAMPLE_PAD and N_META % SUBLANES == 0

    xm = x_prompt.reshape(nbp * seq, d)
    meta = jnp.broadcast_to(meta_tokens[None].astype(x_prompt.dtype), (nbp, N_META, d)).reshape(nbp * N_META, d)
    xs_pad = jnp.pad(x_sample, ((0, 0), (0, SAMPLE_PAD - ls), (0, 0))).reshape(nbs * SAMPLE_PAD, d)
    xs = jnp.concatenate([meta, xs_pad], axis=0)

    p_lists = [[] for _ in range(6)]
    s_lists = [[] for _ in range(6)]
    for layer in range(depth):
        j = layer // 2
        if layer % 2 == 0:
            st = (state_mlstm_C[j], state_mlstm_n[j], state_mlstm_m[j], state_rglru_h[j], state_rglru_conv[j])
            xm, xs, pst, sst = _even_layer(
                xm, xs, nbp, seq, nbs, ls, st, pre_norm_a[j], post_norm_a[j], w_in_a[j], w_out_a[j],
                mlstm_b_i[j], mlstm_b_f[j], mlstm_norm[j], conv_w[j], conv_b[j], lru_w_a[j], lru_b_a[j],
                lru_w_x[j], lru_b_x[j], lru_lambda[j])
            for i in range(5):
                p_lists[i].append(pst[i])
                s_lists[i].append(sst[i])
        else:
            xm, xs, p_s, s_s = _odd_layer(
                xm, xs, nbp, seq, nbs, ls, state_gla_S[j], pre_norm_c[j], post_norm_c[j], w_in_c[j], w_out_c[j],
                gla_alpha_up[j], gla_alpha_b[j], gla_norm[j])
            p_lists[5].append(p_s)
            s_lists[5].append(s_s)

    y_prompt = xm.reshape(nbp, seq, d)
    y_sample = xs[nbp * N_META:].reshape(nbs, SAMPLE_PAD, d)[:, :ls]
    return (y_prompt, y_sample) + tuple(jnp.stack(l) for l in p_lists) + tuple(jnp.stack(l) for l in s_lists)
```

```python
import functools

import jax
import jax.numpy as jnp
from jax import lax
from jax.experimental import pallas as pl
from jax.experimental.pallas import tpu as pltpu

F32 = jnp.float32
BF16 = jnp.bfloat16

D_MODEL = 1024
N_META = 16
EPS = 1e-6
N_HEADS = 4
M_DK = 256
M_DV = 256
M_WIDTH = N_HEADS * M_DV
LRU_WIDTH = 1024
LRU_BLOCKS = 8
LRU_BS = LRU_WIDTH // LRU_BLOCKS
CONV_W = 4
LRU_C = 8.0
G_DK = 256
G_DV = 512
G_KW = N_HEADS * G_DK
G_VW = N_HEADS * G_DV
G_RANK = 16
G_TAU = 16.0

LANES = 128
SUBLANES = 8
NEG = -1e30
SAMPLE_PAD = SUBLANES
VMEM_LIMIT = 56 * 1024 * 1024
COL_TILE = 1024

MLSTM_TILES = {"meta": (8, N_META), "main": (4, 128), "sample": (4, SAMPLE_PAD)}
GLA_TILES = {"meta": (8, N_META), "main": (2, 64), "sample": (4, SAMPLE_PAD)}
LRU_TILES = {"meta": (8, N_META), "main": (1, 256), "sample": (16, SAMPLE_PAD)}
PROJ_ROW_TILE = 1024
OUT_ROW_TILE = 512
FUSED_ROW_TILE = 256


def _row_tile(n, cap):
    return next(tm for tm in range(min(n, cap) // SUBLANES * SUBLANES, 0, -SUBLANES) if n % tm == 0)


def _cparams(sem):
    return pltpu.CompilerParams(dimension_semantics=sem, vmem_limit_bytes=VMEM_LIMIT)


def _sigmoid(x):
    return 0.5 * jnp.tanh(0.5 * x) + 0.5


def _silu(x):
    return x * _sigmoid(x)


def _log_sigmoid(x):
    return jnp.minimum(x, 0.0) - jnp.log1p(jnp.exp(-jnp.abs(x)))


def _softplus(x):
    return jnp.maximum(x, 0.0) + jnp.log1p(jnp.exp(-jnp.abs(x)))


def _dot(a, b):
    return jnp.dot(a.astype(BF16), b.astype(BF16), preferred_element_type=F32)


def _dot_nt(a, b):
    return lax.dot_general(a.astype(BF16), b.astype(BF16), (((1,), (1,)), ((), ())),
                           preferred_element_type=F32)


def _dot_tn(a, b):
    return lax.dot_general(a.astype(BF16), b.astype(BF16), (((0,), (0,)), ((), ())),
                           preferred_element_type=F32)


def _transpose_rows(x):
    t, n = x.shape
    if t < LANES:
        x = jnp.concatenate([x, jnp.zeros((LANES - t, n), x.dtype)], axis=0)
    return jnp.transpose(x)


def _row_to_col(r):
    return jnp.transpose(jnp.broadcast_to(r, (LANES, r.shape[1])))[:, 0:1]


def _cumsum_rows(x):
    t = x.shape[0]
    if t <= 2 * SUBLANES:
        rid = lax.broadcasted_iota(jnp.int32, (t, 1), 0)
        s = 1
        while s < t:
            x = x + jnp.where(rid >= s, pltpu.roll(x, s, axis=0), 0.0)
            s *= 2
        return x
    row = lax.broadcasted_iota(jnp.int32, (t, t), 0)
    col = lax.broadcasted_iota(jnp.int32, (t, t), 1)
    tri = jnp.where(row >= col, 1.0, 0.0).astype(BF16)
    hi = x.astype(BF16)
    r1 = x - hi.astype(F32)
    mid = r1.astype(BF16)
    lo = (r1 - mid.astype(F32)).astype(BF16)
    acc = jnp.dot(tri, lo, preferred_element_type=F32)
    acc = acc + jnp.dot(tri, mid, preferred_element_type=F32)
    return acc + jnp.dot(tri, hi, preferred_element_type=F32)


def _head_rms(x, nh):
    hd = x.shape[1] // nh
    parts = []
    for h in range(nh):
        xh = x[:, h * hd:(h + 1) * hd]
        parts.append(xh * lax.rsqrt(jnp.mean(xh * xh, axis=-1, keepdims=True) + EPS))
    return jnp.concatenate(parts, axis=-1)


def _group_loop(nb, group_fn):
    if nb <= 2:
        group_fn(list(range(nb)))
        return

    def body(i, carry):
        group_fn([2 * i, 2 * i + 1])
        return carry

    lax.fori_loop(0, nb // 2, body, 0)


def _norm_proj_kernel(n_lo, x_ref, g_ref, wm_ref, ws_ref, olo_ref, ohi_ref, os_ref, hn_ref):
    j = pl.program_id(1)

    @pl.when(j == 0)
    def _():
        x = x_ref[...]
        y = x * lax.rsqrt(jnp.mean(x * x, axis=-1, keepdims=True) + EPS) * g_ref[...]
        hn = y.astype(BF16)
        hn_ref[...] = hn
        os_ref[...] = jnp.dot(hn, ws_ref[...], preferred_element_type=F32)

    acc = jnp.dot(hn_ref[...], wm_ref[...], preferred_element_type=F32)

    @pl.when(j < n_lo)
    def _():
        olo_ref[...] = acc.astype(olo_ref.dtype)

    @pl.when(j >= n_lo)
    def _():
        ohi_ref[...] = acc


def _norm_proj(x2d, g, w_main, w_small, e_lo, tm):
    n, d = x2d.shape
    e = w_main.shape[1]
    es = w_small.shape[1]
    tn = COL_TILE
    n_lo = e_lo // tn
    return pl.pallas_call(
        functools.partial(_norm_proj_kernel, n_lo),
        grid=(n // tm, e // tn),
        in_specs=[
            pl.BlockSpec((tm, d), lambda i, j: (i, 0)),
            pl.BlockSpec((1, d), lambda i, j: (0, 0)),
            pl.BlockSpec((d, tn), lambda i, j: (0, j)),
            pl.BlockSpec((d, es), lambda i, j: (0, 0)),
        ],
        out_specs=[
            pl.BlockSpec((tm, tn), lambda i, j: (i, jnp.minimum(j, n_lo - 1))),
            pl.BlockSpec((tm, tn), lambda i, j: (i, jnp.maximum(j - n_lo, 0))),
            pl.BlockSpec((tm, es), lambda i, j: (i, 0)),
        ],
        out_shape=[
            jax.ShapeDtypeStruct((n, e_lo), BF16),
            jax.ShapeDtypeStruct((n, e - e_lo), F32),
            jax.ShapeDtypeStruct((n, es), F32),
        ],
        scratch_shapes=[pltpu.VMEM((tm, d), BF16)],
        compiler_params=_cparams(("parallel", "arbitrary")),
        name="norm_proj",
    )(x2d, g.reshape(1, d), w_main, w_small)


def _residual_norm(x_ref, g_ref, y_ref, out):
    nrm = out * lax.rsqrt(jnp.mean(out * out, axis=-1, keepdims=True) + EPS) * g_ref[...]
    y_ref[...] = x_ref[...] + nrm


def _out_even_kernel(x_ref, g_ref, h_ref, og_ref, z_ref, ng_ref, hl_ref, w1_ref, w2_ref, y_ref):
    hm = _sigmoid(og_ref[...]) * h_ref[...].astype(F32)
    hm = _head_rms(hm, N_HEADS) * ng_ref[...] * _silu(z_ref[...])
    out = jnp.dot(hm.astype(BF16), w1_ref[...], preferred_element_type=F32)
    out = out + jnp.dot(hl_ref[...], w2_ref[...], preferred_element_type=F32)
    _residual_norm(x_ref, g_ref, y_ref, out)


def _out_even(x2d, g, h_raw, rest, ng, hl, w1, w2, tm):
    n, d = x2d.shape
    row = lambda width, col: pl.BlockSpec((tm, width), lambda i: (i, col))
    const = lambda a: pl.BlockSpec(a.shape, lambda i: (0, 0))
    return pl.pallas_call(
        _out_even_kernel,
        grid=(n // tm,),
        in_specs=[row(d, 0), const(g), row(M_WIDTH, 0), row(M_WIDTH, 0), row(M_WIDTH, 1), const(ng),
                  row(LRU_WIDTH, 0), const(w1), const(w2)],
        out_specs=row(d, 0),
        out_shape=jax.ShapeDtypeStruct((n, d), F32),
        compiler_params=_cparams(("parallel",)),
        name="out_even",
    )(x2d, g, h_raw, rest, rest, ng, hl, w1, w2)


def _out_odd_kernel(x_ref, g_ref, o_ref, r_ref, ng_ref, w_ref, y_ref):
    o = _head_rms(o_ref[...].astype(F32), N_HEADS) * ng_ref[...] * _silu(r_ref[...])
    out = jnp.dot(o.astype(BF16), w_ref[...], preferred_element_type=F32)
    _residual_norm(x_ref, g_ref, y_ref, out)


def _out_odd(x2d, g, o_raw, r, ng, w, tm):
    n, d = x2d.shape
    row = lambda width: pl.BlockSpec((tm, width), lambda i: (i, 0))
    const = lambda a: pl.BlockSpec(a.shape, lambda i: (0, 0))
    return pl.pallas_call(
        _out_odd_kernel,
        grid=(n // tm,),
        in_specs=[row(d), const(g), row(G_VW), row(G_VW), const(ng), const(w)],
        out_specs=row(d),
        out_shape=jax.ShapeDtypeStruct((n, d), F32),
        compiler_params=_cparams(("parallel",)),
        name="out_odd",
    )(x2d, g, o_raw, r, ng, w)


def _mlstm_kernel(q_ref, k_ref, v_ref, gate_ref, gbias_ref, c0_ref, n0_ref, m0_ref,
                  h_ref, c_ref, n_ref, m_ref, *, nb, t, nc, l_valid):
    c = pl.program_id(1)
    if nc == 1:
        c_in, n_in, m_in = c0_ref, n0_ref, m0_ref
    else:
        c_in, n_in, m_in = c_ref, n_ref, m_ref

        @pl.when(c == 0)
        def _():
            c_ref[...] = c0_ref[...]
            n_ref[...] = n0_ref[...]
            m_ref[...] = m0_ref[...]

    pos = c * t + lax.broadcasted_iota(jnp.int32, (t, 1), 0)
    valid = pos < l_valid
    row = lax.broadcasted_iota(jnp.int32, (t, t), 0)
    col = lax.broadcasted_iota(jnp.int32, (t, t), 1)
    causal = row >= col
    lane = lax.broadcasted_iota(jnp.int32, (1, LANES), 1)
    gbias = gbias_ref[...]
    heads = [(h, slice(h * M_DK, (h + 1) * M_DK)) for h in range(N_HEADS)]

    def group(rows):
        pairs = [(i, r, h, sl) for i, r in enumerate(rows) for h, sl in heads]
        gates = []
        for r in rows:
            g = gate_ref[r] + gbias
            li = jnp.where(valid, g[:, :LANES], NEG)
            lf = jnp.where(valid, _log_sigmoid(g[:, LANES:]), 0.0)
            b = _cumsum_rows(lf)
            gates.append((li, b, _transpose_rows(li - b)[:, :t], m_in[r]))
        st = {}
        for i, r, h, sl in pairs:
            li, b, r_t, m_vec = gates[i]
            b_col = b[:, h:h + 1]
            inter = b_col + m_vec[:, h:h + 1]
            dmat = jnp.where(causal, b_col + r_t[h:h + 1, :], NEG)
            mt = jnp.maximum(inter, jnp.max(dmat, axis=-1, keepdims=True))
            k = k_ref[r, :, sl] * (M_DK ** -0.5)
            s = _dot_nt(q_ref[r, :, sl], k) * jnp.exp(dmat - mt)
            st[i, h] = (inter, mt, k, s)
        for i, r, h, sl in pairs:
            inter, mt, k, s = st[i, h]
            q = q_ref[r, :, sl]
            w_inter = jnp.exp(inter - mt)
            num = w_inter * _dot_nt(q, c_in[r, h]) + _dot(s, v_ref[r, :, sl])
            den = (w_inter * jnp.sum(q.astype(F32) * n_in[r, h], axis=-1, keepdims=True)
                   + jnp.sum(s, axis=-1, keepdims=True))
            h_ref[r, :, sl] = (num / jnp.maximum(jnp.abs(den), jnp.exp(-mt))).astype(h_ref.dtype)
        m_next = [gt[3] for gt in gates]
        for i, r, h, sl in pairs:
            li, b, _, _ = gates[i]
            inter, mt, k, _ = st[i, h]
            b_col = b[:, h:h + 1]
            m_new = mt[t - 1:t, :]
            w_c = jnp.exp(inter[t - 1:t, :] - m_new)
            w_k = jnp.exp(b_col[t - 1:t, :] - b_col + li[:, h:h + 1] - m_new)
            c_ref[r, h] = w_c * c_in[r, h] + _dot_tn(v_ref[r, :, sl].astype(F32) * w_k, k)
            n_ref[r, h] = w_c * n_in[r, h] + jnp.sum(k.astype(F32) * w_k, axis=0, keepdims=True)
            m_next[i] = jnp.where(lane == h, m_new, m_next[i])
        for i, r in enumerate(rows):
            m_ref[r] = m_next[i]

    _group_loop(nb, group)


def _mlstm(qkv3, gates3, row0, nseq, tiles, nc, l_valid, gbias, c0, n0, m0):
    nb, t = tiles
    assert nseq % nb == 0 and row0 % nb == 0
    wd = M_WIDTH
    r0 = row0 // nb
    seq = lambda col: pl.BlockSpec((nb, t, wd), lambda b, c: (r0 + b, c, col))
    st_c = pl.BlockSpec((nb, N_HEADS, M_DV, M_DK), lambda b, c: (b, 0, 0, 0))
    st_n = pl.BlockSpec((nb, N_HEADS, 1, M_DK), lambda b, c: (b, 0, 0, 0))
    st_m = pl.BlockSpec((nb, 1, LANES), lambda b, c: (b, 0, 0))
    kern = functools.partial(_mlstm_kernel, nb=nb, t=t, nc=nc, l_valid=l_valid)
    return pl.pallas_call(
        kern,
        grid=(nseq // nb, nc),
        in_specs=[
            seq(0), seq(1), seq(2),
            pl.BlockSpec((nb, t, 2 * LANES), lambda b, c: (r0 + b, c, 0)),
            pl.BlockSpec(gbias.shape, lambda b, c: (0, 0)), st_c, st_n, st_m,
        ],
        out_specs=[pl.BlockSpec((nb, t, wd), lambda b, c: (b, c, 0)), st_c, st_n, st_m],
        out_shape=[
            jax.ShapeDtypeStruct((nseq, t * nc, wd), BF16),
            jax.ShapeDtypeStruct((nseq, N_HEADS, M_DV, M_DK), F32),
            jax.ShapeDtypeStruct((nseq, N_HEADS, 1, M_DK), F32),
            jax.ShapeDtypeStruct((nseq, 1, LANES), F32),
        ],
        compiler_params=_cparams(("parallel", "arbitrary")),
        name="mlstm",
    )(qkv3, qkv3, qkv3, gates3, gbias, c0, n0, m0)


def _lru_chunk(x, gr, xs_s, h, cw, cb, wa_ref, ba, wx_ref, bx, lam, emit, spread=lambda: None):
    nb, t, w = x.shape
    ng = t // SUBLANES
    xs_s[:, SUBLANES:SUBLANES + t, :] = x
    xc = cb + cw[CONV_W - 1:CONV_W, :] * x
    for s in range(1, CONV_W):
        xc = xc + cw[CONV_W - 1 - s:CONV_W - s, :] * xs_s[:, SUBLANES - s:SUBLANES - s + t, :]
    xs_s[:, 0:SUBLANES, :] = x[:, t - SUBLANES:, :]
    spread()

    xf = xc.reshape(nb * t, w)
    ra, ri = [], []
    for n in range(LRU_BLOCKS):
        xb = xf[:, n * LRU_BS:(n + 1) * LRU_BS].astype(BF16)
        ra.append(jnp.dot(xb, wa_ref[n], preferred_element_type=F32))
        ri.append(jnp.dot(xb, wx_ref[n], preferred_element_type=F32))
    spread()
    r = _sigmoid(jnp.concatenate(ra, axis=-1) + ba)
    i = _sigmoid(jnp.concatenate(ri, axis=-1) + bx)
    log_a = -LRU_C * r * _softplus(-lam)
    a = jnp.exp(log_a)
    u = jnp.sqrt(-jnp.tanh(log_a) * (1.0 + a * a)) * (i * xf)
    spread()

    rid = lax.broadcasted_iota(jnp.int32, (1, SUBLANES, 1), 1)
    a = a.reshape(nb * ng, SUBLANES, w)
    u = u.reshape(nb * ng, SUBLANES, w)
    for s in (1, 2, 4):
        a_sh = jnp.where(rid >= s, pltpu.roll(a, s, axis=1), 1.0)
        u_sh = jnp.where(rid >= s, pltpu.roll(u, s, axis=1), 0.0)
        u = a * u_sh + u
        a = a * a_sh
    a = a.reshape(nb, ng, SUBLANES, w)
    u = u.reshape(nb, ng, SUBLANES, w)
    gr4 = gr.reshape(nb, ng, SUBLANES, w)
    for g in range(ng):
        if g == ng // 2:
            spread()
        hg = a[:, g] * h + u[:, g]
        h = hg[:, SUBLANES - 1:SUBLANES]
        emit(g, hg, hg * _silu(gr4[:, g]))
    return h


def _rglru_kernel(x_ref, gr_ref, cw_ref, cb_ref, wa_ref, ba_ref, wx_ref, bx_ref, lam_ref, h0_ref, conv0_ref,
                  hl_ref, hlast_ref, h_s, xs_s, *, nb, t, l_valid):
    c = pl.program_id(1)

    @pl.when(c == 0)
    def _():
        h_s[...] = h0_ref[...]
        xs_s[:, 0:SUBLANES, :] = conv0_ref[...]

    g_last, r_last = divmod((l_valid - 1) % t, SUBLANES)

    def emit(g, hg, gated):
        hl_ref[:, g * SUBLANES:(g + 1) * SUBLANES, :] = gated.astype(hl_ref.dtype)
        if g == g_last:
            @pl.when(c == (l_valid - 1) // t)
            def _():
                hlast_ref[...] = hg[:, r_last:r_last + 1]

    h_s[...] = _lru_chunk(x_ref[...], gr_ref[...], xs_s, h_s[...], cw_ref[...], cb_ref[...], wa_ref, ba_ref[...],
                          wx_ref, bx_ref[...], lam_ref[...], emit)


def _rglru(rest3, row0, nseq, tiles, nc, l_valid, cw, cb, wa, ba, wx, bx, lam, h0, conv0):
    nb, t = tiles
    assert nseq % nb == 0 and row0 % nb == 0
    w = LRU_WIDTH
    r0 = row0 // nb
    full2 = lambda a: pl.BlockSpec(a.shape, lambda b, c: (0, 0))
    full3 = lambda a: pl.BlockSpec(a.shape, lambda b, c: (0, 0, 0))
    kern = functools.partial(_rglru_kernel, nb=nb, t=t, l_valid=l_valid)
    return pl.pallas_call(
        kern,
        grid=(nseq // nb, nc),
        in_specs=[
            pl.BlockSpec((nb, t, w), lambda b, c: (r0 + b, c, 2)),
            pl.BlockSpec((nb, t, w), lambda b, c: (r0 + b, c, 3)),
            full2(cw), full2(cb), full3(wa), full2(ba), full3(wx), full2(bx), full2(lam),
            pl.BlockSpec((nb, 1, w), lambda b, c: (b, 0, 0)),
            pl.BlockSpec((nb, SUBLANES, w), lambda b, c: (b, 0, 0)),
        ],
        out_specs=[
            pl.BlockSpec((nb, t, w), lambda b, c: (b, c, 0)),
            pl.BlockSpec((nb, 1, w), lambda b, c: (b, 0, 0)),
        ],
        out_shape=[
            jax.ShapeDtypeStruct((nseq, t * nc, w), BF16),
            jax.ShapeDtypeStruct((nseq, 1, w), F32),
        ],
        scratch_shapes=[pltpu.VMEM((nb, 1, w), F32), pltpu.VMEM((nb, SUBLANES + t, w), F32)],
        compiler_params=_cparams(("parallel", "arbitrary")),
        name="rglru",
    )(rest3, rest3, cw, cb, wa, ba, wx, bx, lam, h0, conv0)


def _gla_kernel(q_ref, k_ref, v_ref, al_ref, aup_ref, ab_ref, s0_ref, o_ref, s_ref, *, nb, t, nc, l_valid):
    c = pl.program_id(1)
    if nc == 1:
        s_in = s0_ref
    else:
        s_in = s_ref

        @pl.when(c == 0)
        def _():
            s_ref[...] = s0_ref[...]

    pos = c * t + lax.broadcasted_iota(jnp.int32, (t, 1), 0)
    valid = pos < l_valid
    row = lax.broadcasted_iota(jnp.int32, (t, t), 0)
    col = lax.broadcasted_iota(jnp.int32, (t, t), 1)
    causal = row >= col
    aup = aup_ref[...]
    ab = ab_ref[...]
    heads = [(h, slice(h * G_DK, (h + 1) * G_DK), slice(h * G_DV, (h + 1) * G_DV)) for h in range(N_HEADS)]

    def group(rows):
        pairs = [(i, r, h, ks, vs) for i, r in enumerate(rows) for h, ks, vs in heads]
        dec = []
        for r in rows:
            pre = _dot(al_ref[r], aup) + ab
            lg = jnp.where(valid, _log_sigmoid(pre) / G_TAU, 0.0)
            b = _cumsum_rows(lg)
            b_last = b[t - 1:t, :]
            k = jnp.where(valid, k_ref[r].astype(F32), 0.0)
            qd = ((q_ref[r].astype(F32) * (G_DK ** -0.5)) * jnp.exp(b)).astype(BF16)
            kn = (k * jnp.exp(-b)).astype(BF16)
            kr = (k * jnp.exp(b_last - b)).astype(BF16)
            dec.append((qd, kn, kr, _row_to_col(jnp.exp(b_last))))
        att = {}
        for i, r, h, ks, vs in pairs:
            qd, kn, _, _ = dec[i]
            att[i, h] = jnp.where(causal, _dot_nt(qd[:, ks], kn[:, ks]), 0.0).astype(BF16)
        for i, r, h, ks, vs in pairs:
            qd, _, kr, ebl_col = dec[i]
            s_h = s_in[r, h]
            v = v_ref[r, :, vs]
            o_ref[r, :, vs] = (_dot(qd[:, ks], s_h) + _dot(att[i, h], v)).astype(o_ref.dtype)
            s_ref[r, h] = ebl_col[ks, :] * s_h + _dot_tn(kr[:, ks], v)

    _group_loop(nb, group)


def _gla(qkv3, al3, row0, nseq, tiles, nc, l_valid, aup, ab, s0):
    nb, t = tiles
    assert nseq % nb == 0 and row0 % nb == 0
    r0 = row0 // nb
    full2 = lambda a: pl.BlockSpec(a.shape, lambda b, c: (0, 0))
    st = pl.BlockSpec((nb, N_HEADS, G_DK, G_DV), lambda b, c: (b, 0, 0, 0))
    kern = functools.partial(_gla_kernel, nb=nb, t=t, nc=nc, l_valid=l_valid)
    return pl.pallas_call(
        kern,
        grid=(nseq // nb, nc),
        in_specs=[
            pl.BlockSpec((nb, t, G_KW), lambda b, c: (r0 + b, c, 0)),
            pl.BlockSpec((nb, t, G_KW), lambda b, c: (r0 + b, c, 1)),
            pl.BlockSpec((nb, t, G_VW), lambda b, c: (r0 + b, c, 1)),
            pl.BlockSpec((nb, t, LANES), lambda b, c: (r0 + b, c, 0)),
            full2(aup), full2(ab), st,
        ],
        out_specs=[pl.BlockSpec((nb, t, G_VW), lambda b, c: (b, c, 0)), st],
        out_shape=[
            jax.ShapeDtypeStruct((nseq, t * nc, G_VW), BF16),
            jax.ShapeDtypeStruct((nseq, N_HEADS, G_DK, G_DV), F32),
        ],
        compiler_params=_cparams(("parallel", "arbitrary")),
        name="gla",
    )(qkv3, qkv3, qkv3, al3, aup, ab, s0)


def _even_layer_kernel(x_ref, pre_ref, post_ref, win_ref, wgate_ref, gbias_ref, ng_ref, cw_ref, cb_ref, wa_ref,
                       ba_ref, wx_ref, bx_ref, lam_ref, w1_ref, w2_ref, c0_ref, n0_ref, m0_ref, h0_ref, conv0_ref,
                       y_ref, c_ref, n_ref, m_ref, hlast_ref, conv_ref,
                       qkv_s, rest_s, gate_s, hn_s, hm_s, hl_s, xkeep_s, hlru_s, xs_s, *, rows, tiles_per_seq):
    g = pl.program_id(0)
    slot = g % 2
    prev = 1 - slot
    t = MLSTM_TILES["main"][1]
    e_lo = 3 * M_WIDTH
    e = win_ref.shape[1]
    n_chunks = rows // t
    cols_per_chunk = e // n_chunks

    @pl.when(g == 0)
    def _():
        qkv_s[1] = jnp.zeros(qkv_s.shape[1:], qkv_s.dtype)
        rest_s[1] = jnp.zeros(rest_s.shape[1:], rest_s.dtype)
        gate_s[1] = jnp.zeros(gate_s.shape[1:], gate_s.dtype)
        xkeep_s[...] = jnp.zeros(xkeep_s.shape, xkeep_s.dtype)

    @pl.when(jnp.maximum(g - 1, 0) % tiles_per_seq == 0)
    def _():
        c_ref[...] = c0_ref[...]
        n_ref[...] = n0_ref[...]
        m_ref[...] = m0_ref[...]
        hlru_s[...] = h0_ref[...]
        xs_s[:, 0:SUBLANES, :] = conv0_ref[...]

    x = x_ref[0]
    hn_s[...] = (x * lax.rsqrt(jnp.mean(x * x, axis=-1, keepdims=True) + EPS) * pre_ref[...]).astype(BF16)
    gate_s[slot] = jnp.dot(hn_s[...], wgate_ref[...], preferred_element_type=F32)

    def project(c0, c1):
        acc = jnp.dot(hn_s[...], win_ref[:, c0:c1], preferred_element_type=F32)
        if c0 < e_lo:
            hi = min(c1, e_lo)
            qkv_s[slot, :, c0:hi] = acc[:, :hi - c0].astype(BF16)
        if c1 > e_lo:
            lo = max(c0, e_lo)
            rest_s[slot, :, lo - e_lo:c1 - e_lo] = acc[:, lo - c0:]

    row = lax.broadcasted_iota(jnp.int32, (t, t), 0)
    col = lax.broadcasted_iota(jnp.int32, (t, t), 1)
    causal = row >= col
    lane = lax.broadcasted_iota(jnp.int32, (1, LANES), 1)
    gbias = gbias_ref[...]
    heads = [(h, slice(h * M_DK, (h + 1) * M_DK)) for h in range(N_HEADS)]
    piece = 512
    pieces = [(c0, c0 + piece) for c0 in range(0, e, piece)]
    per_chunk = 5
    assert e % piece == 0 and e_lo % piece == 0 and len(pieces) >= n_chunks * per_chunk + 2

    for j in range(n_chunks):
        rs = slice(j * t, (j + 1) * t)
        todo = pieces[j * per_chunk:(j + 1) * per_chunk]
        gt = gate_s[prev, rs, :] + gbias
        li = gt[:, :LANES]
        b = _cumsum_rows(_log_sigmoid(gt[:, LANES:]))
        project(*todo[0])
        r_t = _transpose_rows(li - b)[:, :t]
        m_vec = m_ref[0]
        st = []
        for h, sl in heads:
            b_col = b[:, h:h + 1]
            inter = b_col + m_vec[:, h:h + 1]
            dmat = jnp.where(causal, b_col + r_t[h:h + 1, :], NEG)
            mt = jnp.maximum(inter, jnp.max(dmat, axis=-1, keepdims=True))
            k = qkv_s[prev, rs, M_WIDTH + h * M_DK:M_WIDTH + (h + 1) * M_DK] * (M_DK ** -0.5)
            s = _dot_nt(qkv_s[prev, rs, sl], k) * jnp.exp(dmat - mt)
            st.append((inter, mt, k, s))
        project(*todo[1])
        project(*todo[2])
        for h, sl in heads:
            inter, mt, k, s = st[h]
            q = qkv_s[prev, rs, sl]
            v = qkv_s[prev, rs, 2 * M_WIDTH + h * M_DV:2 * M_WIDTH + (h + 1) * M_DV]
            w_inter = jnp.exp(inter - mt)
            num = w_inter * _dot_nt(q, c_ref[0, h]) + _dot(s, v)
            den = (w_inter * jnp.sum(q.astype(F32) * n_ref[0, h], axis=-1, keepdims=True)
                   + jnp.sum(s, axis=-1, keepdims=True))
            hm_s[rs, sl] = (num / jnp.maximum(jnp.abs(den), jnp.exp(-mt))).astype(hm_s.dtype)
        project(*todo[3])
        m_next = m_vec
        for h, sl in heads:
            inter, mt, k, _ = st[h]
            v = qkv_s[prev, rs, 2 * M_WIDTH + h * M_DV:2 * M_WIDTH + (h + 1) * M_DV]
            b_col = b[:, h:h + 1]
            m_new = mt[t - 1:t, :]
            w_c = jnp.exp(inter[t - 1:t, :] - m_new)
            w_k = jnp.exp(b_col[t - 1:t, :] - b_col + li[:, h:h + 1] - m_new)
            c_ref[0, h] = w_c * c_ref[0, h] + _dot_tn(v.astype(F32) * w_k, k)
            n_ref[0, h] = w_c * n_ref[0, h] + jnp.sum(k.astype(F32) * w_k, axis=0, keepdims=True)
            m_next = jnp.where(lane == h, m_new, m_next)
        m_ref[0] = m_next
        project(*todo[4])

    w = LRU_WIDTH
    queue = list(pieces[n_chunks * per_chunk:])

    def spread():
        for p in queue[:2]:
            project(*p)
        del queue[:2]

    def emit(gi, hg, gated):
        hl_s[gi * SUBLANES:(gi + 1) * SUBLANES, :] = gated[0].astype(hl_s.dtype)

    h_end = _lru_chunk(rest_s[prev, :, 2 * w:3 * w][None], rest_s[prev, :, 3 * w:4 * w][None], xs_s, hlru_s[...],
                       cw_ref[...], cb_ref[...], wa_ref, ba_ref[...], wx_ref, bx_ref[...], lam_ref[...], emit, spread)
    for p in queue:
        project(*p)
    hlru_s[...] = h_end
    hlast_ref[...] = h_end
    conv_ref[...] = xs_s[:, 0:SUBLANES, :]

    hm = _sigmoid(rest_s[prev, :, 0:w]) * hm_s[...].astype(F32)
    hm = _head_rms(hm, N_HEADS) * ng_ref[...] * _silu(rest_s[prev, :, w:2 * w])
    out = jnp.dot(hm.astype(BF16), w1_ref[...], preferred_element_type=F32)
    out = out + jnp.dot(hl_s[...], w2_ref[...], preferred_element_type=F32)
    nrm = out * lax.rsqrt(jnp.mean(out * out, axis=-1, keepdims=True) + EPS) * post_ref[...]
    y_ref[0] = xkeep_s[...] + nrm
    xkeep_s[...] = x


def _even_layer_main(x3, pre, post, w_main, w_gate, gbias, ng, lru, w1, w2, c0, n0, m0, h0, conv0, rows):
    nseq, seq, d = x3.shape
    assert seq % rows == 0 and rows % MLSTM_TILES["main"][1] == 0
    tps = seq // rows
    n_tiles = nseq * tps
    e = w_main.shape[1]
    e_lo = 3 * M_WIDTH
    w = LRU_WIDTH
    const = lambda a: pl.BlockSpec(a.shape, lambda g: (0,) * a.ndim, pipeline_mode=pl.Buffered(1))
    cur = lambda g: jnp.minimum(g, n_tiles - 1)
    old = lambda g: jnp.maximum(g - 1, 0)
    per_seq = lambda *blk: pl.BlockSpec((1,) + blk, lambda g: (old(g) // tps,) + (0,) * len(blk))
    st_c, st_n, st_m = per_seq(N_HEADS, M_DV, M_DK), per_seq(N_HEADS, 1, M_DK), per_seq(1, LANES)
    st_h, st_conv = per_seq(1, w), per_seq(SUBLANES, w)
    kern = functools.partial(_even_layer_kernel, rows=rows, tiles_per_seq=tps)
    return pl.pallas_call(
        kern,
        grid=(n_tiles + 1,),
        in_specs=[
            pl.BlockSpec((1, rows, d), lambda g: (cur(g) // tps, cur(g) % tps, 0)),
            const(pre), const(post), const(w_main), const(w_gate), const(gbias), const(ng),
            *[const(a) for a in lru], const(w1), const(w2), st_c, st_n, st_m, st_h, st_conv,
        ],
        out_specs=[
            pl.BlockSpec((1, rows, d), lambda g: (old(g) // tps, old(g) % tps, 0)),
            st_c, st_n, st_m, st_h, st_conv,
        ],
        out_shape=[
            jax.ShapeDtypeStruct((nseq, seq, d), F32),
            jax.ShapeDtypeStruct((nseq, N_HEADS, M_DV, M_DK), F32),
            jax.ShapeDtypeStruct((nseq, N_HEADS, 1, M_DK), F32),
            jax.ShapeDtypeStruct((nseq, 1, LANES), F32),
            jax.ShapeDtypeStruct((nseq, 1, w), F32),
            jax.ShapeDtypeStruct((nseq, SUBLANES, w), F32),
        ],
        scratch_shapes=[
            pltpu.VMEM((2, rows, e_lo), BF16),
            pltpu.VMEM((2, rows, e - e_lo), F32),
            pltpu.VMEM((2, rows, 2 * LANES), F32),
            pltpu.VMEM((rows, d), BF16),
            pltpu.VMEM((rows, M_WIDTH), BF16),
            pltpu.VMEM((rows, w), BF16),
            pltpu.VMEM((rows, d), F32),
            pltpu.VMEM((1, 1, w), F32),
            pltpu.VMEM((1, SUBLANES + rows, w), F32),
        ],
        compiler_params=_cparams(("arbitrary",)),
        name="even_layer",
    )(x3, pre, post, w_main, w_gate, gbias, ng, *lru, w1, w2, c0, n0, m0, h0, conv0)


def _gla_layer_kernel(x_ref, pre_ref, post_ref, win_ref, wlow_ref, aup_ref, ab_ref, ng_ref, wout_ref, s0_ref,
                      y_ref, s_ref, qkv_s, r_s, al_s, hn_s, o_s, xkeep_s, *, rows, tiles_per_seq):
    g = pl.program_id(0)
    slot = g % 2
    prev = 1 - slot
    t = GLA_TILES["main"][1]
    e_lo = 2 * G_KW + G_VW
    e = e_lo + G_VW
    n_chunks = rows // t
    cols_per_chunk = e // n_chunks

    @pl.when(g == 0)
    def _():
        qkv_s[1] = jnp.zeros(qkv_s.shape[1:], qkv_s.dtype)
        r_s[1] = jnp.zeros(r_s.shape[1:], r_s.dtype)
        al_s[1] = jnp.zeros(al_s.shape[1:], al_s.dtype)
        xkeep_s[...] = jnp.zeros(xkeep_s.shape, xkeep_s.dtype)

    @pl.when(jnp.maximum(g - 1, 0) % tiles_per_seq == 0)
    def _():
        s_ref[...] = s0_ref[...]

    x = x_ref[0]
    hn_s[...] = (x * lax.rsqrt(jnp.mean(x * x, axis=-1, keepdims=True) + EPS) * pre_ref[...]).astype(BF16)
    al_s[slot] = jnp.dot(hn_s[...], wlow_ref[...], preferred_element_type=F32)

    row = lax.broadcasted_iota(jnp.int32, (t, t), 0)
    col = lax.broadcasted_iota(jnp.int32, (t, t), 1)
    causal = row >= col
    aup = aup_ref[...]
    ab = ab_ref[...]
    heads = [(h, slice(h * G_DK, (h + 1) * G_DK), slice(h * G_DV, (h + 1) * G_DV)) for h in range(N_HEADS)]

    def project(c0, c1):
        acc = jnp.dot(hn_s[...], win_ref[:, c0:c1], preferred_element_type=F32)
        if c0 < e_lo:
            hi = min(c1, e_lo)
            qkv_s[slot, :, c0:hi] = acc[:, :hi - c0].astype(BF16)
        if c1 > e_lo:
            lo = max(c0, e_lo)
            r_s[slot, :, lo - e_lo:c1 - e_lo] = acc[:, lo - c0:]

    n_sub = 3
    sub = cols_per_chunk // n_sub
    for j in range(n_chunks):
        c0 = j * cols_per_chunk
        rs = slice(j * t, (j + 1) * t)
        pre = _dot(al_s[prev, rs, :], aup) + ab
        project(c0, c0 + sub)
        b = _cumsum_rows(_log_sigmoid(pre) / G_TAU)
        b_last = b[t - 1:t, :]
        project(c0 + sub, c0 + 2 * sub)
        k = qkv_s[prev, rs, G_KW:2 * G_KW].astype(F32)
        qd = ((qkv_s[prev, rs, 0:G_KW].astype(F32) * (G_DK ** -0.5)) * jnp.exp(b)).astype(BF16)
        kn = (k * jnp.exp(-b)).astype(BF16)
        kr = (k * jnp.exp(b_last - b)).astype(BF16)
        ebl_col = _row_to_col(jnp.exp(b_last))
        att = [jnp.where(causal, _dot_nt(qd[:, ks], kn[:, ks]), 0.0).astype(BF16) for _, ks, _ in heads]
        project(c0 + 2 * sub, c0 + cols_per_chunk)
        for h, ks, vs in heads:
            s_h = s_ref[0, h]
            v = qkv_s[prev, rs, 2 * G_KW + h * G_DV:2 * G_KW + (h + 1) * G_DV]
            o_s[rs, vs] = (_dot(qd[:, ks], s_h) + _dot(att[h], v)).astype(o_s.dtype)
            s_ref[0, h] = ebl_col[ks, :] * s_h + _dot_tn(kr[:, ks], v)

    o = _head_rms(o_s[...].astype(F32), N_HEADS) * ng_ref[...] * _silu(r_s[prev])
    out = jnp.dot(o.astype(BF16), wout_ref[...], preferred_element_type=F32)
    nrm = out * lax.rsqrt(jnp.mean(out * out, axis=-1, keepdims=True) + EPS) * post_ref[...]
    y_ref[0] = xkeep_s[...] + nrm
    xkeep_s[...] = x


def _gla_layer(x3, pre, post, w_in, w_low, aup, ab, ng, w_out, s0, rows):
    nseq, seq, d = x3.shape
    assert seq % rows == 0 and rows % GLA_TILES["main"][1] == 0
    tps = seq // rows
    n_tiles = nseq * tps
    e = w_in.shape[1]
    e_lo = 2 * G_KW + G_VW
    const = lambda a: pl.BlockSpec(a.shape, lambda g: (0,) * a.ndim, pipeline_mode=pl.Buffered(1))
    cur = lambda g: jnp.minimum(g, n_tiles - 1)
    old = lambda g: jnp.maximum(g - 1, 0)
    st = pl.BlockSpec((1, N_HEADS, G_DK, G_DV), lambda g: (old(g) // tps, 0, 0, 0))
    kern = functools.partial(_gla_layer_kernel, rows=rows, tiles_per_seq=tps)
    return pl.pallas_call(
        kern,
        grid=(n_tiles + 1,),
        in_specs=[
            pl.BlockSpec((1, rows, d), lambda g: (cur(g) // tps, cur(g) % tps, 0)),
            const(pre), const(post), const(w_in), const(w_low), const(aup), const(ab), const(ng), const(w_out), st,
        ],
        out_specs=[pl.BlockSpec((1, rows, d), lambda g: (old(g) // tps, old(g) % tps, 0)), st],
        out_shape=[
            jax.ShapeDtypeStruct((nseq, seq, d), F32),
            jax.ShapeDtypeStruct((nseq, N_HEADS, G_DK, G_DV), F32),
        ],
        scratch_shapes=[
            pltpu.VMEM((2, rows, e_lo), BF16),
            pltpu.VMEM((2, rows, e - e_lo), F32),
            pltpu.VMEM((2, rows, LANES), F32),
            pltpu.VMEM((rows, d), BF16),
            pltpu.VMEM((rows, G_VW), BF16),
            pltpu.VMEM((rows, d), F32),
        ],
        compiler_params=_cparams(("arbitrary",)),
        name="gla_layer",
    )(x3, pre, post, w_in, w_low, aup, ab, ng, w_out, s0)


def _pad_lanes(a, width):
    return jnp.pad(a, [(0, 0)] * (a.ndim - 1) + [(0, width - a.shape[-1])])


def _even_layer(xm, xs, nbp, seq, nbs, ls, st, pre_g, post_g, w_in, w_out, b_i, b_f, m_norm_g,
                conv_w, conv_b, wa, ba, wx, bx, lam):
    c0s, n0s, m0s, h0s, conv0s = st
    wq, wk, wv, wog, wig, wfg, wz, wxr, wgr = jnp.split(
        w_in, [1024, 2048, 3072, 4096, 4100, 4104, 5128, 6152], axis=1)
    w_main = jnp.concatenate([wq, wk, wv, wog, wz, wxr, wgr], axis=1).astype(BF16)
    w_gate = jnp.concatenate([_pad_lanes(wig, LANES), _pad_lanes(wfg, LANES)], axis=1).astype(BF16)
    gbias = jnp.concatenate([_pad_lanes(b_i[None], LANES), _pad_lanes(b_f[None], LANES)], axis=1)
    ng = m_norm_g[None]
    post = post_g[None]
    cw, cb = conv_w, conv_b[None]
    wab, wxb = wa.astype(BF16), wx.astype(BF16)
    bav, bxv, lamv = ba[None], bx[None], lam[None]
    w_o1, w_o2 = w_out[:M_WIDTH].astype(BF16), w_out[M_WIDTH:].astype(BF16)
    e_lo = 3 * M_WIDTH
    e_hi = w_main.shape[1] - e_lo
    lru = (cw, cb, wab, bav, wxb, bxv, lamv)

    qs, ps, gs = _norm_proj(xs, pre_g, w_main, w_gate, e_lo, xs.shape[0])
    n_meta_rows = nbp * N_META
    xr_cols = slice(2 * LRU_WIDTH, 3 * LRU_WIDTH)

    zc = jnp.zeros((nbp, N_HEADS, M_DV, M_DK), F32)
    zn = jnp.zeros((nbp, N_HEADS, 1, M_DK), F32)
    zm = jnp.zeros((nbp, 1, LANES), F32)
    zh = jnp.zeros((nbp, 1, LRU_WIDTH), F32)
    zconv = jnp.zeros((nbp, SUBLANES, LRU_WIDTH), F32)
    qs_meta = qs.reshape(-1, N_META, e_lo)
    ps_meta = ps.reshape(-1, N_META, e_hi)
    gs_meta = gs.reshape(-1, N_META, 2 * LANES)
    h_meta, c1, n1, m1 = _mlstm(qs_meta, gs_meta, 0, nbp, MLSTM_TILES["meta"], 1, N_META, gbias, zc, zn, zm)
    hl_meta, h1 = _rglru(ps_meta, 0, nbp, LRU_TILES["meta"], 1, N_META, *lru, zh, zconv)
    conv1 = ps_meta[:nbp, N_META - SUBLANES:, xr_cols]

    ym3, pc, pn, pmm, ph, conv_tail = _even_layer_main(
        xm.reshape(nbp, seq, -1), pre_g[None], post, w_main, w_gate, gbias, ng, lru, w_o1, w_o2,
        c1, n1, m1, h1, conv1, FUSED_ROW_TILE)
    ym = ym3.reshape(xm.shape)
    pconv = conv_tail[:, SUBLANES - (CONV_W - 1):, :]

    qs_s = qs.reshape(-1, SAMPLE_PAD, e_lo)
    ps_s = ps.reshape(-1, SAMPLE_PAD, e_hi)
    gs_s = gs.reshape(-1, SAMPLE_PAD, 2 * LANES)
    r0 = n_meta_rows // SAMPLE_PAD
    h_s, sc, sn, sm = _mlstm(qs_s, gs_s, r0, nbs, MLSTM_TILES["sample"], 1, ls, gbias,
                             c0s, n0s[:, :, None, :], _pad_lanes(m0s, LANES)[:, None, :])
    conv0p = jnp.pad(conv0s, ((0, 0), (SUBLANES - (CONV_W - 1), 0), (0, 0)))
    hl_s, sh = _rglru(ps_s, r0, nbs, LRU_TILES["sample"], 1, ls, *lru, h0s[:, None, :], conv0p)
    xr_s = ps_s[r0:, :ls, xr_cols]
    sconv = jnp.concatenate([conv0s, xr_s], axis=1)[:, -(CONV_W - 1):]

    h_small = jnp.concatenate([h_meta.reshape(-1, M_WIDTH), h_s.reshape(-1, M_WIDTH)], axis=0)
    hl_small = jnp.concatenate([hl_meta.reshape(-1, LRU_WIDTH), hl_s.reshape(-1, LRU_WIDTH)], axis=0)
    ys = _out_even(xs, post, h_small, ps, ng, hl_small, w_o1, w_o2, _row_tile(xs.shape[0], OUT_ROW_TILE))

    p_state = (pc, pn[:, :, 0, :], pmm[:, 0, :N_HEADS], ph[:, 0, :], pconv)
    s_state = (sc, sn[:, :, 0, :], sm[:, 0, :N_HEADS], sh[:, 0, :], sconv)
    return ym, ys, p_state, s_state


def _odd_layer(xm, xs, nbp, seq, nbs, ls, s0s, pre_g, post_g, w_in, w_out, a_up, a_b, g_norm_g):
    e_lo = 2 * G_KW + G_VW
    w_main = w_in[:, :2 * G_KW + 2 * G_VW].astype(BF16)
    w_low = _pad_lanes(w_in[:, 2 * G_KW + 2 * G_VW:], LANES).astype(BF16)
    aup = jnp.pad(a_up, ((0, LANES - G_RANK), (0, 0))).astype(BF16)
    ab = a_b[None]
    ng = g_norm_g[None]
    post = post_g[None]
    w_o = w_out.astype(BF16)

    qs, rs, as_ = _norm_proj(xs, pre_g, w_main, w_low, e_lo, xs.shape[0])
    n_meta_rows = nbp * N_META

    zs = jnp.zeros((nbp, N_HEADS, G_DK, G_DV), F32)
    o_meta, s1 = _gla(qs.reshape(-1, N_META, e_lo), as_.reshape(-1, N_META, LANES),
                      0, nbp, GLA_TILES["meta"], 1, N_META, aup, ab, zs)
    ym3, p_s = _gla_layer(xm.reshape(nbp, seq, -1), pre_g[None], post, w_main, w_low, aup, ab, ng, w_o, s1,
                          FUSED_ROW_TILE)
    ym = ym3.reshape(xm.shape)
    r0 = n_meta_rows // SAMPLE_PAD
    o_s, s_s = _gla(qs.reshape(-1, SAMPLE_PAD, e_lo), as_.reshape(-1, SAMPLE_PAD, LANES),
                    r0, nbs, GLA_TILES["sample"], 1, ls, aup, ab, s0s)

    o_small = jnp.concatenate([o_meta.reshape(-1, G_VW), o_s.reshape(-1, G_VW)], axis=0)
    ys = _out_odd(xs, post, o_small, rs, ng, w_o, _row_tile(xs.shape[0], OUT_ROW_TILE))
    return ym, ys, p_s, s_s


def kernel(x_prompt, x_sample, state_mlstm_C, state_mlstm_n, state_mlstm_m, state_rglru_h, state_rglru_conv,
           state_gla_S, meta_tokens, pre_norm_a, post_norm_a, w_in_a, w_out_a, mlstm_b_i, mlstm_b_f, mlstm_norm,
           conv_w, conv_b, lru_w_a, lru_b_a, lru_w_x, lru_b_x, lru_lambda, pre_norm_c, post_norm_c, w_in_c,
           w_out_c, gla_alpha_up, gla_alpha_b, gla_norm):
    nbp, seq, d = x_prompt.shape
    nbs, ls, _ = x_sample.shape
    depth = pre_norm_a.shape[0] + pre_norm_c.shape[0]
    assert ls >= CONV_W - 1 and ls <= SAMPLE_PAD and N_META % SUBLANES == 0

    xm = x_prompt.reshape(nbp * seq, d)
    meta = jnp.broadcast_to(meta_tokens[None].astype(x_prompt.dtype), (nbp, N_META, d)).reshape(nbp * N_META, d)
    xs_pad = jnp.pad(x_sample, ((0, 0), (0, SAMPLE_PAD - ls), (0, 0))).reshape(nbs * SAMPLE_PAD, d)
    xs = jnp.concatenate([meta, xs_pad], axis=0)

    p_lists = [[] for _ in range(6)]
    s_lists = [[] for _ in range(6)]
    for layer in range(depth):
        j = layer // 2
        if layer % 2 == 0:
            st = (state_mlstm_C[j], state_mlstm_n[j], state_mlstm_m[j], state_rglru_h[j], state_rglru_conv[j])
            xm, xs, pst, sst = _even_layer(
                xm, xs, nbp, seq, nbs, ls, st, pre_norm_a[j], post_norm_a[j], w_in_a[j], w_out_a[j],
                mlstm_b_i[j], mlstm_b_f[j], mlstm_norm[j], conv_w[j], conv_b[j], lru_w_a[j], lru_b_a[j],
                lru_w_x[j], lru_b_x[j], lru_lambda[j])
            for i in range(5):
                p_lists[i].append(pst[i])
                s_lists[i].append(sst[i])
        else:
            xm, xs, p_s, s_s = _odd_layer(
                xm, xs, nbp, seq, nbs, ls, state_gla_S[j], pre_norm_c[j], post_norm_c[j], w_in_c[j], w_out_c[j],
                gla_alpha_up[j], gla_alpha_b[j], gla_norm[j])
            p_lists[5].append(p_s)
            s_lists[5].append(s_s)

    y_prompt = xm.reshape(nbp, seq, d)
    y_sample = xs[nbp * N_META:].reshape(nbs, SAMPLE_PAD, d)[:, :ls]
    return (y_prompt, y_sample) + tuple(jnp.stack(l) for l in p_lists) + tuple(jnp.stack(l) for l in s_lists)
```

```python
import functools

import jax
import jax.numpy as jnp
from jax import lax
from jax.experimental import pallas as pl
from jax.experimental.pallas import tpu as pltpu

F32 = jnp.float32
BF16 = jnp.bfloat16

D_MODEL = 1024
N_META = 16
EPS = 1e-6
N_HEADS = 4
M_DK = 256
M_DV = 256
M_WIDTH = N_HEADS * M_DV
LRU_WIDTH = 1024
LRU_BLOCKS = 8
LRU_BS = LRU_WIDTH // LRU_BLOCKS
CONV_W = 4
LRU_C = 8.0
G_DK = 256
G_DV = 512
G_KW = N_HEADS * G_DK
G_VW = N_HEADS * G_DV
G_RANK = 16
G_TAU = 16.0

LANES = 128
SUBLANES = 8
NEG = -1e30
SAMPLE_PAD = SUBLANES
VMEM_LIMIT = 56 * 1024 * 1024
COL_TILE = 1024

MLSTM_TILES = {"meta": (8, N_META), "main": (4, 128), "sample": (4, SAMPLE_PAD)}
GLA_TILES = {"meta": (8, N_META), "main": (2, 64), "sample": (4, SAMPLE_PAD)}
LRU_TILES = {"meta": (8, N_META), "main": (1, 256), "sample": (16, SAMPLE_PAD)}
OUT_ROW_TILE = 512
FUSED_ROW_TILE = 256
LRU_SUB = 64


def _row_tile(n, cap):
    return next(tm for tm in range(min(n, cap) // SUBLANES * SUBLANES, 0, -SUBLANES) if n % tm == 0)


def _cparams(sem):
    return pltpu.CompilerParams(dimension_semantics=sem, vmem_limit_bytes=VMEM_LIMIT)


def _sigmoid(x):
    return 0.5 * jnp.tanh(0.5 * x) + 0.5


def _silu(x):
    return x * _sigmoid(x)


def _log_sigmoid(x):
    return jnp.minimum(x, 0.0) - jnp.log1p(jnp.exp(-jnp.abs(x)))


def _softplus(x):
    return jnp.maximum(x, 0.0) + jnp.log1p(jnp.exp(-jnp.abs(x)))


def _dot(a, b):
    return jnp.dot(a.astype(BF16), b.astype(BF16), preferred_element_type=F32)


def _dot_nt(a, b):
    return lax.dot_general(a.astype(BF16), b.astype(BF16), (((1,), (1,)), ((), ())),
                           preferred_element_type=F32)


def _dot_tn(a, b):
    return lax.dot_general(a.astype(BF16), b.astype(BF16), (((0,), (0,)), ((), ())),
                           preferred_element_type=F32)


def _transpose_rows(x):
    t, n = x.shape
    if t < LANES:
        x = jnp.concatenate([x, jnp.zeros((LANES - t, n), x.dtype)], axis=0)
    return jnp.transpose(x)


def _row_to_col(r):
    return jnp.transpose(jnp.broadcast_to(r, (LANES, r.shape[1])))[:, 0:1]


def _cumsum_rows(x):
    t = x.shape[0]
    if t <= 2 * SUBLANES:
        rid = lax.broadcasted_iota(jnp.int32, (t, 1), 0)
        s = 1
        while s < t:
            x = x + jnp.where(rid >= s, pltpu.roll(x, s, axis=0), 0.0)
            s *= 2
        return x
    row = lax.broadcasted_iota(jnp.int32, (t, t), 0)
    col = lax.broadcasted_iota(jnp.int32, (t, t), 1)
    tri = jnp.where(row >= col, 1.0, 0.0).astype(BF16)
    hi = x.astype(BF16)
    r1 = x - hi.astype(F32)
    mid = r1.astype(BF16)
    lo = (r1 - mid.astype(F32)).astype(BF16)
    acc = jnp.dot(tri, lo, preferred_element_type=F32)
    acc = acc + jnp.dot(tri, mid, preferred_element_type=F32)
    return acc + jnp.dot(tri, hi, preferred_element_type=F32)


def _head_rms(x, nh):
    hd = x.shape[1] // nh
    parts = []
    for h in range(nh):
        xh = x[:, h * hd:(h + 1) * hd]
        parts.append(xh * lax.rsqrt(jnp.mean(xh * xh, axis=-1, keepdims=True) + EPS))
    return jnp.concatenate(parts, axis=-1)


GROUP = 2


def _group_loop(nb, group_fn):
    if nb <= GROUP:
        group_fn(list(range(nb)))
        return

    def body(i, carry):
        group_fn([GROUP * i + r for r in range(GROUP)])
        return carry

    lax.fori_loop(0, nb // GROUP, body, 0)


def _norm_proj_kernel(n_lo, x_ref, g_ref, wm_ref, ws_ref, olo_ref, ohi_ref, os_ref, hn_ref):
    j = pl.program_id(1)

    @pl.when(j == 0)
    def _():
        x = x_ref[...]
        y = x * lax.rsqrt(jnp.mean(x * x, axis=-1, keepdims=True) + EPS) * g_ref[...]
        hn = y.astype(BF16)
        hn_ref[...] = hn
        os_ref[...] = jnp.dot(hn, ws_ref[...], preferred_element_type=F32)

    acc = jnp.dot(hn_ref[...], wm_ref[...], preferred_element_type=F32)

    @pl.when(j < n_lo)
    def _():
        olo_ref[...] = acc.astype(olo_ref.dtype)

    @pl.when(j >= n_lo)
    def _():
        ohi_ref[...] = acc


def _norm_proj(x2d, g, w_main, w_small, e_lo, tm):
    n, d = x2d.shape
    e = w_main.shape[1]
    es = w_small.shape[1]
    tn = COL_TILE
    n_lo = e_lo // tn
    return pl.pallas_call(
        functools.partial(_norm_proj_kernel, n_lo),
        grid=(n // tm, e // tn),
        in_specs=[
            pl.BlockSpec((tm, d), lambda i, j: (i, 0)),
            pl.BlockSpec((1, d), lambda i, j: (0, 0)),
            pl.BlockSpec((d, tn), lambda i, j: (0, j)),
            pl.BlockSpec((d, es), lambda i, j: (0, 0)),
        ],
        out_specs=[
            pl.BlockSpec((tm, tn), lambda i, j: (i, jnp.minimum(j, n_lo - 1))),
            pl.BlockSpec((tm, tn), lambda i, j: (i, jnp.maximum(j - n_lo, 0))),
            pl.BlockSpec((tm, es), lambda i, j: (i, 0)),
        ],
        out_shape=[
            jax.ShapeDtypeStruct((n, e_lo), BF16),
            jax.ShapeDtypeStruct((n, e - e_lo), F32),
            jax.ShapeDtypeStruct((n, es), F32),
        ],
        scratch_shapes=[pltpu.VMEM((tm, d), BF16)],
        compiler_params=_cparams(("parallel", "arbitrary")),
        name="norm_proj",
    )(x2d, g.reshape(1, d), w_main, w_small)


def _residual_norm(x_ref, g_ref, y_ref, out):
    nrm = out * lax.rsqrt(jnp.mean(out * out, axis=-1, keepdims=True) + EPS) * g_ref[...]
    y_ref[...] = x_ref[...] + nrm


def _out_even_kernel(x_ref, g_ref, h_ref, og_ref, z_ref, ng_ref, hl_ref, w1_ref, w2_ref, y_ref):
    hm = _sigmoid(og_ref[...]) * h_ref[...].astype(F32)
    hm = _head_rms(hm, N_HEADS) * ng_ref[...] * _silu(z_ref[...])
    out = jnp.dot(hm.astype(BF16), w1_ref[...], preferred_element_type=F32)
    out = out + jnp.dot(hl_ref[...], w2_ref[...], preferred_element_type=F32)
    _residual_norm(x_ref, g_ref, y_ref, out)


def _out_even(x2d, g, h_raw, rest, ng, hl, w1, w2, tm):
    n, d = x2d.shape
    row = lambda width, col: pl.BlockSpec((tm, width), lambda i: (i, col))
    const = lambda a: pl.BlockSpec(a.shape, lambda i: (0, 0))
    return pl.pallas_call(
        _out_even_kernel,
        grid=(n // tm,),
        in_specs=[row(d, 0), const(g), row(M_WIDTH, 0), row(M_WIDTH, 0), row(M_WIDTH, 1), const(ng),
                  row(LRU_WIDTH, 0), const(w1), const(w2)],
        out_specs=row(d, 0),
        out_shape=jax.ShapeDtypeStruct((n, d), F32),
        compiler_params=_cparams(("parallel",)),
        name="out_even",
    )(x2d, g, h_raw, rest, rest, ng, hl, w1, w2)


def _out_odd_kernel(x_ref, g_ref, o_ref, r_ref, ng_ref, w_ref, y_ref):
    o = _head_rms(o_ref[...].astype(F32), N_HEADS) * ng_ref[...] * _silu(r_ref[...])
    out = jnp.dot(o.astype(BF16), w_ref[...], preferred_element_type=F32)
    _residual_norm(x_ref, g_ref, y_ref, out)


def _out_odd(x2d, g, o_raw, r, ng, w, tm):
    n, d = x2d.shape
    row = lambda width: pl.BlockSpec((tm, width), lambda i: (i, 0))
    const = lambda a: pl.BlockSpec(a.shape, lambda i: (0, 0))
    return pl.pallas_call(
        _out_odd_kernel,
        grid=(n // tm,),
        in_specs=[row(d), const(g), row(G_VW), row(G_VW), const(ng), const(w)],
        out_specs=row(d),
        out_shape=jax.ShapeDtypeStruct((n, d), F32),
        compiler_params=_cparams(("parallel",)),
        name="out_odd",
    )(x2d, g, o_raw, r, ng, w)


def _mlstm_kernel(q_ref, k_ref, v_ref, gate_ref, gbias_ref, c0_ref, n0_ref, m0_ref,
                  h_ref, c_ref, n_ref, m_ref, *, nb, t, nc, l_valid):
    c = pl.program_id(1)
    if nc == 1:
        c_in, n_in, m_in = c0_ref, n0_ref, m0_ref
    else:
        c_in, n_in, m_in = c_ref, n_ref, m_ref

        @pl.when(c == 0)
        def _():
            c_ref[...] = c0_ref[...]
            n_ref[...] = n0_ref[...]
            m_ref[...] = m0_ref[...]

    pos = c * t + lax.broadcasted_iota(jnp.int32, (t, 1), 0)
    valid = pos < l_valid
    row = lax.broadcasted_iota(jnp.int32, (t, t), 0)
    col = lax.broadcasted_iota(jnp.int32, (t, t), 1)
    causal = row >= col
    lane = lax.broadcasted_iota(jnp.int32, (1, LANES), 1)
    gbias = gbias_ref[...]
    heads = [(h, slice(h * M_DK, (h + 1) * M_DK)) for h in range(N_HEADS)]

    def group(rows):
        pairs = [(i, r, h, sl) for i, r in enumerate(rows) for h, sl in heads]
        gates = []
        for r in rows:
            g = gate_ref[r] + gbias
            li = jnp.where(valid, g[:, :LANES], NEG)
            lf = jnp.where(valid, _log_sigmoid(g[:, LANES:]), 0.0)
            b = _cumsum_rows(lf)
            gates.append((li, b, _transpose_rows(li - b)[:, :t], m_in[r]))
        st = {}
        for i, r, h, sl in pairs:
            li, b, r_t, m_vec = gates[i]
            b_col = b[:, h:h + 1]
            inter = b_col + m_vec[:, h:h + 1]
            dmat = jnp.where(causal, b_col + r_t[h:h + 1, :], NEG)
            mt = jnp.maximum(inter, jnp.max(dmat, axis=-1, keepdims=True))
            k = k_ref[r, :, sl] * (M_DK ** -0.5)
            s = _dot_nt(q_ref[r, :, sl], k) * jnp.exp(dmat - mt)
            st[i, h] = (inter, mt, k, s)
        for i, r, h, sl in pairs:
            inter, mt, k, s = st[i, h]
            q = q_ref[r, :, sl]
            w_inter = jnp.exp(inter - mt)
            num = w_inter * _dot_nt(q, c_in[r, h]) + _dot(s, v_ref[r, :, sl])
            den = (w_inter * jnp.sum(q.astype(F32) * n_in[r, h], axis=-1, keepdims=True)
                   + jnp.sum(s, axis=-1, keepdims=True))
            h_ref[r, :, sl] = (num / jnp.maximum(jnp.abs(den), jnp.exp(-mt))).astype(h_ref.dtype)
        m_next = [gt[3] for gt in gates]
        for i, r, h, sl in pairs:
            li, b, _, _ = gates[i]
            inter, mt, k, _ = st[i, h]
            b_col = b[:, h:h + 1]
            m_new = mt[t - 1:t, :]
            w_c = jnp.exp(inter[t - 1:t, :] - m_new)
            w_k = jnp.exp(b_col[t - 1:t, :] - b_col + li[:, h:h + 1] - m_new)
            c_ref[r, h] = w_c * c_in[r, h] + _dot_tn(v_ref[r, :, sl].astype(F32) * w_k, k)
            n_ref[r, h] = w_c * n_in[r, h] + jnp.sum(k.astype(F32) * w_k, axis=0, keepdims=True)
            m_next[i] = jnp.where(lane == h, m_new, m_next[i])
        for i, r in enumerate(rows):
            m_ref[r] = m_next[i]

    _group_loop(nb, group)


def _mlstm(qkv3, gates3, row0, nseq, tiles, nc, l_valid, gbias, c0, n0, m0):
    nb, t = tiles
    assert nseq % nb == 0 and row0 % nb == 0
    wd = M_WIDTH
    r0 = row0 // nb
    seq = lambda col: pl.BlockSpec((nb, t, wd), lambda b, c: (r0 + b, c, col))
    st_c = pl.BlockSpec((nb, N_HEADS, M_DV, M_DK), lambda b, c: (b, 0, 0, 0))
    st_n = pl.BlockSpec((nb, N_HEADS, 1, M_DK), lambda b, c: (b, 0, 0, 0))
    st_m = pl.BlockSpec((nb, 1, LANES), lambda b, c: (b, 0, 0))
    kern = functools.partial(_mlstm_kernel, nb=nb, t=t, nc=nc, l_valid=l_valid)
    return pl.pallas_call(
        kern,
        grid=(nseq // nb, nc),
        in_specs=[
            seq(0), seq(1), seq(2),
            pl.BlockSpec((nb, t, 2 * LANES), lambda b, c: (r0 + b, c, 0)),
            pl.BlockSpec(gbias.shape, lambda b, c: (0, 0)), st_c, st_n, st_m,
        ],
        out_specs=[pl.BlockSpec((nb, t, wd), lambda b, c: (b, c, 0)), st_c, st_n, st_m],
        out_shape=[
            jax.ShapeDtypeStruct((nseq, t * nc, wd), BF16),
            jax.ShapeDtypeStruct((nseq, N_HEADS, M_DV, M_DK), F32),
            jax.ShapeDtypeStruct((nseq, N_HEADS, 1, M_DK), F32),
            jax.ShapeDtypeStruct((nseq, 1, LANES), F32),
        ],
        compiler_params=_cparams(("parallel", "arbitrary")),
        name="mlstm",
    )(qkv3, qkv3, qkv3, gates3, gbias, c0, n0, m0)


def _lru_chunk(x, gr, xs_s, h, cw, cb, wa_ref, ba, wx_ref, bx, lam, emit, spread=lambda: None):
    nb, t, w = x.shape
    ng = t // SUBLANES
    xs_s[:, SUBLANES:SUBLANES + t, :] = x
    xc = cb + cw[CONV_W - 1:CONV_W, :] * x
    for s in range(1, CONV_W):
        xc = xc + cw[CONV_W - 1 - s:CONV_W - s, :] * xs_s[:, SUBLANES - s:SUBLANES - s + t, :]
    xs_s[:, 0:SUBLANES, :] = x[:, t - SUBLANES:, :]
    spread()

    xf = xc.reshape(nb * t, w)
    ra, ri = [], []
    for n in range(LRU_BLOCKS):
        xb = xf[:, n * LRU_BS:(n + 1) * LRU_BS].astype(BF16)
        ra.append(jnp.dot(xb, wa_ref[n], preferred_element_type=F32))
        ri.append(jnp.dot(xb, wx_ref[n], preferred_element_type=F32))
    spread()
    r = _sigmoid(jnp.concatenate(ra, axis=-1) + ba)
    i = _sigmoid(jnp.concatenate(ri, axis=-1) + bx)
    log_a = -LRU_C * r * _softplus(-lam)
    a = jnp.exp(log_a)
    u = jnp.sqrt(-jnp.tanh(log_a) * (1.0 + a * a)) * (i * xf)
    spread()

    rid = lax.broadcasted_iota(jnp.int32, (1, SUBLANES, 1), 1)
    a = a.reshape(nb * ng, SUBLANES, w)
    u = u.reshape(nb * ng, SUBLANES, w)
    for s in (1, 2, 4):
        a_sh = jnp.where(rid >= s, pltpu.roll(a, s, axis=1), 1.0)
        u_sh = jnp.where(rid >= s, pltpu.roll(u, s, axis=1), 0.0)
        u = a * u_sh + u
        a = a * a_sh
    a = a.reshape(nb, ng, SUBLANES, w)
    u = u.reshape(nb, ng, SUBLANES, w)
    gr4 = gr.reshape(nb, ng, SUBLANES, w)
    for g in range(ng):
        if g == ng // 2:
            spread()
        hg = a[:, g] * h + u[:, g]
        h = hg[:, SUBLANES - 1:SUBLANES]
        emit(g, hg, hg * _silu(gr4[:, g]))
    return h


def _rglru_kernel(x_ref, gr_ref, cw_ref, cb_ref, wa_ref, ba_ref, wx_ref, bx_ref, lam_ref, h0_ref, conv0_ref,
                  hl_ref, hlast_ref, h_s, xs_s, *, nb, t, l_valid):
    c = pl.program_id(1)

    @pl.when(c == 0)
    def _():
        h_s[...] = h0_ref[...]
        xs_s[:, 0:SUBLANES, :] = conv0_ref[...]

    g_last, r_last = divmod((l_valid - 1) % t, SUBLANES)

    def emit(g, hg, gated):
        hl_ref[:, g * SUBLANES:(g + 1) * SUBLANES, :] = gated.astype(hl_ref.dtype)
        if g == g_last:
            @pl.when(c == (l_valid - 1) // t)
            def _():
                hlast_ref[...] = hg[:, r_last:r_last + 1]

    h_s[...] = _lru_chunk(x_ref[...], gr_ref[...], xs_s, h_s[...], cw_ref[...], cb_ref[...], wa_ref, ba_ref[...],
                          wx_ref, bx_ref[...], lam_ref[...], emit)


def _rglru(rest3, row0, nseq, tiles, nc, l_valid, cw, cb, wa, ba, wx, bx, lam, h0, conv0):
    nb, t = tiles
    assert nseq % nb == 0 and row0 % nb == 0
    w = LRU_WIDTH
    r0 = row0 // nb
    full2 = lambda a: pl.BlockSpec(a.shape, lambda b, c: (0, 0))
    full3 = lambda a: pl.BlockSpec(a.shape, lambda b, c: (0, 0, 0))
    kern = functools.partial(_rglru_kernel, nb=nb, t=t, l_valid=l_valid)
    return pl.pallas_call(
        kern,
        grid=(nseq // nb, nc),
        in_specs=[
            pl.BlockSpec((nb, t, w), lambda b, c: (r0 + b, c, 2)),
            pl.BlockSpec((nb, t, w), lambda b, c: (r0 + b, c, 3)),
            full2(cw), full2(cb), full3(wa), full2(ba), full3(wx), full2(bx), full2(lam),
            pl.BlockSpec((nb, 1, w), lambda b, c: (b, 0, 0)),
            pl.BlockSpec((nb, SUBLANES, w), lambda b, c: (b, 0, 0)),
        ],
        out_specs=[
            pl.BlockSpec((nb, t, w), lambda b, c: (b, c, 0)),
            pl.BlockSpec((nb, 1, w), lambda b, c: (b, 0, 0)),
        ],
        out_shape=[
            jax.ShapeDtypeStruct((nseq, t * nc, w), BF16),
            jax.ShapeDtypeStruct((nseq, 1, w), F32),
        ],
        scratch_shapes=[pltpu.VMEM((nb, 1, w), F32), pltpu.VMEM((nb, SUBLANES + t, w), F32)],
        compiler_params=_cparams(("parallel", "arbitrary")),
        name="rglru",
    )(rest3, rest3, cw, cb, wa, ba, wx, bx, lam, h0, conv0)


def _gla_kernel(q_ref, k_ref, v_ref, al_ref, aup_ref, ab_ref, s0_ref, o_ref, s_ref, *, nb, t, nc, l_valid):
    c = pl.program_id(1)
    if nc == 1:
        s_in = s0_ref
    else:
        s_in = s_ref

        @pl.when(c == 0)
        def _():
            s_ref[...] = s0_ref[...]

    pos = c * t + lax.broadcasted_iota(jnp.int32, (t, 1), 0)
    valid = pos < l_valid
    row = lax.broadcasted_iota(jnp.int32, (t, t), 0)
    col = lax.broadcasted_iota(jnp.int32, (t, t), 1)
    causal = row >= col
    aup = aup_ref[...]
    ab = ab_ref[...]
    heads = [(h, slice(h * G_DK, (h + 1) * G_DK), slice(h * G_DV, (h + 1) * G_DV)) for h in range(N_HEADS)]

    def group(rows):
        pairs = [(i, r, h, ks, vs) for i, r in enumerate(rows) for h, ks, vs in heads]
        dec = []
        for r in rows:
            pre = _dot(al_ref[r], aup) + ab
            lg = jnp.where(valid, _log_sigmoid(pre) / G_TAU, 0.0)
            b = _cumsum_rows(lg)
            b_last = b[t - 1:t, :]
            k = jnp.where(valid, k_ref[r].astype(F32), 0.0)
            qd = ((q_ref[r].astype(F32) * (G_DK ** -0.5)) * jnp.exp(b)).astype(BF16)
            kn = (k * jnp.exp(-b)).astype(BF16)
            kr = (k * jnp.exp(b_last - b)).astype(BF16)
            dec.append((qd, kn, kr, _row_to_col(jnp.exp(b_last))))
        att = {}
        for i, r, h, ks, vs in pairs:
            qd, kn, _, _ = dec[i]
            att[i, h] = jnp.where(causal, _dot_nt(qd[:, ks], kn[:, ks]), 0.0).astype(BF16)
        for i, r, h, ks, vs in pairs:
            qd, _, kr, ebl_col = dec[i]
            s_h = s_in[r, h]
            v = v_ref[r, :, vs]
            o_ref[r, :, vs] = (_dot(qd[:, ks], s_h) + _dot(att[i, h], v)).astype(o_ref.dtype)
            s_ref[r, h] = ebl_col[ks, :] * s_h + _dot_tn(kr[:, ks], v)

    _group_loop(nb, group)


def _gla(qkv3, al3, row0, nseq, tiles, nc, l_valid, aup, ab, s0):
    nb, t = tiles
    assert nseq % nb == 0 and row0 % nb == 0
    r0 = row0 // nb
    full2 = lambda a: pl.BlockSpec(a.shape, lambda b, c: (0, 0))
    st = pl.BlockSpec((nb, N_HEADS, G_DK, G_DV), lambda b, c: (b, 0, 0, 0))
    kern = functools.partial(_gla_kernel, nb=nb, t=t, nc=nc, l_valid=l_valid)
    return pl.pallas_call(
        kern,
        grid=(nseq // nb, nc),
        in_specs=[
            pl.BlockSpec((nb, t, G_KW), lambda b, c: (r0 + b, c, 0)),
            pl.BlockSpec((nb, t, G_KW), lambda b, c: (r0 + b, c, 1)),
            pl.BlockSpec((nb, t, G_VW), lambda b, c: (r0 + b, c, 1)),
            pl.BlockSpec((nb, t, LANES), lambda b, c: (r0 + b, c, 0)),
            full2(aup), full2(ab), st,
        ],
        out_specs=[pl.BlockSpec((nb, t, G_VW), lambda b, c: (b, c, 0)), st],
        out_shape=[
            jax.ShapeDtypeStruct((nseq, t * nc, G_VW), BF16),
            jax.ShapeDtypeStruct((nseq, N_HEADS, G_DK, G_DV), F32),
        ],
        compiler_params=_cparams(("parallel", "arbitrary")),
        name="gla",
    )(qkv3, qkv3, qkv3, al3, aup, ab, s0)


def _even_layer_kernel(x_ref, pre_ref, post_ref, win_ref, wgate_ref, gbias_ref, ng_ref, cw_ref, cb_ref, wa_ref,
                       ba_ref, wx_ref, bx_ref, lam_ref, w1_ref, w2_ref, c0_ref, n0_ref, m0_ref, h0_ref, conv0_ref,
                       y_ref, c_ref, n_ref, m_ref, hlast_ref, conv_ref,
                       qkv_s, rest_s, gate_s, hn_s, hm_s, hl_s, xkeep_s, hlru_s, xs_s, *, rows, tiles_per_seq):
    g = pl.program_id(0)
    slot = g % 2
    prev = 1 - slot
    t = MLSTM_TILES["main"][1]
    e_lo = 3 * M_WIDTH
    e = win_ref.shape[1]
    n_chunks = rows // t
    cols_per_chunk = e // n_chunks

    @pl.when(g == 0)
    def _():
        qkv_s[1] = jnp.zeros(qkv_s.shape[1:], qkv_s.dtype)
        rest_s[1] = jnp.zeros(rest_s.shape[1:], rest_s.dtype)
        gate_s[1] = jnp.zeros(gate_s.shape[1:], gate_s.dtype)
        xkeep_s[...] = jnp.zeros(xkeep_s.shape, xkeep_s.dtype)

    @pl.when(jnp.maximum(g - 1, 0) % tiles_per_seq == 0)
    def _():
        c_ref[...] = c0_ref[...]
        n_ref[...] = n0_ref[...]
        m_ref[...] = m0_ref[...]
        hlru_s[...] = h0_ref[...]
        xs_s[:, 0:SUBLANES, :] = conv0_ref[...]

    x = x_ref[0]
    hn_s[...] = (x * lax.rsqrt(jnp.mean(x * x, axis=-1, keepdims=True) + EPS) * pre_ref[...]).astype(BF16)
    gate_s[slot] = jnp.dot(hn_s[...], wgate_ref[...], preferred_element_type=F32)

    def project(c0, c1):
        acc = jnp.dot(hn_s[...], win_ref[:, c0:c1], preferred_element_type=F32)
        if c0 < e_lo:
            hi = min(c1, e_lo)
            qkv_s[slot, :, c0:hi] = acc[:, :hi - c0].astype(BF16)
        if c1 > e_lo:
            lo = max(c0, e_lo)
            rest_s[slot, :, lo - e_lo:c1 - e_lo] = acc[:, lo - c0:]

    row = lax.broadcasted_iota(jnp.int32, (t, t), 0)
    col = lax.broadcasted_iota(jnp.int32, (t, t), 1)
    causal = row >= col
    lane = lax.broadcasted_iota(jnp.int32, (1, LANES), 1)
    gbias = gbias_ref[...]
    heads = [(h, slice(h * M_DK, (h + 1) * M_DK)) for h in range(N_HEADS)]
    piece = 512
    pieces = [(c0, c0 + piece) for c0 in range(0, e, piece)]
    per_chunk = 5
    assert e % piece == 0 and e_lo % piece == 0 and len(pieces) >= n_chunks * per_chunk + 2

    w = LRU_WIDTH
    lru_h = [hlru_s[...]]
    subs_per_chunk = rows // LRU_SUB // n_chunks

    def lru_sub(i):
        r0 = i * LRU_SUB

        def emit(gi, hg, gated):
            hl_s[r0 + gi * SUBLANES:r0 + (gi + 1) * SUBLANES, :] = gated[0].astype(hl_s.dtype)

        lru_h[0] = _lru_chunk(rest_s[prev, r0:r0 + LRU_SUB, 2 * w:3 * w][None],
                              rest_s[prev, r0:r0 + LRU_SUB, 3 * w:4 * w][None], xs_s, lru_h[0],
                              cw_ref[...], cb_ref[...], wa_ref, ba_ref[...], wx_ref, bx_ref[...], lam_ref[...], emit)

    n_gaps = 4

    def lru_between(j, gap):
        for i in range(gap * subs_per_chunk // n_gaps, (gap + 1) * subs_per_chunk // n_gaps):
            lru_sub(j * subs_per_chunk + i)

    for j in range(n_chunks):
        rs = slice(j * t, (j + 1) * t)
        todo = pieces[j * per_chunk:(j + 1) * per_chunk]
        gt = gate_s[prev, rs, :] + gbias
        li = gt[:, :LANES]
        b = _cumsum_rows(_log_sigmoid(gt[:, LANES:]))
        project(*todo[0])
        lru_between(j, 0)
        r_t = _transpose_rows(li - b)[:, :t]
        m_vec = m_ref[0]
        st = []
        for h, sl in heads:
            b_col = b[:, h:h + 1]
            inter = b_col + m_vec[:, h:h + 1]
            dmat = jnp.where(causal, b_col + r_t[h:h + 1, :], NEG)
            mt = jnp.maximum(inter, jnp.max(dmat, axis=-1, keepdims=True))
            k = qkv_s[prev, rs, M_WIDTH + h * M_DK:M_WIDTH + (h + 1) * M_DK] * (M_DK ** -0.5)
            s = _dot_nt(qkv_s[prev, rs, sl], k) * jnp.exp(dmat - mt)
            st.append((inter, mt, k, s))
        project(*todo[1])
        lru_between(j, 1)
        project(*todo[2])
        for h, sl in heads:
            inter, mt, k, s = st[h]
            q = qkv_s[prev, rs, sl]
            v = qkv_s[prev, rs, 2 * M_WIDTH + h * M_DV:2 * M_WIDTH + (h + 1) * M_DV]
            w_inter = jnp.exp(inter - mt)
            num = w_inter * _dot_nt(q, c_ref[0, h]) + _dot(s, v)
            den = (w_inter * jnp.sum(q.astype(F32) * n_ref[0, h], axis=-1, keepdims=True)
                   + jnp.sum(s, axis=-1, keepdims=True))
            hm_s[rs, sl] = (num / jnp.maximum(jnp.abs(den), jnp.exp(-mt))).astype(hm_s.dtype)
        project(*todo[3])
        lru_between(j, 2)
        m_next = m_vec
        for h, sl in heads:
            inter, mt, k, _ = st[h]
            v = qkv_s[prev, rs, 2 * M_WIDTH + h * M_DV:2 * M_WIDTH + (h + 1) * M_DV]
            b_col = b[:, h:h + 1]
            m_new = mt[t - 1:t, :]
            w_c = jnp.exp(inter[t - 1:t, :] - m_new)
            w_k = jnp.exp(b_col[t - 1:t, :] - b_col + li[:, h:h + 1] - m_new)
            c_ref[0, h] = w_c * c_ref[0, h] + _dot_tn(v.astype(F32) * w_k, k)
            n_ref[0, h] = w_c * n_ref[0, h] + jnp.sum(k.astype(F32) * w_k, axis=0, keepdims=True)
            m_next = jnp.where(lane == h, m_new, m_next)
        m_ref[0] = m_next
        lru_between(j, 3)
        project(*todo[4])

    for p in pieces[n_chunks * per_chunk:]:
        project(*p)
    h_end = lru_h[0]
    hlru_s[...] = h_end
    hlast_ref[...] = h_end
    conv_ref[...] = xs_s[:, 0:SUBLANES, :]

    hm = _sigmoid(rest_s[prev, :, 0:w]) * hm_s[...].astype(F32)
    hm = _head_rms(hm, N_HEADS) * ng_ref[...] * _silu(rest_s[prev, :, w:2 * w])
    out = jnp.dot(hm.astype(BF16), w1_ref[...], preferred_element_type=F32)
    out = out + jnp.dot(hl_s[...], w2_ref[...], preferred_element_type=F32)
    nrm = out * lax.rsqrt(jnp.mean(out * out, axis=-1, keepdims=True) + EPS) * post_ref[...]
    y_ref[0] = xkeep_s[...] + nrm
    xkeep_s[...] = x


def _even_layer_main(x3, pre, post, w_main, w_gate, gbias, ng, lru, w1, w2, c0, n0, m0, h0, conv0, rows):
    nseq, seq, d = x3.shape
    assert seq % rows == 0 and rows % MLSTM_TILES["main"][1] == 0
    tps = seq // rows
    n_tiles = nseq * tps
    e = w_main.shape[1]
    e_lo = 3 * M_WIDTH
    w = LRU_WIDTH
    const = lambda a: pl.BlockSpec(a.shape, lambda g: (0,) * a.ndim, pipeline_mode=pl.Buffered(1))
    cur = lambda g: jnp.minimum(g, n_tiles - 1)
    old = lambda g: jnp.maximum(g - 1, 0)
    per_seq = lambda *blk: pl.BlockSpec((1,) + blk, lambda g: (old(g) // tps,) + (0,) * len(blk))
    st_c, st_n, st_m = per_seq(N_HEADS, M_DV, M_DK), per_seq(N_HEADS, 1, M_DK), per_seq(1, LANES)
    st_h, st_conv = per_seq(1, w), per_seq(SUBLANES, w)
    kern = functools.partial(_even_layer_kernel, rows=rows, tiles_per_seq=tps)
    return pl.pallas_call(
        kern,
        grid=(n_tiles + 1,),
        in_specs=[
            pl.BlockSpec((1, rows, d), lambda g: (cur(g) // tps, cur(g) % tps, 0)),
            const(pre), const(post), const(w_main), const(w_gate), const(gbias), const(ng),
            *[const(a) for a in lru], const(w1), const(w2), st_c, st_n, st_m, st_h, st_conv,
        ],
        out_specs=[
            pl.BlockSpec((1, rows, d), lambda g: (old(g) // tps, old(g) % tps, 0)),
            st_c, st_n, st_m, st_h, st_conv,
        ],
        out_shape=[
            jax.ShapeDtypeStruct((nseq, seq, d), F32),
            jax.ShapeDtypeStruct((nseq, N_HEADS, M_DV, M_DK), F32),
            jax.ShapeDtypeStruct((nseq, N_HEADS, 1, M_DK), F32),
            jax.ShapeDtypeStruct((nseq, 1, LANES), F32),
            jax.ShapeDtypeStruct((nseq, 1, w), F32),
            jax.ShapeDtypeStruct((nseq, SUBLANES, w), F32),
        ],
        scratch_shapes=[
            pltpu.VMEM((2, rows, e_lo), BF16),
            pltpu.VMEM((2, rows, e - e_lo), F32),
            pltpu.VMEM((2, rows, 2 * LANES), F32),
            pltpu.VMEM((rows, d), BF16),
            pltpu.VMEM((rows, M_WIDTH), BF16),
            pltpu.VMEM((rows, w), BF16),
            pltpu.VMEM((rows, d), F32),
            pltpu.VMEM((1, 1, w), F32),
            pltpu.VMEM((1, SUBLANES + LRU_SUB, w), F32),
        ],
        compiler_params=_cparams(("arbitrary",)),
        name="even_layer",
    )(x3, pre, post, w_main, w_gate, gbias, ng, *lru, w1, w2, c0, n0, m0, h0, conv0)


def _gla_layer_kernel(x_ref, pre_ref, post_ref, win_ref, wlow_ref, aup_ref, ab_ref, ng_ref, wout_ref, s0_ref,
                      y_ref, s_ref, qkv_s, r_s, al_s, hn_s, o_s, xkeep_s, *, rows, tiles_per_seq):
    g = pl.program_id(0)
    slot = g % 2
    prev = 1 - slot
    t = GLA_TILES["main"][1]
    e_lo = 2 * G_KW + G_VW
    e = e_lo + G_VW
    n_chunks = rows // t
    cols_per_chunk = e // n_chunks

    @pl.when(g == 0)
    def _():
        qkv_s[1] = jnp.zeros(qkv_s.shape[1:], qkv_s.dtype)
        r_s[1] = jnp.zeros(r_s.shape[1:], r_s.dtype)
        al_s[1] = jnp.zeros(al_s.shape[1:], al_s.dtype)
        xkeep_s[...] = jnp.zeros(xkeep_s.shape, xkeep_s.dtype)

    @pl.when(jnp.maximum(g - 1, 0) % tiles_per_seq == 0)
    def _():
        s_ref[...] = s0_ref[...]

    x = x_ref[0]
    hn_s[...] = (x * lax.rsqrt(jnp.mean(x * x, axis=-1, keepdims=True) + EPS) * pre_ref[...]).astype(BF16)
    al_s[slot] = jnp.dot(hn_s[...], wlow_ref[...], preferred_element_type=F32)

    row = lax.broadcasted_iota(jnp.int32, (t, t), 0)
    col = lax.broadcasted_iota(jnp.int32, (t, t), 1)
    causal = row >= col
    aup = aup_ref[...]
    ab = ab_ref[...]
    heads = [(h, slice(h * G_DK, (h + 1) * G_DK), slice(h * G_DV, (h + 1) * G_DV)) for h in range(N_HEADS)]

    def project(c0, c1):
        acc = jnp.dot(hn_s[...], win_ref[:, c0:c1], preferred_element_type=F32)
        if c0 < e_lo:
            hi = min(c1, e_lo)
            qkv_s[slot, :, c0:hi] = acc[:, :hi - c0].astype(BF16)
        if c1 > e_lo:
            lo = max(c0, e_lo)
            r_s[slot, :, lo - e_lo:c1 - e_lo] = acc[:, lo - c0:]

    n_sub = 3
    sub = cols_per_chunk // n_sub
    for j in range(n_chunks):
        c0 = j * cols_per_chunk
        rs = slice(j * t, (j + 1) * t)
        pre = _dot(al_s[prev, rs, :], aup) + ab
        project(c0, c0 + sub)
        b = _cumsum_rows(_log_sigmoid(pre) / G_TAU)
        b_last = b[t - 1:t, :]
        project(c0 + sub, c0 + 2 * sub)
        k = qkv_s[prev, rs, G_KW:2 * G_KW].astype(F32)
        qd = ((qkv_s[prev, rs, 0:G_KW].astype(F32) * (G_DK ** -0.5)) * jnp.exp(b)).astype(BF16)
        kn = (k * jnp.exp(-b)).astype(BF16)
        kr = (k * jnp.exp(b_last - b)).astype(BF16)
        ebl_col = _row_to_col(jnp.exp(b_last))
        att = [jnp.where(causal, _dot_nt(qd[:, ks], kn[:, ks]), 0.0).astype(BF16) for _, ks, _ in heads]
        project(c0 + 2 * sub, c0 + cols_per_chunk)
        for h, ks, vs in heads:
            s_h = s_ref[0, h]
            v = qkv_s[prev, rs, 2 * G_KW + h * G_DV:2 * G_KW + (h + 1) * G_DV]
            o_s[rs, vs] = (_dot(qd[:, ks], s_h) + _dot(att[h], v)).astype(o_s.dtype)
            s_ref[0, h] = ebl_col[ks, :] * s_h + _dot_tn(kr[:, ks], v)

    o = _head_rms(o_s[...].astype(F32), N_HEADS) * ng_ref[...] * _silu(r_s[prev])
    out = jnp.dot(o.astype(BF16), wout_ref[...], preferred_element_type=F32)
    nrm = out * lax.rsqrt(jnp.mean(out * out, axis=-1, keepdims=True) + EPS) * post_ref[...]
    y_ref[0] = xkeep_s[...] + nrm
    xkeep_s[...] = x


def _gla_layer(x3, pre, post, w_in, w_low, aup, ab, ng, w_out, s0, rows):
    nseq, seq, d = x3.shape
    assert seq % rows == 0 and rows % GLA_TILES["main"][1] == 0
    tps = seq // rows
    n_tiles = nseq * tps
    e = w_in.shape[1]
    e_lo = 2 * G_KW + G_VW
    const = lambda a: pl.BlockSpec(a.shape, lambda g: (0,) * a.ndim, pipeline_mode=pl.Buffered(1))
    cur = lambda g: jnp.minimum(g, n_tiles - 1)
    old = lambda g: jnp.maximum(g - 1, 0)
    st = pl.BlockSpec((1, N_HEADS, G_DK, G_DV), lambda g: (old(g) // tps, 0, 0, 0))
    kern = functools.partial(_gla_layer_kernel, rows=rows, tiles_per_seq=tps)
    return pl.pallas_call(
        kern,
        grid=(n_tiles + 1,),
        in_specs=[
            pl.BlockSpec((1, rows, d), lambda g: (cur(g) // tps, cur(g) % tps, 0)),
            const(pre), const(post), const(w_in), const(w_low), const(aup), const(ab), const(ng), const(w_out), st,
        ],
        out_specs=[pl.BlockSpec((1, rows, d), lambda g: (old(g) // tps, old(g) % tps, 0)), st],
        out_shape=[
            jax.ShapeDtypeStruct((nseq, seq, d), F32),
            jax.ShapeDtypeStruct((nseq, N_HEADS, G_DK, G_DV), F32),
        ],
        scratch_shapes=[
            pltpu.VMEM((2, rows, e_lo), BF16),
            pltpu.VMEM((2, rows, e - e_lo), F32),
            pltpu.VMEM((2, rows, LANES), F32),
            pltpu.VMEM((rows, d), BF16),
            pltpu.VMEM((rows, G_VW), BF16),
            pltpu.VMEM((rows, d), F32),
        ],
        compiler_params=_cparams(("arbitrary",)),
        name="gla_layer",
    )(x3, pre, post, w_in, w_low, aup, ab, ng, w_out, s0)


def _pad_lanes(a, width):
    return jnp.pad(a, [(0, 0)] * (a.ndim - 1) + [(0, width - a.shape[-1])])


def _even_layer(xm, xs, nbp, seq, nbs, ls, st, pre_g, post_g, w_in, w_out, b_i, b_f, m_norm_g,
                conv_w, conv_b, wa, ba, wx, bx, lam):
    c0s, n0s, m0s, h0s, conv0s = st
    wq, wk, wv, wog, wig, wfg, wz, wxr, wgr = jnp.split(
        w_in, [1024, 2048, 3072, 4096, 4100, 4104, 5128, 6152], axis=1)
    w_main = jnp.concatenate([wq, wk, wv, wog, wz, wxr, wgr], axis=1).astype(BF16)
    w_gate = jnp.concatenate([_pad_lanes(wig, LANES), _pad_lanes(wfg, LANES)], axis=1).astype(BF16)
    gbias = jnp.concatenate([_pad_lanes(b_i[None], LANES), _pad_lanes(b_f[None], LANES)], axis=1)
    ng = m_norm_g[None]
    post = post_g[None]
    cw, cb = conv_w, conv_b[None]
    wab, wxb = wa.astype(BF16), wx.astype(BF16)
    bav, bxv, lamv = ba[None], bx[None], lam[None]
    w_o1, w_o2 = w_out[:M_WIDTH].astype(BF16), w_out[M_WIDTH:].astype(BF16)
    e_lo = 3 * M_WIDTH
    e_hi = w_main.shape[1] - e_lo
    lru = (cw, cb, wab, bav, wxb, bxv, lamv)

    qs, ps, gs = _norm_proj(xs, pre_g, w_main, w_gate, e_lo, xs.shape[0])
    n_meta_rows = nbp * N_META
    xr_cols = slice(2 * LRU_WIDTH, 3 * LRU_WIDTH)

    zc = jnp.zeros((nbp, N_HEADS, M_DV, M_DK), F32)
    zn = jnp.zeros((nbp, N_HEADS, 1, M_DK), F32)
    zm = jnp.zeros((nbp, 1, LANES), F32)
    zh = jnp.zeros((nbp, 1, LRU_WIDTH), F32)
    zconv = jnp.zeros((nbp, SUBLANES, LRU_WIDTH), F32)
    qs_meta = qs.reshape(-1, N_META, e_lo)
    ps_meta = ps.reshape(-1, N_META, e_hi)
    gs_meta = gs.reshape(-1, N_META, 2 * LANES)
    h_meta, c1, n1, m1 = _mlstm(qs_meta, gs_meta, 0, nbp, MLSTM_TILES["meta"], 1, N_META, gbias, zc, zn, zm)
    hl_meta, h1 = _rglru(ps_meta, 0, nbp, LRU_TILES["meta"], 1, N_META, *lru, zh, zconv)
    conv1 = ps_meta[:nbp, N_META - SUBLANES:, xr_cols]

    ym3, pc, pn, pmm, ph, conv_tail = _even_layer_main(
        xm.reshape(nbp, seq, -1), pre_g[None], post, w_main, w_gate, gbias, ng, lru, w_o1, w_o2,
        c1, n1, m1, h1, conv1, FUSED_ROW_TILE)
    ym = ym3.reshape(xm.shape)
    pconv = conv_tail[:, SUBLANES - (CONV_W - 1):, :]

    qs_s = qs.reshape(-1, SAMPLE_PAD, e_lo)
    ps_s = ps.reshape(-1, SAMPLE_PAD, e_hi)
    gs_s = gs.reshape(-1, SAMPLE_PAD, 2 * LANES)
    r0 = n_meta_rows // SAMPLE_PAD
    h_s, sc, sn, sm = _mlstm(qs_s, gs_s, r0, nbs, MLSTM_TILES["sample"], 1, ls, gbias,
                             c0s, n0s[:, :, None, :], _pad_lanes(m0s, LANES)[:, None, :])
    conv0p = jnp.pad(conv0s, ((0, 0), (SUBLANES - (CONV_W - 1), 0), (0, 0)))
    hl_s, sh = _rglru(ps_s, r0, nbs, LRU_TILES["sample"], 1, ls, *lru, h0s[:, None, :], conv0p)
    xr_s = ps_s[r0:, :ls, xr_cols]
    sconv = jnp.concatenate([conv0s, xr_s], axis=1)[:, -(CONV_W - 1):]

    h_small = jnp.concatenate([h_meta.reshape(-1, M_WIDTH), h_s.reshape(-1, M_WIDTH)], axis=0)
    hl_small = jnp.concatenate([hl_meta.reshape(-1, LRU_WIDTH), hl_s.reshape(-1, LRU_WIDTH)], axis=0)
    ys = _out_even(xs, post, h_small, ps, ng, hl_small, w_o1, w_o2, _row_tile(xs.shape[0], OUT_ROW_TILE))

    p_state = (pc, pn[:, :, 0, :], pmm[:, 0, :N_HEADS], ph[:, 0, :], pconv)
    s_state = (sc, sn[:, :, 0, :], sm[:, 0, :N_HEADS], sh[:, 0, :], sconv)
    return ym, ys, p_state, s_state


def _odd_layer(xm, xs, nbp, seq, nbs, ls, s0s, pre_g, post_g, w_in, w_out, a_up, a_b, g_norm_g):
    e_lo = 2 * G_KW + G_VW
    w_main = w_in[:, :2 * G_KW + 2 * G_VW].astype(BF16)
    w_low = _pad_lanes(w_in[:, 2 * G_KW + 2 * G_VW:], LANES).astype(BF16)
    aup = jnp.pad(a_up, ((0, LANES - G_RANK), (0, 0))).astype(BF16)
    ab = a_b[None]
    ng = g_norm_g[None]
    post = post_g[None]
    w_o = w_out.astype(BF16)

    qs, rs, as_ = _norm_proj(xs, pre_g, w_main, w_low, e_lo, xs.shape[0])
    n_meta_rows = nbp * N_META

    zs = jnp.zeros((nbp, N_HEADS, G_DK, G_DV), F32)
    o_meta, s1 = _gla(qs.reshape(-1, N_META, e_lo), as_.reshape(-1, N_META, LANES),
                      0, nbp, GLA_TILES["meta"], 1, N_META, aup, ab, zs)
    ym3, p_s = _gla_layer(xm.reshape(nbp, seq, -1), pre_g[None], post, w_main, w_low, aup, ab, ng, w_o, s1,
                          FUSED_ROW_TILE)
    ym = ym3.reshape(xm.shape)
    r0 = n_meta_rows // SAMPLE_PAD
    o_s, s_s = _gla(qs.reshape(-1, SAMPLE_PAD, e_lo), as_.reshape(-1, SAMPLE_PAD, LANES),
                    r0, nbs, GLA_TILES["sample"], 1, ls, aup, ab, s0s)

    o_small = jnp.concatenate([o_meta.reshape(-1, G_VW), o_s.reshape(-1, G_VW)], axis=0)
    ys = _out_odd(xs, post, o_small, rs, ng, w_o, _row_tile(xs.shape[0], OUT_ROW_TILE))
    return ym, ys, p_s, s_s


def kernel(x_prompt, x_sample, state_mlstm_C, state_mlstm_n, state_mlstm_m, state_rglru_h, state_rglru_conv,
           state_gla_S, meta_tokens, pre_norm_a, post_norm_a, w_in_a, w_out_a, mlstm_b_i, mlstm_b_f, mlstm_norm,
           conv_w, conv_b, lru_w_a, lru_b_a, lru_w_x, lru_b_x, lru_lambda, pre_norm_c, post_norm_c, w_in_c,
           w_out_c, gla_alpha_up, gla_alpha_b, gla_norm):
    nbp, seq, d = x_prompt.shape
    nbs, ls, _ = x_sample.shape
    depth = pre_norm_a.shape[0] + pre_norm_c.shape[0]
    assert ls >= CONV_W - 1 and ls <= SAMPLE_PAD and N_META % SUBLANES == 0

    xm = x_prompt.reshape(nbp * seq, d)
    meta = jnp.broadcast_to(meta_tokens[None].astype(x_prompt.dtype), (nbp, N_META, d)).reshape(nbp * N_META, d)
    xs_pad = jnp.pad(x_sample, ((0, 0), (0, SAMPLE_PAD - ls), (0, 0))).reshape(nbs * SAMPLE_PAD, d)
    xs = jnp.concatenate([meta, xs_pad], axis=0)

    p_lists = [[] for _ in range(6)]
    s_lists = [[] for _ in range(6)]
    for layer in range(depth):
        j = layer // 2
        if layer % 2 == 0:
            st = (state_mlstm_C[j], state_mlstm_n[j], state_mlstm_m[j], state_rglru_h[j], state_rglru_conv[j])
            xm, xs, pst, sst = _even_layer(
                xm, xs, nbp, seq, nbs, ls, st, pre_norm_a[j], post_norm_a[j], w_in_a[j], w_out_a[j],
                mlstm_b_i[j], mlstm_b_f[j], mlstm_norm[j], conv_w[j], conv_b[j], lru_w_a[j], lru_b_a[j],
                lru_w_x[j], lru_b_x[j], lru_lambda[j])
            for i in range(5):
                p_lists[i].append(pst[i])
                s_lists[i].append(sst[i])
        else:
            xm, xs, p_s, s_s = _odd_layer(
                xm, xs, nbp, seq, nbs, ls, state_gla_S[j], pre_norm_c[j], post_norm_c[j], w_in_c[j], w_out_c[j],
                gla_alpha_up[j], gla_alpha_b[j], gla_norm[j])
            p_lists[5].append(p_s)
            s_lists[5].append(s_s)

    y_prompt = xm.reshape(nbp, seq, d)
    y_sample = xs[nbp * N_META:].reshape(nbs, SAMPLE_PAD, d)[:, :ls]
    return (y_prompt, y_sample) + tuple(jnp.stack(l) for l in p_lists) + tuple(jnp.stack(l) for l in s_lists)
```

```python
import functools

import jax
import jax.numpy as jnp
from jax import lax
from jax.experimental import pallas as pl
from jax.experimental.pallas import tpu as pltpu

F32 = jnp.float32
BF16 = jnp.bfloat16

D_MODEL = 1024
N_META = 16
EPS = 1e-6
N_HEADS = 4
M_DK = 256
M_DV = 256
M_WIDTH = N_HEADS * M_DV
LRU_WIDTH = 1024
LRU_BLOCKS = 8
LRU_BS = LRU_WIDTH // LRU_BLOCKS
CONV_W = 4
LRU_C = 8.0
G_DK = 256
G_DV = 512
G_KW = N_HEADS * G_DK
G_VW = N_HEADS * G_DV
G_RANK = 16
G_TAU = 16.0

LANES = 128
SUBLANES = 8
NEG = -1e30
SAMPLE_PAD = SUBLANES
VMEM_LIMIT = 56 * 1024 * 1024
COL_TILE = 1024

MLSTM_TILES = {"meta": (1, N_META), "main": (4, 128), "sample": (4, SAMPLE_PAD)}
GLA_TILES = {"meta": (1, N_META), "main": (2, 64), "sample": (4, SAMPLE_PAD)}
LRU_TILES = {"meta": (1, N_META), "main": (1, 256), "sample": (16, SAMPLE_PAD)}
OUT_ROW_TILE = 512
FUSED_ROW_TILE = 256
GLA_ROW_TILE = 256
LRU_SUB = 64


def _row_tile(n, cap):
    return next(tm for tm in range(min(n, cap) // SUBLANES * SUBLANES, 0, -SUBLANES) if n % tm == 0)


def _cparams(sem):
    return pltpu.CompilerParams(dimension_semantics=sem, vmem_limit_bytes=VMEM_LIMIT)


def _sigmoid(x):
    return 0.5 * jnp.tanh(0.5 * x) + 0.5


def _silu(x):
    return x * _sigmoid(x)


def _log_sigmoid(x):
    return jnp.minimum(x, 0.0) - jnp.log1p(jnp.exp(-jnp.abs(x)))


def _softplus(x):
    return jnp.maximum(x, 0.0) + jnp.log1p(jnp.exp(-jnp.abs(x)))


def _dot(a, b):
    return jnp.dot(a.astype(BF16), b.astype(BF16), preferred_element_type=F32)


def _dot_nt(a, b):
    return lax.dot_general(a.astype(BF16), b.astype(BF16), (((1,), (1,)), ((), ())),
                           preferred_element_type=F32)


def _dot_tn(a, b):
    return lax.dot_general(a.astype(BF16), b.astype(BF16), (((0,), (0,)), ((), ())),
                           preferred_element_type=F32)


def _transpose_rows(x):
    t, n = x.shape
    if t < LANES:
        x = jnp.concatenate([x, jnp.zeros((LANES - t, n), x.dtype)], axis=0)
    return jnp.transpose(x)


def _row_to_col(r):
    return jnp.transpose(jnp.broadcast_to(r, (LANES, r.shape[1])))[:, 0:1]


def _cumsum_rows(x):
    t = x.shape[0]
    if t <= 2 * SUBLANES:
        rid = lax.broadcasted_iota(jnp.int32, (t, 1), 0)
        s = 1
        while s < t:
            x = x + jnp.where(rid >= s, pltpu.roll(x, s, axis=0), 0.0)
            s *= 2
        return x
    row = lax.broadcasted_iota(jnp.int32, (t, t), 0)
    col = lax.broadcasted_iota(jnp.int32, (t, t), 1)
    tri = jnp.where(row >= col, 1.0, 0.0).astype(BF16)
    hi = x.astype(BF16)
    r1 = x - hi.astype(F32)
    mid = r1.astype(BF16)
    lo = (r1 - mid.astype(F32)).astype(BF16)
    acc = jnp.dot(tri, lo, preferred_element_type=F32)
    acc = acc + jnp.dot(tri, mid, preferred_element_type=F32)
    return acc + jnp.dot(tri, hi, preferred_element_type=F32)


def _head_rms(x, nh):
    hd = x.shape[1] // nh
    parts = []
    for h in range(nh):
        xh = x[:, h * hd:(h + 1) * hd]
        parts.append(xh * lax.rsqrt(jnp.mean(xh * xh, axis=-1, keepdims=True) + EPS))
    return jnp.concatenate(parts, axis=-1)


GROUP = 2


def _group_loop(nb, group_fn):
    if nb <= GROUP:
        group_fn(list(range(nb)))
        return

    def body(i, carry):
        group_fn([GROUP * i + r for r in range(GROUP)])
        return carry

    lax.fori_loop(0, nb // GROUP, body, 0)


def _norm_proj_kernel(n_lo, x_ref, g_ref, wm_ref, ws_ref, olo_ref, ohi_ref, os_ref, hn_ref):
    j = pl.program_id(1)

    @pl.when(j == 0)
    def _():
        x = x_ref[...]
        y = x * lax.rsqrt(jnp.mean(x * x, axis=-1, keepdims=True) + EPS) * g_ref[...]
        hn = y.astype(BF16)
        hn_ref[...] = hn
        os_ref[...] = jnp.dot(hn, ws_ref[...], preferred_element_type=F32)

    acc = jnp.dot(hn_ref[...], wm_ref[...], preferred_element_type=F32)

    @pl.when(j < n_lo)
    def _():
        olo_ref[...] = acc.astype(olo_ref.dtype)

    @pl.when(j >= n_lo)
    def _():
        ohi_ref[...] = acc


def _norm_proj(x2d, g, w_main, w_small, e_lo, tm):
    n, d = x2d.shape
    e = w_main.shape[1]
    es = w_small.shape[1]
    tn = COL_TILE
    n_lo = e_lo // tn
    return pl.pallas_call(
        functools.partial(_norm_proj_kernel, n_lo),
        grid=(n // tm, e // tn),
        in_specs=[
            pl.BlockSpec((tm, d), lambda i, j: (i, 0)),
            pl.BlockSpec((1, d), lambda i, j: (0, 0)),
            pl.BlockSpec((d, tn), lambda i, j: (0, j)),
            pl.BlockSpec((d, es), lambda i, j: (0, 0)),
        ],
        out_specs=[
            pl.BlockSpec((tm, tn), lambda i, j: (i, jnp.minimum(j, n_lo - 1))),
            pl.BlockSpec((tm, tn), lambda i, j: (i, jnp.maximum(j - n_lo, 0))),
            pl.BlockSpec((tm, es), lambda i, j: (i, 0)),
        ],
        out_shape=[
            jax.ShapeDtypeStruct((n, e_lo), BF16),
            jax.ShapeDtypeStruct((n, e - e_lo), F32),
            jax.ShapeDtypeStruct((n, es), F32),
        ],
        scratch_shapes=[pltpu.VMEM((tm, d), BF16)],
        compiler_params=_cparams(("parallel", "arbitrary")),
        name="norm_proj",
    )(x2d, g.reshape(1, d), w_main, w_small)


def _residual_norm(x_ref, g_ref, y_ref, out):
    nrm = out * lax.rsqrt(jnp.mean(out * out, axis=-1, keepdims=True) + EPS) * g_ref[...]
    y_ref[...] = x_ref[...] + nrm


def _out_even_kernel(x_ref, g_ref, h_ref, og_ref, z_ref, ng_ref, hl_ref, w1_ref, w2_ref, y_ref):
    hm = _sigmoid(og_ref[...]) * h_ref[...].astype(F32)
    hm = _head_rms(hm, N_HEADS) * ng_ref[...] * _silu(z_ref[...])
    out = jnp.dot(hm.astype(BF16), w1_ref[...], preferred_element_type=F32)
    out = out + jnp.dot(hl_ref[...], w2_ref[...], preferred_element_type=F32)
    _residual_norm(x_ref, g_ref, y_ref, out)


def _out_even(x2d, g, h_raw, rest, ng, hl, w1, w2, tm):
    n, d = x2d.shape
    row = lambda width, col: pl.BlockSpec((tm, width), lambda i: (i, col))
    const = lambda a: pl.BlockSpec(a.shape, lambda i: (0, 0))
    return pl.pallas_call(
        _out_even_kernel,
        grid=(n // tm,),
        in_specs=[row(d, 0), const(g), row(M_WIDTH, 0), row(M_WIDTH, 0), row(M_WIDTH, 1), const(ng),
                  row(LRU_WIDTH, 0), const(w1), const(w2)],
        out_specs=row(d, 0),
        out_shape=jax.ShapeDtypeStruct((n, d), F32),
        compiler_params=_cparams(("parallel",)),
        name="out_even",
    )(x2d, g, h_raw, rest, rest, ng, hl, w1, w2)


def _out_odd_kernel(x_ref, g_ref, o_ref, r_ref, ng_ref, w_ref, y_ref):
    o = _head_rms(o_ref[...].astype(F32), N_HEADS) * ng_ref[...] * _silu(r_ref[...])
    out = jnp.dot(o.astype(BF16), w_ref[...], preferred_element_type=F32)
    _residual_norm(x_ref, g_ref, y_ref, out)


def _out_odd(x2d, g, o_raw, r, ng, w, tm):
    n, d = x2d.shape
    row = lambda width: pl.BlockSpec((tm, width), lambda i: (i, 0))
    const = lambda a: pl.BlockSpec(a.shape, lambda i: (0, 0))
    return pl.pallas_call(
        _out_odd_kernel,
        grid=(n // tm,),
        in_specs=[row(d), const(g), row(G_VW), row(G_VW), const(ng), const(w)],
        out_specs=row(d),
        out_shape=jax.ShapeDtypeStruct((n, d), F32),
        compiler_params=_cparams(("parallel",)),
        name="out_odd",
    )(x2d, g, o_raw, r, ng, w)


def _mlstm_kernel(q_ref, k_ref, v_ref, gate_ref, gbias_ref, c0_ref, n0_ref, m0_ref,
                  h_ref, c_ref, n_ref, m_ref, *, nb, t, nc, l_valid):
    c = pl.program_id(1)
    if nc == 1:
        c_in, n_in, m_in = c0_ref, n0_ref, m0_ref
    else:
        c_in, n_in, m_in = c_ref, n_ref, m_ref

        @pl.when(c == 0)
        def _():
            c_ref[...] = c0_ref[...]
            n_ref[...] = n0_ref[...]
            m_ref[...] = m0_ref[...]

    pos = c * t + lax.broadcasted_iota(jnp.int32, (t, 1), 0)
    valid = pos < l_valid
    row = lax.broadcasted_iota(jnp.int32, (t, t), 0)
    col = lax.broadcasted_iota(jnp.int32, (t, t), 1)
    causal = row >= col
    lane = lax.broadcasted_iota(jnp.int32, (1, LANES), 1)
    gbias = gbias_ref[...]
    heads = [(h, slice(h * M_DK, (h + 1) * M_DK)) for h in range(N_HEADS)]

    def group(rows):
        pairs = [(i, r, h, sl) for i, r in enumerate(rows) for h, sl in heads]
        gates = []
        for r in rows:
            g = gate_ref[r] + gbias
            li = jnp.where(valid, g[:, :LANES], NEG)
            lf = jnp.where(valid, _log_sigmoid(g[:, LANES:]), 0.0)
            b = _cumsum_rows(lf)
            gates.append((li, b, _transpose_rows(li - b)[:, :t], m_in[r]))
        st = {}
        for i, r, h, sl in pairs:
            li, b, r_t, m_vec = gates[i]
            b_col = b[:, h:h + 1]
            inter = b_col + m_vec[:, h:h + 1]
            dmat = jnp.where(causal, b_col + r_t[h:h + 1, :], NEG)
            mt = jnp.maximum(inter, jnp.max(dmat, axis=-1, keepdims=True))
            k = k_ref[r, :, sl] * (M_DK ** -0.5)
            s = _dot_nt(q_ref[r, :, sl], k) * jnp.exp(dmat - mt)
            st[i, h] = (inter, mt, k, s)
        for i, r, h, sl in pairs:
            inter, mt, k, s = st[i, h]
            q = q_ref[r, :, sl]
            w_inter = jnp.exp(inter - mt)
            num = w_inter * _dot_nt(q, c_in[r, h]) + _dot(s, v_ref[r, :, sl])
            den = (w_inter * jnp.sum(q.astype(F32) * n_in[r, h], axis=-1, keepdims=True)
                   + jnp.sum(s, axis=-1, keepdims=True))
            h_ref[r, :, sl] = (num / jnp.maximum(jnp.abs(den), jnp.exp(-mt))).astype(h_ref.dtype)
        m_next = [gt[3] for gt in gates]
        for i, r, h, sl in pairs:
            li, b, _, _ = gates[i]
            inter, mt, k, _ = st[i, h]
            b_col = b[:, h:h + 1]
            m_new = mt[t - 1:t, :]
            w_c = jnp.exp(inter[t - 1:t, :] - m_new)
            w_k = jnp.exp(b_col[t - 1:t, :] - b_col + li[:, h:h + 1] - m_new)
            c_ref[r, h] = w_c * c_in[r, h] + _dot_tn(v_ref[r, :, sl].astype(F32) * w_k, k)
            n_ref[r, h] = w_c * n_in[r, h] + jnp.sum(k.astype(F32) * w_k, axis=0, keepdims=True)
            m_next[i] = jnp.where(lane == h, m_new, m_next[i])
        for i, r in enumerate(rows):
            m_ref[r] = m_next[i]

    _group_loop(nb, group)


def _mlstm(qkv3, gates3, row0, nseq, tiles, nc, l_valid, gbias, c0, n0, m0):
    nb, t = tiles
    assert nseq % nb == 0 and row0 % nb == 0
    wd = M_WIDTH
    r0 = row0 // nb
    seq = lambda col: pl.BlockSpec((nb, t, wd), lambda b, c: (r0 + b, c, col))
    st_c = pl.BlockSpec((nb, N_HEADS, M_DV, M_DK), lambda b, c: (b, 0, 0, 0))
    st_n = pl.BlockSpec((nb, N_HEADS, 1, M_DK), lambda b, c: (b, 0, 0, 0))
    st_m = pl.BlockSpec((nb, 1, LANES), lambda b, c: (b, 0, 0))
    kern = functools.partial(_mlstm_kernel, nb=nb, t=t, nc=nc, l_valid=l_valid)
    return pl.pallas_call(
        kern,
        grid=(nseq // nb, nc),
        in_specs=[
            seq(0), seq(1), seq(2),
            pl.BlockSpec((nb, t, 2 * LANES), lambda b, c: (r0 + b, c, 0)),
            pl.BlockSpec(gbias.shape, lambda b, c: (0, 0)), st_c, st_n, st_m,
        ],
        out_specs=[pl.BlockSpec((nb, t, wd), lambda b, c: (b, c, 0)), st_c, st_n, st_m],
        out_shape=[
            jax.ShapeDtypeStruct((nseq, t * nc, wd), BF16),
            jax.ShapeDtypeStruct((nseq, N_HEADS, M_DV, M_DK), F32),
            jax.ShapeDtypeStruct((nseq, N_HEADS, 1, M_DK), F32),
            jax.ShapeDtypeStruct((nseq, 1, LANES), F32),
        ],
        compiler_params=_cparams(("parallel", "arbitrary")),
        name="mlstm",
    )(qkv3, qkv3, qkv3, gates3, gbias, c0, n0, m0)


def _lru_chunk(x, gr, xs_s, h, cw, cb, wa_ref, ba, wx_ref, bx, lam, emit, spread=lambda: None):
    nb, t, w = x.shape
    ng = t // SUBLANES
    xs_s[:, SUBLANES:SUBLANES + t, :] = x
    xc = cb + cw[CONV_W - 1:CONV_W, :] * x
    for s in range(1, CONV_W):
        xc = xc + cw[CONV_W - 1 - s:CONV_W - s, :] * xs_s[:, SUBLANES - s:SUBLANES - s + t, :]
    xs_s[:, 0:SUBLANES, :] = x[:, t - SUBLANES:, :]
    spread()

    xf = xc.reshape(nb * t, w)
    ra, ri = [], []
    for n in range(LRU_BLOCKS):
        xb = xf[:, n * LRU_BS:(n + 1) * LRU_BS].astype(BF16)
        ra.append(jnp.dot(xb, wa_ref[n], preferred_element_type=F32))
        ri.append(jnp.dot(xb, wx_ref[n], preferred_element_type=F32))
    spread()
    r = _sigmoid(jnp.concatenate(ra, axis=-1) + ba)
    i = _sigmoid(jnp.concatenate(ri, axis=-1) + bx)
    log_a = -LRU_C * r * _softplus(-lam)
    a = jnp.exp(log_a)
    u = jnp.sqrt(-jnp.tanh(log_a) * (1.0 + a * a)) * (i * xf)
    spread()

    rid = lax.broadcasted_iota(jnp.int32, (1, SUBLANES, 1), 1)
    a = a.reshape(nb * ng, SUBLANES, w)
    u = u.reshape(nb * ng, SUBLANES, w)
    for s in (1, 2, 4):
        a_sh = jnp.where(rid >= s, pltpu.roll(a, s, axis=1), 1.0)
        u_sh = jnp.where(rid >= s, pltpu.roll(u, s, axis=1), 0.0)
        u = a * u_sh + u
        a = a * a_sh
    a = a.reshape(nb, ng, SUBLANES, w)
    u = u.reshape(nb, ng, SUBLANES, w)
    gr4 = gr.reshape(nb, ng, SUBLANES, w)
    for g in range(ng):
        if g == ng // 2:
            spread()
        hg = a[:, g] * h + u[:, g]
        h = hg[:, SUBLANES - 1:SUBLANES]
        emit(g, hg, hg * _silu(gr4[:, g]))
    return h


def _rglru_kernel(x_ref, gr_ref, cw_ref, cb_ref, wa_ref, ba_ref, wx_ref, bx_ref, lam_ref, h0_ref, conv0_ref,
                  hl_ref, hlast_ref, h_s, xs_s, *, nb, t, l_valid):
    c = pl.program_id(1)

    @pl.when(c == 0)
    def _():
        h_s[...] = h0_ref[...]
        xs_s[:, 0:SUBLANES, :] = conv0_ref[...]

    g_last, r_last = divmod((l_valid - 1) % t, SUBLANES)

    def emit(g, hg, gated):
        hl_ref[:, g * SUBLANES:(g + 1) * SUBLANES, :] = gated.astype(hl_ref.dtype)
        if g == g_last:
            @pl.when(c == (l_valid - 1) // t)
            def _():
                hlast_ref[...] = hg[:, r_last:r_last + 1]

    h_s[...] = _lru_chunk(x_ref[...], gr_ref[...], xs_s, h_s[...], cw_ref[...], cb_ref[...], wa_ref, ba_ref[...],
                          wx_ref, bx_ref[...], lam_ref[...], emit)


def _rglru(rest3, row0, nseq, tiles, nc, l_valid, cw, cb, wa, ba, wx, bx, lam, h0, conv0):
    nb, t = tiles
    assert nseq % nb == 0 and row0 % nb == 0
    w = LRU_WIDTH
    r0 = row0 // nb
    full2 = lambda a: pl.BlockSpec(a.shape, lambda b, c: (0, 0))
    full3 = lambda a: pl.BlockSpec(a.shape, lambda b, c: (0, 0, 0))
    kern = functools.partial(_rglru_kernel, nb=nb, t=t, l_valid=l_valid)
    return pl.pallas_call(
        kern,
        grid=(nseq // nb, nc),
        in_specs=[
            pl.BlockSpec((nb, t, w), lambda b, c: (r0 + b, c, 2)),
            pl.BlockSpec((nb, t, w), lambda b, c: (r0 + b, c, 3)),
            full2(cw), full2(cb), full3(wa), full2(ba), full3(wx), full2(bx), full2(lam),
            pl.BlockSpec((nb, 1, w), lambda b, c: (b, 0, 0)),
            pl.BlockSpec((nb, SUBLANES, w), lambda b, c: (b, 0, 0)),
        ],
        out_specs=[
            pl.BlockSpec((nb, t, w), lambda b, c: (b, c, 0)),
            pl.BlockSpec((nb, 1, w), lambda b, c: (b, 0, 0)),
        ],
        out_shape=[
            jax.ShapeDtypeStruct((nseq, t * nc, w), BF16),
            jax.ShapeDtypeStruct((nseq, 1, w), F32),
        ],
        scratch_shapes=[pltpu.VMEM((nb, 1, w), F32), pltpu.VMEM((nb, SUBLANES + t, w), F32)],
        compiler_params=_cparams(("parallel", "arbitrary")),
        name="rglru",
    )(rest3, rest3, cw, cb, wa, ba, wx, bx, lam, h0, conv0)


def _gla_kernel(q_ref, k_ref, v_ref, al_ref, aup_ref, ab_ref, s0_ref, o_ref, s_ref, *, nb, t, nc, l_valid):
    c = pl.program_id(1)
    if nc == 1:
        s_in = s0_ref
    else:
        s_in = s_ref

        @pl.when(c == 0)
        def _():
            s_ref[...] = s0_ref[...]

    pos = c * t + lax.broadcasted_iota(jnp.int32, (t, 1), 0)
    valid = pos < l_valid
    row = lax.broadcasted_iota(jnp.int32, (t, t), 0)
    col = lax.broadcasted_iota(jnp.int32, (t, t), 1)
    causal = row >= col
    aup = aup_ref[...]
    ab = ab_ref[...]
    heads = [(h, slice(h * G_DK, (h + 1) * G_DK), slice(h * G_DV, (h + 1) * G_DV)) for h in range(N_HEADS)]

    def group(rows):
        pairs = [(i, r, h, ks, vs) for i, r in enumerate(rows) for h, ks, vs in heads]
        dec = []
        for r in rows:
            pre = _dot(al_ref[r], aup) + ab
            lg = jnp.where(valid, _log_sigmoid(pre) / G_TAU, 0.0)
            b = _cumsum_rows(lg)
            b_last = b[t - 1:t, :]
            k = jnp.where(valid, k_ref[r].astype(F32), 0.0)
            qd = ((q_ref[r].astype(F32) * (G_DK ** -0.5)) * jnp.exp(b)).astype(BF16)
            kn = (k * jnp.exp(-b)).astype(BF16)
            kr = (k * jnp.exp(b_last - b)).astype(BF16)
            dec.append((qd, kn, kr, _row_to_col(jnp.exp(b_last))))
        att = {}
        for i, r, h, ks, vs in pairs:
            qd, kn, _, _ = dec[i]
            att[i, h] = jnp.where(causal, _dot_nt(qd[:, ks], kn[:, ks]), 0.0).astype(BF16)
        for i, r, h, ks, vs in pairs:
            qd, _, kr, ebl_col = dec[i]
            s_h = s_in[r, h]
            v = v_ref[r, :, vs]
            o_ref[r, :, vs] = (_dot(qd[:, ks], s_h) + _dot(att[i, h], v)).astype(o_ref.dtype)
            s_ref[r, h] = ebl_col[ks, :] * s_h + _dot_tn(kr[:, ks], v)

    _group_loop(nb, group)


def _gla(qkv3, al3, row0, nseq, tiles, nc, l_valid, aup, ab, s0):
    nb, t = tiles
    assert nseq % nb == 0 and row0 % nb == 0
    r0 = row0 // nb
    full2 = lambda a: pl.BlockSpec(a.shape, lambda b, c: (0, 0))
    st = pl.BlockSpec((nb, N_HEADS, G_DK, G_DV), lambda b, c: (b, 0, 0, 0))
    kern = functools.partial(_gla_kernel, nb=nb, t=t, nc=nc, l_valid=l_valid)
    return pl.pallas_call(
        kern,
        grid=(nseq // nb, nc),
        in_specs=[
            pl.BlockSpec((nb, t, G_KW), lambda b, c: (r0 + b, c, 0)),
            pl.BlockSpec((nb, t, G_KW), lambda b, c: (r0 + b, c, 1)),
            pl.BlockSpec((nb, t, G_VW), lambda b, c: (r0 + b, c, 1)),
            pl.BlockSpec((nb, t, LANES), lambda b, c: (r0 + b, c, 0)),
            full2(aup), full2(ab), st,
        ],
        out_specs=[pl.BlockSpec((nb, t, G_VW), lambda b, c: (b, c, 0)), st],
        out_shape=[
            jax.ShapeDtypeStruct((nseq, t * nc, G_VW), BF16),
            jax.ShapeDtypeStruct((nseq, N_HEADS, G_DK, G_DV), F32),
        ],
        compiler_params=_cparams(("parallel", "arbitrary")),
        name="gla",
    )(qkv3, qkv3, qkv3, al3, aup, ab, s0)


def _even_layer_kernel(x_ref, pre_ref, post_ref, win_ref, wgate_ref, gbias_ref, ng_ref, cw_ref, cb_ref, wa_ref,
                       ba_ref, wx_ref, bx_ref, lam_ref, w1_ref, w2_ref, c0_ref, n0_ref, m0_ref, h0_ref, conv0_ref,
                       y_ref, c_ref, n_ref, m_ref, hlast_ref, conv_ref,
                       qkv_s, rest_s, gate_s, hn_s, hm_s, hl_s, xkeep_s, hlru_s, xs_s, *, rows, tiles_per_seq):
    g = pl.program_id(0)
    slot = g % 2
    prev = 1 - slot
    t = MLSTM_TILES["main"][1]
    e_lo = 3 * M_WIDTH
    e = win_ref.shape[1]
    n_chunks = rows // t
    cols_per_chunk = e // n_chunks

    @pl.when(g == 0)
    def _():
        qkv_s[1] = jnp.zeros(qkv_s.shape[1:], qkv_s.dtype)
        rest_s[1] = jnp.zeros(rest_s.shape[1:], rest_s.dtype)
        gate_s[1] = jnp.zeros(gate_s.shape[1:], gate_s.dtype)
        xkeep_s[...] = jnp.zeros(xkeep_s.shape, xkeep_s.dtype)

    @pl.when(jnp.maximum(g - 1, 0) % tiles_per_seq == 0)
    def _():
        c_ref[...] = c0_ref[...]
        n_ref[...] = n0_ref[...]
        m_ref[...] = m0_ref[...]
        hlru_s[...] = h0_ref[...]
        xs_s[:, 0:SUBLANES, :] = conv0_ref[...]

    x = x_ref[0]
    hn_s[...] = (x * lax.rsqrt(jnp.mean(x * x, axis=-1, keepdims=True) + EPS) * pre_ref[...]).astype(BF16)
    gate_s[slot] = jnp.dot(hn_s[...], wgate_ref[...], preferred_element_type=F32)

    def project(c0, c1):
        acc = jnp.dot(hn_s[...], win_ref[:, c0:c1], preferred_element_type=F32)
        if c0 < e_lo:
            hi = min(c1, e_lo)
            qkv_s[slot, :, c0:hi] = acc[:, :hi - c0].astype(BF16)
        if c1 > e_lo:
            lo = max(c0, e_lo)
            rest_s[slot, :, lo - e_lo:c1 - e_lo] = acc[:, lo - c0:]

    row = lax.broadcasted_iota(jnp.int32, (t, t), 0)
    col = lax.broadcasted_iota(jnp.int32, (t, t), 1)
    causal = row >= col
    lane = lax.broadcasted_iota(jnp.int32, (1, LANES), 1)
    gbias = gbias_ref[...]
    heads = [(h, slice(h * M_DK, (h + 1) * M_DK)) for h in range(N_HEADS)]
    piece = 512
    pieces = [(c0, c0 + piece) for c0 in range(0, e, piece)]
    per_chunk = 5
    assert e % piece == 0 and e_lo % piece == 0 and len(pieces) >= n_chunks * per_chunk + 2

    w = LRU_WIDTH
    lru_h = [hlru_s[...]]
    subs_per_chunk = rows // LRU_SUB // n_chunks

    def lru_sub(i):
        r0 = i * LRU_SUB

        def emit(gi, hg, gated):
            hl_s[r0 + gi * SUBLANES:r0 + (gi + 1) * SUBLANES, :] = gated[0].astype(hl_s.dtype)

        lru_h[0] = _lru_chunk(rest_s[prev, r0:r0 + LRU_SUB, 2 * w:3 * w][None],
                              rest_s[prev, r0:r0 + LRU_SUB, 3 * w:4 * w][None], xs_s, lru_h[0],
                              cw_ref[...], cb_ref[...], wa_ref, ba_ref[...], wx_ref, bx_ref[...], lam_ref[...], emit)

    n_gaps = 4

    def lru_between(j, gap):
        for i in range(gap * subs_per_chunk // n_gaps, (gap + 1) * subs_per_chunk // n_gaps):
            lru_sub(j * subs_per_chunk + i)

    for j in range(n_chunks):
        rs = slice(j * t, (j + 1) * t)
        todo = pieces[j * per_chunk:(j + 1) * per_chunk]
        gt = gate_s[prev, rs, :] + gbias
        li = gt[:, :LANES]
        b = _cumsum_rows(_log_sigmoid(gt[:, LANES:]))
        project(*todo[0])
        lru_between(j, 0)
        r_t = _transpose_rows(li - b)[:, :t]
        m_vec = m_ref[0]
        st = []
        for h, sl in heads:
            b_col = b[:, h:h + 1]
            inter = b_col + m_vec[:, h:h + 1]
            dmat = jnp.where(causal, b_col + r_t[h:h + 1, :], NEG)
            mt = jnp.maximum(inter, jnp.max(dmat, axis=-1, keepdims=True))
            k = qkv_s[prev, rs, M_WIDTH + h * M_DK:M_WIDTH + (h + 1) * M_DK] * (M_DK ** -0.5)
            s = _dot_nt(qkv_s[prev, rs, sl], k) * jnp.exp(dmat - mt)
            st.append((inter, mt, k, s))
        project(*todo[1])
        lru_between(j, 1)
        project(*todo[2])
        for h, sl in heads:
            inter, mt, k, s = st[h]
            q = qkv_s[prev, rs, sl]
            v = qkv_s[prev, rs, 2 * M_WIDTH + h * M_DV:2 * M_WIDTH + (h + 1) * M_DV]
            w_inter = jnp.exp(inter - mt)
            num = w_inter * _dot_nt(q, c_ref[0, h]) + _dot(s, v)
            den = (w_inter * jnp.sum(q.astype(F32) * n_ref[0, h], axis=-1, keepdims=True)
                   + jnp.sum(s, axis=-1, keepdims=True))
            hm_s[rs, sl] = (num / jnp.maximum(jnp.abs(den), jnp.exp(-mt))).astype(hm_s.dtype)
        project(*todo[3])
        lru_between(j, 2)
        m_next = m_vec
        for h, sl in heads:
            inter, mt, k, _ = st[h]
            v = qkv_s[prev, rs, 2 * M_WIDTH + h * M_DV:2 * M_WIDTH + (h + 1) * M_DV]
            b_col = b[:, h:h + 1]
            m_new = mt[t - 1:t, :]
            w_c = jnp.exp(inter[t - 1:t, :] - m_new)
            w_k = jnp.exp(b_col[t - 1:t, :] - b_col + li[:, h:h + 1] - m_new)
            c_ref[0, h] = w_c * c_ref[0, h] + _dot_tn(v.astype(F32) * w_k, k)
            n_ref[0, h] = w_c * n_ref[0, h] + jnp.sum(k.astype(F32) * w_k, axis=0, keepdims=True)
            m_next = jnp.where(lane == h, m_new, m_next)
        m_ref[0] = m_next
        lru_between(j, 3)
        project(*todo[4])

    for p in pieces[n_chunks * per_chunk:]:
        project(*p)
    h_end = lru_h[0]
    hlru_s[...] = h_end
    hlast_ref[...] = h_end
    conv_ref[...] = xs_s[:, 0:SUBLANES, :]

    hm = _sigmoid(rest_s[prev, :, 0:w]) * hm_s[...].astype(F32)
    hm = _head_rms(hm, N_HEADS) * ng_ref[...] * _silu(rest_s[prev, :, w:2 * w])
    out = jnp.dot(hm.astype(BF16), w1_ref[...], preferred_element_type=F32)
    out = out + jnp.dot(hl_s[...], w2_ref[...], preferred_element_type=F32)
    nrm = out * lax.rsqrt(jnp.mean(out * out, axis=-1, keepdims=True) + EPS) * post_ref[...]
    y_ref[0] = xkeep_s[...] + nrm
    xkeep_s[...] = x


def _even_layer_main(x3, pre, post, w_main, w_gate, gbias, ng, lru, w1, w2, c0, n0, m0, h0, conv0, rows):
    nseq, seq, d = x3.shape
    assert seq % rows == 0 and rows % MLSTM_TILES["main"][1] == 0
    tps = seq // rows
    n_tiles = nseq * tps
    e = w_main.shape[1]
    e_lo = 3 * M_WIDTH
    w = LRU_WIDTH
    const = lambda a: pl.BlockSpec(a.shape, lambda g: (0,) * a.ndim, pipeline_mode=pl.Buffered(1))
    cur = lambda g: jnp.minimum(g, n_tiles - 1)
    old = lambda g: jnp.maximum(g - 1, 0)
    per_seq = lambda *blk: pl.BlockSpec((1,) + blk, lambda g: (old(g) // tps,) + (0,) * len(blk))
    st_c, st_n, st_m = per_seq(N_HEADS, M_DV, M_DK), per_seq(N_HEADS, 1, M_DK), per_seq(1, LANES)
    st_h, st_conv = per_seq(1, w), per_seq(SUBLANES, w)
    out_states = [st_c, st_n, st_m, st_h, st_conv]
    if c0.shape[0] == nseq:
        in_states = out_states
    else:
        in_states = [pl.BlockSpec(s.block_shape, lambda g, nd=len(s.block_shape): (0,) * nd) for s in out_states]
    kern = functools.partial(_even_layer_kernel, rows=rows, tiles_per_seq=tps)
    return pl.pallas_call(
        kern,
        grid=(n_tiles + 1,),
        in_specs=[
            pl.BlockSpec((1, rows, d), lambda g: (cur(g) // tps, cur(g) % tps, 0)),
            const(pre), const(post), const(w_main), const(w_gate), const(gbias), const(ng),
            *[const(a) for a in lru], const(w1), const(w2), *in_states,
        ],
        out_specs=[
            pl.BlockSpec((1, rows, d), lambda g: (old(g) // tps, old(g) % tps, 0)),
            st_c, st_n, st_m, st_h, st_conv,
        ],
        out_shape=[
            jax.ShapeDtypeStruct((nseq, seq, d), F32),
            jax.ShapeDtypeStruct((nseq, N_HEADS, M_DV, M_DK), F32),
            jax.ShapeDtypeStruct((nseq, N_HEADS, 1, M_DK), F32),
            jax.ShapeDtypeStruct((nseq, 1, LANES), F32),
            jax.ShapeDtypeStruct((nseq, 1, w), F32),
            jax.ShapeDtypeStruct((nseq, SUBLANES, w), F32),
        ],
        scratch_shapes=[
            pltpu.VMEM((2, rows, e_lo), BF16),
            pltpu.VMEM((2, rows, e - e_lo), F32),
            pltpu.VMEM((2, rows, 2 * LANES), F32),
            pltpu.VMEM((rows, d), BF16),
            pltpu.VMEM((rows, M_WIDTH), BF16),
            pltpu.VMEM((rows, w), BF16),
            pltpu.VMEM((rows, d), F32),
            pltpu.VMEM((1, 1, w), F32),
            pltpu.VMEM((1, SUBLANES + LRU_SUB, w), F32),
        ],
        compiler_params=_cparams(("arbitrary",)),
        name="even_layer",
    )(x3, pre, post, w_main, w_gate, gbias, ng, *lru, w1, w2, c0, n0, m0, h0, conv0)


def _gla_layer_kernel(x_ref, pre_ref, post_ref, win_ref, wlow_ref, aup_ref, ab_ref, ng_ref, wout_ref, s0_ref,
                      y_ref, s_ref, qkv_s, r_s, al_s, hn_s, o_s, xkeep_s, *, rows, tiles_per_seq):
    g = pl.program_id(0)
    slot = g % 2
    prev = 1 - slot
    t = GLA_TILES["main"][1]
    e_lo = 2 * G_KW + G_VW
    e = e_lo + G_VW
    n_chunks = rows // t
    cols_per_chunk = e // n_chunks

    @pl.when(g == 0)
    def _():
        qkv_s[1] = jnp.zeros(qkv_s.shape[1:], qkv_s.dtype)
        r_s[1] = jnp.zeros(r_s.shape[1:], r_s.dtype)
        al_s[1] = jnp.zeros(al_s.shape[1:], al_s.dtype)
        xkeep_s[...] = jnp.zeros(xkeep_s.shape, xkeep_s.dtype)

    @pl.when(jnp.maximum(g - 1, 0) % tiles_per_seq == 0)
    def _():
        s_ref[...] = s0_ref[...]

    x = x_ref[0]
    hn_s[...] = (x * lax.rsqrt(jnp.mean(x * x, axis=-1, keepdims=True) + EPS) * pre_ref[...]).astype(BF16)
    al_s[slot] = jnp.dot(hn_s[...], wlow_ref[...], preferred_element_type=F32)

    row = lax.broadcasted_iota(jnp.int32, (t, t), 0)
    col = lax.broadcasted_iota(jnp.int32, (t, t), 1)
    causal = row >= col
    aup = aup_ref[...]
    ab = ab_ref[...]
    heads = [(h, slice(h * G_DK, (h + 1) * G_DK), slice(h * G_DV, (h + 1) * G_DV)) for h in range(N_HEADS)]

    def project(c0, c1):
        acc = jnp.dot(hn_s[...], win_ref[:, c0:c1], preferred_element_type=F32)
        if c0 < e_lo:
            hi = min(c1, e_lo)
            qkv_s[slot, :, c0:hi] = acc[:, :hi - c0].astype(BF16)
        if c1 > e_lo:
            lo = max(c0, e_lo)
            r_s[slot, :, lo - e_lo:c1 - e_lo] = acc[:, lo - c0:]

    n_sub = 3
    sub = cols_per_chunk // n_sub
    for j in range(n_chunks):
        c0 = j * cols_per_chunk
        rs = slice(j * t, (j + 1) * t)
        pre = _dot(al_s[prev, rs, :], aup) + ab
        project(c0, c0 + sub)
        b = _cumsum_rows(_log_sigmoid(pre) / G_TAU)
        b_last = b[t - 1:t, :]
        project(c0 + sub, c0 + 2 * sub)
        k = qkv_s[prev, rs, G_KW:2 * G_KW].astype(F32)
        qd = ((qkv_s[prev, rs, 0:G_KW].astype(F32) * (G_DK ** -0.5)) * jnp.exp(b)).astype(BF16)
        kn = (k * jnp.exp(-b)).astype(BF16)
        kr = (k * jnp.exp(b_last - b)).astype(BF16)
        ebl_col = _row_to_col(jnp.exp(b_last))
        att = [jnp.where(causal, _dot_nt(qd[:, ks], kn[:, ks]), 0.0).astype(BF16) for _, ks, _ in heads]
        project(c0 + 2 * sub, c0 + cols_per_chunk)
        for h, ks, vs in heads:
            s_h = s_ref[0, h]
            v = qkv_s[prev, rs, 2 * G_KW + h * G_DV:2 * G_KW + (h + 1) * G_DV]
            o_s[rs, vs] = (_dot(qd[:, ks], s_h) + _dot(att[h], v)).astype(o_s.dtype)
            s_ref[0, h] = ebl_col[ks, :] * s_h + _dot_tn(kr[:, ks], v)

    o = _head_rms(o_s[...].astype(F32), N_HEADS) * ng_ref[...] * _silu(r_s[prev])
    out = jnp.dot(o.astype(BF16), wout_ref[...], preferred_element_type=F32)
    nrm = out * lax.rsqrt(jnp.mean(out * out, axis=-1, keepdims=True) + EPS) * post_ref[...]
    y_ref[0] = xkeep_s[...] + nrm
    xkeep_s[...] = x


def _gla_layer(x3, pre, post, w_in, w_low, aup, ab, ng, w_out, s0, rows):
    nseq, seq, d = x3.shape
    assert seq % rows == 0 and rows % GLA_TILES["main"][1] == 0
    tps = seq // rows
    n_tiles = nseq * tps
    e = w_in.shape[1]
    e_lo = 2 * G_KW + G_VW
    const = lambda a: pl.BlockSpec(a.shape, lambda g: (0,) * a.ndim, pipeline_mode=pl.Buffered(1))
    cur = lambda g: jnp.minimum(g, n_tiles - 1)
    old = lambda g: jnp.maximum(g - 1, 0)
    st = pl.BlockSpec((1, N_HEADS, G_DK, G_DV), lambda g: (old(g) // tps, 0, 0, 0))
    st_in = st if s0.shape[0] == nseq else pl.BlockSpec((1, N_HEADS, G_DK, G_DV), lambda g: (0, 0, 0, 0))
    kern = functools.partial(_gla_layer_kernel, rows=rows, tiles_per_seq=tps)
    return pl.pallas_call(
        kern,
        grid=(n_tiles + 1,),
        in_specs=[
            pl.BlockSpec((1, rows, d), lambda g: (cur(g) // tps, cur(g) % tps, 0)),
            const(pre), const(post), const(w_in), const(w_low), const(aup), const(ab), const(ng), const(w_out), st_in,
        ],
        out_specs=[pl.BlockSpec((1, rows, d), lambda g: (old(g) // tps, old(g) % tps, 0)), st],
        out_shape=[
            jax.ShapeDtypeStruct((nseq, seq, d), F32),
            jax.ShapeDtypeStruct((nseq, N_HEADS, G_DK, G_DV), F32),
        ],
        scratch_shapes=[
            pltpu.VMEM((2, rows, e_lo), BF16),
            pltpu.VMEM((2, rows, e - e_lo), F32),
            pltpu.VMEM((2, rows, LANES), F32),
            pltpu.VMEM((rows, d), BF16),
            pltpu.VMEM((rows, G_VW), BF16),
            pltpu.VMEM((rows, d), F32),
        ],
        compiler_params=_cparams(("arbitrary",)),
        name="gla_layer",
    )(x3, pre, post, w_in, w_low, aup, ab, ng, w_out, s0)


def _pad_lanes(a, width):
    return jnp.pad(a, [(0, 0)] * (a.ndim - 1) + [(0, width - a.shape[-1])])


def _even_layer(xm, xs, nbp, seq, nbs, ls, st, pre_g, post_g, w_in, w_out, b_i, b_f, m_norm_g,
                conv_w, conv_b, wa, ba, wx, bx, lam):
    c0s, n0s, m0s, h0s, conv0s = st
    wq, wk, wv, wog, wig, wfg, wz, wxr, wgr = jnp.split(
        w_in, [1024, 2048, 3072, 4096, 4100, 4104, 5128, 6152], axis=1)
    w_main = jnp.concatenate([wq, wk, wv, wog, wz, wxr, wgr], axis=1).astype(BF16)
    w_gate = jnp.concatenate([_pad_lanes(wig, LANES), _pad_lanes(wfg, LANES)], axis=1).astype(BF16)
    gbias = jnp.concatenate([_pad_lanes(b_i[None], LANES), _pad_lanes(b_f[None], LANES)], axis=1)
    ng = m_norm_g[None]
    post = post_g[None]
    cw, cb = conv_w, conv_b[None]
    wab, wxb = wa.astype(BF16), wx.astype(BF16)
    bav, bxv, lamv = ba[None], bx[None], lam[None]
    w_o1, w_o2 = w_out[:M_WIDTH].astype(BF16), w_out[M_WIDTH:].astype(BF16)
    e_lo = 3 * M_WIDTH
    e_hi = w_main.shape[1] - e_lo
    lru = (cw, cb, wab, bav, wxb, bxv, lamv)

    qs, ps, gs = _norm_proj(xs, pre_g, w_main, w_gate, e_lo, xs.shape[0])
    mrow = nbs * SAMPLE_PAD // N_META
    xr_cols = slice(2 * LRU_WIDTH, 3 * LRU_WIDTH)

    zc = jnp.zeros((1, N_HEADS, M_DV, M_DK), F32)
    zn = jnp.zeros((1, N_HEADS, 1, M_DK), F32)
    zm = jnp.zeros((1, 1, LANES), F32)
    zh = jnp.zeros((1, 1, LRU_WIDTH), F32)
    zconv = jnp.zeros((1, SUBLANES, LRU_WIDTH), F32)
    qs_meta = qs.reshape(-1, N_META, e_lo)
    ps_meta = ps.reshape(-1, N_META, e_hi)
    gs_meta = gs.reshape(-1, N_META, 2 * LANES)
    h_meta, c1, n1, m1 = _mlstm(qs_meta, gs_meta, mrow, 1, MLSTM_TILES["meta"], 1, N_META, gbias, zc, zn, zm)
    hl_meta, h1 = _rglru(ps_meta, mrow, 1, LRU_TILES["meta"], 1, N_META, *lru, zh, zconv)
    conv1 = ps_meta[mrow:mrow + 1, N_META - SUBLANES:, xr_cols]

    ym3, pc, pn, pmm, ph, conv_tail = _even_layer_main(
        xm.reshape(nbp, seq, -1), pre_g[None], post, w_main, w_gate, gbias, ng, lru, w_o1, w_o2,
        c1, n1, m1, h1, conv1, FUSED_ROW_TILE)
    ym = ym3.reshape(xm.shape)
    pconv = conv_tail[:, SUBLANES - (CONV_W - 1):, :]

    qs_s = qs.reshape(-1, SAMPLE_PAD, e_lo)
    ps_s = ps.reshape(-1, SAMPLE_PAD, e_hi)
    gs_s = gs.reshape(-1, SAMPLE_PAD, 2 * LANES)
    h_s, sc, sn, sm = _mlstm(qs_s, gs_s, 0, nbs, MLSTM_TILES["sample"], 1, ls, gbias,
                             c0s, n0s[:, :, None, :], _pad_lanes(m0s, LANES)[:, None, :])
    conv0p = jnp.pad(conv0s, ((0, 0), (SUBLANES - (CONV_W - 1), 0), (0, 0)))
    hl_s, sh = _rglru(ps_s, 0, nbs, LRU_TILES["sample"], 1, ls, *lru, h0s[:, None, :], conv0p)
    xr_s = ps_s[:nbs, :ls, xr_cols]
    sconv = jnp.concatenate([conv0s, xr_s], axis=1)[:, -(CONV_W - 1):]

    h_small = jnp.concatenate([h_s.reshape(-1, M_WIDTH), h_meta.reshape(-1, M_WIDTH)], axis=0)
    hl_small = jnp.concatenate([hl_s.reshape(-1, LRU_WIDTH), hl_meta.reshape(-1, LRU_WIDTH)], axis=0)
    ys = _out_even(xs, post, h_small, ps, ng, hl_small, w_o1, w_o2, _row_tile(xs.shape[0], OUT_ROW_TILE))

    p_state = (pc, pn[:, :, 0, :], pmm[:, 0, :N_HEADS], ph[:, 0, :], pconv)
    s_state = (sc, sn[:, :, 0, :], sm[:, 0, :N_HEADS], sh[:, 0, :], sconv)
    return ym, ys, p_state, s_state


def _odd_layer(xm, xs, nbp, seq, nbs, ls, s0s, pre_g, post_g, w_in, w_out, a_up, a_b, g_norm_g):
    e_lo = 2 * G_KW + G_VW
    w_main = w_in[:, :2 * G_KW + 2 * G_VW].astype(BF16)
    w_low = _pad_lanes(w_in[:, 2 * G_KW + 2 * G_VW:], LANES).astype(BF16)
    aup = jnp.pad(a_up, ((0, LANES - G_RANK), (0, 0))).astype(BF16)
    ab = a_b[None]
    ng = g_norm_g[None]
    post = post_g[None]
    w_o = w_out.astype(BF16)

    qs, rs, as_ = _norm_proj(xs, pre_g, w_main, w_low, e_lo, xs.shape[0])
    mrow = nbs * SAMPLE_PAD // N_META

    zs = jnp.zeros((1, N_HEADS, G_DK, G_DV), F32)
    o_meta, s1 = _gla(qs.reshape(-1, N_META, e_lo), as_.reshape(-1, N_META, LANES),
                      mrow, 1, GLA_TILES["meta"], 1, N_META, aup, ab, zs)
    ym3, p_s = _gla_layer(xm.reshape(nbp, seq, -1), pre_g[None], post, w_main, w_low, aup, ab, ng, w_o, s1,
                          GLA_ROW_TILE)
    ym = ym3.reshape(xm.shape)
    o_s, s_s = _gla(qs.reshape(-1, SAMPLE_PAD, e_lo), as_.reshape(-1, SAMPLE_PAD, LANES),
                    0, nbs, GLA_TILES["sample"], 1, ls, aup, ab, s0s)

    o_small = jnp.concatenate([o_s.reshape(-1, G_VW), o_meta.reshape(-1, G_VW)], axis=0)
    ys = _out_odd(xs, post, o_small, rs, ng, w_o, _row_tile(xs.shape[0], OUT_ROW_TILE))
    return ym, ys, p_s, s_s


def kernel(x_prompt, x_sample, state_mlstm_C, state_mlstm_n, state_mlstm_m, state_rglru_h, state_rglru_conv,
           state_gla_S, meta_tokens, pre_norm_a, post_norm_a, w_in_a, w_out_a, mlstm_b_i, mlstm_b_f, mlstm_norm,
           conv_w, conv_b, lru_w_a, lru_b_a, lru_w_x, lru_b_x, lru_lambda, pre_norm_c, post_norm_c, w_in_c,
           w_out_c, gla_alpha_up, gla_alpha_b, gla_norm):
    nbp, seq, d = x_prompt.shape
    nbs, ls, _ = x_sample.shape
    depth = pre_norm_a.shape[0] + pre_norm_c.shape[0]
    assert ls >= CONV_W - 1 and ls <= SAMPLE_PAD and N_META % SUBLANES == 0 and nbs * SAMPLE_PAD % N_META == 0

    xm = x_prompt.reshape(nbp * seq, d)
    xs_pad = jnp.pad(x_sample, ((0, 0), (0, SAMPLE_PAD - ls), (0, 0))).reshape(nbs * SAMPLE_PAD, d)
    xs = jnp.concatenate([xs_pad, meta_tokens.astype(x_prompt.dtype)], axis=0)

    p_lists = [[] for _ in range(6)]
    s_lists = [[] for _ in range(6)]
    for layer in range(depth):
        j = layer // 2
        if layer % 2 == 0:
            st = (state_mlstm_C[j], state_mlstm_n[j], state_mlstm_m[j], state_rglru_h[j], state_rglru_conv[j])
            xm, xs, pst, sst = _even_layer(
                xm, xs, nbp, seq, nbs, ls, st, pre_norm_a[j], post_norm_a[j], w_in_a[j], w_out_a[j],
                mlstm_b_i[j], mlstm_b_f[j], mlstm_norm[j], conv_w[j], conv_b[j], lru_w_a[j], lru_b_a[j],
                lru_w_x[j], lru_b_x[j], lru_lambda[j])
            for i in range(5):
                p_lists[i].append(pst[i])
                s_lists[i].append(sst[i])
        else:
            xm, xs, p_s, s_s = _odd_layer(
                xm, xs, nbp, seq, nbs, ls, state_gla_S[j], pre_norm_c[j], post_norm_c[j], w_in_c[j], w_out_c[j],
                gla_alpha_up[j], gla_alpha_b[j], gla_norm[j])
            p_lists[5].append(p_s)
            s_lists[5].append(s_s)

    y_prompt = xm.reshape(nbp, seq, d)
    y_sample = xs[:nbs * SAMPLE_PAD].reshape(nbs, SAMPLE_PAD, d)[:, :ls]
    return (y_prompt, y_sample) + tuple(jnp.stack(l) for l in p_lists) + tuple(jnp.stack(l) for l in s_lists)
```

```python
import functools

import jax
import jax.numpy as jnp
from jax import lax
from jax.experimental import pallas as pl
from jax.experimental.pallas import tpu as pltpu

F32 = jnp.float32
BF16 = jnp.bfloat16

D_MODEL = 1024
N_META = 16
EPS = 1e-6
N_HEADS = 4
M_DK = 256
M_DV = 256
M_WIDTH = N_HEADS * M_DV
LRU_WIDTH = 1024
LRU_BLOCKS = 8
LRU_BS = LRU_WIDTH // LRU_BLOCKS
CONV_W = 4
LRU_C = 8.0
G_DK = 256
G_DV = 512
G_KW = N_HEADS * G_DK
G_VW = N_HEADS * G_DV
G_RANK = 16
G_TAU = 16.0

LANES = 128
SUBLANES = 8
NEG = -1e30
SAMPLE_PAD = SUBLANES
VMEM_LIMIT = 56 * 1024 * 1024
COL_TILE = 1024

MLSTM_TILES = {"meta": (1, N_META), "main": (4, 128), "sample": (4, SAMPLE_PAD)}
GLA_TILES = {"meta": (1, N_META), "main": (2, 64), "sample": (4, SAMPLE_PAD)}
LRU_TILES = {"meta": (1, N_META), "main": (1, 256), "sample": (16, SAMPLE_PAD)}
OUT_ROW_TILE = 512
FUSED_ROW_TILE = 256
GLA_ROW_TILE = 256
LRU_SUB = 64


def _row_tile(n, cap):
    return next(tm for tm in range(min(n, cap) // SUBLANES * SUBLANES, 0, -SUBLANES) if n % tm == 0)


def _cparams(sem):
    return pltpu.CompilerParams(dimension_semantics=sem, vmem_limit_bytes=VMEM_LIMIT)


def _sigmoid(x):
    return 0.5 * jnp.tanh(0.5 * x) + 0.5


def _silu(x):
    return x * _sigmoid(x)


def _log_sigmoid(x):
    return jnp.minimum(x, 0.0) - jnp.log1p(jnp.exp(-jnp.abs(x)))


def _softplus(x):
    return jnp.maximum(x, 0.0) + jnp.log1p(jnp.exp(-jnp.abs(x)))


def _dot(a, b):
    return jnp.dot(a.astype(BF16), b.astype(BF16), preferred_element_type=F32)


def _dot_nt(a, b):
    return lax.dot_general(a.astype(BF16), b.astype(BF16), (((1,), (1,)), ((), ())),
                           preferred_element_type=F32)


def _dot_tn(a, b):
    return lax.dot_general(a.astype(BF16), b.astype(BF16), (((0,), (0,)), ((), ())),
                           preferred_element_type=F32)


def _transpose_rows(x):
    t, n = x.shape
    if t < LANES:
        x = jnp.concatenate([x, jnp.zeros((LANES - t, n), x.dtype)], axis=0)
    return jnp.transpose(x)


def _row_to_col(r):
    return jnp.transpose(jnp.broadcast_to(r, (LANES, r.shape[1])))[:, 0:1]


def _cumsum_rows(x):
    t = x.shape[0]
    if t <= 2 * SUBLANES:
        rid = lax.broadcasted_iota(jnp.int32, (t, 1), 0)
        s = 1
        while s < t:
            x = x + jnp.where(rid >= s, pltpu.roll(x, s, axis=0), 0.0)
            s *= 2
        return x
    row = lax.broadcasted_iota(jnp.int32, (t, t), 0)
    col = lax.broadcasted_iota(jnp.int32, (t, t), 1)
    tri = jnp.where(row >= col, 1.0, 0.0).astype(BF16)
    hi = x.astype(BF16)
    r1 = x - hi.astype(F32)
    mid = r1.astype(BF16)
    lo = (r1 - mid.astype(F32)).astype(BF16)
    acc = jnp.dot(tri, lo, preferred_element_type=F32)
    acc = acc + jnp.dot(tri, mid, preferred_element_type=F32)
    return acc + jnp.dot(tri, hi, preferred_element_type=F32)


def _head_rms(x, nh):
    hd = x.shape[1] // nh
    parts = []
    for h in range(nh):
        xh = x[:, h * hd:(h + 1) * hd]
        parts.append(xh * lax.rsqrt(jnp.mean(xh * xh, axis=-1, keepdims=True) + EPS))
    return jnp.concatenate(parts, axis=-1)


GROUP = 2


def _group_loop(nb, group_fn):
    if nb <= GROUP:
        group_fn(list(range(nb)))
        return

    def body(i, carry):
        group_fn([GROUP * i + r for r in range(GROUP)])
        return carry

    lax.fori_loop(0, nb // GROUP, body, 0)


def _norm_proj_kernel(n_lo, n_a, x_ref, g_ref, wa_ref, wb_ref, ws_ref, olo_ref, ohi_ref, os_ref, hn_ref):
    j = pl.program_id(1)

    @pl.when(j == 0)
    def _():
        x = x_ref[...]
        y = x * lax.rsqrt(jnp.mean(x * x, axis=-1, keepdims=True) + EPS) * g_ref[...]
        hn = y.astype(BF16)
        hn_ref[...] = hn
        os_ref[...] = jnp.dot(hn, ws_ref[...], preferred_element_type=F32)

    @pl.when(j < n_lo)
    def _():
        olo_ref[...] = jnp.dot(hn_ref[...], wa_ref[...], preferred_element_type=F32).astype(olo_ref.dtype)

    @pl.when((j >= n_lo) & (j < n_a))
    def _():
        ohi_ref[...] = jnp.dot(hn_ref[...], wa_ref[...], preferred_element_type=F32)

    @pl.when(j >= n_a)
    def _():
        ohi_ref[...] = jnp.dot(hn_ref[...], wb_ref[...], preferred_element_type=F32)


def _norm_proj(x2d, g, w_a, e_a, w_b, w_small, e_lo, tm):
    n, d = x2d.shape
    tn = COL_TILE
    if w_b is None:
        w_b, e = w_a, e_a
    else:
        e = e_a + w_b.shape[1]
    es = w_small.shape[1]
    n_lo, n_a = e_lo // tn, e_a // tn
    assert e_lo <= e_a and e_a % tn == 0 and e % tn == 0
    return pl.pallas_call(
        functools.partial(_norm_proj_kernel, n_lo, n_a),
        grid=(n // tm, e // tn),
        in_specs=[
            pl.BlockSpec((tm, d), lambda i, j: (i, 0)),
            pl.BlockSpec((1, d), lambda i, j: (0, 0)),
            pl.BlockSpec((d, tn), lambda i, j: (0, jnp.minimum(j, n_a - 1))),
            pl.BlockSpec((d, tn), lambda i, j: (0, jnp.maximum(j - n_a, 0))),
            pl.BlockSpec((d, es), lambda i, j: (0, 0)),
        ],
        out_specs=[
            pl.BlockSpec((tm, tn), lambda i, j: (i, jnp.minimum(j, n_lo - 1))),
            pl.BlockSpec((tm, tn), lambda i, j: (i, jnp.maximum(j - n_lo, 0))),
            pl.BlockSpec((tm, es), lambda i, j: (i, 0)),
        ],
        out_shape=[
            jax.ShapeDtypeStruct((n, e_lo), BF16),
            jax.ShapeDtypeStruct((n, e - e_lo), F32),
            jax.ShapeDtypeStruct((n, es), F32),
        ],
        scratch_shapes=[pltpu.VMEM((tm, d), BF16)],
        compiler_params=_cparams(("parallel", "arbitrary")),
        name="norm_proj",
    )(x2d, g.reshape(1, d), w_a, w_b, w_small)


def _residual_norm(x_ref, g_ref, y_ref, out):
    nrm = out * lax.rsqrt(jnp.mean(out * out, axis=-1, keepdims=True) + EPS) * g_ref[...]
    y_ref[...] = x_ref[...] + nrm


def _out_even_kernel(x_ref, g_ref, h_ref, og_ref, z_ref, ng_ref, hl_ref, w1_ref, w2_ref, y_ref):
    hm = _sigmoid(og_ref[...]) * h_ref[...].astype(F32)
    hm = _head_rms(hm, N_HEADS) * ng_ref[...] * _silu(z_ref[...])
    out = jnp.dot(hm.astype(BF16), w1_ref[...], preferred_element_type=F32)
    out = out + jnp.dot(hl_ref[...], w2_ref[...], preferred_element_type=F32)
    _residual_norm(x_ref, g_ref, y_ref, out)


def _out_even(x2d, g, h_raw, rest, ng, hl, w1, w2, tm):
    n, d = x2d.shape
    row = lambda width, col: pl.BlockSpec((tm, width), lambda i: (i, col))
    const = lambda a: pl.BlockSpec(a.shape, lambda i: (0, 0))
    return pl.pallas_call(
        _out_even_kernel,
        grid=(n // tm,),
        in_specs=[row(d, 0), const(g), row(M_WIDTH, 0), row(M_WIDTH, 0), row(M_WIDTH, 1), const(ng),
                  row(LRU_WIDTH, 0), const(w1), const(w2)],
        out_specs=row(d, 0),
        out_shape=jax.ShapeDtypeStruct((n, d), F32),
        compiler_params=_cparams(("parallel",)),
        name="out_even",
    )(x2d, g, h_raw, rest, rest, ng, hl, w1, w2)


def _out_odd_kernel(x_ref, g_ref, o_ref, r_ref, ng_ref, w_ref, y_ref):
    o = _head_rms(o_ref[...].astype(F32), N_HEADS) * ng_ref[...] * _silu(r_ref[...])
    out = jnp.dot(o.astype(BF16), w_ref[...], preferred_element_type=F32)
    _residual_norm(x_ref, g_ref, y_ref, out)


def _out_odd(x2d, g, o_raw, r, ng, w, tm):
    n, d = x2d.shape
    row = lambda width: pl.BlockSpec((tm, width), lambda i: (i, 0))
    const = lambda a: pl.BlockSpec(a.shape, lambda i: (0, 0))
    return pl.pallas_call(
        _out_odd_kernel,
        grid=(n // tm,),
        in_specs=[row(d), const(g), row(G_VW), row(G_VW), const(ng), const(w)],
        out_specs=row(d),
        out_shape=jax.ShapeDtypeStruct((n, d), F32),
        compiler_params=_cparams(("parallel",)),
        name="out_odd",
    )(x2d, g, o_raw, r, ng, w)


def _mlstm_kernel(q_ref, k_ref, v_ref, gate_ref, gbias_ref, c0_ref, n0_ref, m0_ref,
                  h_ref, c_ref, n_ref, m_ref, *, nb, t, nc, l_valid):
    c = pl.program_id(1)
    if nc == 1:
        c_in, n_in, m_in = c0_ref, n0_ref, m0_ref
    else:
        c_in, n_in, m_in = c_ref, n_ref, m_ref

        @pl.when(c == 0)
        def _():
            c_ref[...] = c0_ref[...]
            n_ref[...] = n0_ref[...]
            m_ref[...] = m0_ref[...]

    pos = c * t + lax.broadcasted_iota(jnp.int32, (t, 1), 0)
    valid = pos < l_valid
    row = lax.broadcasted_iota(jnp.int32, (t, t), 0)
    col = lax.broadcasted_iota(jnp.int32, (t, t), 1)
    causal = row >= col
    lane = lax.broadcasted_iota(jnp.int32, (1, LANES), 1)
    gbias = gbias_ref[...]
    heads = [(h, slice(h * M_DK, (h + 1) * M_DK)) for h in range(N_HEADS)]

    def group(rows):
        pairs = [(i, r, h, sl) for i, r in enumerate(rows) for h, sl in heads]
        gates = []
        for r in rows:
            g = gate_ref[r] + gbias
            li = jnp.where(valid, g[:, :LANES], NEG)
            lf = jnp.where(valid, _log_sigmoid(g[:, LANES:]), 0.0)
            b = _cumsum_rows(lf)
            gates.append((li, b, _transpose_rows(li - b)[:, :t], m_in[r]))
        st = {}
        for i, r, h, sl in pairs:
            li, b, r_t, m_vec = gates[i]
            b_col = b[:, h:h + 1]
            inter = b_col + m_vec[:, h:h + 1]
            dmat = jnp.where(causal, b_col + r_t[h:h + 1, :], NEG)
            mt = jnp.maximum(inter, jnp.max(dmat, axis=-1, keepdims=True))
            k = k_ref[r, :, sl] * (M_DK ** -0.5)
            s = _dot_nt(q_ref[r, :, sl], k) * jnp.exp(dmat - mt)
            st[i, h] = (inter, mt, k, s)
        for i, r, h, sl in pairs:
            inter, mt, k, s = st[i, h]
            q = q_ref[r, :, sl]
            w_inter = jnp.exp(inter - mt)
            num = w_inter * _dot_nt(q, c_in[r, h]) + _dot(s, v_ref[r, :, sl])
            den = (w_inter * jnp.sum(q.astype(F32) * n_in[r, h], axis=-1, keepdims=True)
                   + jnp.sum(s, axis=-1, keepdims=True))
            h_ref[r, :, sl] = (num / jnp.maximum(jnp.abs(den), jnp.exp(-mt))).astype(h_ref.dtype)
        m_next = [gt[3] for gt in gates]
        for i, r, h, sl in pairs:
            li, b, _, _ = gates[i]
            inter, mt, k, _ = st[i, h]
            b_col = b[:, h:h + 1]
            m_new = mt[t - 1:t, :]
            w_c = jnp.exp(inter[t - 1:t, :] - m_new)
            w_k = jnp.exp(b_col[t - 1:t, :] - b_col + li[:, h:h + 1] - m_new)
            c_ref[r, h] = w_c * c_in[r, h] + _dot_tn(v_ref[r, :, sl].astype(F32) * w_k, k)
            n_ref[r, h] = w_c * n_in[r, h] + jnp.sum(k.astype(F32) * w_k, axis=0, keepdims=True)
            m_next[i] = jnp.where(lane == h, m_new, m_next[i])
        for i, r in enumerate(rows):
            m_ref[r] = m_next[i]

    _group_loop(nb, group)


def _mlstm(qkv3, gates3, row0, nseq, tiles, nc, l_valid, gbias, c0, n0, m0):
    nb, t = tiles
    assert nseq % nb == 0 and row0 % nb == 0
    wd = M_WIDTH
    r0 = row0 // nb
    seq = lambda col: pl.BlockSpec((nb, t, wd), lambda b, c: (r0 + b, c, col))
    st_c = pl.BlockSpec((nb, N_HEADS, M_DV, M_DK), lambda b, c: (b, 0, 0, 0))
    st_n = pl.BlockSpec((nb, N_HEADS, 1, M_DK), lambda b, c: (b, 0, 0, 0))
    st_m = pl.BlockSpec((nb, 1, LANES), lambda b, c: (b, 0, 0))
    kern = functools.partial(_mlstm_kernel, nb=nb, t=t, nc=nc, l_valid=l_valid)
    return pl.pallas_call(
        kern,
        grid=(nseq // nb, nc),
        in_specs=[
            seq(0), seq(1), seq(2),
            pl.BlockSpec((nb, t, 2 * LANES), lambda b, c: (r0 + b, c, 0)),
            pl.BlockSpec(gbias.shape, lambda b, c: (0, 0)), st_c, st_n, st_m,
        ],
        out_specs=[pl.BlockSpec((nb, t, wd), lambda b, c: (b, c, 0)), st_c, st_n, st_m],
        out_shape=[
            jax.ShapeDtypeStruct((nseq, t * nc, wd), BF16),
            jax.ShapeDtypeStruct((nseq, N_HEADS, M_DV, M_DK), F32),
            jax.ShapeDtypeStruct((nseq, N_HEADS, 1, M_DK), F32),
            jax.ShapeDtypeStruct((nseq, 1, LANES), F32),
        ],
        compiler_params=_cparams(("parallel", "arbitrary")),
        name="mlstm",
    )(qkv3, qkv3, qkv3, gates3, gbias, c0, n0, m0)


def _lru_chunk(x, gr, xs_s, h, cw, cb, wa_ref, ba, wx_ref, bx, lam, emit, spread=lambda: None):
    nb, t, w = x.shape
    ng = t // SUBLANES
    xs_s[:, SUBLANES:SUBLANES + t, :] = x
    xc = cb + cw[CONV_W - 1:CONV_W, :] * x
    for s in range(1, CONV_W):
        xc = xc + cw[CONV_W - 1 - s:CONV_W - s, :] * xs_s[:, SUBLANES - s:SUBLANES - s + t, :]
    xs_s[:, 0:SUBLANES, :] = x[:, t - SUBLANES:, :]
    spread()

    xf = xc.reshape(nb * t, w)
    ra, ri = [], []
    for n in range(LRU_BLOCKS):
        xb = xf[:, n * LRU_BS:(n + 1) * LRU_BS].astype(BF16)
        ra.append(jnp.dot(xb, wa_ref[n], preferred_element_type=F32))
        ri.append(jnp.dot(xb, wx_ref[n], preferred_element_type=F32))
    spread()
    r = _sigmoid(jnp.concatenate(ra, axis=-1) + ba)
    i = _sigmoid(jnp.concatenate(ri, axis=-1) + bx)
    log_a = -LRU_C * r * _softplus(-lam)
    a = jnp.exp(log_a)
    u = jnp.sqrt(-jnp.tanh(log_a) * (1.0 + a * a)) * (i * xf)
    spread()

    rid = lax.broadcasted_iota(jnp.int32, (1, SUBLANES, 1), 1)
    a = a.reshape(nb * ng, SUBLANES, w)
    u = u.reshape(nb * ng, SUBLANES, w)
    for s in (1, 2, 4):
        a_sh = jnp.where(rid >= s, pltpu.roll(a, s, axis=1), 1.0)
        u_sh = jnp.where(rid >= s, pltpu.roll(u, s, axis=1), 0.0)
        u = a * u_sh + u
        a = a * a_sh
    a = a.reshape(nb, ng, SUBLANES, w)
    u = u.reshape(nb, ng, SUBLANES, w)
    gr4 = gr.reshape(nb, ng, SUBLANES, w)
    for g in range(ng):
        if g == ng // 2:
            spread()
        hg = a[:, g] * h + u[:, g]
        h = hg[:, SUBLANES - 1:SUBLANES]
        emit(g, hg, hg * _silu(gr4[:, g]))
    return h


def _rglru_kernel(x_ref, gr_ref, cw_ref, cb_ref, wa_ref, ba_ref, wx_ref, bx_ref, lam_ref, h0_ref, conv0_ref,
                  hl_ref, hlast_ref, h_s, xs_s, *, nb, t, l_valid):
    c = pl.program_id(1)

    @pl.when(c == 0)
    def _():
        h_s[...] = h0_ref[...]
        xs_s[:, 0:SUBLANES, :] = conv0_ref[...]

    g_last, r_last = divmod((l_valid - 1) % t, SUBLANES)

    def emit(g, hg, gated):
        hl_ref[:, g * SUBLANES:(g + 1) * SUBLANES, :] = gated.astype(hl_ref.dtype)
        if g == g_last:
            @pl.when(c == (l_valid - 1) // t)
            def _():
                hlast_ref[...] = hg[:, r_last:r_last + 1]

    h_s[...] = _lru_chunk(x_ref[...], gr_ref[...], xs_s, h_s[...], cw_ref[...], cb_ref[...], wa_ref, ba_ref[...],
                          wx_ref, bx_ref[...], lam_ref[...], emit)


def _rglru(rest3, row0, nseq, tiles, nc, l_valid, cw, cb, wa, ba, wx, bx, lam, h0, conv0):
    nb, t = tiles
    assert nseq % nb == 0 and row0 % nb == 0
    w = LRU_WIDTH
    r0 = row0 // nb
    full2 = lambda a: pl.BlockSpec(a.shape, lambda b, c: (0, 0))
    full3 = lambda a: pl.BlockSpec(a.shape, lambda b, c: (0, 0, 0))
    kern = functools.partial(_rglru_kernel, nb=nb, t=t, l_valid=l_valid)
    return pl.pallas_call(
        kern,
        grid=(nseq // nb, nc),
        in_specs=[
            pl.BlockSpec((nb, t, w), lambda b, c: (r0 + b, c, 2)),
            pl.BlockSpec((nb, t, w), lambda b, c: (r0 + b, c, 3)),
            full2(cw), full2(cb), full3(wa), full2(ba), full3(wx), full2(bx), full2(lam),
            pl.BlockSpec((nb, 1, w), lambda b, c: (b, 0, 0)),
            pl.BlockSpec((nb, SUBLANES, w), lambda b, c: (b, 0, 0)),
        ],
        out_specs=[
            pl.BlockSpec((nb, t, w), lambda b, c: (b, c, 0)),
            pl.BlockSpec((nb, 1, w), lambda b, c: (b, 0, 0)),
        ],
        out_shape=[
            jax.ShapeDtypeStruct((nseq, t * nc, w), BF16),
            jax.ShapeDtypeStruct((nseq, 1, w), F32),
        ],
        scratch_shapes=[pltpu.VMEM((nb, 1, w), F32), pltpu.VMEM((nb, SUBLANES + t, w), F32)],
        compiler_params=_cparams(("parallel", "arbitrary")),
        name="rglru",
    )(rest3, rest3, cw, cb, wa, ba, wx, bx, lam, h0, conv0)


def _gla_kernel(q_ref, k_ref, v_ref, al_ref, aup_ref, ab_ref, s0_ref, o_ref, s_ref, *, nb, t, nc, l_valid):
    c = pl.program_id(1)
    if nc == 1:
        s_in = s0_ref
    else:
        s_in = s_ref

        @pl.when(c == 0)
        def _():
            s_ref[...] = s0_ref[...]

    pos = c * t + lax.broadcasted_iota(jnp.int32, (t, 1), 0)
    valid = pos < l_valid
    row = lax.broadcasted_iota(jnp.int32, (t, t), 0)
    col = lax.broadcasted_iota(jnp.int32, (t, t), 1)
    causal = row >= col
    aup = aup_ref[...]
    ab = ab_ref[...]
    heads = [(h, slice(h * G_DK, (h + 1) * G_DK), slice(h * G_DV, (h + 1) * G_DV)) for h in range(N_HEADS)]

    def group(rows):
        pairs = [(i, r, h, ks, vs) for i, r in enumerate(rows) for h, ks, vs in heads]
        dec = []
        for r in rows:
            pre = _dot(al_ref[r], aup) + ab
            lg = jnp.where(valid, _log_sigmoid(pre) / G_TAU, 0.0)
            b = _cumsum_rows(lg)
            b_last = b[t - 1:t, :]
            k = jnp.where(valid, k_ref[r].astype(F32), 0.0)
            qd = ((q_ref[r].astype(F32) * (G_DK ** -0.5)) * jnp.exp(b)).astype(BF16)
            kn = (k * jnp.exp(-b)).astype(BF16)
            kr = (k * jnp.exp(b_last - b)).astype(BF16)
            dec.append((qd, kn, kr, _row_to_col(jnp.exp(b_last))))
        att = {}
        for i, r, h, ks, vs in pairs:
            qd, kn, _, _ = dec[i]
            att[i, h] = jnp.where(causal, _dot_nt(qd[:, ks], kn[:, ks]), 0.0).astype(BF16)
        for i, r, h, ks, vs in pairs:
            qd, _, kr, ebl_col = dec[i]
            s_h = s_in[r, h]
            v = v_ref[r, :, vs]
            o_ref[r, :, vs] = (_dot(qd[:, ks], s_h) + _dot(att[i, h], v)).astype(o_ref.dtype)
            s_ref[r, h] = ebl_col[ks, :] * s_h + _dot_tn(kr[:, ks], v)

    _group_loop(nb, group)


def _gla(qkv3, al3, row0, nseq, tiles, nc, l_valid, aup, ab, s0):
    nb, t = tiles
    assert nseq % nb == 0 and row0 % nb == 0
    r0 = row0 // nb
    full2 = lambda a: pl.BlockSpec(a.shape, lambda b, c: (0, 0))
    st = pl.BlockSpec((nb, N_HEADS, G_DK, G_DV), lambda b, c: (b, 0, 0, 0))
    kern = functools.partial(_gla_kernel, nb=nb, t=t, nc=nc, l_valid=l_valid)
    return pl.pallas_call(
        kern,
        grid=(nseq // nb, nc),
        in_specs=[
            pl.BlockSpec((nb, t, G_KW), lambda b, c: (r0 + b, c, 0)),
            pl.BlockSpec((nb, t, G_KW), lambda b, c: (r0 + b, c, 1)),
            pl.BlockSpec((nb, t, G_VW), lambda b, c: (r0 + b, c, 1)),
            pl.BlockSpec((nb, t, LANES), lambda b, c: (r0 + b, c, 0)),
            full2(aup), full2(ab), st,
        ],
        out_specs=[pl.BlockSpec((nb, t, G_VW), lambda b, c: (b, c, 0)), st],
        out_shape=[
            jax.ShapeDtypeStruct((nseq, t * nc, G_VW), BF16),
            jax.ShapeDtypeStruct((nseq, N_HEADS, G_DK, G_DV), F32),
        ],
        compiler_params=_cparams(("parallel", "arbitrary")),
        name="gla",
    )(qkv3, qkv3, qkv3, al3, aup, ab, s0)


def _even_layer_kernel(x_ref, pre_ref, post_ref, win_a_ref, win_b_ref, wgate_ref, gbias_ref, ng_ref, cw_ref, cb_ref, wa_ref,
                       ba_ref, wx_ref, bx_ref, lam_ref, w1_ref, w2_ref, c0_ref, n0_ref, m0_ref, h0_ref, conv0_ref,
                       y_ref, c_ref, n_ref, m_ref, hlast_ref, conv_ref,
                       qkv_s, rest_s, gate_s, hn_s, hm_s, hl_s, xkeep_s, hlru_s, xs_s, *, rows, tiles_per_seq):
    g = pl.program_id(0)
    slot = g % 2
    prev = 1 - slot
    t = MLSTM_TILES["main"][1]
    e_lo = 3 * M_WIDTH
    e_a = win_a_ref.shape[1]
    e = e_a + win_b_ref.shape[1]
    n_chunks = rows // t
    cols_per_chunk = e // n_chunks

    @pl.when(g == 0)
    def _():
        qkv_s[1] = jnp.zeros(qkv_s.shape[1:], qkv_s.dtype)
        rest_s[1] = jnp.zeros(rest_s.shape[1:], rest_s.dtype)
        gate_s[1] = jnp.zeros(gate_s.shape[1:], gate_s.dtype)
        xkeep_s[...] = jnp.zeros(xkeep_s.shape, xkeep_s.dtype)

    @pl.when(jnp.maximum(g - 1, 0) % tiles_per_seq == 0)
    def _():
        c_ref[...] = c0_ref[...]
        n_ref[...] = n0_ref[...]
        m_ref[...] = m0_ref[...]
        hlru_s[...] = h0_ref[...]
        xs_s[:, 0:SUBLANES, :] = conv0_ref[...]

    x = x_ref[0]
    hn_s[...] = (x * lax.rsqrt(jnp.mean(x * x, axis=-1, keepdims=True) + EPS) * pre_ref[...]).astype(BF16)
    gate_s[slot] = jnp.dot(hn_s[...], wgate_ref[...], preferred_element_type=F32)

    def project(c0, c1):
        w_cols = win_a_ref[:, c0:c1] if c1 <= e_a else win_b_ref[:, c0 - e_a:c1 - e_a]
        acc = jnp.dot(hn_s[...], w_cols, preferred_element_type=F32)
        if c0 < e_lo:
            hi = min(c1, e_lo)
            qkv_s[slot, :, c0:hi] = acc[:, :hi - c0].astype(BF16)
        if c1 > e_lo:
            lo = max(c0, e_lo)
            rest_s[slot, :, lo - e_lo:c1 - e_lo] = acc[:, lo - c0:]

    row = lax.broadcasted_iota(jnp.int32, (t, t), 0)
    col = lax.broadcasted_iota(jnp.int32, (t, t), 1)
    causal = row >= col
    lane = lax.broadcasted_iota(jnp.int32, (1, LANES), 1)
    gbias = gbias_ref[...]
    heads = [(h, slice(h * M_DK, (h + 1) * M_DK)) for h in range(N_HEADS)]
    piece = 512
    pieces = [(c0, c0 + piece) for c0 in range(0, e, piece)]
    per_chunk = 5
    assert e % piece == 0 and e_lo % piece == 0 and e_a % piece == 0 and len(pieces) >= n_chunks * per_chunk

    w = LRU_WIDTH
    lru_h = [hlru_s[...]]
    subs_per_chunk = rows // LRU_SUB // n_chunks

    def lru_sub(i):
        r0 = i * LRU_SUB

        def emit(gi, hg, gated):
            hl_s[r0 + gi * SUBLANES:r0 + (gi + 1) * SUBLANES, :] = gated[0].astype(hl_s.dtype)

        lru_h[0] = _lru_chunk(rest_s[prev, r0:r0 + LRU_SUB, 2 * w:3 * w][None],
                              rest_s[prev, r0:r0 + LRU_SUB, 3 * w:4 * w][None], xs_s, lru_h[0],
                              cw_ref[...], cb_ref[...], wa_ref, ba_ref[...], wx_ref, bx_ref[...], lam_ref[...], emit)

    n_gaps = 4

    def lru_between(j, gap):
        for i in range(gap * subs_per_chunk // n_gaps, (gap + 1) * subs_per_chunk // n_gaps):
            lru_sub(j * subs_per_chunk + i)

    for j in range(n_chunks):
        rs = slice(j * t, (j + 1) * t)
        todo = pieces[j * per_chunk:(j + 1) * per_chunk]
        gt = gate_s[prev, rs, :] + gbias
        li = gt[:, :LANES]
        b = _cumsum_rows(_log_sigmoid(gt[:, LANES:]))
        project(*todo[0])
        lru_between(j, 0)
        r_t = _transpose_rows(li - b)[:, :t]
        m_vec = m_ref[0]
        st = []
        for h, sl in heads:
            b_col = b[:, h:h + 1]
            inter = b_col + m_vec[:, h:h + 1]
            dmat = jnp.where(causal, b_col + r_t[h:h + 1, :], NEG)
            mt = jnp.maximum(inter, jnp.max(dmat, axis=-1, keepdims=True))
            k = qkv_s[prev, rs, M_WIDTH + h * M_DK:M_WIDTH + (h + 1) * M_DK] * (M_DK ** -0.5)
            s = _dot_nt(qkv_s[prev, rs, sl], k) * jnp.exp(dmat - mt)
            st.append((inter, mt, k, s))
        project(*todo[1])
        lru_between(j, 1)
        project(*todo[2])
        for h, sl in heads:
            inter, mt, k, s = st[h]
            q = qkv_s[prev, rs, sl]
            v = qkv_s[prev, rs, 2 * M_WIDTH + h * M_DV:2 * M_WIDTH + (h + 1) * M_DV]
            w_inter = jnp.exp(inter - mt)
            num = w_inter * _dot_nt(q, c_ref[0, h]) + _dot(s, v)
            den = (w_inter * jnp.sum(q.astype(F32) * n_ref[0, h], axis=-1, keepdims=True)
                   + jnp.sum(s, axis=-1, keepdims=True))
            hm_s[rs, sl] = (num / jnp.maximum(jnp.abs(den), jnp.exp(-mt))).astype(hm_s.dtype)
        project(*todo[3])
        lru_between(j, 2)
        m_next = m_vec
        for h, sl in heads:
            inter, mt, k, _ = st[h]
            v = qkv_s[prev, rs, 2 * M_WIDTH + h * M_DV:2 * M_WIDTH + (h + 1) * M_DV]
            b_col = b[:, h:h + 1]
            m_new = mt[t - 1:t, :]
            w_c = jnp.exp(inter[t - 1:t, :] - m_new)
            w_k = jnp.exp(b_col[t - 1:t, :] - b_col + li[:, h:h + 1] - m_new)
            c_ref[0, h] = w_c * c_ref[0, h] + _dot_tn(v.astype(F32) * w_k, k)
            n_ref[0, h] = w_c * n_ref[0, h] + jnp.sum(k.astype(F32) * w_k, axis=0, keepdims=True)
            m_next = jnp.where(lane == h, m_new, m_next)
        m_ref[0] = m_next
        lru_between(j, 3)
        project(*todo[4])

    for p in pieces[n_chunks * per_chunk:]:
        project(*p)
    h_end = lru_h[0]
    hlru_s[...] = h_end
    hlast_ref[...] = h_end
    conv_ref[...] = xs_s[:, 0:SUBLANES, :]

    hm = _sigmoid(rest_s[prev, :, 0:w]) * hm_s[...].astype(F32)
    hm = _head_rms(hm, N_HEADS) * ng_ref[...] * _silu(rest_s[prev, :, w:2 * w])
    out = jnp.dot(hm.astype(BF16), w1_ref[...], preferred_element_type=F32)
    out = out + jnp.dot(hl_s[...], w2_ref[...], preferred_element_type=F32)
    nrm = out * lax.rsqrt(jnp.mean(out * out, axis=-1, keepdims=True) + EPS) * post_ref[...]
    y_ref[0] = xkeep_s[...] + nrm
    xkeep_s[...] = x


def _even_layer_main(x3, pre, post, w_a, e_a, w_b, w_gate, gbias, ng, lru, w1, w2, c0, n0, m0, h0, conv0, rows):
    nseq, seq, d = x3.shape
    assert seq % rows == 0 and rows % MLSTM_TILES["main"][1] == 0
    tps = seq // rows
    n_tiles = nseq * tps
    e = e_a + w_b.shape[1]
    e_lo = 3 * M_WIDTH
    w = LRU_WIDTH
    const = lambda a: pl.BlockSpec(a.shape, lambda g: (0,) * a.ndim, pipeline_mode=pl.Buffered(1))
    cur = lambda g: jnp.minimum(g, n_tiles - 1)
    old = lambda g: jnp.maximum(g - 1, 0)
    per_seq = lambda *blk: pl.BlockSpec((1,) + blk, lambda g: (old(g) // tps,) + (0,) * len(blk))
    st_c, st_n, st_m = per_seq(N_HEADS, M_DV, M_DK), per_seq(N_HEADS, 1, M_DK), per_seq(1, LANES)
    st_h, st_conv = per_seq(1, w), per_seq(SUBLANES, w)
    out_states = [st_c, st_n, st_m, st_h, st_conv]
    if c0.shape[0] == nseq:
        in_states = out_states
    else:
        in_states = [pl.BlockSpec(s.block_shape, lambda g, nd=len(s.block_shape): (0,) * nd) for s in out_states]
    kern = functools.partial(_even_layer_kernel, rows=rows, tiles_per_seq=tps)
    return pl.pallas_call(
        kern,
        grid=(n_tiles + 1,),
        in_specs=[
            pl.BlockSpec((1, rows, d), lambda g: (cur(g) // tps, cur(g) % tps, 0)),
            const(pre), const(post),
            pl.BlockSpec((d, e_a), lambda g: (0, 0), pipeline_mode=pl.Buffered(1)), const(w_b),
            const(w_gate), const(gbias), const(ng),
            *[const(a) for a in lru], const(w1), const(w2), *in_states,
        ],
        out_specs=[
            pl.BlockSpec((1, rows, d), lambda g: (old(g) // tps, old(g) % tps, 0)),
            st_c, st_n, st_m, st_h, st_conv,
        ],
        out_shape=[
            jax.ShapeDtypeStruct((nseq, seq, d), F32),
            jax.ShapeDtypeStruct((nseq, N_HEADS, M_DV, M_DK), F32),
            jax.ShapeDtypeStruct((nseq, N_HEADS, 1, M_DK), F32),
            jax.ShapeDtypeStruct((nseq, 1, LANES), F32),
            jax.ShapeDtypeStruct((nseq, 1, w), F32),
            jax.ShapeDtypeStruct((nseq, SUBLANES, w), F32),
        ],
        scratch_shapes=[
            pltpu.VMEM((2, rows, e_lo), BF16),
            pltpu.VMEM((2, rows, e - e_lo), F32),
            pltpu.VMEM((2, rows, 2 * LANES), F32),
            pltpu.VMEM((rows, d), BF16),
            pltpu.VMEM((rows, M_WIDTH), BF16),
            pltpu.VMEM((rows, w), BF16),
            pltpu.VMEM((rows, d), F32),
            pltpu.VMEM((1, 1, w), F32),
            pltpu.VMEM((1, SUBLANES + LRU_SUB, w), F32),
        ],
        compiler_params=_cparams(("arbitrary",)),
        name="even_layer",
    )(x3, pre, post, w_a, w_b, w_gate, gbias, ng, *lru, w1, w2, c0, n0, m0, h0, conv0)


def _gla_layer_kernel(x_ref, pre_ref, post_ref, win_ref, wlow_ref, aup_ref, ab_ref, ng_ref, wout_ref, s0_ref,
                      y_ref, s_ref, qkv_s, r_s, al_s, hn_s, o_s, xkeep_s, *, rows, tiles_per_seq):
    g = pl.program_id(0)
    slot = g % 2
    prev = 1 - slot
    t = GLA_TILES["main"][1]
    e_lo = 2 * G_KW + G_VW
    e = e_lo + G_VW
    n_chunks = rows // t
    cols_per_chunk = e // n_chunks

    @pl.when(g == 0)
    def _():
        qkv_s[1] = jnp.zeros(qkv_s.shape[1:], qkv_s.dtype)
        r_s[1] = jnp.zeros(r_s.shape[1:], r_s.dtype)
        al_s[1] = jnp.zeros(al_s.shape[1:], al_s.dtype)
        xkeep_s[...] = jnp.zeros(xkeep_s.shape, xkeep_s.dtype)

    @pl.when(jnp.maximum(g - 1, 0) % tiles_per_seq == 0)
    def _():
        s_ref[...] = s0_ref[...]

    x = x_ref[0]
    hn_s[...] = (x * lax.rsqrt(jnp.mean(x * x, axis=-1, keepdims=True) + EPS) * pre_ref[...]).astype(BF16)
    al_s[slot] = jnp.dot(hn_s[...], wlow_ref[...], preferred_element_type=F32)

    row = lax.broadcasted_iota(jnp.int32, (t, t), 0)
    col = lax.broadcasted_iota(jnp.int32, (t, t), 1)
    causal = row >= col
    aup = aup_ref[...]
    ab = ab_ref[...]
    heads = [(h, slice(h * G_DK, (h + 1) * G_DK), slice(h * G_DV, (h + 1) * G_DV)) for h in range(N_HEADS)]

    def project(c0, c1):
        acc = jnp.dot(hn_s[...], win_ref[:, c0:c1], preferred_element_type=F32)
        if c0 < e_lo:
            hi = min(c1, e_lo)
            qkv_s[slot, :, c0:hi] = acc[:, :hi - c0].astype(BF16)
        if c1 > e_lo:
            lo = max(c0, e_lo)
            r_s[slot, :, lo - e_lo:c1 - e_lo] = acc[:, lo - c0:]

    n_sub = 3
    sub = cols_per_chunk // n_sub
    for j in range(n_chunks):
        c0 = j * cols_per_chunk
        rs = slice(j * t, (j + 1) * t)
        pre = _dot(al_s[prev, rs, :], aup) + ab
        project(c0, c0 + sub)
        b = _cumsum_rows(_log_sigmoid(pre) / G_TAU)
        b_last = b[t - 1:t, :]
        project(c0 + sub, c0 + 2 * sub)
        k = qkv_s[prev, rs, G_KW:2 * G_KW].astype(F32)
        qd = ((qkv_s[prev, rs, 0:G_KW].astype(F32) * (G_DK ** -0.5)) * jnp.exp(b)).astype(BF16)
        kn = (k * jnp.exp(-b)).astype(BF16)
        kr = (k * jnp.exp(b_last - b)).astype(BF16)
        ebl_col = _row_to_col(jnp.exp(b_last))
        att = [jnp.where(causal, _dot_nt(qd[:, ks], kn[:, ks]), 0.0).astype(BF16) for _, ks, _ in heads]
        project(c0 + 2 * sub, c0 + cols_per_chunk)
        for h, ks, vs in heads:
            s_h = s_ref[0, h]
            v = qkv_s[prev, rs, 2 * G_KW + h * G_DV:2 * G_KW + (h + 1) * G_DV]
            o_s[rs, vs] = (_dot(qd[:, ks], s_h) + _dot(att[h], v)).astype(o_s.dtype)
            s_ref[0, h] = ebl_col[ks, :] * s_h + _dot_tn(kr[:, ks], v)

    o = _head_rms(o_s[...].astype(F32), N_HEADS) * ng_ref[...] * _silu(r_s[prev])
    out = jnp.dot(o.astype(BF16), wout_ref[...], preferred_element_type=F32)
    nrm = out * lax.rsqrt(jnp.mean(out * out, axis=-1, keepdims=True) + EPS) * post_ref[...]
    y_ref[0] = xkeep_s[...] + nrm
    xkeep_s[...] = x


def _gla_layer(x3, pre, post, w_in, w_low, aup, ab, ng, w_out, s0, rows):
    nseq, seq, d = x3.shape
    assert seq % rows == 0 and rows % GLA_TILES["main"][1] == 0
    tps = seq // rows
    n_tiles = nseq * tps
    e_lo = 2 * G_KW + G_VW
    e = e_lo + G_VW
    const = lambda a: pl.BlockSpec(a.shape, lambda g: (0,) * a.ndim, pipeline_mode=pl.Buffered(1))
    cur = lambda g: jnp.minimum(g, n_tiles - 1)
    old = lambda g: jnp.maximum(g - 1, 0)
    st = pl.BlockSpec((1, N_HEADS, G_DK, G_DV), lambda g: (old(g) // tps, 0, 0, 0))
    st_in = st if s0.shape[0] == nseq else pl.BlockSpec((1, N_HEADS, G_DK, G_DV), lambda g: (0, 0, 0, 0))
    kern = functools.partial(_gla_layer_kernel, rows=rows, tiles_per_seq=tps)
    return pl.pallas_call(
        kern,
        grid=(n_tiles + 1,),
        in_specs=[
            pl.BlockSpec((1, rows, d), lambda g: (cur(g) // tps, cur(g) % tps, 0)),
            const(pre), const(post), pl.BlockSpec((d, e), lambda g: (0, 0), pipeline_mode=pl.Buffered(1)),
            const(w_low), const(aup), const(ab), const(ng), const(w_out), st_in,
        ],
        out_specs=[pl.BlockSpec((1, rows, d), lambda g: (old(g) // tps, old(g) % tps, 0)), st],
        out_shape=[
            jax.ShapeDtypeStruct((nseq, seq, d), F32),
            jax.ShapeDtypeStruct((nseq, N_HEADS, G_DK, G_DV), F32),
        ],
        scratch_shapes=[
            pltpu.VMEM((2, rows, e_lo), BF16),
            pltpu.VMEM((2, rows, e - e_lo), F32),
            pltpu.VMEM((2, rows, LANES), F32),
            pltpu.VMEM((rows, d), BF16),
            pltpu.VMEM((rows, G_VW), BF16),
            pltpu.VMEM((rows, d), F32),
        ],
        compiler_params=_cparams(("arbitrary",)),
        name="gla_layer",
    )(x3, pre, post, w_in, w_low, aup, ab, ng, w_out, s0)


def _pad_lanes(a, width):
    return jnp.pad(a, [(0, 0)] * (a.ndim - 1) + [(0, width - a.shape[-1])])


def _even_layer(xm, xs, nbp, seq, nbs, ls, st, pre_g, post_g, w_in, w_out, b_i, b_f, m_norm_g,
                conv_w, conv_b, wa, ba, wx, bx, lam):
    c0s, n0s, m0s, h0s, conv0s = st
    e_a = 4 * M_WIDTH
    n_gate = 2 * N_HEADS
    w_bf = w_in.astype(BF16)
    w_b = w_bf[:, e_a + n_gate:]
    wig, wfg = w_in[:, e_a:e_a + N_HEADS], w_in[:, e_a + N_HEADS:e_a + n_gate]
    w_gate = jnp.concatenate([_pad_lanes(wig, LANES), _pad_lanes(wfg, LANES)], axis=1).astype(BF16)
    gbias = jnp.concatenate([_pad_lanes(b_i[None], LANES), _pad_lanes(b_f[None], LANES)], axis=1)
    ng = m_norm_g[None]
    post = post_g[None]
    cw, cb = conv_w, conv_b[None]
    wab, wxb = wa.astype(BF16), wx.astype(BF16)
    bav, bxv, lamv = ba[None], bx[None], lam[None]
    w_o1, w_o2 = w_out[:M_WIDTH].astype(BF16), w_out[M_WIDTH:].astype(BF16)
    e_lo = 3 * M_WIDTH
    e_hi = e_a + w_b.shape[1] - e_lo
    lru = (cw, cb, wab, bav, wxb, bxv, lamv)

    qs, ps, gs = _norm_proj(xs, pre_g, w_bf, e_a, w_b, w_gate, e_lo, xs.shape[0])
    mrow = nbs * SAMPLE_PAD // N_META
    xr_cols = slice(2 * LRU_WIDTH, 3 * LRU_WIDTH)

    zc = jnp.zeros((1, N_HEADS, M_DV, M_DK), F32)
    zn = jnp.zeros((1, N_HEADS, 1, M_DK), F32)
    zm = jnp.zeros((1, 1, LANES), F32)
    zh = jnp.zeros((1, 1, LRU_WIDTH), F32)
    zconv = jnp.zeros((1, SUBLANES, LRU_WIDTH), F32)
    qs_meta = qs.reshape(-1, N_META, e_lo)
    ps_meta = ps.reshape(-1, N_META, e_hi)
    gs_meta = gs.reshape(-1, N_META, 2 * LANES)
    h_meta, c1, n1, m1 = _mlstm(qs_meta, gs_meta, mrow, 1, MLSTM_TILES["meta"], 1, N_META, gbias, zc, zn, zm)
    hl_meta, h1 = _rglru(ps_meta, mrow, 1, LRU_TILES["meta"], 1, N_META, *lru, zh, zconv)
    conv1 = ps_meta[mrow:mrow + 1, N_META - SUBLANES:, xr_cols]

    ym3, pc, pn, pmm, ph, conv_tail = _even_layer_main(
        xm.reshape(nbp, seq, -1), pre_g[None], post, w_bf, e_a, w_b, w_gate, gbias, ng, lru, w_o1, w_o2,
        c1, n1, m1, h1, conv1, FUSED_ROW_TILE)
    ym = ym3.reshape(xm.shape)
    pconv = conv_tail[:, SUBLANES - (CONV_W - 1):, :]

    qs_s = qs.reshape(-1, SAMPLE_PAD, e_lo)
    ps_s = ps.reshape(-1, SAMPLE_PAD, e_hi)
    gs_s = gs.reshape(-1, SAMPLE_PAD, 2 * LANES)
    h_s, sc, sn, sm = _mlstm(qs_s, gs_s, 0, nbs, MLSTM_TILES["sample"], 1, ls, gbias,
                             c0s, n0s[:, :, None, :], _pad_lanes(m0s, LANES)[:, None, :])
    conv0p = jnp.pad(conv0s, ((0, 0), (SUBLANES - (CONV_W - 1), 0), (0, 0)))
    hl_s, sh = _rglru(ps_s, 0, nbs, LRU_TILES["sample"], 1, ls, *lru, h0s[:, None, :], conv0p)
    xr_s = ps_s[:nbs, :ls, xr_cols]
    sconv = jnp.concatenate([conv0s, xr_s], axis=1)[:, -(CONV_W - 1):]

    h_small = jnp.concatenate([h_s.reshape(-1, M_WIDTH), h_meta.reshape(-1, M_WIDTH)], axis=0)
    hl_small = jnp.concatenate([hl_s.reshape(-1, LRU_WIDTH), hl_meta.reshape(-1, LRU_WIDTH)], axis=0)
    ys = _out_even(xs, post, h_small, ps, ng, hl_small, w_o1, w_o2, _row_tile(xs.shape[0], OUT_ROW_TILE))

    p_state = (pc, pn[:, :, 0, :], pmm[:, 0, :N_HEADS], ph[:, 0, :], pconv)
    s_state = (sc, sn[:, :, 0, :], sm[:, 0, :N_HEADS], sh[:, 0, :], sconv)
    return ym, ys, p_state, s_state


def _odd_layer(xm, xs, nbp, seq, nbs, ls, s0s, pre_g, post_g, w_in, w_out, a_up, a_b, g_norm_g):
    e_lo = 2 * G_KW + G_VW
    e_a = 2 * G_KW + 2 * G_VW
    w_bf = w_in.astype(BF16)
    w_low = _pad_lanes(w_in[:, 2 * G_KW + 2 * G_VW:], LANES).astype(BF16)
    aup = jnp.pad(a_up, ((0, LANES - G_RANK), (0, 0))).astype(BF16)
    ab = a_b[None]
    ng = g_norm_g[None]
    post = post_g[None]
    w_o = w_out.astype(BF16)

    qs, rs, as_ = _norm_proj(xs, pre_g, w_bf, e_a, None, w_low, e_lo, xs.shape[0])
    mrow = nbs * SAMPLE_PAD // N_META

    zs = jnp.zeros((1, N_HEADS, G_DK, G_DV), F32)
    o_meta, s1 = _gla(qs.reshape(-1, N_META, e_lo), as_.reshape(-1, N_META, LANES),
                      mrow, 1, GLA_TILES["meta"], 1, N_META, aup, ab, zs)
    ym3, p_s = _gla_layer(xm.reshape(nbp, seq, -1), pre_g[None], post, w_bf, w_low, aup, ab, ng, w_o, s1,
                          GLA_ROW_TILE)
    ym = ym3.reshape(xm.shape)
    o_s, s_s = _gla(qs.reshape(-1, SAMPLE_PAD, e_lo), as_.reshape(-1, SAMPLE_PAD, LANES),
                    0, nbs, GLA_TILES["sample"], 1, ls, aup, ab, s0s)

    o_small = jnp.concatenate([o_s.reshape(-1, G_VW), o_meta.reshape(-1, G_VW)], axis=0)
    ys = _out_odd(xs, post, o_small, rs, ng, w_o, _row_tile(xs.shape[0], OUT_ROW_TILE))
    return ym, ys, p_s, s_s


def kernel(x_prompt, x_sample, state_mlstm_C, state_mlstm_n, state_mlstm_m, state_rglru_h, state_rglru_conv,
           state_gla_S, meta_tokens, pre_norm_a, post_norm_a, w_in_a, w_out_a, mlstm_b_i, mlstm_b_f, mlstm_norm,
           conv_w, conv_b, lru_w_a, lru_b_a, lru_w_x, lru_b_x, lru_lambda, pre_norm_c, post_norm_c, w_in_c,
           w_out_c, gla_alpha_up, gla_alpha_b, gla_norm):
    nbp, seq, d = x_prompt.shape
    nbs, ls, _ = x_sample.shape
    depth = pre_norm_a.shape[0] + pre_norm_c.shape[0]
    assert ls >= CONV_W - 1 and ls <= SAMPLE_PAD and N_META % SUBLANES == 0 and nbs * SAMPLE_PAD % N_META == 0

    xm = x_prompt.reshape(nbp * seq, d)
    xs_pad = jnp.pad(x_sample, ((0, 0), (0, SAMPLE_PAD - ls), (0, 0))).reshape(nbs * SAMPLE_PAD, d)
    xs = jnp.concatenate([xs_pad, meta_tokens.astype(x_prompt.dtype)], axis=0)

    p_lists = [[] for _ in range(6)]
    s_lists = [[] for _ in range(6)]
    for layer in range(depth):
        j = layer // 2
        if layer % 2 == 0:
            st = (state_mlstm_C[j], state_mlstm_n[j], state_mlstm_m[j], state_rglru_h[j], state_rglru_conv[j])
            xm, xs, pst, sst = _even_layer(
                xm, xs, nbp, seq, nbs, ls, st, pre_norm_a[j], post_norm_a[j], w_in_a[j], w_out_a[j],
                mlstm_b_i[j], mlstm_b_f[j], mlstm_norm[j], conv_w[j], conv_b[j], lru_w_a[j], lru_b_a[j],
                lru_w_x[j], lru_b_x[j], lru_lambda[j])
            for i in range(5):
                p_lists[i].append(pst[i])
                s_lists[i].append(sst[i])
        else:
            xm, xs, p_s, s_s = _odd_layer(
                xm, xs, nbp, seq, nbs, ls, state_gla_S[j], pre_norm_c[j], post_norm_c[j], w_in_c[j], w_out_c[j],
                gla_alpha_up[j], gla_alpha_b[j], gla_norm[j])
            p_lists[5].append(p_s)
            s_lists[5].append(s_s)

    y_prompt = xm.reshape(nbp, seq, d)
    y_sample = xs[:nbs * SAMPLE_PAD].reshape(nbs, SAMPLE_PAD, d)[:, :ls]
    return (y_prompt, y_sample) + tuple(jnp.stack(l) for l in p_lists) + tuple(jnp.stack(l) for l in s_lists)
```

```python
import functools

import jax
import jax.numpy as jnp
from jax import lax
from jax.experimental import pallas as pl
from jax.experimental.pallas import tpu as pltpu

F32 = jnp.float32
BF16 = jnp.bfloat16

D_MODEL = 1024
N_META = 16
EPS = 1e-6
N_HEADS = 4
M_DK = 256
M_DV = 256
M_WIDTH = N_HEADS * M_DV
LRU_WIDTH = 1024
LRU_BLOCKS = 8
LRU_BS = LRU_WIDTH // LRU_BLOCKS
CONV_W = 4
LRU_C = 8.0
G_DK = 256
G_DV = 512
G_KW = N_HEADS * G_DK
G_VW = N_HEADS * G_DV
G_RANK = 16
G_TAU = 16.0

LANES = 128
SUBLANES = 8
NEG = -1e30
SAMPLE_PAD = SUBLANES
VMEM_LIMIT = 56 * 1024 * 1024
COL_TILE = 1024

MLSTM_TILES = {"meta": (1, N_META), "main": (1, 128), "sample": (8, SAMPLE_PAD)}
GLA_TILES = {"meta": (1, N_META), "main": (1, 64), "sample": (4, SAMPLE_PAD)}
LRU_TILES = {"meta": (1, N_META), "sample": (32, SAMPLE_PAD)}
OUT_ROW_TILE = 512
FUSED_ROW_TILE = 256
GLA_ROW_TILE = 256
LRU_SUB = 64


def _row_tile(n, cap):
    return next(tm for tm in range(min(n, cap) // SUBLANES * SUBLANES, 0, -SUBLANES) if n % tm == 0)


def _cparams(sem):
    return pltpu.CompilerParams(dimension_semantics=sem, vmem_limit_bytes=VMEM_LIMIT)


def _sigmoid(x):
    return 0.5 * jnp.tanh(0.5 * x) + 0.5


def _silu(x):
    return x * _sigmoid(x)


def _log_sigmoid(x):
    return jnp.minimum(x, 0.0) - jnp.log1p(jnp.exp(-jnp.abs(x)))


def _softplus(x):
    return jnp.maximum(x, 0.0) + jnp.log1p(jnp.exp(-jnp.abs(x)))


def _dot(a, b):
    return jnp.dot(a.astype(BF16), b.astype(BF16), preferred_element_type=F32)


def _dot_nt(a, b):
    return lax.dot_general(a.astype(BF16), b.astype(BF16), (((1,), (1,)), ((), ())),
                           preferred_element_type=F32)


def _dot_tn(a, b):
    return lax.dot_general(a.astype(BF16), b.astype(BF16), (((0,), (0,)), ((), ())),
                           preferred_element_type=F32)


def _transpose_rows(x):
    t, n = x.shape
    if t < LANES:
        x = jnp.concatenate([x, jnp.zeros((LANES - t, n), x.dtype)], axis=0)
    return jnp.transpose(x)


def _row_to_col(r):
    return jnp.transpose(jnp.broadcast_to(r, (LANES, r.shape[1])))[:, 0:1]


def _cumsum_rows(x):
    t = x.shape[0]
    if t <= 2 * SUBLANES:
        rid = lax.broadcasted_iota(jnp.int32, (t, 1), 0)
        s = 1
        while s < t:
            x = x + jnp.where(rid >= s, pltpu.roll(x, s, axis=0), 0.0)
            s *= 2
        return x
    row = lax.broadcasted_iota(jnp.int32, (t, t), 0)
    col = lax.broadcasted_iota(jnp.int32, (t, t), 1)
    tri = jnp.where(row >= col, 1.0, 0.0).astype(BF16)
    hi = x.astype(BF16)
    r1 = x - hi.astype(F32)
    mid = r1.astype(BF16)
    lo = (r1 - mid.astype(F32)).astype(BF16)
    acc = jnp.dot(tri, lo, preferred_element_type=F32)
    acc = acc + jnp.dot(tri, mid, preferred_element_type=F32)
    return acc + jnp.dot(tri, hi, preferred_element_type=F32)


def _head_rms(x, nh):
    hd = x.shape[1] // nh
    parts = []
    for h in range(nh):
        xh = x[:, h * hd:(h + 1) * hd]
        parts.append(xh * lax.rsqrt(jnp.mean(xh * xh, axis=-1, keepdims=True) + EPS))
    return jnp.concatenate(parts, axis=-1)


GROUP = 2


def _group_loop(nb, group_fn):
    if nb <= GROUP:
        group_fn(list(range(nb)))
        return

    def body(i, carry):
        group_fn([GROUP * i + r for r in range(GROUP)])
        return carry

    lax.fori_loop(0, nb // GROUP, body, 0)


def _norm_proj_kernel(n_lo, n_a, x_ref, g_ref, wa_ref, wb_ref, ws_ref, olo_ref, ohi_ref, os_ref, hn_ref):
    j = pl.program_id(1)

    @pl.when(j == 0)
    def _():
        x = x_ref[...]
        y = x * lax.rsqrt(jnp.mean(x * x, axis=-1, keepdims=True) + EPS) * g_ref[...]
        hn = y.astype(BF16)
        hn_ref[...] = hn
        os_ref[...] = jnp.dot(hn, ws_ref[...], preferred_element_type=F32)

    @pl.when(j < n_lo)
    def _():
        olo_ref[...] = jnp.dot(hn_ref[...], wa_ref[...], preferred_element_type=F32).astype(olo_ref.dtype)

    @pl.when((j >= n_lo) & (j < n_a))
    def _():
        ohi_ref[...] = jnp.dot(hn_ref[...], wa_ref[...], preferred_element_type=F32)

    @pl.when(j >= n_a)
    def _():
        ohi_ref[...] = jnp.dot(hn_ref[...], wb_ref[...], preferred_element_type=F32)


def _norm_proj(x2d, g, w_a, e_a, w_b, w_small, e_lo, tm):
    n, d = x2d.shape
    tn = COL_TILE
    if w_b is None:
        w_b, e = w_a, e_a
    else:
        e = e_a + w_b.shape[1]
    es = w_small.shape[1]
    n_lo, n_a = e_lo // tn, e_a // tn
    assert e_lo <= e_a and e_a % tn == 0 and e % tn == 0
    return pl.pallas_call(
        functools.partial(_norm_proj_kernel, n_lo, n_a),
        grid=(n // tm, e // tn),
        in_specs=[
            pl.BlockSpec((tm, d), lambda i, j: (i, 0)),
            pl.BlockSpec((1, d), lambda i, j: (0, 0)),
            pl.BlockSpec((d, tn), lambda i, j: (0, jnp.minimum(j, n_a - 1))),
            pl.BlockSpec((d, tn), lambda i, j: (0, jnp.maximum(j - n_a, 0))),
            pl.BlockSpec((d, es), lambda i, j: (0, 0)),
        ],
        out_specs=[
            pl.BlockSpec((tm, tn), lambda i, j: (i, jnp.minimum(j, n_lo - 1))),
            pl.BlockSpec((tm, tn), lambda i, j: (i, jnp.maximum(j - n_lo, 0))),
            pl.BlockSpec((tm, es), lambda i, j: (i, 0)),
        ],
        out_shape=[
            jax.ShapeDtypeStruct((n, e_lo), BF16),
            jax.ShapeDtypeStruct((n, e - e_lo), F32),
            jax.ShapeDtypeStruct((n, es), F32),
        ],
        scratch_shapes=[pltpu.VMEM((tm, d), BF16)],
        compiler_params=_cparams(("parallel", "arbitrary")),
        name="norm_proj",
    )(x2d, g.reshape(1, d), w_a, w_b, w_small)


def _residual_norm(x_ref, g_ref, y_ref, out):
    nrm = out * lax.rsqrt(jnp.mean(out * out, axis=-1, keepdims=True) + EPS) * g_ref[...]
    y_ref[...] = x_ref[...] + nrm


def _out_even_kernel(x_ref, g_ref, h_ref, og_ref, z_ref, ng_ref, hl_ref, w1_ref, w2_ref, y_ref):
    hm = _sigmoid(og_ref[...]) * h_ref[...].astype(F32)
    hm = _head_rms(hm, N_HEADS) * ng_ref[...] * _silu(z_ref[...])
    out = jnp.dot(hm.astype(BF16), w1_ref[...], preferred_element_type=F32)
    out = out + jnp.dot(hl_ref[...], w2_ref[...], preferred_element_type=F32)
    _residual_norm(x_ref, g_ref, y_ref, out)


def _out_even(x2d, g, h_raw, rest, ng, hl, w1, w2, tm):
    n, d = x2d.shape
    row = lambda width, col: pl.BlockSpec((tm, width), lambda i: (i, col))
    const = lambda a: pl.BlockSpec(a.shape, lambda i: (0, 0))
    return pl.pallas_call(
        _out_even_kernel,
        grid=(n // tm,),
        in_specs=[row(d, 0), const(g), row(M_WIDTH, 0), row(M_WIDTH, 0), row(M_WIDTH, 1), const(ng),
                  row(LRU_WIDTH, 0), const(w1), const(w2)],
        out_specs=row(d, 0),
        out_shape=jax.ShapeDtypeStruct((n, d), F32),
        compiler_params=_cparams(("parallel",)),
        name="out_even",
    )(x2d, g, h_raw, rest, rest, ng, hl, w1, w2)


def _out_odd_kernel(x_ref, g_ref, o_ref, r_ref, ng_ref, w_ref, y_ref):
    o = _head_rms(o_ref[...].astype(F32), N_HEADS) * ng_ref[...] * _silu(r_ref[...])
    out = jnp.dot(o.astype(BF16), w_ref[...], preferred_element_type=F32)
    _residual_norm(x_ref, g_ref, y_ref, out)


def _out_odd(x2d, g, o_raw, r, ng, w, tm):
    n, d = x2d.shape
    row = lambda width: pl.BlockSpec((tm, width), lambda i: (i, 0))
    const = lambda a: pl.BlockSpec(a.shape, lambda i: (0, 0))
    return pl.pallas_call(
        _out_odd_kernel,
        grid=(n // tm,),
        in_specs=[row(d), const(g), row(G_VW), row(G_VW), const(ng), const(w)],
        out_specs=row(d),
        out_shape=jax.ShapeDtypeStruct((n, d), F32),
        compiler_params=_cparams(("parallel",)),
        name="out_odd",
    )(x2d, g, o_raw, r, ng, w)


def _mlstm_kernel(q_ref, k_ref, v_ref, gate_ref, gbias_ref, c0_ref, n0_ref, m0_ref,
                  h_ref, c_ref, n_ref, m_ref, *, nb, t, nc, l_valid):
    c = pl.program_id(1)
    if nc == 1:
        c_in, n_in, m_in = c0_ref, n0_ref, m0_ref
    else:
        c_in, n_in, m_in = c_ref, n_ref, m_ref

        @pl.when(c == 0)
        def _():
            c_ref[...] = c0_ref[...]
            n_ref[...] = n0_ref[...]
            m_ref[...] = m0_ref[...]

    pos = c * t + lax.broadcasted_iota(jnp.int32, (t, 1), 0)
    valid = pos < l_valid
    row = lax.broadcasted_iota(jnp.int32, (t, t), 0)
    col = lax.broadcasted_iota(jnp.int32, (t, t), 1)
    causal = row >= col
    lane = lax.broadcasted_iota(jnp.int32, (1, LANES), 1)
    gbias = gbias_ref[...]
    heads = [(h, slice(h * M_DK, (h + 1) * M_DK)) for h in range(N_HEADS)]

    def group(rows):
        pairs = [(i, r, h, sl) for i, r in enumerate(rows) for h, sl in heads]
        gates = []
        for r in rows:
            g = gate_ref[r] + gbias
            li = jnp.where(valid, g[:, :LANES], NEG)
            lf = jnp.where(valid, _log_sigmoid(g[:, LANES:]), 0.0)
            b = _cumsum_rows(lf)
            gates.append((li, b, _transpose_rows(li - b)[:, :t], m_in[r]))
        st = {}
        for i, r, h, sl in pairs:
            li, b, r_t, m_vec = gates[i]
            b_col = b[:, h:h + 1]
            inter = b_col + m_vec[:, h:h + 1]
            dmat = jnp.where(causal, b_col + r_t[h:h + 1, :], NEG)
            mt = jnp.maximum(inter, jnp.max(dmat, axis=-1, keepdims=True))
            k = k_ref[r, :, sl] * (M_DK ** -0.5)
            s = _dot_nt(q_ref[r, :, sl], k) * jnp.exp(dmat - mt)
            st[i, h] = (inter, mt, k, s)
        for i, r, h, sl in pairs:
            inter, mt, k, s = st[i, h]
            q = q_ref[r, :, sl]
            w_inter = jnp.exp(inter - mt)
            num = w_inter * _dot_nt(q, c_in[r, h]) + _dot(s, v_ref[r, :, sl])
            den = (w_inter * jnp.sum(q.astype(F32) * n_in[r, h], axis=-1, keepdims=True)
                   + jnp.sum(s, axis=-1, keepdims=True))
            h_ref[r, :, sl] = (num / jnp.maximum(jnp.abs(den), jnp.exp(-mt))).astype(h_ref.dtype)
        m_next = [gt[3] for gt in gates]
        for i, r, h, sl in pairs:
            li, b, _, _ = gates[i]
            inter, mt, k, _ = st[i, h]
            b_col = b[:, h:h + 1]
            m_new = mt[t - 1:t, :]
            w_c = jnp.exp(inter[t - 1:t, :] - m_new)
            w_k = jnp.exp(b_col[t - 1:t, :] - b_col + li[:, h:h + 1] - m_new)
            c_ref[r, h] = w_c * c_in[r, h] + _dot_tn(v_ref[r, :, sl].astype(F32) * w_k, k)
            n_ref[r, h] = w_c * n_in[r, h] + jnp.sum(k.astype(F32) * w_k, axis=0, keepdims=True)
            m_next[i] = jnp.where(lane == h, m_new, m_next[i])
        for i, r in enumerate(rows):
            m_ref[r] = m_next[i]

    _group_loop(nb, group)


def _mlstm(qkv3, gates3, row0, nseq, tiles, nc, l_valid, gbias, c0, n0, m0):
    nb, t = tiles
    assert nseq % nb == 0 and row0 % nb == 0
    wd = M_WIDTH
    r0 = row0 // nb
    seq = lambda col: pl.BlockSpec((nb, t, wd), lambda b, c: (r0 + b, c, col))
    st_c = pl.BlockSpec((nb, N_HEADS, M_DV, M_DK), lambda b, c: (b, 0, 0, 0))
    st_n = pl.BlockSpec((nb, N_HEADS, 1, M_DK), lambda b, c: (b, 0, 0, 0))
    st_m = pl.BlockSpec((nb, 1, LANES), lambda b, c: (b, 0, 0))
    kern = functools.partial(_mlstm_kernel, nb=nb, t=t, nc=nc, l_valid=l_valid)
    return pl.pallas_call(
        kern,
        grid=(nseq // nb, nc),
        in_specs=[
            seq(0), seq(1), seq(2),
            pl.BlockSpec((nb, t, 2 * LANES), lambda b, c: (r0 + b, c, 0)),
            pl.BlockSpec(gbias.shape, lambda b, c: (0, 0)), st_c, st_n, st_m,
        ],
        out_specs=[pl.BlockSpec((nb, t, wd), lambda b, c: (b, c, 0)), st_c, st_n, st_m],
        out_shape=[
            jax.ShapeDtypeStruct((nseq, t * nc, wd), BF16),
            jax.ShapeDtypeStruct((nseq, N_HEADS, M_DV, M_DK), F32),
            jax.ShapeDtypeStruct((nseq, N_HEADS, 1, M_DK), F32),
            jax.ShapeDtypeStruct((nseq, 1, LANES), F32),
        ],
        compiler_params=_cparams(("parallel", "arbitrary")),
        name="mlstm",
    )(qkv3, qkv3, qkv3, gates3, gbias, c0, n0, m0)


def _lru_chunk(x, gr, xs_s, h, cw, cb, wa_ref, ba, wx_ref, bx, lam, emit, spread=lambda: None):
    nb, t, w = x.shape
    ng = t // SUBLANES
    xs_s[:, SUBLANES:SUBLANES + t, :] = x
    xc = cb + cw[CONV_W - 1:CONV_W, :] * x
    for s in range(1, CONV_W):
        xc = xc + cw[CONV_W - 1 - s:CONV_W - s, :] * xs_s[:, SUBLANES - s:SUBLANES - s + t, :]
    xs_s[:, 0:SUBLANES, :] = x[:, t - SUBLANES:, :]
    spread()

    xf = xc.reshape(nb * t, w)
    ra, ri = [], []
    for n in range(LRU_BLOCKS):
        xb = xf[:, n * LRU_BS:(n + 1) * LRU_BS].astype(BF16)
        ra.append(jnp.dot(xb, wa_ref[n], preferred_element_type=F32))
        ri.append(jnp.dot(xb, wx_ref[n], preferred_element_type=F32))
    spread()
    r = _sigmoid(jnp.concatenate(ra, axis=-1) + ba)
    i = _sigmoid(jnp.concatenate(ri, axis=-1) + bx)
    log_a = -LRU_C * r * _softplus(-lam)
    a = jnp.exp(log_a)
    u = jnp.sqrt(-jnp.tanh(log_a) * (1.0 + a * a)) * (i * xf)
    spread()

    rid = lax.broadcasted_iota(jnp.int32, (1, SUBLANES, 1), 1)
    a = a.reshape(nb * ng, SUBLANES, w)
    u = u.reshape(nb * ng, SUBLANES, w)
    for s in (1, 2, 4):
        a_sh = jnp.where(rid >= s, pltpu.roll(a, s, axis=1), 1.0)
        u_sh = jnp.where(rid >= s, pltpu.roll(u, s, axis=1), 0.0)
        u = a * u_sh + u
        a = a * a_sh
    a = a.reshape(nb, ng, SUBLANES, w)
    u = u.reshape(nb, ng, SUBLANES, w)
    gr4 = gr.reshape(nb, ng, SUBLANES, w)
    for g in range(ng):
        if g == ng // 2:
            spread()
        hg = a[:, g] * h + u[:, g]
        h = hg[:, SUBLANES - 1:SUBLANES]
        emit(g, hg, hg * _silu(gr4[:, g]))
    return h


def _rglru_kernel(x_ref, gr_ref, cw_ref, cb_ref, wa_ref, ba_ref, wx_ref, bx_ref, lam_ref, h0_ref, conv0_ref,
                  hl_ref, hlast_ref, h_s, xs_s, *, nb, t, l_valid):
    c = pl.program_id(1)

    @pl.when(c == 0)
    def _():
        h_s[...] = h0_ref[...]
        xs_s[:, 0:SUBLANES, :] = conv0_ref[...]

    g_last, r_last = divmod((l_valid - 1) % t, SUBLANES)

    def emit(g, hg, gated):
        hl_ref[:, g * SUBLANES:(g + 1) * SUBLANES, :] = gated.astype(hl_ref.dtype)
        if g == g_last:
            @pl.when(c == (l_valid - 1) // t)
            def _():
                hlast_ref[...] = hg[:, r_last:r_last + 1]

    h_s[...] = _lru_chunk(x_ref[...], gr_ref[...], xs_s, h_s[...], cw_ref[...], cb_ref[...], wa_ref, ba_ref[...],
                          wx_ref, bx_ref[...], lam_ref[...], emit)


def _rglru(rest3, row0, nseq, tiles, nc, l_valid, cw, cb, wa, ba, wx, bx, lam, h0, conv0):
    nb, t = tiles
    assert nseq % nb == 0 and row0 % nb == 0
    w = LRU_WIDTH
    r0 = row0 // nb
    full2 = lambda a: pl.BlockSpec(a.shape, lambda b, c: (0, 0))
    full3 = lambda a: pl.BlockSpec(a.shape, lambda b, c: (0, 0, 0))
    kern = functools.partial(_rglru_kernel, nb=nb, t=t, l_valid=l_valid)
    return pl.pallas_call(
        kern,
        grid=(nseq // nb, nc),
        in_specs=[
            pl.BlockSpec((nb, t, w), lambda b, c: (r0 + b, c, 2)),
            pl.BlockSpec((nb, t, w), lambda b, c: (r0 + b, c, 3)),
            full2(cw), full2(cb), full3(wa), full2(ba), full3(wx), full2(bx), full2(lam),
            pl.BlockSpec((nb, 1, w), lambda b, c: (b, 0, 0)),
            pl.BlockSpec((nb, SUBLANES, w), lambda b, c: (b, 0, 0)),
        ],
        out_specs=[
            pl.BlockSpec((nb, t, w), lambda b, c: (b, c, 0)),
            pl.BlockSpec((nb, 1, w), lambda b, c: (b, 0, 0)),
        ],
        out_shape=[
            jax.ShapeDtypeStruct((nseq, t * nc, w), BF16),
            jax.ShapeDtypeStruct((nseq, 1, w), F32),
        ],
        scratch_shapes=[pltpu.VMEM((nb, 1, w), F32), pltpu.VMEM((nb, SUBLANES + t, w), F32)],
        compiler_params=_cparams(("parallel", "arbitrary")),
        name="rglru",
    )(rest3, rest3, cw, cb, wa, ba, wx, bx, lam, h0, conv0)


def _gla_kernel(q_ref, k_ref, v_ref, al_ref, aup_ref, ab_ref, s0_ref, o_ref, s_ref, *, nb, t, nc, l_valid):
    c = pl.program_id(1)
    if nc == 1:
        s_in = s0_ref
    else:
        s_in = s_ref

        @pl.when(c == 0)
        def _():
            s_ref[...] = s0_ref[...]

    pos = c * t + lax.broadcasted_iota(jnp.int32, (t, 1), 0)
    valid = pos < l_valid
    row = lax.broadcasted_iota(jnp.int32, (t, t), 0)
    col = lax.broadcasted_iota(jnp.int32, (t, t), 1)
    causal = row >= col
    aup = aup_ref[...]
    ab = ab_ref[...]
    heads = [(h, slice(h * G_DK, (h + 1) * G_DK), slice(h * G_DV, (h + 1) * G_DV)) for h in range(N_HEADS)]

    def group(rows):
        pairs = [(i, r, h, ks, vs) for i, r in enumerate(rows) for h, ks, vs in heads]
        dec = []
        for r in rows:
            pre = _dot(al_ref[r], aup) + ab
            lg = jnp.where(valid, _log_sigmoid(pre) / G_TAU, 0.0)
            b = _cumsum_rows(lg)
            b_last = b[t - 1:t, :]
            k = jnp.where(valid, k_ref[r].astype(F32), 0.0)
            qd = ((q_ref[r].astype(F32) * (G_DK ** -0.5)) * jnp.exp(b)).astype(BF16)
            kn = (k * jnp.exp(-b)).astype(BF16)
            kr = (k * jnp.exp(b_last - b)).astype(BF16)
            dec.append((qd, kn, kr, _row_to_col(jnp.exp(b_last))))
        att = {}
        for i, r, h, ks, vs in pairs:
            qd, kn, _, _ = dec[i]
            att[i, h] = jnp.where(causal, _dot_nt(qd[:, ks], kn[:, ks]), 0.0).astype(BF16)
        for i, r, h, ks, vs in pairs:
            qd, _, kr, ebl_col = dec[i]
            s_h = s_in[r, h]
            v = v_ref[r, :, vs]
            o_ref[r, :, vs] = (_dot(qd[:, ks], s_h) + _dot(att[i, h], v)).astype(o_ref.dtype)
            s_ref[r, h] = ebl_col[ks, :] * s_h + _dot_tn(kr[:, ks], v)

    _group_loop(nb, group)


def _gla(qkv3, al3, row0, nseq, tiles, nc, l_valid, aup, ab, s0):
    nb, t = tiles
    assert nseq % nb == 0 and row0 % nb == 0
    r0 = row0 // nb
    full2 = lambda a: pl.BlockSpec(a.shape, lambda b, c: (0, 0))
    st = pl.BlockSpec((nb, N_HEADS, G_DK, G_DV), lambda b, c: (b, 0, 0, 0))
    kern = functools.partial(_gla_kernel, nb=nb, t=t, nc=nc, l_valid=l_valid)
    return pl.pallas_call(
        kern,
        grid=(nseq // nb, nc),
        in_specs=[
            pl.BlockSpec((nb, t, G_KW), lambda b, c: (r0 + b, c, 0)),
            pl.BlockSpec((nb, t, G_KW), lambda b, c: (r0 + b, c, 1)),
            pl.BlockSpec((nb, t, G_VW), lambda b, c: (r0 + b, c, 1)),
            pl.BlockSpec((nb, t, LANES), lambda b, c: (r0 + b, c, 0)),
            full2(aup), full2(ab), st,
        ],
        out_specs=[pl.BlockSpec((nb, t, G_VW), lambda b, c: (b, c, 0)), st],
        out_shape=[
            jax.ShapeDtypeStruct((nseq, t * nc, G_VW), BF16),
            jax.ShapeDtypeStruct((nseq, N_HEADS, G_DK, G_DV), F32),
        ],
        compiler_params=_cparams(("parallel", "arbitrary")),
        name="gla",
    )(qkv3, qkv3, qkv3, al3, aup, ab, s0)


def _even_layer_kernel(x_ref, pre_ref, post_ref, win_a_ref, win_b_ref, wgate_ref, gbias_ref, ng_ref, cw_ref, cb_ref, wa_ref,
                       ba_ref, wx_ref, bx_ref, lam_ref, w1_ref, w2_ref, c0_ref, n0_ref, m0_ref, h0_ref, conv0_ref,
                       y_ref, c_ref, n_ref, m_ref, hlast_ref, conv_ref,
                       qkv_s, rest_s, gate_s, hn_s, hm_s, hl_s, xkeep_s, hlru_s, xs_s, *, rows, tiles_per_seq):
    g = pl.program_id(0)
    slot = g % 2
    prev = 1 - slot
    t = MLSTM_TILES["main"][1]
    e_lo = 3 * M_WIDTH
    e_a = win_a_ref.shape[1]
    e = e_a + win_b_ref.shape[1]
    n_chunks = rows // t
    cols_per_chunk = e // n_chunks

    @pl.when(g == 0)
    def _():
        qkv_s[1] = jnp.zeros(qkv_s.shape[1:], qkv_s.dtype)
        rest_s[1] = jnp.zeros(rest_s.shape[1:], rest_s.dtype)
        gate_s[1] = jnp.zeros(gate_s.shape[1:], gate_s.dtype)
        xkeep_s[...] = jnp.zeros(xkeep_s.shape, xkeep_s.dtype)

    @pl.when(jnp.maximum(g - 1, 0) % tiles_per_seq == 0)
    def _():
        c_ref[...] = c0_ref[...]
        n_ref[...] = n0_ref[...]
        m_ref[...] = m0_ref[...]
        hlru_s[...] = h0_ref[...]
        xs_s[:, 0:SUBLANES, :] = conv0_ref[...]

    x = x_ref[0]
    hn_s[...] = (x * lax.rsqrt(jnp.mean(x * x, axis=-1, keepdims=True) + EPS) * pre_ref[...]).astype(BF16)
    gate_s[slot] = jnp.dot(hn_s[...], wgate_ref[...], preferred_element_type=F32)

    def project(c0, c1):
        w_cols = win_a_ref[:, c0:c1] if c1 <= e_a else win_b_ref[:, c0 - e_a:c1 - e_a]
        acc = jnp.dot(hn_s[...], w_cols, preferred_element_type=F32)
        if c0 < e_lo:
            hi = min(c1, e_lo)
            qkv_s[slot, :, c0:hi] = acc[:, :hi - c0].astype(BF16)
        if c1 > e_lo:
            lo = max(c0, e_lo)
            rest_s[slot, :, lo - e_lo:c1 - e_lo] = acc[:, lo - c0:]

    row = lax.broadcasted_iota(jnp.int32, (t, t), 0)
    col = lax.broadcasted_iota(jnp.int32, (t, t), 1)
    causal = row >= col
    lane = lax.broadcasted_iota(jnp.int32, (1, LANES), 1)
    gbias = gbias_ref[...]
    heads = [(h, slice(h * M_DK, (h + 1) * M_DK)) for h in range(N_HEADS)]
    piece = 512
    pieces = [(c0, c0 + piece) for c0 in range(0, e, piece)]
    per_chunk = 5
    assert e % piece == 0 and e_lo % piece == 0 and e_a % piece == 0 and len(pieces) >= n_chunks * per_chunk

    w = LRU_WIDTH
    lru_h = [hlru_s[...]]
    subs_per_chunk = rows // LRU_SUB // n_chunks

    def lru_sub(i):
        r0 = i * LRU_SUB

        def emit(gi, hg, gated):
            hl_s[r0 + gi * SUBLANES:r0 + (gi + 1) * SUBLANES, :] = gated[0].astype(hl_s.dtype)

        lru_h[0] = _lru_chunk(rest_s[prev, r0:r0 + LRU_SUB, 2 * w:3 * w][None],
                              rest_s[prev, r0:r0 + LRU_SUB, 3 * w:4 * w][None], xs_s, lru_h[0],
                              cw_ref[...], cb_ref[...], wa_ref, ba_ref[...], wx_ref, bx_ref[...], lam_ref[...], emit)

    n_gaps = 4

    def lru_between(j, gap):
        for i in range(-(-gap * subs_per_chunk // n_gaps), -(-(gap + 1) * subs_per_chunk // n_gaps)):
            lru_sub(j * subs_per_chunk + i)

    for j in range(n_chunks):
        rs = slice(j * t, (j + 1) * t)
        todo = pieces[j * per_chunk:(j + 1) * per_chunk]
        gt = gate_s[prev, rs, :] + gbias
        li = gt[:, :LANES]
        b = _cumsum_rows(_log_sigmoid(gt[:, LANES:]))
        project(*todo[0])
        lru_between(j, 0)
        r_t = _transpose_rows(li - b)[:, :t]
        m_vec = m_ref[0]
        st = []
        for h, sl in heads:
            b_col = b[:, h:h + 1]
            inter = b_col + m_vec[:, h:h + 1]
            dmat = jnp.where(causal, b_col + r_t[h:h + 1, :], NEG)
            mt = jnp.maximum(inter, jnp.max(dmat, axis=-1, keepdims=True))
            k = qkv_s[prev, rs, M_WIDTH + h * M_DK:M_WIDTH + (h + 1) * M_DK] * (M_DK ** -0.5)
            s = _dot_nt(qkv_s[prev, rs, sl], k) * jnp.exp(dmat - mt)
            st.append((inter, mt, k, s))
        project(*todo[1])
        lru_between(j, 1)
        project(*todo[2])
        for h, sl in heads:
            inter, mt, k, s = st[h]
            q = qkv_s[prev, rs, sl]
            v = qkv_s[prev, rs, 2 * M_WIDTH + h * M_DV:2 * M_WIDTH + (h + 1) * M_DV]
            w_inter = jnp.exp(inter - mt)
            num = w_inter * _dot_nt(q, c_ref[0, h]) + _dot(s, v)
            den = (w_inter * jnp.sum(q.astype(F32) * n_ref[0, h], axis=-1, keepdims=True)
                   + jnp.sum(s, axis=-1, keepdims=True))
            hm_s[rs, sl] = (num / jnp.maximum(jnp.abs(den), jnp.exp(-mt))).astype(hm_s.dtype)
        project(*todo[3])
        lru_between(j, 2)
        m_next = m_vec
        for h, sl in heads:
            inter, mt, k, _ = st[h]
            v = qkv_s[prev, rs, 2 * M_WIDTH + h * M_DV:2 * M_WIDTH + (h + 1) * M_DV]
            b_col = b[:, h:h + 1]
            m_new = mt[t - 1:t, :]
            w_c = jnp.exp(inter[t - 1:t, :] - m_new)
            w_k = jnp.exp(b_col[t - 1:t, :] - b_col + li[:, h:h + 1] - m_new)
            c_ref[0, h] = w_c * c_ref[0, h] + _dot_tn(v.astype(F32) * w_k, k)
            n_ref[0, h] = w_c * n_ref[0, h] + jnp.sum(k.astype(F32) * w_k, axis=0, keepdims=True)
            m_next = jnp.where(lane == h, m_new, m_next)
        m_ref[0] = m_next
        lru_between(j, 3)
        project(*todo[4])

    for p in pieces[n_chunks * per_chunk:]:
        project(*p)
    h_end = lru_h[0]
    hlru_s[...] = h_end
    hlast_ref[...] = h_end
    conv_ref[...] = xs_s[:, 0:SUBLANES, :]

    hm = _sigmoid(rest_s[prev, :, 0:w]) * hm_s[...].astype(F32)
    hm = _head_rms(hm, N_HEADS) * ng_ref[...] * _silu(rest_s[prev, :, w:2 * w])
    out = jnp.dot(hm.astype(BF16), w1_ref[...], preferred_element_type=F32)
    out = out + jnp.dot(hl_s[...], w2_ref[...], preferred_element_type=F32)
    nrm = out * lax.rsqrt(jnp.mean(out * out, axis=-1, keepdims=True) + EPS) * post_ref[...]
    y_ref[0] = xkeep_s[...] + nrm
    xkeep_s[...] = x


def _even_layer_main(x3, pre, post, w_a, e_a, w_b, w_gate, gbias, ng, lru, w1, w2, c0, n0, m0, h0, conv0, rows):
    nseq, seq, d = x3.shape
    assert seq % rows == 0 and rows % MLSTM_TILES["main"][1] == 0
    tps = seq // rows
    n_tiles = nseq * tps
    e = e_a + w_b.shape[1]
    e_lo = 3 * M_WIDTH
    w = LRU_WIDTH
    const = lambda a: pl.BlockSpec(a.shape, lambda g: (0,) * a.ndim, pipeline_mode=pl.Buffered(1))
    cur = lambda g: jnp.minimum(g, n_tiles - 1)
    old = lambda g: jnp.maximum(g - 1, 0)
    per_seq = lambda *blk: pl.BlockSpec((1,) + blk, lambda g: (old(g) // tps,) + (0,) * len(blk))
    st_c, st_n, st_m = per_seq(N_HEADS, M_DV, M_DK), per_seq(N_HEADS, 1, M_DK), per_seq(1, LANES)
    st_h, st_conv = per_seq(1, w), per_seq(SUBLANES, w)
    out_states = [st_c, st_n, st_m, st_h, st_conv]
    if c0.shape[0] == nseq:
        in_states = out_states
    else:
        in_states = [pl.BlockSpec(s.block_shape, lambda g, nd=len(s.block_shape): (0,) * nd) for s in out_states]
    kern = functools.partial(_even_layer_kernel, rows=rows, tiles_per_seq=tps)
    return pl.pallas_call(
        kern,
        grid=(n_tiles + 1,),
        in_specs=[
            pl.BlockSpec((1, rows, d), lambda g: (cur(g) // tps, cur(g) % tps, 0)),
            const(pre), const(post),
            pl.BlockSpec((d, e_a), lambda g: (0, 0), pipeline_mode=pl.Buffered(1)), const(w_b),
            const(w_gate), const(gbias), const(ng),
            *[const(a) for a in lru], const(w1), const(w2), *in_states,
        ],
        out_specs=[
            pl.BlockSpec((1, rows, d), lambda g: (old(g) // tps, old(g) % tps, 0)),
            st_c, st_n, st_m, st_h, st_conv,
        ],
        out_shape=[
            jax.ShapeDtypeStruct((nseq, seq, d), F32),
            jax.ShapeDtypeStruct((nseq, N_HEADS, M_DV, M_DK), F32),
            jax.ShapeDtypeStruct((nseq, N_HEADS, 1, M_DK), F32),
            jax.ShapeDtypeStruct((nseq, 1, LANES), F32),
            jax.ShapeDtypeStruct((nseq, 1, w), F32),
            jax.ShapeDtypeStruct((nseq, SUBLANES, w), F32),
        ],
        scratch_shapes=[
            pltpu.VMEM((2, rows, e_lo), BF16),
            pltpu.VMEM((2, rows, e - e_lo), F32),
            pltpu.VMEM((2, rows, 2 * LANES), F32),
            pltpu.VMEM((rows, d), BF16),
            pltpu.VMEM((rows, M_WIDTH), BF16),
            pltpu.VMEM((rows, w), BF16),
            pltpu.VMEM((rows, d), F32),
            pltpu.VMEM((1, 1, w), F32),
            pltpu.VMEM((1, SUBLANES + LRU_SUB, w), F32),
        ],
        compiler_params=_cparams(("arbitrary",)),
        name="even_layer",
    )(x3, pre, post, w_a, w_b, w_gate, gbias, ng, *lru, w1, w2, c0, n0, m0, h0, conv0)


def _gla_layer_kernel(x_ref, pre_ref, post_ref, win_ref, wlow_ref, aup_ref, ab_ref, ng_ref, wout_ref, s0_ref,
                      y_ref, s_ref, qkv_s, r_s, al_s, hn_s, o_s, xkeep_s, *, rows, tiles_per_seq):
    g = pl.program_id(0)
    slot = g % 2
    prev = 1 - slot
    t = GLA_TILES["main"][1]
    e_lo = 2 * G_KW + G_VW
    e = e_lo + G_VW
    n_chunks = rows // t
    cols_per_chunk = e // n_chunks

    @pl.when(g == 0)
    def _():
        qkv_s[1] = jnp.zeros(qkv_s.shape[1:], qkv_s.dtype)
        r_s[1] = jnp.zeros(r_s.shape[1:], r_s.dtype)
        al_s[1] = jnp.zeros(al_s.shape[1:], al_s.dtype)
        xkeep_s[...] = jnp.zeros(xkeep_s.shape, xkeep_s.dtype)

    @pl.when(jnp.maximum(g - 1, 0) % tiles_per_seq == 0)
    def _():
        s_ref[...] = s0_ref[...]

    x = x_ref[0]
    hn_s[...] = (x * lax.rsqrt(jnp.mean(x * x, axis=-1, keepdims=True) + EPS) * pre_ref[...]).astype(BF16)
    al_s[slot] = jnp.dot(hn_s[...], wlow_ref[...], preferred_element_type=F32)

    row = lax.broadcasted_iota(jnp.int32, (t, t), 0)
    col = lax.broadcasted_iota(jnp.int32, (t, t), 1)
    causal = row >= col
    aup = aup_ref[...]
    ab = ab_ref[...]
    heads = [(h, slice(h * G_DK, (h + 1) * G_DK), slice(h * G_DV, (h + 1) * G_DV)) for h in range(N_HEADS)]

    def project(c0, c1):
        acc = jnp.dot(hn_s[...], win_ref[:, c0:c1], preferred_element_type=F32)
        if c0 < e_lo:
            hi = min(c1, e_lo)
            qkv_s[slot, :, c0:hi] = acc[:, :hi - c0].astype(BF16)
        if c1 > e_lo:
            lo = max(c0, e_lo)
            r_s[slot, :, lo - e_lo:c1 - e_lo] = acc[:, lo - c0:]

    n_sub = 3
    sub = cols_per_chunk // n_sub
    for j in range(n_chunks):
        c0 = j * cols_per_chunk
        rs = slice(j * t, (j + 1) * t)
        pre = _dot(al_s[prev, rs, :], aup) + ab
        project(c0, c0 + sub)
        b = _cumsum_rows(_log_sigmoid(pre) / G_TAU)
        b_last = b[t - 1:t, :]
        project(c0 + sub, c0 + 2 * sub)
        k = qkv_s[prev, rs, G_KW:2 * G_KW].astype(F32)
        qd = ((qkv_s[prev, rs, 0:G_KW].astype(F32) * (G_DK ** -0.5)) * jnp.exp(b)).astype(BF16)
        kn = (k * jnp.exp(-b)).astype(BF16)
        kr = (k * jnp.exp(b_last - b)).astype(BF16)
        ebl_col = _row_to_col(jnp.exp(b_last))
        att = [jnp.where(causal, _dot_nt(qd[:, ks], kn[:, ks]), 0.0).astype(BF16) for _, ks, _ in heads]
        project(c0 + 2 * sub, c0 + cols_per_chunk)
        for h, ks, vs in heads:
            s_h = s_ref[0, h]
            v = qkv_s[prev, rs, 2 * G_KW + h * G_DV:2 * G_KW + (h + 1) * G_DV]
            o_s[rs, vs] = (_dot(qd[:, ks], s_h) + _dot(att[h], v)).astype(o_s.dtype)
            s_ref[0, h] = ebl_col[ks, :] * s_h + _dot_tn(kr[:, ks], v)

    o = _head_rms(o_s[...].astype(F32), N_HEADS) * ng_ref[...] * _silu(r_s[prev])
    out = jnp.dot(o.astype(BF16), wout_ref[...], preferred_element_type=F32)
    nrm = out * lax.rsqrt(jnp.mean(out * out, axis=-1, keepdims=True) + EPS) * post_ref[...]
    y_ref[0] = xkeep_s[...] + nrm
    xkeep_s[...] = x


def _gla_layer(x3, pre, post, w_in, w_low, aup, ab, ng, w_out, s0, rows):
    nseq, seq, d = x3.shape
    assert seq % rows == 0 and rows % GLA_TILES["main"][1] == 0
    tps = seq // rows
    n_tiles = nseq * tps
    e_lo = 2 * G_KW + G_VW
    e = e_lo + G_VW
    const = lambda a: pl.BlockSpec(a.shape, lambda g: (0,) * a.ndim, pipeline_mode=pl.Buffered(1))
    cur = lambda g: jnp.minimum(g, n_tiles - 1)
    old = lambda g: jnp.maximum(g - 1, 0)
    st = pl.BlockSpec((1, N_HEADS, G_DK, G_DV), lambda g: (old(g) // tps, 0, 0, 0))
    st_in = st if s0.shape[0] == nseq else pl.BlockSpec((1, N_HEADS, G_DK, G_DV), lambda g: (0, 0, 0, 0))
    kern = functools.partial(_gla_layer_kernel, rows=rows, tiles_per_seq=tps)
    return pl.pallas_call(
        kern,
        grid=(n_tiles + 1,),
        in_specs=[
            pl.BlockSpec((1, rows, d), lambda g: (cur(g) // tps, cur(g) % tps, 0)),
            const(pre), const(post), pl.BlockSpec((d, e), lambda g: (0, 0), pipeline_mode=pl.Buffered(1)),
            const(w_low), const(aup), const(ab), const(ng), const(w_out), st_in,
        ],
        out_specs=[pl.BlockSpec((1, rows, d), lambda g: (old(g) // tps, old(g) % tps, 0)), st],
        out_shape=[
            jax.ShapeDtypeStruct((nseq, seq, d), F32),
            jax.ShapeDtypeStruct((nseq, N_HEADS, G_DK, G_DV), F32),
        ],
        scratch_shapes=[
            pltpu.VMEM((2, rows, e_lo), BF16),
            pltpu.VMEM((2, rows, e - e_lo), F32),
            pltpu.VMEM((2, rows, LANES), F32),
            pltpu.VMEM((rows, d), BF16),
            pltpu.VMEM((rows, G_VW), BF16),
            pltpu.VMEM((rows, d), F32),
        ],
        compiler_params=_cparams(("arbitrary",)),
        name="gla_layer",
    )(x3, pre, post, w_in, w_low, aup, ab, ng, w_out, s0)


def _pad_lanes(a, width):
    return jnp.pad(a, [(0, 0)] * (a.ndim - 1) + [(0, width - a.shape[-1])])


def _even_layer(xm, xs, nbp, seq, nbs, ls, st, pre_g, post_g, w_in, w_out, b_i, b_f, m_norm_g,
                conv_w, conv_b, wa, ba, wx, bx, lam):
    c0s, n0s, m0s, h0s, conv0s = st
    e_a = 4 * M_WIDTH
    n_gate = 2 * N_HEADS
    w_bf = w_in.astype(BF16)
    w_b = w_bf[:, e_a + n_gate:]
    wig, wfg = w_in[:, e_a:e_a + N_HEADS], w_in[:, e_a + N_HEADS:e_a + n_gate]
    w_gate = jnp.concatenate([_pad_lanes(wig, LANES), _pad_lanes(wfg, LANES)], axis=1).astype(BF16)
    gbias = jnp.concatenate([_pad_lanes(b_i[None], LANES), _pad_lanes(b_f[None], LANES)], axis=1)
    ng = m_norm_g[None]
    post = post_g[None]
    cw, cb = conv_w, conv_b[None]
    wab, wxb = wa.astype(BF16), wx.astype(BF16)
    bav, bxv, lamv = ba[None], bx[None], lam[None]
    w_o1, w_o2 = w_out[:M_WIDTH].astype(BF16), w_out[M_WIDTH:].astype(BF16)
    e_lo = 3 * M_WIDTH
    e_hi = e_a + w_b.shape[1] - e_lo
    lru = (cw, cb, wab, bav, wxb, bxv, lamv)

    qs, ps, gs = _norm_proj(xs, pre_g, w_bf, e_a, w_b, w_gate, e_lo, xs.shape[0])
    mrow = nbs * SAMPLE_PAD // N_META
    xr_cols = slice(2 * LRU_WIDTH, 3 * LRU_WIDTH)

    zc = jnp.zeros((1, N_HEADS, M_DV, M_DK), F32)
    zn = jnp.zeros((1, N_HEADS, 1, M_DK), F32)
    zm = jnp.zeros((1, 1, LANES), F32)
    zh = jnp.zeros((1, 1, LRU_WIDTH), F32)
    zconv = jnp.zeros((1, SUBLANES, LRU_WIDTH), F32)
    qs_meta = qs.reshape(-1, N_META, e_lo)
    ps_meta = ps.reshape(-1, N_META, e_hi)
    gs_meta = gs.reshape(-1, N_META, 2 * LANES)
    h_meta, c1, n1, m1 = _mlstm(qs_meta, gs_meta, mrow, 1, MLSTM_TILES["meta"], 1, N_META, gbias, zc, zn, zm)
    hl_meta, h1 = _rglru(ps_meta, mrow, 1, LRU_TILES["meta"], 1, N_META, *lru, zh, zconv)
    conv1 = ps_meta[mrow:mrow + 1, N_META - SUBLANES:, xr_cols]

    ym3, pc, pn, pmm, ph, conv_tail = _even_layer_main(
        xm.reshape(nbp, seq, -1), pre_g[None], post, w_bf, e_a, w_b, w_gate, gbias, ng, lru, w_o1, w_o2,
        c1, n1, m1, h1, conv1, FUSED_ROW_TILE)
    ym = ym3.reshape(xm.shape)
    pconv = conv_tail[:, SUBLANES - (CONV_W - 1):, :]

    qs_s = qs.reshape(-1, SAMPLE_PAD, e_lo)
    ps_s = ps.reshape(-1, SAMPLE_PAD, e_hi)
    gs_s = gs.reshape(-1, SAMPLE_PAD, 2 * LANES)
    h_s, sc, sn, sm = _mlstm(qs_s, gs_s, 0, nbs, MLSTM_TILES["sample"], 1, ls, gbias,
                             c0s, n0s[:, :, None, :], _pad_lanes(m0s, LANES)[:, None, :])
    conv0p = jnp.pad(conv0s, ((0, 0), (SUBLANES - (CONV_W - 1), 0), (0, 0)))
    hl_s, sh = _rglru(ps_s, 0, nbs, LRU_TILES["sample"], 1, ls, *lru, h0s[:, None, :], conv0p)
    xr_s = ps_s[:nbs, :ls, xr_cols]
    sconv = jnp.concatenate([conv0s, xr_s], axis=1)[:, -(CONV_W - 1):]

    h_small = jnp.concatenate([h_s.reshape(-1, M_WIDTH), h_meta.reshape(-1, M_WIDTH)], axis=0)
    hl_small = jnp.concatenate([hl_s.reshape(-1, LRU_WIDTH), hl_meta.reshape(-1, LRU_WIDTH)], axis=0)
    ys = _out_even(xs, post, h_small, ps, ng, hl_small, w_o1, w_o2, _row_tile(xs.shape[0], OUT_ROW_TILE))

    p_state = (pc, pn[:, :, 0, :], pmm[:, 0, :N_HEADS], ph[:, 0, :], pconv)
    s_state = (sc, sn[:, :, 0, :], sm[:, 0, :N_HEADS], sh[:, 0, :], sconv)
    return ym, ys, p_state, s_state


def _odd_layer(xm, xs, nbp, seq, nbs, ls, s0s, pre_g, post_g, w_in, w_out, a_up, a_b, g_norm_g):
    e_lo = 2 * G_KW + G_VW
    e_a = 2 * G_KW + 2 * G_VW
    w_bf = w_in.astype(BF16)
    w_low = _pad_lanes(w_in[:, 2 * G_KW + 2 * G_VW:], LANES).astype(BF16)
    aup = jnp.pad(a_up, ((0, LANES - G_RANK), (0, 0))).astype(BF16)
    ab = a_b[None]
    ng = g_norm_g[None]
    post = post_g[None]
    w_o = w_out.astype(BF16)

    qs, rs, as_ = _norm_proj(xs, pre_g, w_bf, e_a, None, w_low, e_lo, xs.shape[0])
    mrow = nbs * SAMPLE_PAD // N_META

    zs = jnp.zeros((1, N_HEADS, G_DK, G_DV), F32)
    o_meta, s1 = _gla(qs.reshape(-1, N_META, e_lo), as_.reshape(-1, N_META, LANES),
                      mrow, 1, GLA_TILES["meta"], 1, N_META, aup, ab, zs)
    ym3, p_s = _gla_layer(xm.reshape(nbp, seq, -1), pre_g[None], post, w_bf, w_low, aup, ab, ng, w_o, s1,
                          GLA_ROW_TILE)
    ym = ym3.reshape(xm.shape)
    o_s, s_s = _gla(qs.reshape(-1, SAMPLE_PAD, e_lo), as_.reshape(-1, SAMPLE_PAD, LANES),
                    0, nbs, GLA_TILES["sample"], 1, ls, aup, ab, s0s)

    o_small = jnp.concatenate([o_s.reshape(-1, G_VW), o_meta.reshape(-1, G_VW)], axis=0)
    ys = _out_odd(xs, post, o_small, rs, ng, w_o, _row_tile(xs.shape[0], OUT_ROW_TILE))
    return ym, ys, p_s, s_s


def kernel(x_prompt, x_sample, state_mlstm_C, state_mlstm_n, state_mlstm_m, state_rglru_h, state_rglru_conv,
           state_gla_S, meta_tokens, pre_norm_a, post_norm_a, w_in_a, w_out_a, mlstm_b_i, mlstm_b_f, mlstm_norm,
           conv_w, conv_b, lru_w_a, lru_b_a, lru_w_x, lru_b_x, lru_lambda, pre_norm_c, post_norm_c, w_in_c,
           w_out_c, gla_alpha_up, gla_alpha_b, gla_norm):
    nbp, seq, d = x_prompt.shape
    nbs, ls, _ = x_sample.shape
    depth = pre_norm_a.shape[0] + pre_norm_c.shape[0]
    assert ls >= CONV_W - 1 and ls <= SAMPLE_PAD and N_META % SUBLANES == 0 and nbs * SAMPLE_PAD % N_META == 0

    xm = x_prompt.reshape(nbp * seq, d)
    xs_pad = jnp.pad(x_sample, ((0, 0), (0, SAMPLE_PAD - ls), (0, 0))).reshape(nbs * SAMPLE_PAD, d)
    xs = jnp.concatenate([xs_pad, meta_tokens.astype(x_prompt.dtype)], axis=0)

    p_lists = [[] for _ in range(6)]
    s_lists = [[] for _ in range(6)]
    for layer in range(depth):
        j = layer // 2
        if layer % 2 == 0:
            st = (state_mlstm_C[j], state_mlstm_n[j], state_mlstm_m[j], state_rglru_h[j], state_rglru_conv[j])
            xm, xs, pst, sst = _even_layer(
                xm, xs, nbp, seq, nbs, ls, st, pre_norm_a[j], post_norm_a[j], w_in_a[j], w_out_a[j],
                mlstm_b_i[j], mlstm_b_f[j], mlstm_norm[j], conv_w[j], conv_b[j], lru_w_a[j], lru_b_a[j],
                lru_w_x[j], lru_b_x[j], lru_lambda[j])
            for i in range(5):
                p_lists[i].append(pst[i])
                s_lists[i].append(sst[i])
        else:
            xm, xs, p_s, s_s = _odd_layer(
                xm, xs, nbp, seq, nbs, ls, state_gla_S[j], pre_norm_c[j], post_norm_c[j], w_in_c[j], w_out_c[j],
                gla_alpha_up[j], gla_alpha_b[j], gla_norm[j])
            p_lists[5].append(p_s)
            s_lists[5].append(s_s)

    y_prompt = xm.reshape(nbp, seq, d)
    y_sample = xs[:nbs * SAMPLE_PAD].reshape(nbs, SAMPLE_PAD, d)[:, :ls]
    return (y_prompt, y_sample) + tuple(jnp.stack(l) for l in p_lists) + tuple(jnp.stack(l) for l in s_lists)
```

```python
import functools

import jax
import jax.numpy as jnp
from jax import lax
from jax.experimental import pallas as pl
from jax.experimental.pallas import tpu as pltpu

F32 = jnp.float32
BF16 = jnp.bfloat16

D_MODEL = 1024
N_META = 16
EPS = 1e-6
N_HEADS = 4
M_DK = 256
M_DV = 256
M_WIDTH = N_HEADS * M_DV
LRU_WIDTH = 1024
LRU_BLOCKS = 8
LRU_BS = LRU_WIDTH // LRU_BLOCKS
CONV_W = 4
LRU_C = 8.0
G_DK = 256
G_DV = 512
G_KW = N_HEADS * G_DK
G_VW = N_HEADS * G_DV
G_RANK = 16
G_TAU = 16.0

LANES = 128
SUBLANES = 8
NEG = -1e30
SAMPLE_PAD = SUBLANES
VMEM_LIMIT = 56 * 1024 * 1024
COL_TILE = 1024

MLSTM_TILES = {"meta": (1, N_META), "main": (1, 128), "sample": (8, SAMPLE_PAD)}
GLA_TILES = {"meta": (1, N_META), "main": (1, 64), "sample": (4, SAMPLE_PAD)}
LRU_TILES = {"meta": (1, N_META), "sample": (32, SAMPLE_PAD)}
OUT_ROW_TILE = 512
FUSED_ROW_TILE = 256
GLA_ROW_TILE = 256
LRU_SUB = 64


def _row_tile(n, cap):
    return next(tm for tm in range(min(n, cap) // SUBLANES * SUBLANES, 0, -SUBLANES) if n % tm == 0)


def _cparams(sem):
    return pltpu.CompilerParams(dimension_semantics=sem, vmem_limit_bytes=VMEM_LIMIT)


def _sigmoid(x):
    return 0.5 * jnp.tanh(0.5 * x) + 0.5


def _silu(x):
    return x * _sigmoid(x)


def _log_sigmoid(x):
    return jnp.minimum(x, 0.0) - jnp.log1p(jnp.exp(-jnp.abs(x)))


def _softplus(x):
    return jnp.maximum(x, 0.0) + jnp.log1p(jnp.exp(-jnp.abs(x)))


def _dot(a, b):
    return jnp.dot(a.astype(BF16), b.astype(BF16), preferred_element_type=F32)


def _dot_nt(a, b):
    return lax.dot_general(a.astype(BF16), b.astype(BF16), (((1,), (1,)), ((), ())),
                           preferred_element_type=F32)


def _dot_tn(a, b):
    return lax.dot_general(a.astype(BF16), b.astype(BF16), (((0,), (0,)), ((), ())),
                           preferred_element_type=F32)


def _transpose_rows(x):
    t, n = x.shape
    if t < LANES:
        x = jnp.concatenate([x, jnp.zeros((LANES - t, n), x.dtype)], axis=0)
    return jnp.transpose(x)


def _row_to_col(r):
    return jnp.transpose(jnp.broadcast_to(r, (LANES, r.shape[1])))[:, 0:1]


def _cumsum_rows(x):
    t = x.shape[0]
    if t <= 2 * SUBLANES:
        rid = lax.broadcasted_iota(jnp.int32, (t, 1), 0)
        s = 1
        while s < t:
            x = x + jnp.where(rid >= s, pltpu.roll(x, s, axis=0), 0.0)
            s *= 2
        return x
    row = lax.broadcasted_iota(jnp.int32, (t, t), 0)
    col = lax.broadcasted_iota(jnp.int32, (t, t), 1)
    tri = jnp.where(row >= col, 1.0, 0.0).astype(BF16)
    hi = x.astype(BF16)
    r1 = x - hi.astype(F32)
    mid = r1.astype(BF16)
    lo = (r1 - mid.astype(F32)).astype(BF16)
    acc = jnp.dot(tri, lo, preferred_element_type=F32)
    acc = acc + jnp.dot(tri, mid, preferred_element_type=F32)
    return acc + jnp.dot(tri, hi, preferred_element_type=F32)


def _head_rms(x, nh):
    hd = x.shape[1] // nh
    parts = []
    for h in range(nh):
        xh = x[:, h * hd:(h + 1) * hd]
        parts.append(xh * lax.rsqrt(jnp.mean(xh * xh, axis=-1, keepdims=True) + EPS))
    return jnp.concatenate(parts, axis=-1)


GROUP = 2


def _group_loop(nb, group_fn):
    if nb <= GROUP:
        group_fn(list(range(nb)))
        return

    def body(i, carry):
        group_fn([GROUP * i + r for r in range(GROUP)])
        return carry

    lax.fori_loop(0, nb // GROUP, body, 0)


def _norm_proj_kernel(n_lo, n_a, x_ref, g_ref, wa_ref, wb_ref, ws_ref, olo_ref, ohi_ref, os_ref, hn_ref):
    j = pl.program_id(1)

    @pl.when(j == 0)
    def _():
        x = x_ref[...]
        y = x * lax.rsqrt(jnp.mean(x * x, axis=-1, keepdims=True) + EPS) * g_ref[...]
        hn = y.astype(BF16)
        hn_ref[...] = hn
        os_ref[...] = jnp.dot(hn, ws_ref[...], preferred_element_type=F32)

    @pl.when(j < n_lo)
    def _():
        olo_ref[...] = jnp.dot(hn_ref[...], wa_ref[...], preferred_element_type=F32).astype(olo_ref.dtype)

    @pl.when((j >= n_lo) & (j < n_a))
    def _():
        ohi_ref[...] = jnp.dot(hn_ref[...], wa_ref[...], preferred_element_type=F32)

    @pl.when(j >= n_a)
    def _():
        ohi_ref[...] = jnp.dot(hn_ref[...], wb_ref[...], preferred_element_type=F32)


def _norm_proj(x2d, g, w_a, e_a, w_b, w_small, e_lo, tm):
    n, d = x2d.shape
    tn = COL_TILE
    if w_b is None:
        w_b, e = w_a, e_a
    else:
        e = e_a + w_b.shape[1]
    es = w_small.shape[1]
    n_lo, n_a = e_lo // tn, e_a // tn
    assert e_lo <= e_a and e_a % tn == 0 and e % tn == 0
    return pl.pallas_call(
        functools.partial(_norm_proj_kernel, n_lo, n_a),
        grid=(n // tm, e // tn),
        in_specs=[
            pl.BlockSpec((tm, d), lambda i, j: (i, 0)),
            pl.BlockSpec((1, d), lambda i, j: (0, 0)),
            pl.BlockSpec((d, tn), lambda i, j: (0, jnp.minimum(j, n_a - 1))),
            pl.BlockSpec((d, tn), lambda i, j: (0, jnp.maximum(j - n_a, 0))),
            pl.BlockSpec((d, es), lambda i, j: (0, 0)),
        ],
        out_specs=[
            pl.BlockSpec((tm, tn), lambda i, j: (i, jnp.minimum(j, n_lo - 1))),
            pl.BlockSpec((tm, tn), lambda i, j: (i, jnp.maximum(j - n_lo, 0))),
            pl.BlockSpec((tm, es), lambda i, j: (i, 0)),
        ],
        out_shape=[
            jax.ShapeDtypeStruct((n, e_lo), BF16),
            jax.ShapeDtypeStruct((n, e - e_lo), F32),
            jax.ShapeDtypeStruct((n, es), F32),
        ],
        scratch_shapes=[pltpu.VMEM((tm, d), BF16)],
        compiler_params=_cparams(("parallel", "arbitrary")),
        name="norm_proj",
    )(x2d, g.reshape(1, d), w_a, w_b, w_small)


def _residual_norm(x_ref, g_ref, y_ref, out):
    nrm = out * lax.rsqrt(jnp.mean(out * out, axis=-1, keepdims=True) + EPS) * g_ref[...]
    y_ref[...] = x_ref[...] + nrm


def _out_even_kernel(x_ref, g_ref, h_ref, og_ref, z_ref, ng_ref, hl_ref, w1_ref, w2_ref, y_ref):
    hm = _sigmoid(og_ref[...]) * h_ref[...].astype(F32)
    hm = _head_rms(hm, N_HEADS) * ng_ref[...] * _silu(z_ref[...])
    out = jnp.dot(hm.astype(BF16), w1_ref[...], preferred_element_type=F32)
    out = out + jnp.dot(hl_ref[...], w2_ref[...], preferred_element_type=F32)
    _residual_norm(x_ref, g_ref, y_ref, out)


def _out_even(x2d, g, h_raw, rest, ng, hl, w1, w2, tm):
    n, d = x2d.shape
    row = lambda width, col: pl.BlockSpec((tm, width), lambda i: (i, col))
    const = lambda a: pl.BlockSpec(a.shape, lambda i: (0, 0))
    return pl.pallas_call(
        _out_even_kernel,
        grid=(n // tm,),
        in_specs=[row(d, 0), const(g), row(M_WIDTH, 0), row(M_WIDTH, 0), row(M_WIDTH, 1), const(ng),
                  row(LRU_WIDTH, 0), const(w1), const(w2)],
        out_specs=row(d, 0),
        out_shape=jax.ShapeDtypeStruct((n, d), F32),
        compiler_params=_cparams(("parallel",)),
        name="out_even",
    )(x2d, g, h_raw, rest, rest, ng, hl, w1, w2)


def _out_odd_kernel(x_ref, g_ref, o_ref, r_ref, ng_ref, w_ref, y_ref):
    o = _head_rms(o_ref[...].astype(F32), N_HEADS) * ng_ref[...] * _silu(r_ref[...])
    out = jnp.dot(o.astype(BF16), w_ref[...], preferred_element_type=F32)
    _residual_norm(x_ref, g_ref, y_ref, out)


def _out_odd(x2d, g, o_raw, r, ng, w, tm):
    n, d = x2d.shape
    row = lambda width: pl.BlockSpec((tm, width), lambda i: (i, 0))
    const = lambda a: pl.BlockSpec(a.shape, lambda i: (0, 0))
    return pl.pallas_call(
        _out_odd_kernel,
        grid=(n // tm,),
        in_specs=[row(d), const(g), row(G_VW), row(G_VW), const(ng), const(w)],
        out_specs=row(d),
        out_shape=jax.ShapeDtypeStruct((n, d), F32),
        compiler_params=_cparams(("parallel",)),
        name="out_odd",
    )(x2d, g, o_raw, r, ng, w)


def _mlstm_kernel(q_ref, k_ref, v_ref, gate_ref, gbias_ref, c0_ref, n0_ref, m0_ref,
                  h_ref, c_ref, n_ref, m_ref, *, nb, t, nc, l_valid):
    c = pl.program_id(1)
    if nc == 1:
        c_in, n_in, m_in = c0_ref, n0_ref, m0_ref
    else:
        c_in, n_in, m_in = c_ref, n_ref, m_ref

        @pl.when(c == 0)
        def _():
            c_ref[...] = c0_ref[...]
            n_ref[...] = n0_ref[...]
            m_ref[...] = m0_ref[...]

    pos = c * t + lax.broadcasted_iota(jnp.int32, (t, 1), 0)
    valid = pos < l_valid
    row = lax.broadcasted_iota(jnp.int32, (t, t), 0)
    col = lax.broadcasted_iota(jnp.int32, (t, t), 1)
    causal = row >= col
    lane = lax.broadcasted_iota(jnp.int32, (1, LANES), 1)
    gbias = gbias_ref[...]
    heads = [(h, slice(h * M_DK, (h + 1) * M_DK)) for h in range(N_HEADS)]

    def group(rows):
        pairs = [(i, r, h, sl) for i, r in enumerate(rows) for h, sl in heads]
        gates = []
        for r in rows:
            g = gate_ref[r] + gbias
            li = jnp.where(valid, g[:, :LANES], NEG)
            lf = jnp.where(valid, _log_sigmoid(g[:, LANES:]), 0.0)
            b = _cumsum_rows(lf)
            gates.append((li, b, _transpose_rows(li - b)[:, :t], m_in[r]))
        st = {}
        for i, r, h, sl in pairs:
            li, b, r_t, m_vec = gates[i]
            b_col = b[:, h:h + 1]
            inter = b_col + m_vec[:, h:h + 1]
            dmat = jnp.where(causal, b_col + r_t[h:h + 1, :], NEG)
            mt = jnp.maximum(inter, jnp.max(dmat, axis=-1, keepdims=True))
            k = k_ref[r, :, sl] * (M_DK ** -0.5)
            s = _dot_nt(q_ref[r, :, sl], k) * jnp.exp(dmat - mt)
            st[i, h] = (inter, mt, k, s)
        for i, r, h, sl in pairs:
            inter, mt, k, s = st[i, h]
            q = q_ref[r, :, sl]
            w_inter = jnp.exp(inter - mt)
            num = w_inter * _dot_nt(q, c_in[r, h]) + _dot(s, v_ref[r, :, sl])
            den = (w_inter * jnp.sum(q.astype(F32) * n_in[r, h], axis=-1, keepdims=True)
                   + jnp.sum(s, axis=-1, keepdims=True))
            h_ref[r, :, sl] = (num / jnp.maximum(jnp.abs(den), jnp.exp(-mt))).astype(h_ref.dtype)
        m_next = [gt[3] for gt in gates]
        for i, r, h, sl in pairs:
            li, b, _, _ = gates[i]
            inter, mt, k, _ = st[i, h]
            b_col = b[:, h:h + 1]
            m_new = mt[t - 1:t, :]
            w_c = jnp.exp(inter[t - 1:t, :] - m_new)
            w_k = jnp.exp(b_col[t - 1:t, :] - b_col + li[:, h:h + 1] - m_new)
            c_ref[r, h] = w_c * c_in[r, h] + _dot_tn(v_ref[r, :, sl].astype(F32) * w_k, k)
            n_ref[r, h] = w_c * n_in[r, h] + jnp.sum(k.astype(F32) * w_k, axis=0, keepdims=True)
            m_next[i] = jnp.where(lane == h, m_new, m_next[i])
        for i, r in enumerate(rows):
            m_ref[r] = m_next[i]

    _group_loop(nb, group)


def _mlstm(qkv3, gates3, row0, nseq, tiles, nc, l_valid, gbias, c0, n0, m0):
    nb, t = tiles
    assert nseq % nb == 0 and row0 % nb == 0
    wd = M_WIDTH
    r0 = row0 // nb
    seq = lambda col: pl.BlockSpec((nb, t, wd), lambda b, c: (r0 + b, c, col))
    st_c = pl.BlockSpec((nb, N_HEADS, M_DV, M_DK), lambda b, c: (b, 0, 0, 0))
    st_n = pl.BlockSpec((nb, N_HEADS, 1, M_DK), lambda b, c: (b, 0, 0, 0))
    st_m = pl.BlockSpec((nb, 1, LANES), lambda b, c: (b, 0, 0))
    kern = functools.partial(_mlstm_kernel, nb=nb, t=t, nc=nc, l_valid=l_valid)
    return pl.pallas_call(
        kern,
        grid=(nseq // nb, nc),
        in_specs=[
            seq(0), seq(1), seq(2),
            pl.BlockSpec((nb, t, 2 * LANES), lambda b, c: (r0 + b, c, 0)),
            pl.BlockSpec(gbias.shape, lambda b, c: (0, 0)), st_c, st_n, st_m,
        ],
        out_specs=[pl.BlockSpec((nb, t, wd), lambda b, c: (b, c, 0)), st_c, st_n, st_m],
        out_shape=[
            jax.ShapeDtypeStruct((nseq, t * nc, wd), BF16),
            jax.ShapeDtypeStruct((nseq, N_HEADS, M_DV, M_DK), F32),
            jax.ShapeDtypeStruct((nseq, N_HEADS, 1, M_DK), F32),
            jax.ShapeDtypeStruct((nseq, 1, LANES), F32),
        ],
        compiler_params=_cparams(("parallel", "arbitrary")),
        name="mlstm",
    )(qkv3, qkv3, qkv3, gates3, gbias, c0, n0, m0)


def _lru_chunk(x, gr, xs_s, h, cw, cb, wa_ref, ba, wx_ref, bx, lam, emit, spread=lambda: None):
    nb, t, w = x.shape
    ng = t // SUBLANES
    xs_s[:, SUBLANES:SUBLANES + t, :] = x
    xc = cb + cw[CONV_W - 1:CONV_W, :] * x
    for s in range(1, CONV_W):
        xc = xc + cw[CONV_W - 1 - s:CONV_W - s, :] * xs_s[:, SUBLANES - s:SUBLANES - s + t, :]
    xs_s[:, 0:SUBLANES, :] = x[:, t - SUBLANES:, :]
    spread()

    xf = xc.reshape(nb * t, w)
    ra, ri = [], []
    for n in range(LRU_BLOCKS):
        xb = xf[:, n * LRU_BS:(n + 1) * LRU_BS].astype(BF16)
        ra.append(jnp.dot(xb, wa_ref[n], preferred_element_type=F32))
        ri.append(jnp.dot(xb, wx_ref[n], preferred_element_type=F32))
    spread()
    r = _sigmoid(jnp.concatenate(ra, axis=-1) + ba)
    i = _sigmoid(jnp.concatenate(ri, axis=-1) + bx)
    log_a = -LRU_C * r * _softplus(-lam)
    a = jnp.exp(log_a)
    u = jnp.sqrt(-jnp.tanh(log_a) * (1.0 + a * a)) * (i * xf)
    spread()

    rid = lax.broadcasted_iota(jnp.int32, (1, SUBLANES, 1), 1)
    a = a.reshape(nb * ng, SUBLANES, w)
    u = u.reshape(nb * ng, SUBLANES, w)
    for s in (1, 2, 4):
        a_sh = jnp.where(rid >= s, pltpu.roll(a, s, axis=1), 1.0)
        u_sh = jnp.where(rid >= s, pltpu.roll(u, s, axis=1), 0.0)
        u = a * u_sh + u
        a = a * a_sh
    a = a.reshape(nb, ng, SUBLANES, w)
    u = u.reshape(nb, ng, SUBLANES, w)
    gr4 = gr.reshape(nb, ng, SUBLANES, w)
    for g in range(ng):
        if g == ng // 2:
            spread()
        hg = a[:, g] * h + u[:, g]
        h = hg[:, SUBLANES - 1:SUBLANES]
        emit(g, hg, hg * _silu(gr4[:, g]))
    return h


def _rglru_kernel(x_ref, gr_ref, cw_ref, cb_ref, wa_ref, ba_ref, wx_ref, bx_ref, lam_ref, h0_ref, conv0_ref,
                  hl_ref, hlast_ref, h_s, xs_s, *, nb, t, l_valid):
    c = pl.program_id(1)

    @pl.when(c == 0)
    def _():
        h_s[...] = h0_ref[...]
        xs_s[:, 0:SUBLANES, :] = conv0_ref[...]

    g_last, r_last = divmod((l_valid - 1) % t, SUBLANES)

    def emit(g, hg, gated):
        hl_ref[:, g * SUBLANES:(g + 1) * SUBLANES, :] = gated.astype(hl_ref.dtype)
        if g == g_last:
            @pl.when(c == (l_valid - 1) // t)
            def _():
                hlast_ref[...] = hg[:, r_last:r_last + 1]

    h_s[...] = _lru_chunk(x_ref[...], gr_ref[...], xs_s, h_s[...], cw_ref[...], cb_ref[...], wa_ref, ba_ref[...],
                          wx_ref, bx_ref[...], lam_ref[...], emit)


def _rglru(rest3, row0, nseq, tiles, nc, l_valid, cw, cb, wa, ba, wx, bx, lam, h0, conv0):
    nb, t = tiles
    assert nseq % nb == 0 and row0 % nb == 0
    w = LRU_WIDTH
    r0 = row0 // nb
    full2 = lambda a: pl.BlockSpec(a.shape, lambda b, c: (0, 0))
    full3 = lambda a: pl.BlockSpec(a.shape, lambda b, c: (0, 0, 0))
    kern = functools.partial(_rglru_kernel, nb=nb, t=t, l_valid=l_valid)
    return pl.pallas_call(
        kern,
        grid=(nseq // nb, nc),
        in_specs=[
            pl.BlockSpec((nb, t, w), lambda b, c: (r0 + b, c, 2)),
            pl.BlockSpec((nb, t, w), lambda b, c: (r0 + b, c, 3)),
            full2(cw), full2(cb), full3(wa), full2(ba), full3(wx), full2(bx), full2(lam),
            pl.BlockSpec((nb, 1, w), lambda b, c: (b, 0, 0)),
            pl.BlockSpec((nb, SUBLANES, w), lambda b, c: (b, 0, 0)),
        ],
        out_specs=[
            pl.BlockSpec((nb, t, w), lambda b, c: (b, c, 0)),
            pl.BlockSpec((nb, 1, w), lambda b, c: (b, 0, 0)),
        ],
        out_shape=[
            jax.ShapeDtypeStruct((nseq, t * nc, w), BF16),
            jax.ShapeDtypeStruct((nseq, 1, w), F32),
        ],
        scratch_shapes=[pltpu.VMEM((nb, 1, w), F32), pltpu.VMEM((nb, SUBLANES + t, w), F32)],
        compiler_params=_cparams(("parallel", "arbitrary")),
        name="rglru",
    )(rest3, rest3, cw, cb, wa, ba, wx, bx, lam, h0, conv0)


def _gla_kernel(q_ref, k_ref, v_ref, al_ref, aup_ref, ab_ref, s0_ref, o_ref, s_ref, *, nb, t, nc, l_valid):
    c = pl.program_id(1)
    if nc == 1:
        s_in = s0_ref
    else:
        s_in = s_ref

        @pl.when(c == 0)
        def _():
            s_ref[...] = s0_ref[...]

    pos = c * t + lax.broadcasted_iota(jnp.int32, (t, 1), 0)
    valid = pos < l_valid
    row = lax.broadcasted_iota(jnp.int32, (t, t), 0)
    col = lax.broadcasted_iota(jnp.int32, (t, t), 1)
    causal = row >= col
    aup = aup_ref[...]
    ab = ab_ref[...]
    heads = [(h, slice(h * G_DK, (h + 1) * G_DK), slice(h * G_DV, (h + 1) * G_DV)) for h in range(N_HEADS)]

    def group(rows):
        pairs = [(i, r, h, ks, vs) for i, r in enumerate(rows) for h, ks, vs in heads]
        dec = []
        for r in rows:
            pre = _dot(al_ref[r], aup) + ab
            lg = jnp.where(valid, _log_sigmoid(pre) / G_TAU, 0.0)
            b = _cumsum_rows(lg)
            b_last = b[t - 1:t, :]
            k = jnp.where(valid, k_ref[r].astype(F32), 0.0)
            qd = ((q_ref[r].astype(F32) * (G_DK ** -0.5)) * jnp.exp(b)).astype(BF16)
            kn = (k * jnp.exp(-b)).astype(BF16)
            kr = (k * jnp.exp(b_last - b)).astype(BF16)
            dec.append((qd, kn, kr, _row_to_col(jnp.exp(b_last))))
        att = {}
        for i, r, h, ks, vs in pairs:
            qd, kn, _, _ = dec[i]
            att[i, h] = jnp.where(causal, _dot_nt(qd[:, ks], kn[:, ks]), 0.0).astype(BF16)
        for i, r, h, ks, vs in pairs:
            qd, _, kr, ebl_col = dec[i]
            s_h = s_in[r, h]
            v = v_ref[r, :, vs]
            o_ref[r, :, vs] = (_dot(qd[:, ks], s_h) + _dot(att[i, h], v)).astype(o_ref.dtype)
            s_ref[r, h] = ebl_col[ks, :] * s_h + _dot_tn(kr[:, ks], v)

    _group_loop(nb, group)


def _gla(qkv3, al3, row0, nseq, tiles, nc, l_valid, aup, ab, s0):
    nb, t = tiles
    assert nseq % nb == 0 and row0 % nb == 0
    r0 = row0 // nb
    full2 = lambda a: pl.BlockSpec(a.shape, lambda b, c: (0, 0))
    st = pl.BlockSpec((nb, N_HEADS, G_DK, G_DV), lambda b, c: (b, 0, 0, 0))
    kern = functools.partial(_gla_kernel, nb=nb, t=t, nc=nc, l_valid=l_valid)
    return pl.pallas_call(
        kern,
        grid=(nseq // nb, nc),
        in_specs=[
            pl.BlockSpec((nb, t, G_KW), lambda b, c: (r0 + b, c, 0)),
            pl.BlockSpec((nb, t, G_KW), lambda b, c: (r0 + b, c, 1)),
            pl.BlockSpec((nb, t, G_VW), lambda b, c: (r0 + b, c, 1)),
            pl.BlockSpec((nb, t, LANES), lambda b, c: (r0 + b, c, 0)),
            full2(aup), full2(ab), st,
        ],
        out_specs=[pl.BlockSpec((nb, t, G_VW), lambda b, c: (b, c, 0)), st],
        out_shape=[
            jax.ShapeDtypeStruct((nseq, t * nc, G_VW), BF16),
            jax.ShapeDtypeStruct((nseq, N_HEADS, G_DK, G_DV), F32),
        ],
        compiler_params=_cparams(("parallel", "arbitrary")),
        name="gla",
    )(qkv3, qkv3, qkv3, al3, aup, ab, s0)


def _even_layer_kernel(x_ref, pre_ref, post_ref, win_a_ref, win_b_ref, wgate_ref, gbias_ref, ng_ref, cw_ref, cb_ref, wa_ref,
                       ba_ref, wx_ref, bx_ref, lam_ref, w1_ref, w2_ref, c0_ref, n0_ref, m0_ref, h0_ref, conv0_ref,
                       y_ref, c_ref, n_ref, m_ref, hlast_ref, conv_ref,
                       qkv_s, rest_s, gate_s, hn_s, hm_s, hl_s, xkeep_s, hlru_s, xs_s, *, rows, tiles_per_seq):
    g = pl.program_id(0)
    slot = g % 2
    prev = 1 - slot
    t = MLSTM_TILES["main"][1]
    e_lo = 3 * M_WIDTH
    e_a = win_a_ref.shape[1]
    e = e_a + win_b_ref.shape[1]
    n_chunks = rows // t
    cols_per_chunk = e // n_chunks

    @pl.when(g == 0)
    def _():
        qkv_s[1] = jnp.zeros(qkv_s.shape[1:], qkv_s.dtype)
        rest_s[1] = jnp.zeros(rest_s.shape[1:], rest_s.dtype)
        gate_s[1] = jnp.zeros(gate_s.shape[1:], gate_s.dtype)
        xkeep_s[...] = jnp.zeros(xkeep_s.shape, xkeep_s.dtype)

    @pl.when(jnp.maximum(g - 1, 0) % tiles_per_seq == 0)
    def _():
        c_ref[...] = c0_ref[...]
        n_ref[...] = n0_ref[...]
        m_ref[...] = m0_ref[...]
        hlru_s[...] = h0_ref[...]
        xs_s[:, 0:SUBLANES, :] = conv0_ref[...]

    x = x_ref[0]
    hn_s[...] = (x * lax.rsqrt(jnp.mean(x * x, axis=-1, keepdims=True) + EPS) * pre_ref[...]).astype(BF16)
    gate_s[slot] = jnp.dot(hn_s[...], wgate_ref[...], preferred_element_type=F32)

    def project(c0, c1):
        w_cols = win_a_ref[:, c0:c1] if c1 <= e_a else win_b_ref[:, c0 - e_a:c1 - e_a]
        acc = jnp.dot(hn_s[...], w_cols, preferred_element_type=F32)
        if c0 < e_lo:
            hi = min(c1, e_lo)
            qkv_s[slot, :, c0:hi] = acc[:, :hi - c0].astype(BF16)
        if c1 > e_lo:
            lo = max(c0, e_lo)
            rest_s[slot, :, lo - e_lo:c1 - e_lo] = acc[:, lo - c0:]

    row = lax.broadcasted_iota(jnp.int32, (t, t), 0)
    col = lax.broadcasted_iota(jnp.int32, (t, t), 1)
    causal = row >= col
    lane = lax.broadcasted_iota(jnp.int32, (1, LANES), 1)
    gbias = gbias_ref[...]
    heads = [(h, slice(h * M_DK, (h + 1) * M_DK)) for h in range(N_HEADS)]
    piece = 512
    pieces = [(c0, c0 + piece) for c0 in range(0, e, piece)]
    per_chunk = 6
    assert e % piece == 0 and e_lo % piece == 0 and e_a % piece == 0 and len(pieces) >= n_chunks * per_chunk

    w = LRU_WIDTH
    lru_h = [hlru_s[...]]
    subs_per_chunk = rows // LRU_SUB // n_chunks

    def lru_sub(i):
        r0 = i * LRU_SUB

        def emit(gi, hg, gated):
            hl_s[r0 + gi * SUBLANES:r0 + (gi + 1) * SUBLANES, :] = gated[0].astype(hl_s.dtype)

        lru_h[0] = _lru_chunk(rest_s[prev, r0:r0 + LRU_SUB, 2 * w:3 * w][None],
                              rest_s[prev, r0:r0 + LRU_SUB, 3 * w:4 * w][None], xs_s, lru_h[0],
                              cw_ref[...], cb_ref[...], wa_ref, ba_ref[...], wx_ref, bx_ref[...], lam_ref[...], emit)

    n_gaps = 4

    def lru_between(j, gap):
        for i in range(-(-gap * subs_per_chunk // n_gaps), -(-(gap + 1) * subs_per_chunk // n_gaps)):
            lru_sub(j * subs_per_chunk + i)

    for j in range(n_chunks):
        rs = slice(j * t, (j + 1) * t)
        todo = pieces[j * per_chunk:(j + 1) * per_chunk]
        gt = gate_s[prev, rs, :] + gbias
        li = gt[:, :LANES]
        b = _cumsum_rows(_log_sigmoid(gt[:, LANES:]))
        project(*todo[0])
        lru_between(j, 0)
        r_t = _transpose_rows(li - b)[:, :t]
        m_vec = m_ref[0]
        st = []
        for h, sl in heads:
            b_col = b[:, h:h + 1]
            inter = b_col + m_vec[:, h:h + 1]
            dmat = jnp.where(causal, b_col + r_t[h:h + 1, :], NEG)
            mt = jnp.maximum(inter, jnp.max(dmat, axis=-1, keepdims=True))
            k = qkv_s[prev, rs, M_WIDTH + h * M_DK:M_WIDTH + (h + 1) * M_DK] * (M_DK ** -0.5)
            s = _dot_nt(qkv_s[prev, rs, sl], k) * jnp.exp(dmat - mt)
            st.append((inter, mt, k, s))
        project(*todo[1])
        lru_between(j, 1)
        project(*todo[2])
        for h, sl in heads:
            inter, mt, k, s = st[h]
            q = qkv_s[prev, rs, sl]
            v = qkv_s[prev, rs, 2 * M_WIDTH + h * M_DV:2 * M_WIDTH + (h + 1) * M_DV]
            w_inter = jnp.exp(inter - mt)
            num = w_inter * _dot_nt(q, c_ref[0, h]) + _dot(s, v)
            den = (w_inter * jnp.sum(q.astype(F32) * n_ref[0, h], axis=-1, keepdims=True)
                   + jnp.sum(s, axis=-1, keepdims=True))
            hm_s[rs, sl] = (num / jnp.maximum(jnp.abs(den), jnp.exp(-mt))).astype(hm_s.dtype)
        project(*todo[3])
        lru_between(j, 2)
        project(*todo[5])
        m_next = m_vec
        for h, sl in heads:
            inter, mt, k, _ = st[h]
            v = qkv_s[prev, rs, 2 * M_WIDTH + h * M_DV:2 * M_WIDTH + (h + 1) * M_DV]
            b_col = b[:, h:h + 1]
            m_new = mt[t - 1:t, :]
            w_c = jnp.exp(inter[t - 1:t, :] - m_new)
            w_k = jnp.exp(b_col[t - 1:t, :] - b_col + li[:, h:h + 1] - m_new)
            c_ref[0, h] = w_c * c_ref[0, h] + _dot_tn(v.astype(F32) * w_k, k)
            n_ref[0, h] = w_c * n_ref[0, h] + jnp.sum(k.astype(F32) * w_k, axis=0, keepdims=True)
            m_next = jnp.where(lane == h, m_new, m_next)
        m_ref[0] = m_next
        lru_between(j, 3)
        project(*todo[4])

    for p in pieces[n_chunks * per_chunk:]:
        project(*p)
    h_end = lru_h[0]
    hlru_s[...] = h_end
    hlast_ref[...] = h_end
    conv_ref[...] = xs_s[:, 0:SUBLANES, :]

    hm = _sigmoid(rest_s[prev, :, 0:w]) * hm_s[...].astype(F32)
    hm = _head_rms(hm, N_HEADS) * ng_ref[...] * _silu(rest_s[prev, :, w:2 * w])
    out = jnp.dot(hm.astype(BF16), w1_ref[...], preferred_element_type=F32)
    out = out + jnp.dot(hl_s[...], w2_ref[...], preferred_element_type=F32)
    nrm = out * lax.rsqrt(jnp.mean(out * out, axis=-1, keepdims=True) + EPS) * post_ref[...]
    y_ref[0] = xkeep_s[...] + nrm
    xkeep_s[...] = x


def _even_layer_main(x3, pre, post, w_a, e_a, w_b, w_gate, gbias, ng, lru, w1, w2, c0, n0, m0, h0, conv0, rows):
    nseq, seq, d = x3.shape
    assert seq % rows == 0 and rows % MLSTM_TILES["main"][1] == 0
    tps = seq // rows
    n_tiles = nseq * tps
    e = e_a + w_b.shape[1]
    e_lo = 3 * M_WIDTH
    w = LRU_WIDTH
    const = lambda a: pl.BlockSpec(a.shape, lambda g: (0,) * a.ndim, pipeline_mode=pl.Buffered(1))
    cur = lambda g: jnp.minimum(g, n_tiles - 1)
    old = lambda g: jnp.maximum(g - 1, 0)
    per_seq = lambda *blk: pl.BlockSpec((1,) + blk, lambda g: (old(g) // tps,) + (0,) * len(blk))
    st_c, st_n, st_m = per_seq(N_HEADS, M_DV, M_DK), per_seq(N_HEADS, 1, M_DK), per_seq(1, LANES)
    st_h, st_conv = per_seq(1, w), per_seq(SUBLANES, w)
    out_states = [st_c, st_n, st_m, st_h, st_conv]
    if c0.shape[0] == nseq:
        in_states = out_states
    else:
        in_states = [pl.BlockSpec(s.block_shape, lambda g, nd=len(s.block_shape): (0,) * nd) for s in out_states]
    kern = functools.partial(_even_layer_kernel, rows=rows, tiles_per_seq=tps)
    return pl.pallas_call(
        kern,
        grid=(n_tiles + 1,),
        in_specs=[
            pl.BlockSpec((1, rows, d), lambda g: (cur(g) // tps, cur(g) % tps, 0)),
            const(pre), const(post),
            pl.BlockSpec((d, e_a), lambda g: (0, 0), pipeline_mode=pl.Buffered(1)), const(w_b),
            const(w_gate), const(gbias), const(ng),
            *[const(a) for a in lru], const(w1), const(w2), *in_states,
        ],
        out_specs=[
            pl.BlockSpec((1, rows, d), lambda g: (old(g) // tps, old(g) % tps, 0)),
            st_c, st_n, st_m, st_h, st_conv,
        ],
        out_shape=[
            jax.ShapeDtypeStruct((nseq, seq, d), F32),
            jax.ShapeDtypeStruct((nseq, N_HEADS, M_DV, M_DK), F32),
            jax.ShapeDtypeStruct((nseq, N_HEADS, 1, M_DK), F32),
            jax.ShapeDtypeStruct((nseq, 1, LANES), F32),
            jax.ShapeDtypeStruct((nseq, 1, w), F32),
            jax.ShapeDtypeStruct((nseq, SUBLANES, w), F32),
        ],
        scratch_shapes=[
            pltpu.VMEM((2, rows, e_lo), BF16),
            pltpu.VMEM((2, rows, e - e_lo), F32),
            pltpu.VMEM((2, rows, 2 * LANES), F32),
            pltpu.VMEM((rows, d), BF16),
            pltpu.VMEM((rows, M_WIDTH), BF16),
            pltpu.VMEM((rows, w), BF16),
            pltpu.VMEM((rows, d), F32),
            pltpu.VMEM((1, 1, w), F32),
            pltpu.VMEM((1, SUBLANES + LRU_SUB, w), F32),
        ],
        compiler_params=_cparams(("arbitrary",)),
        name="even_layer",
    )(x3, pre, post, w_a, w_b, w_gate, gbias, ng, *lru, w1, w2, c0, n0, m0, h0, conv0)


def _gla_layer_kernel(x_ref, pre_ref, post_ref, win_ref, wlow_ref, aup_ref, ab_ref, ng_ref, wout_ref, s0_ref,
                      y_ref, s_ref, qkv_s, r_s, al_s, hn_s, o_s, xkeep_s, *, rows, tiles_per_seq):
    g = pl.program_id(0)
    slot = g % 2
    prev = 1 - slot
    t = GLA_TILES["main"][1]
    e_lo = 2 * G_KW + G_VW
    e = e_lo + G_VW
    n_chunks = rows // t
    cols_per_chunk = e // n_chunks

    @pl.when(g == 0)
    def _():
        qkv_s[1] = jnp.zeros(qkv_s.shape[1:], qkv_s.dtype)
        r_s[1] = jnp.zeros(r_s.shape[1:], r_s.dtype)
        al_s[1] = jnp.zeros(al_s.shape[1:], al_s.dtype)
        xkeep_s[...] = jnp.zeros(xkeep_s.shape, xkeep_s.dtype)

    @pl.when(jnp.maximum(g - 1, 0) % tiles_per_seq == 0)
    def _():
        s_ref[...] = s0_ref[...]

    x = x_ref[0]
    hn_s[...] = (x * lax.rsqrt(jnp.mean(x * x, axis=-1, keepdims=True) + EPS) * pre_ref[...]).astype(BF16)
    al_s[slot] = jnp.dot(hn_s[...], wlow_ref[...], preferred_element_type=F32)

    row = lax.broadcasted_iota(jnp.int32, (t, t), 0)
    col = lax.broadcasted_iota(jnp.int32, (t, t), 1)
    causal = row >= col
    aup = aup_ref[...]
    ab = ab_ref[...]
    heads = [(h, slice(h * G_DK, (h + 1) * G_DK), slice(h * G_DV, (h + 1) * G_DV)) for h in range(N_HEADS)]

    def project(c0, c1):
        acc = jnp.dot(hn_s[...], win_ref[:, c0:c1], preferred_element_type=F32)
        if c0 < e_lo:
            hi = min(c1, e_lo)
            qkv_s[slot, :, c0:hi] = acc[:, :hi - c0].astype(BF16)
        if c1 > e_lo:
            lo = max(c0, e_lo)
            r_s[slot, :, lo - e_lo:c1 - e_lo] = acc[:, lo - c0:]

    n_sub = 3
    sub = cols_per_chunk // n_sub
    for j in range(n_chunks):
        c0 = j * cols_per_chunk
        rs = slice(j * t, (j + 1) * t)
        pre = _dot(al_s[prev, rs, :], aup) + ab
        project(c0, c0 + sub)
        b = _cumsum_rows(_log_sigmoid(pre) / G_TAU)
        b_last = b[t - 1:t, :]
        project(c0 + sub, c0 + 2 * sub)
        k = qkv_s[prev, rs, G_KW:2 * G_KW].astype(F32)
        qd = ((qkv_s[prev, rs, 0:G_KW].astype(F32) * (G_DK ** -0.5)) * jnp.exp(b)).astype(BF16)
        kn = (k * jnp.exp(-b)).astype(BF16)
        kr = (k * jnp.exp(b_last - b)).astype(BF16)
        ebl_col = _row_to_col(jnp.exp(b_last))
        att = [jnp.where(causal, _dot_nt(qd[:, ks], kn[:, ks]), 0.0).astype(BF16) for _, ks, _ in heads]
        project(c0 + 2 * sub, c0 + cols_per_chunk)
        for h, ks, vs in heads:
            s_h = s_ref[0, h]
            v = qkv_s[prev, rs, 2 * G_KW + h * G_DV:2 * G_KW + (h + 1) * G_DV]
            o_s[rs, vs] = (_dot(qd[:, ks], s_h) + _dot(att[h], v)).astype(o_s.dtype)
            s_ref[0, h] = ebl_col[ks, :] * s_h + _dot_tn(kr[:, ks], v)

    o = _head_rms(o_s[...].astype(F32), N_HEADS) * ng_ref[...] * _silu(r_s[prev])
    out = jnp.dot(o.astype(BF16), wout_ref[...], preferred_element_type=F32)
    nrm = out * lax.rsqrt(jnp.mean(out * out, axis=-1, keepdims=True) + EPS) * post_ref[...]
    y_ref[0] = xkeep_s[...] + nrm
    xkeep_s[...] = x


def _gla_layer(x3, pre, post, w_in, w_low, aup, ab, ng, w_out, s0, rows):
    nseq, seq, d = x3.shape
    assert seq % rows == 0 and rows % GLA_TILES["main"][1] == 0
    tps = seq // rows
    n_tiles = nseq * tps
    e_lo = 2 * G_KW + G_VW
    e = e_lo + G_VW
    const = lambda a: pl.BlockSpec(a.shape, lambda g: (0,) * a.ndim, pipeline_mode=pl.Buffered(1))
    cur = lambda g: jnp.minimum(g, n_tiles - 1)
    old = lambda g: jnp.maximum(g - 1, 0)
    st = pl.BlockSpec((1, N_HEADS, G_DK, G_DV), lambda g: (old(g) // tps, 0, 0, 0))
    st_in = st if s0.shape[0] == nseq else pl.BlockSpec((1, N_HEADS, G_DK, G_DV), lambda g: (0, 0, 0, 0))
    kern = functools.partial(_gla_layer_kernel, rows=rows, tiles_per_seq=tps)
    return pl.pallas_call(
        kern,
        grid=(n_tiles + 1,),
        in_specs=[
            pl.BlockSpec((1, rows, d), lambda g: (cur(g) // tps, cur(g) % tps, 0)),
            const(pre), const(post), pl.BlockSpec((d, e), lambda g: (0, 0), pipeline_mode=pl.Buffered(1)),
            const(w_low), const(aup), const(ab), const(ng), const(w_out), st_in,
        ],
        out_specs=[pl.BlockSpec((1, rows, d), lambda g: (old(g) // tps, old(g) % tps, 0)), st],
        out_shape=[
            jax.ShapeDtypeStruct((nseq, seq, d), F32),
            jax.ShapeDtypeStruct((nseq, N_HEADS, G_DK, G_DV), F32),
        ],
        scratch_shapes=[
            pltpu.VMEM((2, rows, e_lo), BF16),
            pltpu.VMEM((2, rows, e - e_lo), F32),
            pltpu.VMEM((2, rows, LANES), F32),
            pltpu.VMEM((rows, d), BF16),
            pltpu.VMEM((rows, G_VW), BF16),
            pltpu.VMEM((rows, d), F32),
        ],
        compiler_params=_cparams(("arbitrary",)),
        name="gla_layer",
    )(x3, pre, post, w_in, w_low, aup, ab, ng, w_out, s0)


def _pad_lanes(a, width):
    return jnp.pad(a, [(0, 0)] * (a.ndim - 1) + [(0, width - a.shape[-1])])


def _even_layer(xm, xs, nbp, seq, nbs, ls, st, pre_g, post_g, w_in, w_out, b_i, b_f, m_norm_g,
                conv_w, conv_b, wa, ba, wx, bx, lam):
    c0s, n0s, m0s, h0s, conv0s = st
    e_a = 4 * M_WIDTH
    n_gate = 2 * N_HEADS
    w_bf = w_in.astype(BF16)
    w_b = w_bf[:, e_a + n_gate:]
    wig, wfg = w_in[:, e_a:e_a + N_HEADS], w_in[:, e_a + N_HEADS:e_a + n_gate]
    w_gate = jnp.concatenate([_pad_lanes(wig, LANES), _pad_lanes(wfg, LANES)], axis=1).astype(BF16)
    gbias = jnp.concatenate([_pad_lanes(b_i[None], LANES), _pad_lanes(b_f[None], LANES)], axis=1)
    ng = m_norm_g[None]
    post = post_g[None]
    cw, cb = conv_w, conv_b[None]
    wab, wxb = wa.astype(BF16), wx.astype(BF16)
    bav, bxv, lamv = ba[None], bx[None], lam[None]
    w_o1, w_o2 = w_out[:M_WIDTH].astype(BF16), w_out[M_WIDTH:].astype(BF16)
    e_lo = 3 * M_WIDTH
    e_hi = e_a + w_b.shape[1] - e_lo
    lru = (cw, cb, wab, bav, wxb, bxv, lamv)

    qs, ps, gs = _norm_proj(xs, pre_g, w_bf, e_a, w_b, w_gate, e_lo, xs.shape[0])
    mrow = nbs * SAMPLE_PAD // N_META
    xr_cols = slice(2 * LRU_WIDTH, 3 * LRU_WIDTH)

    zc = jnp.zeros((1, N_HEADS, M_DV, M_DK), F32)
    zn = jnp.zeros((1, N_HEADS, 1, M_DK), F32)
    zm = jnp.zeros((1, 1, LANES), F32)
    zh = jnp.zeros((1, 1, LRU_WIDTH), F32)
    zconv = jnp.zeros((1, SUBLANES, LRU_WIDTH), F32)
    qs_meta = qs.reshape(-1, N_META, e_lo)
    ps_meta = ps.reshape(-1, N_META, e_hi)
    gs_meta = gs.reshape(-1, N_META, 2 * LANES)
    h_meta, c1, n1, m1 = _mlstm(qs_meta, gs_meta, mrow, 1, MLSTM_TILES["meta"], 1, N_META, gbias, zc, zn, zm)
    hl_meta, h1 = _rglru(ps_meta, mrow, 1, LRU_TILES["meta"], 1, N_META, *lru, zh, zconv)
    conv1 = ps_meta[mrow:mrow + 1, N_META - SUBLANES:, xr_cols]

    ym3, pc, pn, pmm, ph, conv_tail = _even_layer_main(
        xm.reshape(nbp, seq, -1), pre_g[None], post, w_bf, e_a, w_b, w_gate, gbias, ng, lru, w_o1, w_o2,
        c1, n1, m1, h1, conv1, FUSED_ROW_TILE)
    ym = ym3.reshape(xm.shape)
    pconv = conv_tail[:, SUBLANES - (CONV_W - 1):, :]

    qs_s = qs.reshape(-1, SAMPLE_PAD, e_lo)
    ps_s = ps.reshape(-1, SAMPLE_PAD, e_hi)
    gs_s = gs.reshape(-1, SAMPLE_PAD, 2 * LANES)
    h_s, sc, sn, sm = _mlstm(qs_s, gs_s, 0, nbs, MLSTM_TILES["sample"], 1, ls, gbias,
                             c0s, n0s[:, :, None, :], _pad_lanes(m0s, LANES)[:, None, :])
    conv0p = jnp.pad(conv0s, ((0, 0), (SUBLANES - (CONV_W - 1), 0), (0, 0)))
    hl_s, sh = _rglru(ps_s, 0, nbs, LRU_TILES["sample"], 1, ls, *lru, h0s[:, None, :], conv0p)
    xr_s = ps_s[:nbs, :ls, xr_cols]
    sconv = jnp.concatenate([conv0s, xr_s], axis=1)[:, -(CONV_W - 1):]

    h_small = jnp.concatenate([h_s.reshape(-1, M_WIDTH), h_meta.reshape(-1, M_WIDTH)], axis=0)
    hl_small = jnp.concatenate([hl_s.reshape(-1, LRU_WIDTH), hl_meta.reshape(-1, LRU_WIDTH)], axis=0)
    ys = _out_even(xs, post, h_small, ps, ng, hl_small, w_o1, w_o2, _row_tile(xs.shape[0], OUT_ROW_TILE))

    p_state = (pc, pn[:, :, 0, :], pmm[:, 0, :N_HEADS], ph[:, 0, :], pconv)
    s_state = (sc, sn[:, :, 0, :], sm[:, 0, :N_HEADS], sh[:, 0, :], sconv)
    return ym, ys, p_state, s_state


def _odd_layer(xm, xs, nbp, seq, nbs, ls, s0s, pre_g, post_g, w_in, w_out, a_up, a_b, g_norm_g):
    e_lo = 2 * G_KW + G_VW
    e_a = 2 * G_KW + 2 * G_VW
    w_bf = w_in.astype(BF16)
    w_low = _pad_lanes(w_in[:, 2 * G_KW + 2 * G_VW:], LANES).astype(BF16)
    aup = jnp.pad(a_up, ((0, LANES - G_RANK), (0, 0))).astype(BF16)
    ab = a_b[None]
    ng = g_norm_g[None]
    post = post_g[None]
    w_o = w_out.astype(BF16)

    qs, rs, as_ = _norm_proj(xs, pre_g, w_bf, e_a, None, w_low, e_lo, xs.shape[0])
    mrow = nbs * SAMPLE_PAD // N_META

    zs = jnp.zeros((1, N_HEADS, G_DK, G_DV), F32)
    o_meta, s1 = _gla(qs.reshape(-1, N_META, e_lo), as_.reshape(-1, N_META, LANES),
                      mrow, 1, GLA_TILES["meta"], 1, N_META, aup, ab, zs)
    ym3, p_s = _gla_layer(xm.reshape(nbp, seq, -1), pre_g[None], post, w_bf, w_low, aup, ab, ng, w_o, s1,
                          GLA_ROW_TILE)
    ym = ym3.reshape(xm.shape)
    o_s, s_s = _gla(qs.reshape(-1, SAMPLE_PAD, e_lo), as_.reshape(-1, SAMPLE_PAD, LANES),
                    0, nbs, GLA_TILES["sample"], 1, ls, aup, ab, s0s)

    o_small = jnp.concatenate([o_s.reshape(-1, G_VW), o_meta.reshape(-1, G_VW)], axis=0)
    ys = _out_odd(xs, post, o_small, rs, ng, w_o, _row_tile(xs.shape[0], OUT_ROW_TILE))
    return ym, ys, p_s, s_s


def kernel(x_prompt, x_sample, state_mlstm_C, state_mlstm_n, state_mlstm_m, state_rglru_h, state_rglru_conv,
           state_gla_S, meta_tokens, pre_norm_a, post_norm_a, w_in_a, w_out_a, mlstm_b_i, mlstm_b_f, mlstm_norm,
           conv_w, conv_b, lru_w_a, lru_b_a, lru_w_x, lru_b_x, lru_lambda, pre_norm_c, post_norm_c, w_in_c,
           w_out_c, gla_alpha_up, gla_alpha_b, gla_norm):
    nbp, seq, d = x_prompt.shape
    nbs, ls, _ = x_sample.shape
    depth = pre_norm_a.shape[0] + pre_norm_c.shape[0]
    assert ls >= CONV_W - 1 and ls <= SAMPLE_PAD and N_META % SUBLANES == 0 and nbs * SAMPLE_PAD % N_META == 0

    xm = x_prompt.reshape(nbp * seq, d)
    xs_pad = jnp.pad(x_sample, ((0, 0), (0, SAMPLE_PAD - ls), (0, 0))).reshape(nbs * SAMPLE_PAD, d)
    xs = jnp.concatenate([xs_pad, meta_tokens.astype(x_prompt.dtype)], axis=0)

    p_lists = [[] for _ in range(6)]
    s_lists = [[] for _ in range(6)]
    for layer in range(depth):
        j = layer // 2
        if layer % 2 == 0:
            st = (state_mlstm_C[j], state_mlstm_n[j], state_mlstm_m[j], state_rglru_h[j], state_rglru_conv[j])
            xm, xs, pst, sst = _even_layer(
                xm, xs, nbp, seq, nbs, ls, st, pre_norm_a[j], post_norm_a[j], w_in_a[j], w_out_a[j],
                mlstm_b_i[j], mlstm_b_f[j], mlstm_norm[j], conv_w[j], conv_b[j], lru_w_a[j], lru_b_a[j],
                lru_w_x[j], lru_b_x[j], lru_lambda[j])
            for i in range(5):
                p_lists[i].append(pst[i])
                s_lists[i].append(sst[i])
        else:
            xm, xs, p_s, s_s = _odd_layer(
                xm, xs, nbp, seq, nbs, ls, state_gla_S[j], pre_norm_c[j], post_norm_c[j], w_in_c[j], w_out_c[j],
                gla_alpha_up[j], gla_alpha_b[j], gla_norm[j])
            p_lists[5].append(p_s)
            s_lists[5].append(s_s)

    y_prompt = xm.reshape(nbp, seq, d)
    y_sample = xs[:nbs * SAMPLE_PAD].reshape(nbs, SAMPLE_PAD, d)[:, :ls]
    return (y_prompt, y_sample) + tuple(jnp.stack(l) for l in p_lists) + tuple(jnp.stack(l) for l in s_lists)
```

```python
import functools

import jax
import jax.numpy as jnp
from jax import lax
from jax.experimental import pallas as pl
from jax.experimental.pallas import tpu as pltpu

F32 = jnp.float32
BF16 = jnp.bfloat16

D_MODEL = 1024
N_META = 16
EPS = 1e-6
N_HEADS = 4
M_DK = 256
M_DV = 256
M_WIDTH = N_HEADS * M_DV
LRU_WIDTH = 1024
LRU_BLOCKS = 8
LRU_BS = LRU_WIDTH // LRU_BLOCKS
CONV_W = 4
LRU_C = 8.0
G_DK = 256
G_DV = 512
G_KW = N_HEADS * G_DK
G_VW = N_HEADS * G_DV
G_RANK = 16
G_TAU = 16.0

LANES = 128
SUBLANES = 8
NEG = -1e30
SAMPLE_PAD = SUBLANES
VMEM_LIMIT = 56 * 1024 * 1024
COL_TILE = 1024

MLSTM_TILES = {"meta": (1, N_META), "main": (1, 128), "sample": (8, SAMPLE_PAD)}
GLA_TILES = {"meta": (1, N_META), "main": (1, 64), "sample": (4, SAMPLE_PAD)}
LRU_TILES = {"meta": (1, N_META), "sample": (32, SAMPLE_PAD)}
OUT_ROW_TILE = 512
FUSED_ROW_TILE = 256
GLA_ROW_TILE = 256
LRU_GAPS = (1, 2)
LRU_SUB = 64


def _row_tile(n, cap):
    return next(tm for tm in range(min(n, cap) // SUBLANES * SUBLANES, 0, -SUBLANES) if n % tm == 0)


def _cparams(sem):
    return pltpu.CompilerParams(dimension_semantics=sem, vmem_limit_bytes=VMEM_LIMIT)


def _sigmoid(x):
    return 0.5 * jnp.tanh(0.5 * x) + 0.5


def _silu(x):
    return x * _sigmoid(x)


def _log_sigmoid(x):
    return jnp.minimum(x, 0.0) - jnp.log1p(jnp.exp(-jnp.abs(x)))


def _softplus(x):
    return jnp.maximum(x, 0.0) + jnp.log1p(jnp.exp(-jnp.abs(x)))


def _dot(a, b):
    return jnp.dot(a.astype(BF16), b.astype(BF16), preferred_element_type=F32)


def _dot_nt(a, b):
    return lax.dot_general(a.astype(BF16), b.astype(BF16), (((1,), (1,)), ((), ())),
                           preferred_element_type=F32)


def _dot_tn(a, b):
    return lax.dot_general(a.astype(BF16), b.astype(BF16), (((0,), (0,)), ((), ())),
                           preferred_element_type=F32)


def _transpose_rows(x):
    t, n = x.shape
    if t < LANES:
        x = jnp.concatenate([x, jnp.zeros((LANES - t, n), x.dtype)], axis=0)
    return jnp.transpose(x)


def _row_to_col(r):
    return jnp.transpose(jnp.broadcast_to(r, (LANES, r.shape[1])))[:, 0:1]


def _cumsum_rows(x):
    t = x.shape[0]
    if t <= 2 * SUBLANES:
        rid = lax.broadcasted_iota(jnp.int32, (t, 1), 0)
        s = 1
        while s < t:
            x = x + jnp.where(rid >= s, pltpu.roll(x, s, axis=0), 0.0)
            s *= 2
        return x
    row = lax.broadcasted_iota(jnp.int32, (t, t), 0)
    col = lax.broadcasted_iota(jnp.int32, (t, t), 1)
    tri = jnp.where(row >= col, 1.0, 0.0).astype(BF16)
    hi = x.astype(BF16)
    r1 = x - hi.astype(F32)
    mid = r1.astype(BF16)
    lo = (r1 - mid.astype(F32)).astype(BF16)
    acc = jnp.dot(tri, lo, preferred_element_type=F32)
    acc = acc + jnp.dot(tri, mid, preferred_element_type=F32)
    return acc + jnp.dot(tri, hi, preferred_element_type=F32)


def _head_rms(x, nh):
    hd = x.shape[1] // nh
    parts = []
    for h in range(nh):
        xh = x[:, h * hd:(h + 1) * hd]
        parts.append(xh * lax.rsqrt(jnp.mean(xh * xh, axis=-1, keepdims=True) + EPS))
    return jnp.concatenate(parts, axis=-1)


GROUP = 2


def _group_loop(nb, group_fn):
    if nb <= GROUP:
        group_fn(list(range(nb)))
        return

    def body(i, carry):
        group_fn([GROUP * i + r for r in range(GROUP)])
        return carry

    lax.fori_loop(0, nb // GROUP, body, 0)


def _norm_proj_kernel(n_lo, n_a, x_ref, g_ref, wa_ref, wb_ref, ws_ref, olo_ref, ohi_ref, os_ref, hn_ref):
    j = pl.program_id(1)

    @pl.when(j == 0)
    def _():
        x = x_ref[...]
        y = x * lax.rsqrt(jnp.mean(x * x, axis=-1, keepdims=True) + EPS) * g_ref[...]
        hn = y.astype(BF16)
        hn_ref[...] = hn
        os_ref[...] = jnp.dot(hn, ws_ref[...], preferred_element_type=F32)

    @pl.when(j < n_lo)
    def _():
        olo_ref[...] = jnp.dot(hn_ref[...], wa_ref[...], preferred_element_type=F32).astype(olo_ref.dtype)

    @pl.when((j >= n_lo) & (j < n_a))
    def _():
        ohi_ref[...] = jnp.dot(hn_ref[...], wa_ref[...], preferred_element_type=F32)

    @pl.when(j >= n_a)
    def _():
        ohi_ref[...] = jnp.dot(hn_ref[...], wb_ref[...], preferred_element_type=F32)


def _norm_proj(x2d, g, w_a, e_a, w_b, w_small, e_lo, tm):
    n, d = x2d.shape
    tn = COL_TILE
    if w_b is None:
        w_b, e = w_a, e_a
    else:
        e = e_a + w_b.shape[1]
    es = w_small.shape[1]
    n_lo, n_a = e_lo // tn, e_a // tn
    assert e_lo <= e_a and e_a % tn == 0 and e % tn == 0
    return pl.pallas_call(
        functools.partial(_norm_proj_kernel, n_lo, n_a),
        grid=(n // tm, e // tn),
        in_specs=[
            pl.BlockSpec((tm, d), lambda i, j: (i, 0)),
            pl.BlockSpec((1, d), lambda i, j: (0, 0)),
            pl.BlockSpec((d, tn), lambda i, j: (0, jnp.minimum(j, n_a - 1))),
            pl.BlockSpec((d, tn), lambda i, j: (0, jnp.maximum(j - n_a, 0))),
            pl.BlockSpec((d, es), lambda i, j: (0, 0)),
        ],
        out_specs=[
            pl.BlockSpec((tm, tn), lambda i, j: (i, jnp.minimum(j, n_lo - 1))),
            pl.BlockSpec((tm, tn), lambda i, j: (i, jnp.maximum(j - n_lo, 0))),
            pl.BlockSpec((tm, es), lambda i, j: (i, 0)),
        ],
        out_shape=[
            jax.ShapeDtypeStruct((n, e_lo), BF16),
            jax.ShapeDtypeStruct((n, e - e_lo), F32),
            jax.ShapeDtypeStruct((n, es), F32),
        ],
        scratch_shapes=[pltpu.VMEM((tm, d), BF16)],
        compiler_params=_cparams(("parallel", "arbitrary")),
        name="norm_proj",
    )(x2d, g.reshape(1, d), w_a, w_b, w_small)


def _residual_norm(x_ref, g_ref, y_ref, out):
    nrm = out * lax.rsqrt(jnp.mean(out * out, axis=-1, keepdims=True) + EPS) * g_ref[...]
    y_ref[...] = x_ref[...] + nrm


def _out_even_kernel(x_ref, g_ref, h_ref, og_ref, z_ref, ng_ref, hl_ref, w1_ref, w2_ref, y_ref):
    hm = _sigmoid(og_ref[...]) * h_ref[...].astype(F32)
    hm = _head_rms(hm, N_HEADS) * ng_ref[...] * _silu(z_ref[...])
    out = jnp.dot(hm.astype(BF16), w1_ref[...], preferred_element_type=F32)
    out = out + jnp.dot(hl_ref[...], w2_ref[...], preferred_element_type=F32)
    _residual_norm(x_ref, g_ref, y_ref, out)


def _out_even(x2d, g, h_raw, rest, ng, hl, w1, w2, tm):
    n, d = x2d.shape
    row = lambda width, col: pl.BlockSpec((tm, width), lambda i: (i, col))
    const = lambda a: pl.BlockSpec(a.shape, lambda i: (0, 0))
    return pl.pallas_call(
        _out_even_kernel,
        grid=(n // tm,),
        in_specs=[row(d, 0), const(g), row(M_WIDTH, 0), row(M_WIDTH, 0), row(M_WIDTH, 1), const(ng),
                  row(LRU_WIDTH, 0), const(w1), const(w2)],
        out_specs=row(d, 0),
        out_shape=jax.ShapeDtypeStruct((n, d), F32),
        compiler_params=_cparams(("parallel",)),
        name="out_even",
    )(x2d, g, h_raw, rest, rest, ng, hl, w1, w2)


def _out_odd_kernel(x_ref, g_ref, o_ref, r_ref, ng_ref, w_ref, y_ref):
    o = _head_rms(o_ref[...].astype(F32), N_HEADS) * ng_ref[...] * _silu(r_ref[...])
    out = jnp.dot(o.astype(BF16), w_ref[...], preferred_element_type=F32)
    _residual_norm(x_ref, g_ref, y_ref, out)


def _out_odd(x2d, g, o_raw, r, ng, w, tm):
    n, d = x2d.shape
    row = lambda width: pl.BlockSpec((tm, width), lambda i: (i, 0))
    const = lambda a: pl.BlockSpec(a.shape, lambda i: (0, 0))
    return pl.pallas_call(
        _out_odd_kernel,
        grid=(n // tm,),
        in_specs=[row(d), const(g), row(G_VW), row(G_VW), const(ng), const(w)],
        out_specs=row(d),
        out_shape=jax.ShapeDtypeStruct((n, d), F32),
        compiler_params=_cparams(("parallel",)),
        name="out_odd",
    )(x2d, g, o_raw, r, ng, w)


def _mlstm_kernel(q_ref, k_ref, v_ref, gate_ref, gbias_ref, c0_ref, n0_ref, m0_ref,
                  h_ref, c_ref, n_ref, m_ref, *, nb, t, nc, l_valid):
    c = pl.program_id(1)
    if nc == 1:
        c_in, n_in, m_in = c0_ref, n0_ref, m0_ref
    else:
        c_in, n_in, m_in = c_ref, n_ref, m_ref

        @pl.when(c == 0)
        def _():
            c_ref[...] = c0_ref[...]
            n_ref[...] = n0_ref[...]
            m_ref[...] = m0_ref[...]

    pos = c * t + lax.broadcasted_iota(jnp.int32, (t, 1), 0)
    valid = pos < l_valid
    row = lax.broadcasted_iota(jnp.int32, (t, t), 0)
    col = lax.broadcasted_iota(jnp.int32, (t, t), 1)
    causal = row >= col
    lane = lax.broadcasted_iota(jnp.int32, (1, LANES), 1)
    gbias = gbias_ref[...]
    heads = [(h, slice(h * M_DK, (h + 1) * M_DK)) for h in range(N_HEADS)]

    def group(rows):
        pairs = [(i, r, h, sl) for i, r in enumerate(rows) for h, sl in heads]
        gates = []
        for r in rows:
            g = gate_ref[r] + gbias
            li = jnp.where(valid, g[:, :LANES], NEG)
            lf = jnp.where(valid, _log_sigmoid(g[:, LANES:]), 0.0)
            b = _cumsum_rows(lf)
            gates.append((li, b, _transpose_rows(li - b)[:, :t], m_in[r]))
        st = {}
        for i, r, h, sl in pairs:
            li, b, r_t, m_vec = gates[i]
            b_col = b[:, h:h + 1]
            inter = b_col + m_vec[:, h:h + 1]
            dmat = jnp.where(causal, b_col + r_t[h:h + 1, :], NEG)
            mt = jnp.maximum(inter, jnp.max(dmat, axis=-1, keepdims=True))
            k = k_ref[r, :, sl] * (M_DK ** -0.5)
            s = _dot_nt(q_ref[r, :, sl], k) * jnp.exp(dmat - mt)
            st[i, h] = (inter, mt, k, s)
        for i, r, h, sl in pairs:
            inter, mt, k, s = st[i, h]
            q = q_ref[r, :, sl]
            w_inter = jnp.exp(inter - mt)
            num = w_inter * _dot_nt(q, c_in[r, h]) + _dot(s, v_ref[r, :, sl])
            den = (w_inter * jnp.sum(q.astype(F32) * n_in[r, h], axis=-1, keepdims=True)
                   + jnp.sum(s, axis=-1, keepdims=True))
            h_ref[r, :, sl] = (num / jnp.maximum(jnp.abs(den), jnp.exp(-mt))).astype(h_ref.dtype)
        m_next = [gt[3] for gt in gates]
        for i, r, h, sl in pairs:
            li, b, _, _ = gates[i]
            inter, mt, k, _ = st[i, h]
            b_col = b[:, h:h + 1]
            m_new = mt[t - 1:t, :]
            w_c = jnp.exp(inter[t - 1:t, :] - m_new)
            w_k = jnp.exp(b_col[t - 1:t, :] - b_col + li[:, h:h + 1] - m_new)
            c_ref[r, h] = w_c * c_in[r, h] + _dot_tn(v_ref[r, :, sl].astype(F32) * w_k, k)
            n_ref[r, h] = w_c * n_in[r, h] + jnp.sum(k.astype(F32) * w_k, axis=0, keepdims=True)
            m_next[i] = jnp.where(lane == h, m_new, m_next[i])
        for i, r in enumerate(rows):
            m_ref[r] = m_next[i]

    _group_loop(nb, group)


def _mlstm(qkv3, gates3, row0, nseq, tiles, nc, l_valid, gbias, c0, n0, m0):
    nb, t = tiles
    assert nseq % nb == 0 and row0 % nb == 0
    wd = M_WIDTH
    r0 = row0 // nb
    seq = lambda col: pl.BlockSpec((nb, t, wd), lambda b, c: (r0 + b, c, col))
    st_c = pl.BlockSpec((nb, N_HEADS, M_DV, M_DK), lambda b, c: (b, 0, 0, 0))
    st_n = pl.BlockSpec((nb, N_HEADS, 1, M_DK), lambda b, c: (b, 0, 0, 0))
    st_m = pl.BlockSpec((nb, 1, LANES), lambda b, c: (b, 0, 0))
    kern = functools.partial(_mlstm_kernel, nb=nb, t=t, nc=nc, l_valid=l_valid)
    return pl.pallas_call(
        kern,
        grid=(nseq // nb, nc),
        in_specs=[
            seq(0), seq(1), seq(2),
            pl.BlockSpec((nb, t, 2 * LANES), lambda b, c: (r0 + b, c, 0)),
            pl.BlockSpec(gbias.shape, lambda b, c: (0, 0)), st_c, st_n, st_m,
        ],
        out_specs=[pl.BlockSpec((nb, t, wd), lambda b, c: (b, c, 0)), st_c, st_n, st_m],
        out_shape=[
            jax.ShapeDtypeStruct((nseq, t * nc, wd), BF16),
            jax.ShapeDtypeStruct((nseq, N_HEADS, M_DV, M_DK), F32),
            jax.ShapeDtypeStruct((nseq, N_HEADS, 1, M_DK), F32),
            jax.ShapeDtypeStruct((nseq, 1, LANES), F32),
        ],
        compiler_params=_cparams(("parallel", "arbitrary")),
        name="mlstm",
    )(qkv3, qkv3, qkv3, gates3, gbias, c0, n0, m0)


def _lru_chunk(x, gr, xs_s, h, cw, cb, wa_ref, ba, wx_ref, bx, lam, emit, spread=lambda: None):
    nb, t, w = x.shape
    ng = t // SUBLANES
    xs_s[:, SUBLANES:SUBLANES + t, :] = x
    xc = cb + cw[CONV_W - 1:CONV_W, :] * x
    for s in range(1, CONV_W):
        xc = xc + cw[CONV_W - 1 - s:CONV_W - s, :] * xs_s[:, SUBLANES - s:SUBLANES - s + t, :]
    xs_s[:, 0:SUBLANES, :] = x[:, t - SUBLANES:, :]
    spread()

    xf = xc.reshape(nb * t, w)
    ra, ri = [], []
    for n in range(LRU_BLOCKS):
        xb = xf[:, n * LRU_BS:(n + 1) * LRU_BS].astype(BF16)
        ra.append(jnp.dot(xb, wa_ref[n], preferred_element_type=F32))
        ri.append(jnp.dot(xb, wx_ref[n], preferred_element_type=F32))
    spread()
    r = _sigmoid(jnp.concatenate(ra, axis=-1) + ba)
    i = _sigmoid(jnp.concatenate(ri, axis=-1) + bx)
    log_a = -LRU_C * r * _softplus(-lam)
    a = jnp.exp(log_a)
    u = jnp.sqrt(-jnp.tanh(log_a) * (1.0 + a * a)) * (i * xf)
    spread()

    rid = lax.broadcasted_iota(jnp.int32, (1, SUBLANES, 1), 1)
    a = a.reshape(nb * ng, SUBLANES, w)
    u = u.reshape(nb * ng, SUBLANES, w)
    for s in (1, 2, 4):
        a_sh = jnp.where(rid >= s, pltpu.roll(a, s, axis=1), 1.0)
        u_sh = jnp.where(rid >= s, pltpu.roll(u, s, axis=1), 0.0)
        u = a * u_sh + u
        a = a * a_sh
    a = a.reshape(nb, ng, SUBLANES, w)
    u = u.reshape(nb, ng, SUBLANES, w)
    gr4 = gr.reshape(nb, ng, SUBLANES, w)
    for g in range(ng):
        if g == ng // 2:
            spread()
        hg = a[:, g] * h + u[:, g]
        h = hg[:, SUBLANES - 1:SUBLANES]
        emit(g, hg, hg * _silu(gr4[:, g]))
    return h


def _rglru_kernel(x_ref, gr_ref, cw_ref, cb_ref, wa_ref, ba_ref, wx_ref, bx_ref, lam_ref, h0_ref, conv0_ref,
                  hl_ref, hlast_ref, h_s, xs_s, *, nb, t, l_valid):
    c = pl.program_id(1)

    @pl.when(c == 0)
    def _():
        h_s[...] = h0_ref[...]
        xs_s[:, 0:SUBLANES, :] = conv0_ref[...]

    g_last, r_last = divmod((l_valid - 1) % t, SUBLANES)

    def emit(g, hg, gated):
        hl_ref[:, g * SUBLANES:(g + 1) * SUBLANES, :] = gated.astype(hl_ref.dtype)
        if g == g_last:
            @pl.when(c == (l_valid - 1) // t)
            def _():
                hlast_ref[...] = hg[:, r_last:r_last + 1]

    h_s[...] = _lru_chunk(x_ref[...], gr_ref[...], xs_s, h_s[...], cw_ref[...], cb_ref[...], wa_ref, ba_ref[...],
                          wx_ref, bx_ref[...], lam_ref[...], emit)


def _rglru(rest3, row0, nseq, tiles, nc, l_valid, cw, cb, wa, ba, wx, bx, lam, h0, conv0):
    nb, t = tiles
    assert nseq % nb == 0 and row0 % nb == 0
    w = LRU_WIDTH
    r0 = row0 // nb
    full2 = lambda a: pl.BlockSpec(a.shape, lambda b, c: (0, 0))
    full3 = lambda a: pl.BlockSpec(a.shape, lambda b, c: (0, 0, 0))
    kern = functools.partial(_rglru_kernel, nb=nb, t=t, l_valid=l_valid)
    return pl.pallas_call(
        kern,
        grid=(nseq // nb, nc),
        in_specs=[
            pl.BlockSpec((nb, t, w), lambda b, c: (r0 + b, c, 2)),
            pl.BlockSpec((nb, t, w), lambda b, c: (r0 + b, c, 3)),
            full2(cw), full2(cb), full3(wa), full2(ba), full3(wx), full2(bx), full2(lam),
            pl.BlockSpec((nb, 1, w), lambda b, c: (b, 0, 0)),
            pl.BlockSpec((nb, SUBLANES, w), lambda b, c: (b, 0, 0)),
        ],
        out_specs=[
            pl.BlockSpec((nb, t, w), lambda b, c: (b, c, 0)),
            pl.BlockSpec((nb, 1, w), lambda b, c: (b, 0, 0)),
        ],
        out_shape=[
            jax.ShapeDtypeStruct((nseq, t * nc, w), BF16),
            jax.ShapeDtypeStruct((nseq, 1, w), F32),
        ],
        scratch_shapes=[pltpu.VMEM((nb, 1, w), F32), pltpu.VMEM((nb, SUBLANES + t, w), F32)],
        compiler_params=_cparams(("parallel", "arbitrary")),
        name="rglru",
    )(rest3, rest3, cw, cb, wa, ba, wx, bx, lam, h0, conv0)


def _gla_kernel(q_ref, k_ref, v_ref, al_ref, aup_ref, ab_ref, s0_ref, o_ref, s_ref, *, nb, t, nc, l_valid):
    c = pl.program_id(1)
    if nc == 1:
        s_in = s0_ref
    else:
        s_in = s_ref

        @pl.when(c == 0)
        def _():
            s_ref[...] = s0_ref[...]

    pos = c * t + lax.broadcasted_iota(jnp.int32, (t, 1), 0)
    valid = pos < l_valid
    row = lax.broadcasted_iota(jnp.int32, (t, t), 0)
    col = lax.broadcasted_iota(jnp.int32, (t, t), 1)
    causal = row >= col
    aup = aup_ref[...]
    ab = ab_ref[...]
    heads = [(h, slice(h * G_DK, (h + 1) * G_DK), slice(h * G_DV, (h + 1) * G_DV)) for h in range(N_HEADS)]

    def group(rows):
        pairs = [(i, r, h, ks, vs) for i, r in enumerate(rows) for h, ks, vs in heads]
        dec = []
        for r in rows:
            pre = _dot(al_ref[r], aup) + ab
            lg = jnp.where(valid, _log_sigmoid(pre) / G_TAU, 0.0)
            b = _cumsum_rows(lg)
            b_last = b[t - 1:t, :]
            k = jnp.where(valid, k_ref[r].astype(F32), 0.0)
            qd = ((q_ref[r].astype(F32) * (G_DK ** -0.5)) * jnp.exp(b)).astype(BF16)
            kn = (k * jnp.exp(-b)).astype(BF16)
            kr = (k * jnp.exp(b_last - b)).astype(BF16)
            dec.append((qd, kn, kr, _row_to_col(jnp.exp(b_last))))
        att = {}
        for i, r, h, ks, vs in pairs:
            qd, kn, _, _ = dec[i]
            att[i, h] = jnp.where(causal, _dot_nt(qd[:, ks], kn[:, ks]), 0.0).astype(BF16)
        for i, r, h, ks, vs in pairs:
            qd, _, kr, ebl_col = dec[i]
            s_h = s_in[r, h]
            v = v_ref[r, :, vs]
            o_ref[r, :, vs] = (_dot(qd[:, ks], s_h) + _dot(att[i, h], v)).astype(o_ref.dtype)
            s_ref[r, h] = ebl_col[ks, :] * s_h + _dot_tn(kr[:, ks], v)

    _group_loop(nb, group)


def _gla(qkv3, al3, row0, nseq, tiles, nc, l_valid, aup, ab, s0):
    nb, t = tiles
    assert nseq % nb == 0 and row0 % nb == 0
    r0 = row0 // nb
    full2 = lambda a: pl.BlockSpec(a.shape, lambda b, c: (0, 0))
    st = pl.BlockSpec((nb, N_HEADS, G_DK, G_DV), lambda b, c: (b, 0, 0, 0))
    kern = functools.partial(_gla_kernel, nb=nb, t=t, nc=nc, l_valid=l_valid)
    return pl.pallas_call(
        kern,
        grid=(nseq // nb, nc),
        in_specs=[
            pl.BlockSpec((nb, t, G_KW), lambda b, c: (r0 + b, c, 0)),
            pl.BlockSpec((nb, t, G_KW), lambda b, c: (r0 + b, c, 1)),
            pl.BlockSpec((nb, t, G_VW), lambda b, c: (r0 + b, c, 1)),
            pl.BlockSpec((nb, t, LANES), lambda b, c: (r0 + b, c, 0)),
            full2(aup), full2(ab), st,
        ],
        out_specs=[pl.BlockSpec((nb, t, G_VW), lambda b, c: (b, c, 0)), st],
        out_shape=[
            jax.ShapeDtypeStruct((nseq, t * nc, G_VW), BF16),
            jax.ShapeDtypeStruct((nseq, N_HEADS, G_DK, G_DV), F32),
        ],
        compiler_params=_cparams(("parallel", "arbitrary")),
        name="gla",
    )(qkv3, qkv3, qkv3, al3, aup, ab, s0)


def _even_layer_kernel(x_ref, pre_ref, post_ref, win_a_ref, win_b_ref, wgate_ref, gbias_ref, ng_ref, cw_ref, cb_ref, wa_ref,
                       ba_ref, wx_ref, bx_ref, lam_ref, w1_ref, w2_ref, c0_ref, n0_ref, m0_ref, h0_ref, conv0_ref,
                       y_ref, c_ref, n_ref, m_ref, hlast_ref, conv_ref,
                       qkv_s, rest_s, gate_s, hn_s, hm_s, hl_s, xkeep_s, hlru_s, xs_s, *, rows, tiles_per_seq):
    g = pl.program_id(0)
    slot = g % 2
    prev = 1 - slot
    t = MLSTM_TILES["main"][1]
    e_lo = 3 * M_WIDTH
    e_a = win_a_ref.shape[1]
    e = e_a + win_b_ref.shape[1]
    n_chunks = rows // t
    cols_per_chunk = e // n_chunks

    @pl.when(g == 0)
    def _():
        qkv_s[1] = jnp.zeros(qkv_s.shape[1:], qkv_s.dtype)
        rest_s[1] = jnp.zeros(rest_s.shape[1:], rest_s.dtype)
        gate_s[1] = jnp.zeros(gate_s.shape[1:], gate_s.dtype)
        xkeep_s[...] = jnp.zeros(xkeep_s.shape, xkeep_s.dtype)

    @pl.when(jnp.maximum(g - 1, 0) % tiles_per_seq == 0)
    def _():
        c_ref[...] = c0_ref[...]
        n_ref[...] = n0_ref[...]
        m_ref[...] = m0_ref[...]
        hlru_s[...] = h0_ref[...]
        xs_s[:, 0:SUBLANES, :] = conv0_ref[...]

    x = x_ref[0]
    hn_s[...] = (x * lax.rsqrt(jnp.mean(x * x, axis=-1, keepdims=True) + EPS) * pre_ref[...]).astype(BF16)
    gate_s[slot] = jnp.dot(hn_s[...], wgate_ref[...], preferred_element_type=F32)

    def project(c0, c1):
        w_cols = win_a_ref[:, c0:c1] if c1 <= e_a else win_b_ref[:, c0 - e_a:c1 - e_a]
        acc = jnp.dot(hn_s[...], w_cols, preferred_element_type=F32)
        if c0 < e_lo:
            hi = min(c1, e_lo)
            qkv_s[slot, :, c0:hi] = acc[:, :hi - c0].astype(BF16)
        if c1 > e_lo:
            lo = max(c0, e_lo)
            rest_s[slot, :, lo - e_lo:c1 - e_lo] = acc[:, lo - c0:]

    row = lax.broadcasted_iota(jnp.int32, (t, t), 0)
    col = lax.broadcasted_iota(jnp.int32, (t, t), 1)
    causal = row >= col
    lane = lax.broadcasted_iota(jnp.int32, (1, LANES), 1)
    gbias = gbias_ref[...]
    heads = [(h, slice(h * M_DK, (h + 1) * M_DK)) for h in range(N_HEADS)]
    piece = 512
    pieces = [(c0, c0 + piece) for c0 in range(0, e, piece)]
    per_chunk = 5
    assert e % piece == 0 and e_lo % piece == 0 and e_a % piece == 0 and len(pieces) >= n_chunks * per_chunk

    w = LRU_WIDTH
    lru_h = [hlru_s[...]]
    subs_per_chunk = rows // LRU_SUB // n_chunks

    def lru_sub(i):
        r0 = i * LRU_SUB

        def emit(gi, hg, gated):
            hl_s[r0 + gi * SUBLANES:r0 + (gi + 1) * SUBLANES, :] = gated[0].astype(hl_s.dtype)

        lru_h[0] = _lru_chunk(rest_s[prev, r0:r0 + LRU_SUB, 2 * w:3 * w][None],
                              rest_s[prev, r0:r0 + LRU_SUB, 3 * w:4 * w][None], xs_s, lru_h[0],
                              cw_ref[...], cb_ref[...], wa_ref, ba_ref[...], wx_ref, bx_ref[...], lam_ref[...], emit)

    n_gaps = 4

    def lru_between(j, gap):
        for i in range(subs_per_chunk):
            if LRU_GAPS[i] == gap:
                lru_sub(j * subs_per_chunk + i)

    for j in range(n_chunks):
        rs = slice(j * t, (j + 1) * t)
        todo = pieces[j * per_chunk:(j + 1) * per_chunk]
        gt = gate_s[prev, rs, :] + gbias
        li = gt[:, :LANES]
        b = _cumsum_rows(_log_sigmoid(gt[:, LANES:]))
        project(*todo[0])
        lru_between(j, 0)
        r_t = _transpose_rows(li - b)[:, :t]
        m_vec = m_ref[0]
        st = []
        for h, sl in heads:
            b_col = b[:, h:h + 1]
            inter = b_col + m_vec[:, h:h + 1]
            dmat = jnp.where(causal, b_col + r_t[h:h + 1, :], NEG)
            mt = jnp.maximum(inter, jnp.max(dmat, axis=-1, keepdims=True))
            k = qkv_s[prev, rs, M_WIDTH + h * M_DK:M_WIDTH + (h + 1) * M_DK] * (M_DK ** -0.5)
            s = _dot_nt(qkv_s[prev, rs, sl], k) * jnp.exp(dmat - mt)
            st.append((inter, mt, k, s))
        project(*todo[1])
        lru_between(j, 1)
        project(*todo[2])
        for h, sl in heads:
            inter, mt, k, s = st[h]
            q = qkv_s[prev, rs, sl]
            v = qkv_s[prev, rs, 2 * M_WIDTH + h * M_DV:2 * M_WIDTH + (h + 1) * M_DV]
            w_inter = jnp.exp(inter - mt)
            num = w_inter * _dot_nt(q, c_ref[0, h]) + _dot(s, v)
            den = (w_inter * jnp.sum(q.astype(F32) * n_ref[0, h], axis=-1, keepdims=True)
                   + jnp.sum(s, axis=-1, keepdims=True))
            hm_s[rs, sl] = (num / jnp.maximum(jnp.abs(den), jnp.exp(-mt))).astype(hm_s.dtype)
        project(*todo[3])
        lru_between(j, 2)
        m_next = m_vec
        for h, sl in heads:
            inter, mt, k, _ = st[h]
            v = qkv_s[prev, rs, 2 * M_WIDTH + h * M_DV:2 * M_WIDTH + (h + 1) * M_DV]
            b_col = b[:, h:h + 1]
            m_new = mt[t - 1:t, :]
            w_c = jnp.exp(inter[t - 1:t, :] - m_new)
            w_k = jnp.exp(b_col[t - 1:t, :] - b_col + li[:, h:h + 1] - m_new)
            c_ref[0, h] = w_c * c_ref[0, h] + _dot_tn(v.astype(F32) * w_k, k)
            n_ref[0, h] = w_c * n_ref[0, h] + jnp.sum(k.astype(F32) * w_k, axis=0, keepdims=True)
            m_next = jnp.where(lane == h, m_new, m_next)
        m_ref[0] = m_next
        lru_between(j, 3)
        project(*todo[4])

    for p in pieces[n_chunks * per_chunk:]:
        project(*p)
    h_end = lru_h[0]
    hlru_s[...] = h_end
    hlast_ref[...] = h_end
    conv_ref[...] = xs_s[:, 0:SUBLANES, :]

    hm = _sigmoid(rest_s[prev, :, 0:w]) * hm_s[...].astype(F32)
    hm = _head_rms(hm, N_HEADS) * ng_ref[...] * _silu(rest_s[prev, :, w:2 * w])
    out = jnp.dot(hm.astype(BF16), w1_ref[...], preferred_element_type=F32)
    out = out + jnp.dot(hl_s[...], w2_ref[...], preferred_element_type=F32)
    nrm = out * lax.rsqrt(jnp.mean(out * out, axis=-1, keepdims=True) + EPS) * post_ref[...]
    y_ref[0] = xkeep_s[...] + nrm
    xkeep_s[...] = x


def _even_layer_main(x3, pre, post, w_a, e_a, w_b, w_gate, gbias, ng, lru, w1, w2, c0, n0, m0, h0, conv0, rows):
    nseq, seq, d = x3.shape
    assert seq % rows == 0 and rows % MLSTM_TILES["main"][1] == 0
    tps = seq // rows
    n_tiles = nseq * tps
    e = e_a + w_b.shape[1]
    e_lo = 3 * M_WIDTH
    w = LRU_WIDTH
    const = lambda a: pl.BlockSpec(a.shape, lambda g: (0,) * a.ndim, pipeline_mode=pl.Buffered(1))
    cur = lambda g: jnp.minimum(g, n_tiles - 1)
    old = lambda g: jnp.maximum(g - 1, 0)
    per_seq = lambda *blk: pl.BlockSpec((1,) + blk, lambda g: (old(g) // tps,) + (0,) * len(blk))
    st_c, st_n, st_m = per_seq(N_HEADS, M_DV, M_DK), per_seq(N_HEADS, 1, M_DK), per_seq(1, LANES)
    st_h, st_conv = per_seq(1, w), per_seq(SUBLANES, w)
    out_states = [st_c, st_n, st_m, st_h, st_conv]
    if c0.shape[0] == nseq:
        in_states = out_states
    else:
        in_states = [pl.BlockSpec(s.block_shape, lambda g, nd=len(s.block_shape): (0,) * nd) for s in out_states]
    kern = functools.partial(_even_layer_kernel, rows=rows, tiles_per_seq=tps)
    return pl.pallas_call(
        kern,
        grid=(n_tiles + 1,),
        in_specs=[
            pl.BlockSpec((1, rows, d), lambda g: (cur(g) // tps, cur(g) % tps, 0)),
            const(pre), const(post),
            pl.BlockSpec((d, e_a), lambda g: (0, 0), pipeline_mode=pl.Buffered(1)), const(w_b),
            const(w_gate), const(gbias), const(ng),
            *[const(a) for a in lru], const(w1), const(w2), *in_states,
        ],
        out_specs=[
            pl.BlockSpec((1, rows, d), lambda g: (old(g) // tps, old(g) % tps, 0)),
            st_c, st_n, st_m, st_h, st_conv,
        ],
        out_shape=[
            jax.ShapeDtypeStruct((nseq, seq, d), F32),
            jax.ShapeDtypeStruct((nseq, N_HEADS, M_DV, M_DK), F32),
            jax.ShapeDtypeStruct((nseq, N_HEADS, 1, M_DK), F32),
            jax.ShapeDtypeStruct((nseq, 1, LANES), F32),
            jax.ShapeDtypeStruct((nseq, 1, w), F32),
            jax.ShapeDtypeStruct((nseq, SUBLANES, w), F32),
        ],
        scratch_shapes=[
            pltpu.VMEM((2, rows, e_lo), BF16),
            pltpu.VMEM((2, rows, e - e_lo), F32),
            pltpu.VMEM((2, rows, 2 * LANES), F32),
            pltpu.VMEM((rows, d), BF16),
            pltpu.VMEM((rows, M_WIDTH), BF16),
            pltpu.VMEM((rows, w), BF16),
            pltpu.VMEM((rows, d), F32),
            pltpu.VMEM((1, 1, w), F32),
            pltpu.VMEM((1, SUBLANES + LRU_SUB, w), F32),
        ],
        compiler_params=_cparams(("arbitrary",)),
        name="even_layer",
    )(x3, pre, post, w_a, w_b, w_gate, gbias, ng, *lru, w1, w2, c0, n0, m0, h0, conv0)


def _gla_layer_kernel(x_ref, pre_ref, post_ref, win_ref, wlow_ref, aup_ref, ab_ref, ng_ref, wout_ref, s0_ref,
                      y_ref, s_ref, qkv_s, r_s, al_s, hn_s, o_s, xkeep_s, *, rows, tiles_per_seq):
    g = pl.program_id(0)
    slot = g % 2
    prev = 1 - slot
    t = GLA_TILES["main"][1]
    e_lo = 2 * G_KW + G_VW
    e = e_lo + G_VW
    n_chunks = rows // t
    cols_per_chunk = e // n_chunks

    @pl.when(g == 0)
    def _():
        qkv_s[1] = jnp.zeros(qkv_s.shape[1:], qkv_s.dtype)
        r_s[1] = jnp.zeros(r_s.shape[1:], r_s.dtype)
        al_s[1] = jnp.zeros(al_s.shape[1:], al_s.dtype)
        xkeep_s[...] = jnp.zeros(xkeep_s.shape, xkeep_s.dtype)

    @pl.when(jnp.maximum(g - 1, 0) % tiles_per_seq == 0)
    def _():
        s_ref[...] = s0_ref[...]

    x = x_ref[0]
    hn_s[...] = (x * lax.rsqrt(jnp.mean(x * x, axis=-1, keepdims=True) + EPS) * pre_ref[...]).astype(BF16)
    al_s[slot] = jnp.dot(hn_s[...], wlow_ref[...], preferred_element_type=F32)

    row = lax.broadcasted_iota(jnp.int32, (t, t), 0)
    col = lax.broadcasted_iota(jnp.int32, (t, t), 1)
    causal = row >= col
    aup = aup_ref[...]
    ab = ab_ref[...]
    heads = [(h, slice(h * G_DK, (h + 1) * G_DK), slice(h * G_DV, (h + 1) * G_DV)) for h in range(N_HEADS)]

    def project(c0, c1):
        acc = jnp.dot(hn_s[...], win_ref[:, c0:c1], preferred_element_type=F32)
        if c0 < e_lo:
            hi = min(c1, e_lo)
            qkv_s[slot, :, c0:hi] = acc[:, :hi - c0].astype(BF16)
        if c1 > e_lo:
            lo = max(c0, e_lo)
            r_s[slot, :, lo - e_lo:c1 - e_lo] = acc[:, lo - c0:]

    n_sub = 3
    sub = cols_per_chunk // n_sub
    for j in range(n_chunks):
        c0 = j * cols_per_chunk
        rs = slice(j * t, (j + 1) * t)
        pre = _dot(al_s[prev, rs, :], aup) + ab
        project(c0, c0 + sub)
        b = _cumsum_rows(_log_sigmoid(pre) / G_TAU)
        b_last = b[t - 1:t, :]
        project(c0 + sub, c0 + 2 * sub)
        k = qkv_s[prev, rs, G_KW:2 * G_KW].astype(F32)
        qd = ((qkv_s[prev, rs, 0:G_KW].astype(F32) * (G_DK ** -0.5)) * jnp.exp(b)).astype(BF16)
        kn = (k * jnp.exp(-b)).astype(BF16)
        kr = (k * jnp.exp(b_last - b)).astype(BF16)
        ebl_col = _row_to_col(jnp.exp(b_last))
        att = [jnp.where(causal, _dot_nt(qd[:, ks], kn[:, ks]), 0.0).astype(BF16) for _, ks, _ in heads]
        project(c0 + 2 * sub, c0 + cols_per_chunk)
        for h, ks, vs in heads:
            s_h = s_ref[0, h]
            v = qkv_s[prev, rs, 2 * G_KW + h * G_DV:2 * G_KW + (h + 1) * G_DV]
            o_s[rs, vs] = (_dot(qd[:, ks], s_h) + _dot(att[h], v)).astype(o_s.dtype)
            s_ref[0, h] = ebl_col[ks, :] * s_h + _dot_tn(kr[:, ks], v)

    o = _head_rms(o_s[...].astype(F32), N_HEADS) * ng_ref[...] * _silu(r_s[prev])
    out = jnp.dot(o.astype(BF16), wout_ref[...], preferred_element_type=F32)
    nrm = out * lax.rsqrt(jnp.mean(out * out, axis=-1, keepdims=True) + EPS) * post_ref[...]
    y_ref[0] = xkeep_s[...] + nrm
    xkeep_s[...] = x


def _gla_layer(x3, pre, post, w_in, w_low, aup, ab, ng, w_out, s0, rows):
    nseq, seq, d = x3.shape
    assert seq % rows == 0 and rows % GLA_TILES["main"][1] == 0
    tps = seq // rows
    n_tiles = nseq * tps
    e_lo = 2 * G_KW + G_VW
    e = e_lo + G_VW
    const = lambda a: pl.BlockSpec(a.shape, lambda g: (0,) * a.ndim, pipeline_mode=pl.Buffered(1))
    cur = lambda g: jnp.minimum(g, n_tiles - 1)
    old = lambda g: jnp.maximum(g - 1, 0)
    st = pl.BlockSpec((1, N_HEADS, G_DK, G_DV), lambda g: (old(g) // tps, 0, 0, 0))
    st_in = st if s0.shape[0] == nseq else pl.BlockSpec((1, N_HEADS, G_DK, G_DV), lambda g: (0, 0, 0, 0))
    kern = functools.partial(_gla_layer_kernel, rows=rows, tiles_per_seq=tps)
    return pl.pallas_call(
        kern,
        grid=(n_tiles + 1,),
        in_specs=[
            pl.BlockSpec((1, rows, d), lambda g: (cur(g) // tps, cur(g) % tps, 0)),
            const(pre), const(post), pl.BlockSpec((d, e), lambda g: (0, 0), pipeline_mode=pl.Buffered(1)),
            const(w_low), const(aup), const(ab), const(ng), const(w_out), st_in,
        ],
        out_specs=[pl.BlockSpec((1, rows, d), lambda g: (old(g) // tps, old(g) % tps, 0)), st],
        out_shape=[
            jax.ShapeDtypeStruct((nseq, seq, d), F32),
            jax.ShapeDtypeStruct((nseq, N_HEADS, G_DK, G_DV), F32),
        ],
        scratch_shapes=[
            pltpu.VMEM((2, rows, e_lo), BF16),
            pltpu.VMEM((2, rows, e - e_lo), F32),
            pltpu.VMEM((2, rows, LANES), F32),
            pltpu.VMEM((rows, d), BF16),
            pltpu.VMEM((rows, G_VW), BF16),
            pltpu.VMEM((rows, d), F32),
        ],
        compiler_params=_cparams(("arbitrary",)),
        name="gla_layer",
    )(x3, pre, post, w_in, w_low, aup, ab, ng, w_out, s0)


def _pad_lanes(a, width):
    return jnp.pad(a, [(0, 0)] * (a.ndim - 1) + [(0, width - a.shape[-1])])


def _even_layer(xm, xs, nbp, seq, nbs, ls, st, pre_g, post_g, w_in, w_out, b_i, b_f, m_norm_g,
                conv_w, conv_b, wa, ba, wx, bx, lam):
    c0s, n0s, m0s, h0s, conv0s = st
    e_a = 4 * M_WIDTH
    n_gate = 2 * N_HEADS
    w_bf = w_in.astype(BF16)
    w_b = w_bf[:, e_a + n_gate:]
    wig, wfg = w_in[:, e_a:e_a + N_HEADS], w_in[:, e_a + N_HEADS:e_a + n_gate]
    w_gate = jnp.concatenate([_pad_lanes(wig, LANES), _pad_lanes(wfg, LANES)], axis=1).astype(BF16)
    gbias = jnp.concatenate([_pad_lanes(b_i[None], LANES), _pad_lanes(b_f[None], LANES)], axis=1)
    ng = m_norm_g[None]
    post = post_g[None]
    cw, cb = conv_w, conv_b[None]
    wab, wxb = wa.astype(BF16), wx.astype(BF16)
    bav, bxv, lamv = ba[None], bx[None], lam[None]
    w_o1, w_o2 = w_out[:M_WIDTH].astype(BF16), w_out[M_WIDTH:].astype(BF16)
    e_lo = 3 * M_WIDTH
    e_hi = e_a + w_b.shape[1] - e_lo
    lru = (cw, cb, wab, bav, wxb, bxv, lamv)

    qs, ps, gs = _norm_proj(xs, pre_g, w_bf, e_a, w_b, w_gate, e_lo, xs.shape[0])
    mrow = nbs * SAMPLE_PAD // N_META
    xr_cols = slice(2 * LRU_WIDTH, 3 * LRU_WIDTH)

    zc = jnp.zeros((1, N_HEADS, M_DV, M_DK), F32)
    zn = jnp.zeros((1, N_HEADS, 1, M_DK), F32)
    zm = jnp.zeros((1, 1, LANES), F32)
    zh = jnp.zeros((1, 1, LRU_WIDTH), F32)
    zconv = jnp.zeros((1, SUBLANES, LRU_WIDTH), F32)
    qs_meta = qs.reshape(-1, N_META, e_lo)
    ps_meta = ps.reshape(-1, N_META, e_hi)
    gs_meta = gs.reshape(-1, N_META, 2 * LANES)
    h_meta, c1, n1, m1 = _mlstm(qs_meta, gs_meta, mrow, 1, MLSTM_TILES["meta"], 1, N_META, gbias, zc, zn, zm)
    hl_meta, h1 = _rglru(ps_meta, mrow, 1, LRU_TILES["meta"], 1, N_META, *lru, zh, zconv)
    conv1 = ps_meta[mrow:mrow + 1, N_META - SUBLANES:, xr_cols]

    ym3, pc, pn, pmm, ph, conv_tail = _even_layer_main(
        xm.reshape(nbp, seq, -1), pre_g[None], post, w_bf, e_a, w_b, w_gate, gbias, ng, lru, w_o1, w_o2,
        c1, n1, m1, h1, conv1, FUSED_ROW_TILE)
    ym = ym3.reshape(xm.shape)
    pconv = conv_tail[:, SUBLANES - (CONV_W - 1):, :]

    qs_s = qs.reshape(-1, SAMPLE_PAD, e_lo)
    ps_s = ps.reshape(-1, SAMPLE_PAD, e_hi)
    gs_s = gs.reshape(-1, SAMPLE_PAD, 2 * LANES)
    h_s, sc, sn, sm = _mlstm(qs_s, gs_s, 0, nbs, MLSTM_TILES["sample"], 1, ls, gbias,
                             c0s, n0s[:, :, None, :], _pad_lanes(m0s, LANES)[:, None, :])
    conv0p = jnp.pad(conv0s, ((0, 0), (SUBLANES - (CONV_W - 1), 0), (0, 0)))
    hl_s, sh = _rglru(ps_s, 0, nbs, LRU_TILES["sample"], 1, ls, *lru, h0s[:, None, :], conv0p)
    xr_s = ps_s[:nbs, :ls, xr_cols]
    sconv = jnp.concatenate([conv0s, xr_s], axis=1)[:, -(CONV_W - 1):]

    h_small = jnp.concatenate([h_s.reshape(-1, M_WIDTH), h_meta.reshape(-1, M_WIDTH)], axis=0)
    hl_small = jnp.concatenate([hl_s.reshape(-1, LRU_WIDTH), hl_meta.reshape(-1, LRU_WIDTH)], axis=0)
    ys = _out_even(xs, post, h_small, ps, ng, hl_small, w_o1, w_o2, _row_tile(xs.shape[0], OUT_ROW_TILE))

    p_state = (pc, pn[:, :, 0, :], pmm[:, 0, :N_HEADS], ph[:, 0, :], pconv)
    s_state = (sc, sn[:, :, 0, :], sm[:, 0, :N_HEADS], sh[:, 0, :], sconv)
    return ym, ys, p_state, s_state


def _odd_layer(xm, xs, nbp, seq, nbs, ls, s0s, pre_g, post_g, w_in, w_out, a_up, a_b, g_norm_g):
    e_lo = 2 * G_KW + G_VW
    e_a = 2 * G_KW + 2 * G_VW
    w_bf = w_in.astype(BF16)
    w_low = _pad_lanes(w_in[:, 2 * G_KW + 2 * G_VW:], LANES).astype(BF16)
    aup = jnp.pad(a_up, ((0, LANES - G_RANK), (0, 0))).astype(BF16)
    ab = a_b[None]
    ng = g_norm_g[None]
    post = post_g[None]
    w_o = w_out.astype(BF16)

    qs, rs, as_ = _norm_proj(xs, pre_g, w_bf, e_a, None, w_low, e_lo, xs.shape[0])
    mrow = nbs * SAMPLE_PAD // N_META

    zs = jnp.zeros((1, N_HEADS, G_DK, G_DV), F32)
    o_meta, s1 = _gla(qs.reshape(-1, N_META, e_lo), as_.reshape(-1, N_META, LANES),
                      mrow, 1, GLA_TILES["meta"], 1, N_META, aup, ab, zs)
    ym3, p_s = _gla_layer(xm.reshape(nbp, seq, -1), pre_g[None], post, w_bf, w_low, aup, ab, ng, w_o, s1,
                          GLA_ROW_TILE)
    ym = ym3.reshape(xm.shape)
    o_s, s_s = _gla(qs.reshape(-1, SAMPLE_PAD, e_lo), as_.reshape(-1, SAMPLE_PAD, LANES),
                    0, nbs, GLA_TILES["sample"], 1, ls, aup, ab, s0s)

    o_small = jnp.concatenate([o_s.reshape(-1, G_VW), o_meta.reshape(-1, G_VW)], axis=0)
    ys = _out_odd(xs, post, o_small, rs, ng, w_o, _row_tile(xs.shape[0], OUT_ROW_TILE))
    return ym, ys, p_s, s_s


def kernel(x_prompt, x_sample, state_mlstm_C, state_mlstm_n, state_mlstm_m, state_rglru_h, state_rglru_conv,
           state_gla_S, meta_tokens, pre_norm_a, post_norm_a, w_in_a, w_out_a, mlstm_b_i, mlstm_b_f, mlstm_norm,
           conv_w, conv_b, lru_w_a, lru_b_a, lru_w_x, lru_b_x, lru_lambda, pre_norm_c, post_norm_c, w_in_c,
           w_out_c, gla_alpha_up, gla_alpha_b, gla_norm):
    nbp, seq, d = x_prompt.shape
    nbs, ls, _ = x_sample.shape
    depth = pre_norm_a.shape[0] + pre_norm_c.shape[0]
    assert ls >= CONV_W - 1 and ls <= SAMPLE_PAD and N_META % SUBLANES == 0 and nbs * SAMPLE_PAD % N_META == 0

    xm = x_prompt.reshape(nbp * seq, d)
    xs_pad = jnp.pad(x_sample, ((0, 0), (0, SAMPLE_PAD - ls), (0, 0))).reshape(nbs * SAMPLE_PAD, d)
    xs = jnp.concatenate([xs_pad, meta_tokens.astype(x_prompt.dtype)], axis=0)

    p_lists = [[] for _ in range(6)]
    s_lists = [[] for _ in range(6)]
    for layer in range(depth):
        j = layer // 2
        if layer % 2 == 0:
            st = (state_mlstm_C[j], state_mlstm_n[j], state_mlstm_m[j], state_rglru_h[j], state_rglru_conv[j])
            xm, xs, pst, sst = _even_layer(
                xm, xs, nbp, seq, nbs, ls, st, pre_norm_a[j], post_norm_a[j], w_in_a[j], w_out_a[j],
                mlstm_b_i[j], mlstm_b_f[j], mlstm_norm[j], conv_w[j], conv_b[j], lru_w_a[j], lru_b_a[j],
                lru_w_x[j], lru_b_x[j], lru_lambda[j])
            for i in range(5):
                p_lists[i].append(pst[i])
                s_lists[i].append(sst[i])
        else:
            xm, xs, p_s, s_s = _odd_layer(
                xm, xs, nbp, seq, nbs, ls, state_gla_S[j], pre_norm_c[j], post_norm_c[j], w_in_c[j], w_out_c[j],
                gla_alpha_up[j], gla_alpha_b[j], gla_norm[j])
            p_lists[5].append(p_s)
            s_lists[5].append(s_s)

    y_prompt = xm.reshape(nbp, seq, d)
    y_sample = xs[:nbs * SAMPLE_PAD].reshape(nbs, SAMPLE_PAD, d)[:, :ls]
    return (y_prompt, y_sample) + tuple(jnp.stack(l) for l in p_lists) + tuple(jnp.stack(l) for l in s_lists)
```

```python
import functools

import jax
import jax.numpy as jnp
from jax import lax
from jax.experimental import pallas as pl
from jax.experimental.pallas import tpu as pltpu

F32 = jnp.float32
BF16 = jnp.bfloat16

D_MODEL = 1024
N_META = 16
EPS = 1e-6
N_HEADS = 4
M_DK = 256
M_DV = 256
M_WIDTH = N_HEADS * M_DV
LRU_WIDTH = 1024
LRU_BLOCKS = 8
LRU_BS = LRU_WIDTH // LRU_BLOCKS
CONV_W = 4
LRU_C = 8.0
G_DK = 256
G_DV = 512
G_KW = N_HEADS * G_DK
G_VW = N_HEADS * G_DV
G_RANK = 16
G_TAU = 16.0

LANES = 128
SUBLANES = 8
NEG = -1e30
SAMPLE_PAD = SUBLANES
VMEM_LIMIT = 56 * 1024 * 1024
COL_TILE = 1024

MLSTM_TILES = {"meta": (1, N_META), "main": (1, 128), "sample": (8, SAMPLE_PAD)}
GLA_TILES = {"meta": (1, N_META), "main": (1, 64), "sample": (4, SAMPLE_PAD)}
LRU_TILES = {"meta": (1, N_META), "sample": (32, SAMPLE_PAD)}
OUT_ROW_TILE = 512
FUSED_ROW_TILE = 256
GLA_ROW_TILE = 256
LRU_SUB = 64


def _row_tile(n, cap):
    return next(tm for tm in range(min(n, cap) // SUBLANES * SUBLANES, 0, -SUBLANES) if n % tm == 0)


def _cparams(sem):
    return pltpu.CompilerParams(dimension_semantics=sem, vmem_limit_bytes=VMEM_LIMIT)


def _sigmoid(x):
    return 0.5 * jnp.tanh(0.5 * x) + 0.5


def _silu(x):
    return x * _sigmoid(x)


def _log_sigmoid(x):
    return jnp.minimum(x, 0.0) - jnp.log1p(jnp.exp(-jnp.abs(x)))


def _softplus(x):
    return jnp.maximum(x, 0.0) + jnp.log1p(jnp.exp(-jnp.abs(x)))


def _dot(a, b):
    return jnp.dot(a.astype(BF16), b.astype(BF16), preferred_element_type=F32)


def _dot_nt(a, b):
    return lax.dot_general(a.astype(BF16), b.astype(BF16), (((1,), (1,)), ((), ())),
                           preferred_element_type=F32)


def _dot_tn(a, b):
    return lax.dot_general(a.astype(BF16), b.astype(BF16), (((0,), (0,)), ((), ())),
                           preferred_element_type=F32)


def _transpose_rows(x):
    t, n = x.shape
    if t < LANES:
        x = jnp.concatenate([x, jnp.zeros((LANES - t, n), x.dtype)], axis=0)
    return jnp.transpose(x)


def _row_to_col(r):
    return jnp.transpose(jnp.broadcast_to(r, (LANES, r.shape[1])))[:, 0:1]


def _cumsum_rows(x):
    t = x.shape[0]
    if t <= 2 * SUBLANES:
        rid = lax.broadcasted_iota(jnp.int32, (t, 1), 0)
        s = 1
        while s < t:
            x = x + jnp.where(rid >= s, pltpu.roll(x, s, axis=0), 0.0)
            s *= 2
        return x
    row = lax.broadcasted_iota(jnp.int32, (t, t), 0)
    col = lax.broadcasted_iota(jnp.int32, (t, t), 1)
    tri = jnp.where(row >= col, 1.0, 0.0).astype(BF16)
    hi = x.astype(BF16)
    r1 = x - hi.astype(F32)
    mid = r1.astype(BF16)
    lo = (r1 - mid.astype(F32)).astype(BF16)
    acc = jnp.dot(tri, lo, preferred_element_type=F32)
    acc = acc + jnp.dot(tri, mid, preferred_element_type=F32)
    return acc + jnp.dot(tri, hi, preferred_element_type=F32)


def _head_rms(x, nh):
    hd = x.shape[1] // nh
    parts = []
    for h in range(nh):
        xh = x[:, h * hd:(h + 1) * hd]
        parts.append(xh * lax.rsqrt(jnp.mean(xh * xh, axis=-1, keepdims=True) + EPS))
    return jnp.concatenate(parts, axis=-1)


GROUP = 2


def _group_loop(nb, group_fn):
    if nb <= GROUP:
        group_fn(list(range(nb)))
        return

    def body(i, carry):
        group_fn([GROUP * i + r for r in range(GROUP)])
        return carry

    lax.fori_loop(0, nb // GROUP, body, 0)


def _norm_proj_kernel(n_lo, n_a, x_ref, g_ref, wa_ref, wb_ref, ws_ref, olo_ref, ohi_ref, os_ref, hn_ref):
    j = pl.program_id(1)

    @pl.when(j == 0)
    def _():
        x = x_ref[...]
        y = x * lax.rsqrt(jnp.mean(x * x, axis=-1, keepdims=True) + EPS) * g_ref[...]
        hn = y.astype(BF16)
        hn_ref[...] = hn
        os_ref[...] = jnp.dot(hn, ws_ref[...], preferred_element_type=F32)

    @pl.when(j < n_lo)
    def _():
        olo_ref[...] = jnp.dot(hn_ref[...], wa_ref[...], preferred_element_type=F32).astype(olo_ref.dtype)

    @pl.when((j >= n_lo) & (j < n_a))
    def _():
        ohi_ref[...] = jnp.dot(hn_ref[...], wa_ref[...], preferred_element_type=F32)

    @pl.when(j >= n_a)
    def _():
        ohi_ref[...] = jnp.dot(hn_ref[...], wb_ref[...], preferred_element_type=F32)


def _norm_proj(x2d, g, w_a, e_a, w_b, w_small, e_lo, tm):
    n, d = x2d.shape
    tn = COL_TILE
    if w_b is None:
        w_b, e = w_a, e_a
    else:
        e = e_a + w_b.shape[1]
    es = w_small.shape[1]
    n_lo, n_a = e_lo // tn, e_a // tn
    assert e_lo <= e_a and e_a % tn == 0 and e % tn == 0
    return pl.pallas_call(
        functools.partial(_norm_proj_kernel, n_lo, n_a),
        grid=(n // tm, e // tn),
        in_specs=[
            pl.BlockSpec((tm, d), lambda i, j: (i, 0)),
            pl.BlockSpec((1, d), lambda i, j: (0, 0)),
            pl.BlockSpec((d, tn), lambda i, j: (0, jnp.minimum(j, n_a - 1))),
            pl.BlockSpec((d, tn), lambda i, j: (0, jnp.maximum(j - n_a, 0))),
            pl.BlockSpec((d, es), lambda i, j: (0, 0)),
        ],
        out_specs=[
            pl.BlockSpec((tm, tn), lambda i, j: (i, jnp.minimum(j, n_lo - 1))),
            pl.BlockSpec((tm, tn), lambda i, j: (i, jnp.maximum(j - n_lo, 0))),
            pl.BlockSpec((tm, es), lambda i, j: (i, 0)),
        ],
        out_shape=[
            jax.ShapeDtypeStruct((n, e_lo), BF16),
            jax.ShapeDtypeStruct((n, e - e_lo), F32),
            jax.ShapeDtypeStruct((n, es), F32),
        ],
        scratch_shapes=[pltpu.VMEM((tm, d), BF16)],
        compiler_params=_cparams(("parallel", "arbitrary")),
        name="norm_proj",
    )(x2d, g.reshape(1, d), w_a, w_b, w_small)


def _residual_norm(x_ref, g_ref, y_ref, out):
    nrm = out * lax.rsqrt(jnp.mean(out * out, axis=-1, keepdims=True) + EPS) * g_ref[...]
    y_ref[...] = x_ref[...] + nrm


def _out_even_kernel(x_ref, g_ref, h_ref, og_ref, z_ref, ng_ref, hl_ref, w1_ref, w2_ref, y_ref):
    hm = _sigmoid(og_ref[...]) * h_ref[...].astype(F32)
    hm = _head_rms(hm, N_HEADS) * ng_ref[...] * _silu(z_ref[...])
    out = jnp.dot(hm.astype(BF16), w1_ref[...], preferred_element_type=F32)
    out = out + jnp.dot(hl_ref[...], w2_ref[...], preferred_element_type=F32)
    _residual_norm(x_ref, g_ref, y_ref, out)


def _out_even(x2d, g, h_raw, rest, ng, hl, w1, w2, tm):
    n, d = x2d.shape
    row = lambda width, col: pl.BlockSpec((tm, width), lambda i: (i, col))
    const = lambda a: pl.BlockSpec(a.shape, lambda i: (0, 0))
    return pl.pallas_call(
        _out_even_kernel,
        grid=(n // tm,),
        in_specs=[row(d, 0), const(g), row(M_WIDTH, 0), row(M_WIDTH, 0), row(M_WIDTH, 1), const(ng),
                  row(LRU_WIDTH, 0), const(w1), const(w2)],
        out_specs=row(d, 0),
        out_shape=jax.ShapeDtypeStruct((n, d), F32),
        compiler_params=_cparams(("parallel",)),
        name="out_even",
    )(x2d, g, h_raw, rest, rest, ng, hl, w1, w2)


def _out_odd_kernel(x_ref, g_ref, o_ref, r_ref, ng_ref, w_ref, y_ref):
    o = _head_rms(o_ref[...].astype(F32), N_HEADS) * ng_ref[...] * _silu(r_ref[...])
    out = jnp.dot(o.astype(BF16), w_ref[...], preferred_element_type=F32)
    _residual_norm(x_ref, g_ref, y_ref, out)


def _out_odd(x2d, g, o_raw, r, ng, w, tm):
    n, d = x2d.shape
    row = lambda width: pl.BlockSpec((tm, width), lambda i: (i, 0))
    const = lambda a: pl.BlockSpec(a.shape, lambda i: (0, 0))
    return pl.pallas_call(
        _out_odd_kernel,
        grid=(n // tm,),
        in_specs=[row(d), const(g), row(G_VW), row(G_VW), const(ng), const(w)],
        out_specs=row(d),
        out_shape=jax.ShapeDtypeStruct((n, d), F32),
        compiler_params=_cparams(("parallel",)),
        name="out_odd",
    )(x2d, g, o_raw, r, ng, w)


def _mlstm_kernel(q_ref, k_ref, v_ref, gate_ref, gbias_ref, c0_ref, n0_ref, m0_ref,
                  h_ref, c_ref, n_ref, m_ref, *, nb, t, nc, l_valid):
    c = pl.program_id(1)
    if nc == 1:
        c_in, n_in, m_in = c0_ref, n0_ref, m0_ref
    else:
        c_in, n_in, m_in = c_ref, n_ref, m_ref

        @pl.when(c == 0)
        def _():
            c_ref[...] = c0_ref[...]
            n_ref[...] = n0_ref[...]
            m_ref[...] = m0_ref[...]

    pos = c * t + lax.broadcasted_iota(jnp.int32, (t, 1), 0)
    valid = pos < l_valid
    row = lax.broadcasted_iota(jnp.int32, (t, t), 0)
    col = lax.broadcasted_iota(jnp.int32, (t, t), 1)
    causal = row >= col
    lane = lax.broadcasted_iota(jnp.int32, (1, LANES), 1)
    gbias = gbias_ref[...]
    heads = [(h, slice(h * M_DK, (h + 1) * M_DK)) for h in range(N_HEADS)]

    def group(rows):
        pairs = [(i, r, h, sl) for i, r in enumerate(rows) for h, sl in heads]
        gates = []
        for r in rows:
            g = gate_ref[r] + gbias
            li = jnp.where(valid, g[:, :LANES], NEG)
            lf = jnp.where(valid, _log_sigmoid(g[:, LANES:]), 0.0)
            b = _cumsum_rows(lf)
            gates.append((li, b, _transpose_rows(li - b)[:, :t], m_in[r]))
        st = {}
        for i, r, h, sl in pairs:
            li, b, r_t, m_vec = gates[i]
            b_col = b[:, h:h + 1]
            inter = b_col + m_vec[:, h:h + 1]
            dmat = jnp.where(causal, b_col + r_t[h:h + 1, :], NEG)
            mt = jnp.maximum(inter, jnp.max(dmat, axis=-1, keepdims=True))
            k = k_ref[r, :, sl] * (M_DK ** -0.5)
            s = _dot_nt(q_ref[r, :, sl], k) * jnp.exp(dmat - mt)
            st[i, h] = (inter, mt, k, s)
        for i, r, h, sl in pairs:
            inter, mt, k, s = st[i, h]
            q = q_ref[r, :, sl]
            w_inter = jnp.exp(inter - mt)
            num = w_inter * _dot_nt(q, c_in[r, h]) + _dot(s, v_ref[r, :, sl])
            den = (w_inter * jnp.sum(q.astype(F32) * n_in[r, h], axis=-1, keepdims=True)
                   + jnp.sum(s, axis=-1, keepdims=True))
            h_ref[r, :, sl] = (num / jnp.maximum(jnp.abs(den), jnp.exp(-mt))).astype(h_ref.dtype)
        m_next = [gt[3] for gt in gates]
        for i, r, h, sl in pairs:
            li, b, _, _ = gates[i]
            inter, mt, k, _ = st[i, h]
            b_col = b[:, h:h + 1]
            m_new = mt[t - 1:t, :]
            w_c = jnp.exp(inter[t - 1:t, :] - m_new)
            w_k = jnp.exp(b_col[t - 1:t, :] - b_col + li[:, h:h + 1] - m_new)
            c_ref[r, h] = w_c * c_in[r, h] + _dot_tn(v_ref[r, :, sl].astype(F32) * w_k, k)
            n_ref[r, h] = w_c * n_in[r, h] + jnp.sum(k.astype(F32) * w_k, axis=0, keepdims=True)
            m_next[i] = jnp.where(lane == h, m_new, m_next[i])
        for i, r in enumerate(rows):
            m_ref[r] = m_next[i]

    _group_loop(nb, group)


def _mlstm(qkv3, gates3, row0, nseq, tiles, nc, l_valid, gbias, c0, n0, m0):
    nb, t = tiles
    assert nseq % nb == 0 and row0 % nb == 0
    wd = M_WIDTH
    r0 = row0 // nb
    seq = lambda col: pl.BlockSpec((nb, t, wd), lambda b, c: (r0 + b, c, col))
    st_c = pl.BlockSpec((nb, N_HEADS, M_DV, M_DK), lambda b, c: (b, 0, 0, 0))
    st_n = pl.BlockSpec((nb, N_HEADS, 1, M_DK), lambda b, c: (b, 0, 0, 0))
    st_m = pl.BlockSpec((nb, 1, LANES), lambda b, c: (b, 0, 0))
    kern = functools.partial(_mlstm_kernel, nb=nb, t=t, nc=nc, l_valid=l_valid)
    return pl.pallas_call(
        kern,
        grid=(nseq // nb, nc),
        in_specs=[
            seq(0), seq(1), seq(2),
            pl.BlockSpec((nb, t, 2 * LANES), lambda b, c: (r0 + b, c, 0)),
            pl.BlockSpec(gbias.shape, lambda b, c: (0, 0)), st_c, st_n, st_m,
        ],
        out_specs=[pl.BlockSpec((nb, t, wd), lambda b, c: (b, c, 0)), st_c, st_n, st_m],
        out_shape=[
            jax.ShapeDtypeStruct((nseq, t * nc, wd), BF16),
            jax.ShapeDtypeStruct((nseq, N_HEADS, M_DV, M_DK), F32),
            jax.ShapeDtypeStruct((nseq, N_HEADS, 1, M_DK), F32),
            jax.ShapeDtypeStruct((nseq, 1, LANES), F32),
        ],
        compiler_params=_cparams(("parallel", "arbitrary")),
        name="mlstm",
    )(qkv3, qkv3, qkv3, gates3, gbias, c0, n0, m0)


def _lru_chunk(x, gr, xs_s, h, cw, cb, wa_ref, ba, wx_ref, bx, lam, emit, spread=lambda: None):
    nb, t, w = x.shape
    ng = t // SUBLANES
    xs_s[:, SUBLANES:SUBLANES + t, :] = x
    xc = cb + cw[CONV_W - 1:CONV_W, :] * x
    for s in range(1, CONV_W):
        xc = xc + cw[CONV_W - 1 - s:CONV_W - s, :] * xs_s[:, SUBLANES - s:SUBLANES - s + t, :]
    xs_s[:, 0:SUBLANES, :] = x[:, t - SUBLANES:, :]
    spread()

    xf = xc.reshape(nb * t, w)
    ra, ri = [], []
    for n in range(LRU_BLOCKS):
        xb = xf[:, n * LRU_BS:(n + 1) * LRU_BS].astype(BF16)
        ra.append(jnp.dot(xb, wa_ref[n], preferred_element_type=F32))
        ri.append(jnp.dot(xb, wx_ref[n], preferred_element_type=F32))
    spread()
    r = _sigmoid(jnp.concatenate(ra, axis=-1) + ba)
    i = _sigmoid(jnp.concatenate(ri, axis=-1) + bx)
    log_a = -LRU_C * r * _softplus(-lam)
    a = jnp.exp(log_a)
    u = jnp.sqrt(-jnp.tanh(log_a) * (1.0 + a * a)) * (i * xf)
    spread()

    rid = lax.broadcasted_iota(jnp.int32, (1, SUBLANES, 1), 1)
    a = a.reshape(nb * ng, SUBLANES, w)
    u = u.reshape(nb * ng, SUBLANES, w)
    for s in (1, 2, 4):
        a_sh = jnp.where(rid >= s, pltpu.roll(a, s, axis=1), 1.0)
        u_sh = jnp.where(rid >= s, pltpu.roll(u, s, axis=1), 0.0)
        u = a * u_sh + u
        a = a * a_sh
    a = a.reshape(nb, ng, SUBLANES, w)
    u = u.reshape(nb, ng, SUBLANES, w)
    gr4 = gr.reshape(nb, ng, SUBLANES, w)
    for g in range(ng):
        if g == ng // 2:
            spread()
        hg = a[:, g] * h + u[:, g]
        h = hg[:, SUBLANES - 1:SUBLANES]
        emit(g, hg, hg * _silu(gr4[:, g]))
    return h


def _rglru_kernel(x_ref, gr_ref, cw_ref, cb_ref, wa_ref, ba_ref, wx_ref, bx_ref, lam_ref, h0_ref, conv0_ref,
                  hl_ref, hlast_ref, h_s, xs_s, *, nb, t, l_valid):
    c = pl.program_id(1)

    @pl.when(c == 0)
    def _():
        h_s[...] = h0_ref[...]
        xs_s[:, 0:SUBLANES, :] = conv0_ref[...]

    g_last, r_last = divmod((l_valid - 1) % t, SUBLANES)

    def emit(g, hg, gated):
        hl_ref[:, g * SUBLANES:(g + 1) * SUBLANES, :] = gated.astype(hl_ref.dtype)
        if g == g_last:
            @pl.when(c == (l_valid - 1) // t)
            def _():
                hlast_ref[...] = hg[:, r_last:r_last + 1]

    h_s[...] = _lru_chunk(x_ref[...], gr_ref[...], xs_s, h_s[...], cw_ref[...], cb_ref[...], wa_ref, ba_ref[...],
                          wx_ref, bx_ref[...], lam_ref[...], emit)


def _rglru(rest3, row0, nseq, tiles, nc, l_valid, cw, cb, wa, ba, wx, bx, lam, h0, conv0):
    nb, t = tiles
    assert nseq % nb == 0 and row0 % nb == 0
    w = LRU_WIDTH
    r0 = row0 // nb
    full2 = lambda a: pl.BlockSpec(a.shape, lambda b, c: (0, 0))
    full3 = lambda a: pl.BlockSpec(a.shape, lambda b, c: (0, 0, 0))
    kern = functools.partial(_rglru_kernel, nb=nb, t=t, l_valid=l_valid)
    return pl.pallas_call(
        kern,
        grid=(nseq // nb, nc),
        in_specs=[
            pl.BlockSpec((nb, t, w), lambda b, c: (r0 + b, c, 2)),
            pl.BlockSpec((nb, t, w), lambda b, c: (r0 + b, c, 3)),
            full2(cw), full2(cb), full3(wa), full2(ba), full3(wx), full2(bx), full2(lam),
            pl.BlockSpec((nb, 1, w), lambda b, c: (b, 0, 0)),
            pl.BlockSpec((nb, SUBLANES, w), lambda b, c: (b, 0, 0)),
        ],
        out_specs=[
            pl.BlockSpec((nb, t, w), lambda b, c: (b, c, 0)),
            pl.BlockSpec((nb, 1, w), lambda b, c: (b, 0, 0)),
        ],
        out_shape=[
            jax.ShapeDtypeStruct((nseq, t * nc, w), BF16),
            jax.ShapeDtypeStruct((nseq, 1, w), F32),
        ],
        scratch_shapes=[pltpu.VMEM((nb, 1, w), F32), pltpu.VMEM((nb, SUBLANES + t, w), F32)],
        compiler_params=_cparams(("parallel", "arbitrary")),
        name="rglru",
    )(rest3, rest3, cw, cb, wa, ba, wx, bx, lam, h0, conv0)


def _gla_kernel(q_ref, k_ref, v_ref, al_ref, aup_ref, ab_ref, s0_ref, o_ref, s_ref, *, nb, t, nc, l_valid):
    c = pl.program_id(1)
    if nc == 1:
        s_in = s0_ref
    else:
        s_in = s_ref

        @pl.when(c == 0)
        def _():
            s_ref[...] = s0_ref[...]

    pos = c * t + lax.broadcasted_iota(jnp.int32, (t, 1), 0)
    valid = pos < l_valid
    row = lax.broadcasted_iota(jnp.int32, (t, t), 0)
    col = lax.broadcasted_iota(jnp.int32, (t, t), 1)
    causal = row >= col
    aup = aup_ref[...]
    ab = ab_ref[...]
    heads = [(h, slice(h * G_DK, (h + 1) * G_DK), slice(h * G_DV, (h + 1) * G_DV)) for h in range(N_HEADS)]

    def group(rows):
        pairs = [(i, r, h, ks, vs) for i, r in enumerate(rows) for h, ks, vs in heads]
        dec = []
        for r in rows:
            pre = _dot(al_ref[r], aup) + ab
            lg = jnp.where(valid, _log_sigmoid(pre) / G_TAU, 0.0)
            b = _cumsum_rows(lg)
            b_last = b[t - 1:t, :]
            k = jnp.where(valid, k_ref[r].astype(F32), 0.0)
            qd = ((q_ref[r].astype(F32) * (G_DK ** -0.5)) * jnp.exp(b)).astype(BF16)
            kn = (k * jnp.exp(-b)).astype(BF16)
            kr = (k * jnp.exp(b_last - b)).astype(BF16)
            dec.append((qd, kn, kr, _row_to_col(jnp.exp(b_last))))
        att = {}
        for i, r, h, ks, vs in pairs:
            qd, kn, _, _ = dec[i]
            att[i, h] = jnp.where(causal, _dot_nt(qd[:, ks], kn[:, ks]), 0.0).astype(BF16)
        for i, r, h, ks, vs in pairs:
            qd, _, kr, ebl_col = dec[i]
            s_h = s_in[r, h]
            v = v_ref[r, :, vs]
            o_ref[r, :, vs] = (_dot(qd[:, ks], s_h) + _dot(att[i, h], v)).astype(o_ref.dtype)
            s_ref[r, h] = ebl_col[ks, :] * s_h + _dot_tn(kr[:, ks], v)

    _group_loop(nb, group)


def _gla(qkv3, al3, row0, nseq, tiles, nc, l_valid, aup, ab, s0):
    nb, t = tiles
    assert nseq % nb == 0 and row0 % nb == 0
    r0 = row0 // nb
    full2 = lambda a: pl.BlockSpec(a.shape, lambda b, c: (0, 0))
    st = pl.BlockSpec((nb, N_HEADS, G_DK, G_DV), lambda b, c: (b, 0, 0, 0))
    kern = functools.partial(_gla_kernel, nb=nb, t=t, nc=nc, l_valid=l_valid)
    return pl.pallas_call(
        kern,
        grid=(nseq // nb, nc),
        in_specs=[
            pl.BlockSpec((nb, t, G_KW), lambda b, c: (r0 + b, c, 0)),
            pl.BlockSpec((nb, t, G_KW), lambda b, c: (r0 + b, c, 1)),
            pl.BlockSpec((nb, t, G_VW), lambda b, c: (r0 + b, c, 1)),
            pl.BlockSpec((nb, t, LANES), lambda b, c: (r0 + b, c, 0)),
            full2(aup), full2(ab), st,
        ],
        out_specs=[pl.BlockSpec((nb, t, G_VW), lambda b, c: (b, c, 0)), st],
        out_shape=[
            jax.ShapeDtypeStruct((nseq, t * nc, G_VW), BF16),
            jax.ShapeDtypeStruct((nseq, N_HEADS, G_DK, G_DV), F32),
        ],
        compiler_params=_cparams(("parallel", "arbitrary")),
        name="gla",
    )(qkv3, qkv3, qkv3, al3, aup, ab, s0)


def _even_layer_kernel(x_ref, pre_ref, post_ref, win_a_ref, win_b_ref, wgate_ref, gbias_ref, ng_ref, cw_ref, cb_ref, wa_ref,
                       ba_ref, wx_ref, bx_ref, lam_ref, w1_ref, w2_ref, c0_ref, n0_ref, m0_ref, h0_ref, conv0_ref,
                       y_ref, c_ref, n_ref, m_ref, hlast_ref, conv_ref,
                       qkv_s, rest_s, gate_s, hn_s, hm_s, hl_s, xkeep_s, hlru_s, xs_s, *, rows, tiles_per_seq):
    g = pl.program_id(0)
    slot = g % 2
    prev = 1 - slot
    t = MLSTM_TILES["main"][1]
    e_lo = 3 * M_WIDTH
    e_a = win_a_ref.shape[1]
    e = e_a + win_b_ref.shape[1]
    n_chunks = rows // t
    cols_per_chunk = e // n_chunks

    @pl.when(g == 0)
    def _():
        qkv_s[1] = jnp.zeros(qkv_s.shape[1:], qkv_s.dtype)
        rest_s[1] = jnp.zeros(rest_s.shape[1:], rest_s.dtype)
        gate_s[1] = jnp.zeros(gate_s.shape[1:], gate_s.dtype)
        xkeep_s[...] = jnp.zeros(xkeep_s.shape, xkeep_s.dtype)

    @pl.when(jnp.maximum(g - 1, 0) % tiles_per_seq == 0)
    def _():
        c_ref[...] = c0_ref[...]
        n_ref[...] = n0_ref[...]
        m_ref[...] = m0_ref[...]
        hlru_s[...] = h0_ref[...]
        xs_s[:, 0:SUBLANES, :] = conv0_ref[...]

    x = x_ref[0]
    hn_s[...] = (x * lax.rsqrt(jnp.mean(x * x, axis=-1, keepdims=True) + EPS) * pre_ref[...]).astype(BF16)
    gate_s[slot] = jnp.dot(hn_s[...], wgate_ref[...], preferred_element_type=F32)

    def project(c0, c1):
        w_cols = win_a_ref[:, c0:c1] if c1 <= e_a else win_b_ref[:, c0 - e_a:c1 - e_a]
        acc = jnp.dot(hn_s[...], w_cols, preferred_element_type=F32)
        if c0 < e_lo:
            hi = min(c1, e_lo)
            qkv_s[slot, :, c0:hi] = acc[:, :hi - c0].astype(BF16)
        if c1 > e_lo:
            lo = max(c0, e_lo)
            rest_s[slot, :, lo - e_lo:c1 - e_lo] = acc[:, lo - c0:]

    row = lax.broadcasted_iota(jnp.int32, (t, t), 0)
    col = lax.broadcasted_iota(jnp.int32, (t, t), 1)
    causal = row >= col
    lane = lax.broadcasted_iota(jnp.int32, (1, LANES), 1)
    gbias = gbias_ref[...]
    heads = [(h, slice(h * M_DK, (h + 1) * M_DK)) for h in range(N_HEADS)]
    piece = 512
    pieces = [(c0, c0 + piece) for c0 in range(0, e, piece)]
    per_chunk = 5
    assert e % piece == 0 and e_lo % piece == 0 and e_a % piece == 0 and len(pieces) >= n_chunks * per_chunk

    w = LRU_WIDTH
    lru_h = [hlru_s[...]]
    subs_per_chunk = rows // LRU_SUB // n_chunks

    def lru_sub(i):
        r0 = i * LRU_SUB

        def emit(gi, hg, gated):
            hl_s[r0 + gi * SUBLANES:r0 + (gi + 1) * SUBLANES, :] = gated[0].astype(hl_s.dtype)

        lru_h[0] = _lru_chunk(rest_s[prev, r0:r0 + LRU_SUB, 2 * w:3 * w][None],
                              rest_s[prev, r0:r0 + LRU_SUB, 3 * w:4 * w][None], xs_s, lru_h[0],
                              cw_ref[...], cb_ref[...], wa_ref, ba_ref[...], wx_ref, bx_ref[...], lam_ref[...], emit)

    n_gaps = 4

    def lru_between(j, gap):
        for i in range(-(-gap * subs_per_chunk // n_gaps), -(-(gap + 1) * subs_per_chunk // n_gaps)):
            lru_sub(j * subs_per_chunk + i)

    for j in range(n_chunks):
        rs = slice(j * t, (j + 1) * t)
        todo = pieces[j * per_chunk:(j + 1) * per_chunk]
        gt = gate_s[prev, rs, :] + gbias
        li = gt[:, :LANES]
        b = _cumsum_rows(_log_sigmoid(gt[:, LANES:]))
        project(*todo[0])
        lru_between(j, 0)
        r_t = _transpose_rows(li - b)[:, :t]
        m_vec = m_ref[0]
        st = []
        for h, sl in heads:
            b_col = b[:, h:h + 1]
            inter = b_col + m_vec[:, h:h + 1]
            dmat = jnp.where(causal, b_col + r_t[h:h + 1, :], NEG)
            mt = jnp.maximum(inter, jnp.max(dmat, axis=-1, keepdims=True))
            k = qkv_s[prev, rs, M_WIDTH + h * M_DK:M_WIDTH + (h + 1) * M_DK] * (M_DK ** -0.5)
            s = _dot_nt(qkv_s[prev, rs, sl], k) * jnp.exp(dmat - mt)
            st.append((inter, mt, k, s))
        project(*todo[1])
        lru_between(j, 1)
        project(*todo[2])
        for h, sl in heads:
            inter, mt, k, s = st[h]
            q = qkv_s[prev, rs, sl]
            v = qkv_s[prev, rs, 2 * M_WIDTH + h * M_DV:2 * M_WIDTH + (h + 1) * M_DV]
            w_inter = jnp.exp(inter - mt)
            num = w_inter * _dot_nt(q, c_ref[0, h]) + _dot(s, v)
            den = (w_inter * jnp.sum(q.astype(F32) * n_ref[0, h], axis=-1, keepdims=True)
                   + jnp.sum(s, axis=-1, keepdims=True))
            hm_s[rs, sl] = (num / jnp.maximum(jnp.abs(den), jnp.exp(-mt))).astype(hm_s.dtype)
        project(*todo[3])
        lru_between(j, 2)
        m_next = m_vec
        for h, sl in heads:
            inter, mt, k, _ = st[h]
            v = qkv_s[prev, rs, 2 * M_WIDTH + h * M_DV:2 * M_WIDTH + (h + 1) * M_DV]
            b_col = b[:, h:h + 1]
            m_new = mt[t - 1:t, :]
            w_c = jnp.exp(inter[t - 1:t, :] - m_new)
            w_k = jnp.exp(b_col[t - 1:t, :] - b_col + li[:, h:h + 1] - m_new)
            c_ref[0, h] = w_c * c_ref[0, h] + _dot_tn(v.astype(F32) * w_k, k)
            n_ref[0, h] = w_c * n_ref[0, h] + jnp.sum(k.astype(F32) * w_k, axis=0, keepdims=True)
            m_next = jnp.where(lane == h, m_new, m_next)
        m_ref[0] = m_next
        lru_between(j, 3)
        project(*todo[4])

    for p in pieces[n_chunks * per_chunk:]:
        project(*p)
    h_end = lru_h[0]
    hlru_s[...] = h_end
    hlast_ref[...] = h_end
    conv_ref[...] = xs_s[:, 0:SUBLANES, :]

    hm = _sigmoid(rest_s[prev, :, 0:w]) * hm_s[...].astype(F32)
    hm = _head_rms(hm, N_HEADS) * ng_ref[...] * _silu(rest_s[prev, :, w:2 * w])
    out = jnp.dot(hm.astype(BF16), w1_ref[...], preferred_element_type=F32)
    out = out + jnp.dot(hl_s[...], w2_ref[...], preferred_element_type=F32)
    nrm = out * lax.rsqrt(jnp.mean(out * out, axis=-1, keepdims=True) + EPS) * post_ref[...]
    y_ref[0] = xkeep_s[...] + nrm
    xkeep_s[...] = x


def _even_layer_main(x3, pre, post, w_a, e_a, w_b, w_gate, gbias, ng, lru, w1, w2, c0, n0, m0, h0, conv0, rows):
    nseq, seq, d = x3.shape
    assert seq % rows == 0 and rows % MLSTM_TILES["main"][1] == 0
    tps = seq // rows
    n_tiles = nseq * tps
    e = e_a + w_b.shape[1]
    e_lo = 3 * M_WIDTH
    w = LRU_WIDTH
    const = lambda a: pl.BlockSpec(a.shape, lambda g: (0,) * a.ndim, pipeline_mode=pl.Buffered(1))
    cur = lambda g: jnp.minimum(g, n_tiles - 1)
    old = lambda g: jnp.maximum(g - 1, 0)
    per_seq = lambda *blk: pl.BlockSpec((1,) + blk, lambda g: (old(g) // tps,) + (0,) * len(blk))
    st_c, st_n, st_m = per_seq(N_HEADS, M_DV, M_DK), per_seq(N_HEADS, 1, M_DK), per_seq(1, LANES)
    st_h, st_conv = per_seq(1, w), per_seq(SUBLANES, w)
    out_states = [st_c, st_n, st_m, st_h, st_conv]
    if c0.shape[0] == nseq:
        in_states = out_states
    else:
        in_states = [pl.BlockSpec(s.block_shape, lambda g, nd=len(s.block_shape): (0,) * nd) for s in out_states]
    kern = functools.partial(_even_layer_kernel, rows=rows, tiles_per_seq=tps)
    return pl.pallas_call(
        kern,
        grid=(n_tiles + 1,),
        in_specs=[
            pl.BlockSpec((1, rows, d), lambda g: (cur(g) // tps, cur(g) % tps, 0)),
            const(pre), const(post),
            pl.BlockSpec((d, e_a), lambda g: (0, 0), pipeline_mode=pl.Buffered(1)), const(w_b),
            const(w_gate), const(gbias), const(ng),
            *[const(a) for a in lru], const(w1), const(w2), *in_states,
        ],
        out_specs=[
            pl.BlockSpec((1, rows, d), lambda g: (old(g) // tps, old(g) % tps, 0)),
            st_c, st_n, st_m, st_h, st_conv,
        ],
        out_shape=[
            jax.ShapeDtypeStruct((nseq, seq, d), F32),
            jax.ShapeDtypeStruct((nseq, N_HEADS, M_DV, M_DK), F32),
            jax.ShapeDtypeStruct((nseq, N_HEADS, 1, M_DK), F32),
            jax.ShapeDtypeStruct((nseq, 1, LANES), F32),
            jax.ShapeDtypeStruct((nseq, 1, w), F32),
            jax.ShapeDtypeStruct((nseq, SUBLANES, w), F32),
        ],
        scratch_shapes=[
            pltpu.VMEM((2, rows, e_lo), BF16),
            pltpu.VMEM((2, rows, e - e_lo), F32),
            pltpu.VMEM((2, rows, 2 * LANES), F32),
            pltpu.VMEM((rows, d), BF16),
            pltpu.VMEM((rows, M_WIDTH), BF16),
            pltpu.VMEM((rows, w), BF16),
            pltpu.VMEM((rows, d), F32),
            pltpu.VMEM((1, 1, w), F32),
            pltpu.VMEM((1, SUBLANES + LRU_SUB, w), F32),
        ],
        compiler_params=_cparams(("arbitrary",)),
        name="even_layer",
    )(x3, pre, post, w_a, w_b, w_gate, gbias, ng, *lru, w1, w2, c0, n0, m0, h0, conv0)


def _gla_layer_kernel(x_ref, pre_ref, post_ref, win_ref, wlow_ref, aup_ref, ab_ref, ng_ref, wout_ref, s0_ref,
                      y_ref, s_ref, qkv_s, r_s, al_s, hn_s, o_s, xkeep_s, *, rows, tiles_per_seq):
    g = pl.program_id(0)
    slot = g % 2
    prev = 1 - slot
    t = GLA_TILES["main"][1]
    e_lo = 2 * G_KW + G_VW
    e = e_lo + G_VW
    n_chunks = rows // t
    cols_per_chunk = e // n_chunks

    @pl.when(g == 0)
    def _():
        qkv_s[1] = jnp.zeros(qkv_s.shape[1:], qkv_s.dtype)
        r_s[1] = jnp.zeros(r_s.shape[1:], r_s.dtype)
        al_s[1] = jnp.zeros(al_s.shape[1:], al_s.dtype)
        xkeep_s[...] = jnp.zeros(xkeep_s.shape, xkeep_s.dtype)

    @pl.when(jnp.maximum(g - 1, 0) % tiles_per_seq == 0)
    def _():
        s_ref[...] = s0_ref[...]

    x = x_ref[0]
    hn_s[...] = (x * lax.rsqrt(jnp.mean(x * x, axis=-1, keepdims=True) + EPS) * pre_ref[...]).astype(BF16)
    al_s[slot] = jnp.dot(hn_s[...], wlow_ref[...], preferred_element_type=F32)

    row = lax.broadcasted_iota(jnp.int32, (t, t), 0)
    col = lax.broadcasted_iota(jnp.int32, (t, t), 1)
    causal = row >= col
    aup = aup_ref[...]
    ab = ab_ref[...]
    heads = [(h, slice(h * G_DK, (h + 1) * G_DK), slice(h * G_DV, (h + 1) * G_DV)) for h in range(N_HEADS)]

    def project(c0, c1):
        acc = jnp.dot(hn_s[...], win_ref[:, c0:c1], preferred_element_type=F32)
        if c0 < e_lo:
            hi = min(c1, e_lo)
            qkv_s[slot, :, c0:hi] = acc[:, :hi - c0].astype(BF16)
        if c1 > e_lo:
            lo = max(c0, e_lo)
            r_s[slot, :, lo - e_lo:c1 - e_lo] = acc[:, lo - c0:]

    n_sub = 3
    sub = cols_per_chunk // n_sub
    for j in range(n_chunks):
        c0 = j * cols_per_chunk
        rs = slice(j * t, (j + 1) * t)
        pre = _dot(al_s[prev, rs, :], aup) + ab
        project(c0, c0 + sub)
        b = _cumsum_rows(_log_sigmoid(pre) / G_TAU)
        b_last = b[t - 1:t, :]
        project(c0 + sub, c0 + 2 * sub)
        k = qkv_s[prev, rs, G_KW:2 * G_KW].astype(F32)
        qd = ((qkv_s[prev, rs, 0:G_KW].astype(F32) * (G_DK ** -0.5)) * jnp.exp(b)).astype(BF16)
        kn = (k * jnp.exp(-b)).astype(BF16)
        kr = (k * jnp.exp(b_last - b)).astype(BF16)
        ebl_col = _row_to_col(jnp.exp(b_last))
        att = [jnp.where(causal, _dot_nt(qd[:, ks], kn[:, ks]), 0.0).astype(BF16) for _, ks, _ in heads]
        project(c0 + 2 * sub, c0 + cols_per_chunk)
        for h, ks, vs in heads:
            for d0 in range(0, G_DV, G_DV // 2):
                dv = slice(d0, d0 + G_DV // 2)
                oc = slice(vs.start + d0, vs.start + d0 + G_DV // 2)
                s_h = s_ref[0, h, :, dv]
                v = qkv_s[prev, rs, 2 * G_KW + oc.start:2 * G_KW + oc.stop]
                o_s[rs, oc] = (_dot(qd[:, ks], s_h) + _dot(att[h], v)).astype(o_s.dtype)
                s_ref[0, h, :, dv] = ebl_col[ks, :] * s_h + _dot_tn(kr[:, ks], v)

    o = _head_rms(o_s[...].astype(F32), N_HEADS) * ng_ref[...] * _silu(r_s[prev])
    out = jnp.dot(o.astype(BF16), wout_ref[...], preferred_element_type=F32)
    nrm = out * lax.rsqrt(jnp.mean(out * out, axis=-1, keepdims=True) + EPS) * post_ref[...]
    y_ref[0] = xkeep_s[...] + nrm
    xkeep_s[...] = x


def _gla_layer(x3, pre, post, w_in, w_low, aup, ab, ng, w_out, s0, rows):
    nseq, seq, d = x3.shape
    assert seq % rows == 0 and rows % GLA_TILES["main"][1] == 0
    tps = seq // rows
    n_tiles = nseq * tps
    e_lo = 2 * G_KW + G_VW
    e = e_lo + G_VW
    const = lambda a: pl.BlockSpec(a.shape, lambda g: (0,) * a.ndim, pipeline_mode=pl.Buffered(1))
    cur = lambda g: jnp.minimum(g, n_tiles - 1)
    old = lambda g: jnp.maximum(g - 1, 0)
    st = pl.BlockSpec((1, N_HEADS, G_DK, G_DV), lambda g: (old(g) // tps, 0, 0, 0))
    st_in = st if s0.shape[0] == nseq else pl.BlockSpec((1, N_HEADS, G_DK, G_DV), lambda g: (0, 0, 0, 0))
    kern = functools.partial(_gla_layer_kernel, rows=rows, tiles_per_seq=tps)
    return pl.pallas_call(
        kern,
        grid=(n_tiles + 1,),
        in_specs=[
            pl.BlockSpec((1, rows, d), lambda g: (cur(g) // tps, cur(g) % tps, 0)),
            const(pre), const(post), pl.BlockSpec((d, e), lambda g: (0, 0), pipeline_mode=pl.Buffered(1)),
            const(w_low), const(aup), const(ab), const(ng), const(w_out), st_in,
        ],
        out_specs=[pl.BlockSpec((1, rows, d), lambda g: (old(g) // tps, old(g) % tps, 0)), st],
        out_shape=[
            jax.ShapeDtypeStruct((nseq, seq, d), F32),
            jax.ShapeDtypeStruct((nseq, N_HEADS, G_DK, G_DV), F32),
        ],
        scratch_shapes=[
            pltpu.VMEM((2, rows, e_lo), BF16),
            pltpu.VMEM((2, rows, e - e_lo), F32),
            pltpu.VMEM((2, rows, LANES), F32),
            pltpu.VMEM((rows, d), BF16),
            pltpu.VMEM((rows, G_VW), BF16),
            pltpu.VMEM((rows, d), F32),
        ],
        compiler_params=_cparams(("arbitrary",)),
        name="gla_layer",
    )(x3, pre, post, w_in, w_low, aup, ab, ng, w_out, s0)


def _pad_lanes(a, width):
    return jnp.pad(a, [(0, 0)] * (a.ndim - 1) + [(0, width - a.shape[-1])])


def _even_layer(xm, xs, nbp, seq, nbs, ls, st, pre_g, post_g, w_in, w_out, b_i, b_f, m_norm_g,
                conv_w, conv_b, wa, ba, wx, bx, lam):
    c0s, n0s, m0s, h0s, conv0s = st
    e_a = 4 * M_WIDTH
    n_gate = 2 * N_HEADS
    w_bf = w_in.astype(BF16)
    w_b = w_bf[:, e_a + n_gate:]
    wig, wfg = w_in[:, e_a:e_a + N_HEADS], w_in[:, e_a + N_HEADS:e_a + n_gate]
    w_gate = jnp.concatenate([_pad_lanes(wig, LANES), _pad_lanes(wfg, LANES)], axis=1).astype(BF16)
    gbias = jnp.concatenate([_pad_lanes(b_i[None], LANES), _pad_lanes(b_f[None], LANES)], axis=1)
    ng = m_norm_g[None]
    post = post_g[None]
    cw, cb = conv_w, conv_b[None]
    wab, wxb = wa.astype(BF16), wx.astype(BF16)
    bav, bxv, lamv = ba[None], bx[None], lam[None]
    w_o1, w_o2 = w_out[:M_WIDTH].astype(BF16), w_out[M_WIDTH:].astype(BF16)
    e_lo = 3 * M_WIDTH
    e_hi = e_a + w_b.shape[1] - e_lo
    lru = (cw, cb, wab, bav, wxb, bxv, lamv)

    qs, ps, gs = _norm_proj(xs, pre_g, w_bf, e_a, w_b, w_gate, e_lo, xs.shape[0])
    mrow = nbs * SAMPLE_PAD // N_META
    xr_cols = slice(2 * LRU_WIDTH, 3 * LRU_WIDTH)

    zc = jnp.zeros((1, N_HEADS, M_DV, M_DK), F32)
    zn = jnp.zeros((1, N_HEADS, 1, M_DK), F32)
    zm = jnp.zeros((1, 1, LANES), F32)
    zh = jnp.zeros((1, 1, LRU_WIDTH), F32)
    zconv = jnp.zeros((1, SUBLANES, LRU_WIDTH), F32)
    qs_meta = qs.reshape(-1, N_META, e_lo)
    ps_meta = ps.reshape(-1, N_META, e_hi)
    gs_meta = gs.reshape(-1, N_META, 2 * LANES)
    h_meta, c1, n1, m1 = _mlstm(qs_meta, gs_meta, mrow, 1, MLSTM_TILES["meta"], 1, N_META, gbias, zc, zn, zm)
    hl_meta, h1 = _rglru(ps_meta, mrow, 1, LRU_TILES["meta"], 1, N_META, *lru, zh, zconv)
    conv1 = ps_meta[mrow:mrow + 1, N_META - SUBLANES:, xr_cols]

    ym3, pc, pn, pmm, ph, conv_tail = _even_layer_main(
        xm.reshape(nbp, seq, -1), pre_g[None], post, w_bf, e_a, w_b, w_gate, gbias, ng, lru, w_o1, w_o2,
        c1, n1, m1, h1, conv1, FUSED_ROW_TILE)
    ym = ym3.reshape(xm.shape)
    pconv = conv_tail[:, SUBLANES - (CONV_W - 1):, :]

    qs_s = qs.reshape(-1, SAMPLE_PAD, e_lo)
    ps_s = ps.reshape(-1, SAMPLE_PAD, e_hi)
    gs_s = gs.reshape(-1, SAMPLE_PAD, 2 * LANES)
    h_s, sc, sn, sm = _mlstm(qs_s, gs_s, 0, nbs, MLSTM_TILES["sample"], 1, ls, gbias,
                             c0s, n0s[:, :, None, :], _pad_lanes(m0s, LANES)[:, None, :])
    conv0p = jnp.pad(conv0s, ((0, 0), (SUBLANES - (CONV_W - 1), 0), (0, 0)))
    hl_s, sh = _rglru(ps_s, 0, nbs, LRU_TILES["sample"], 1, ls, *lru, h0s[:, None, :], conv0p)
    xr_s = ps_s[:nbs, :ls, xr_cols]
    sconv = jnp.concatenate([conv0s, xr_s], axis=1)[:, -(CONV_W - 1):]

    h_small = jnp.concatenate([h_s.reshape(-1, M_WIDTH), h_meta.reshape(-1, M_WIDTH)], axis=0)
    hl_small = jnp.concatenate([hl_s.reshape(-1, LRU_WIDTH), hl_meta.reshape(-1, LRU_WIDTH)], axis=0)
    ys = _out_even(xs, post, h_small, ps, ng, hl_small, w_o1, w_o2, _row_tile(xs.shape[0], OUT_ROW_TILE))

    p_state = (pc, pn[:, :, 0, :], pmm[:, 0, :N_HEADS], ph[:, 0, :], pconv)
    s_state = (sc, sn[:, :, 0, :], sm[:, 0, :N_HEADS], sh[:, 0, :], sconv)
    return ym, ys, p_state, s_state


def _odd_layer(xm, xs, nbp, seq, nbs, ls, s0s, pre_g, post_g, w_in, w_out, a_up, a_b, g_norm_g):
    e_lo = 2 * G_KW + G_VW
    e_a = 2 * G_KW + 2 * G_VW
    w_bf = w_in.astype(BF16)
    w_low = _pad_lanes(w_in[:, 2 * G_KW + 2 * G_VW:], LANES).astype(BF16)
    aup = jnp.pad(a_up, ((0, LANES - G_RANK), (0, 0))).astype(BF16)
    ab = a_b[None]
    ng = g_norm_g[None]
    post = post_g[None]
    w_o = w_out.astype(BF16)

    qs, rs, as_ = _norm_proj(xs, pre_g, w_bf, e_a, None, w_low, e_lo, xs.shape[0])
    mrow = nbs * SAMPLE_PAD // N_META

    zs = jnp.zeros((1, N_HEADS, G_DK, G_DV), F32)
    o_meta, s1 = _gla(qs.reshape(-1, N_META, e_lo), as_.reshape(-1, N_META, LANES),
                      mrow, 1, GLA_TILES["meta"], 1, N_META, aup, ab, zs)
    ym3, p_s = _gla_layer(xm.reshape(nbp, seq, -1), pre_g[None], post, w_bf, w_low, aup, ab, ng, w_o, s1,
                          GLA_ROW_TILE)
    ym = ym3.reshape(xm.shape)
    o_s, s_s = _gla(qs.reshape(-1, SAMPLE_PAD, e_lo), as_.reshape(-1, SAMPLE_PAD, LANES),
                    0, nbs, GLA_TILES["sample"], 1, ls, aup, ab, s0s)

    o_small = jnp.concatenate([o_s.reshape(-1, G_VW), o_meta.reshape(-1, G_VW)], axis=0)
    ys = _out_odd(xs, post, o_small, rs, ng, w_o, _row_tile(xs.shape[0], OUT_ROW_TILE))
    return ym, ys, p_s, s_s


def kernel(x_prompt, x_sample, state_mlstm_C, state_mlstm_n, state_mlstm_m, state_rglru_h, state_rglru_conv,
           state_gla_S, meta_tokens, pre_norm_a, post_norm_a, w_in_a, w_out_a, mlstm_b_i, mlstm_b_f, mlstm_norm,
           conv_w, conv_b, lru_w_a, lru_b_a, lru_w_x, lru_b_x, lru_lambda, pre_norm_c, post_norm_c, w_in_c,
           w_out_c, gla_alpha_up, gla_alpha_b, gla_norm):
    nbp, seq, d = x_prompt.shape
    nbs, ls, _ = x_sample.shape
    depth = pre_norm_a.shape[0] + pre_norm_c.shape[0]
    assert ls >= CONV_W - 1 and ls <= SAMPLE_PAD and N_META % SUBLANES == 0 and nbs * SAMPLE_PAD % N_META == 0

    xm = x_prompt.reshape(nbp * seq, d)
    xs_pad = jnp.pad(x_sample, ((0, 0), (0, SAMPLE_PAD - ls), (0, 0))).reshape(nbs * SAMPLE_PAD, d)
    xs = jnp.concatenate([xs_pad, meta_tokens.astype(x_prompt.dtype)], axis=0)

    p_lists = [[] for _ in range(6)]
    s_lists = [[] for _ in range(6)]
    for layer in range(depth):
        j = layer // 2
        if layer % 2 == 0:
            st = (state_mlstm_C[j], state_mlstm_n[j], state_mlstm_m[j], state_rglru_h[j], state_rglru_conv[j])
            xm, xs, pst, sst = _even_layer(
                xm, xs, nbp, seq, nbs, ls, st, pre_norm_a[j], post_norm_a[j], w_in_a[j], w_out_a[j],
                mlstm_b_i[j], mlstm_b_f[j], mlstm_norm[j], conv_w[j], conv_b[j], lru_w_a[j], lru_b_a[j],
                lru_w_x[j], lru_b_x[j], lru_lambda[j])
            for i in range(5):
                p_lists[i].append(pst[i])
                s_lists[i].append(sst[i])
        else:
            xm, xs, p_s, s_s = _odd_layer(
                xm, xs, nbp, seq, nbs, ls, state_gla_S[j], pre_norm_c[j], post_norm_c[j], w_in_c[j], w_out_c[j],
                gla_alpha_up[j], gla_alpha_b[j], gla_norm[j])
            p_lists[5].append(p_s)
            s_lists[5].append(s_s)

    y_prompt = xm.reshape(nbp, seq, d)
    y_sample = xs[:nbs * SAMPLE_PAD].reshape(nbs, SAMPLE_PAD, d)[:, :ls]
    return (y_prompt, y_sample) + tuple(jnp.stack(l) for l in p_lists) + tuple(jnp.stack(l) for l in s_lists)
```
